```python
import jax, jax.numpy as jnp
from jax import lax
import numpy as np

D_MODEL = 1024
BATCH = 4
SEQ = 4096
DEPTH = 1
DEC_BATCH = 128
DEC_SEQ = 4
PAST_LEN = 8192
PAGE_SIZE = 128

HEAD_DIM = 64
N_Q_HEADS = 8
N_KV_HEADS = 2
Q_PER_KV = N_Q_HEADS // N_KV_HEADS
WINDOW = 128
ATTN_BLOCK = 128
ATTN_WIDTH = N_Q_HEADS * HEAD_DIM
KV_WIDTH = N_KV_HEADS * HEAD_DIM
CONV_WIDTH = D_MODEL // 4
CONV_K = 3
N_MEM = 256
N_MEM_HEADS = 4
MEM_WIDTH = N_MEM_HEADS * HEAD_DIM
MIX_WIDTH = ATTN_WIDTH + CONV_WIDTH + MEM_WIDTH
IN_WIDTH = ATTN_WIDTH + 2 * KV_WIDTH + 3 * CONV_WIDTH + MEM_WIDTH
N_EXPERTS = 32
TOP_K = 4
D_FF = D_MODEL
SWIGLU_LIMIT = 7.0
SWIGLU_ALPHA = 1.702
MOE_BLOCK = 128
EPS = 1e-5
ATTN_SCALE = HEAD_DIM ** -0.5

kernel_name = "hybrid_swa_shortconv_memxattn_moe_step"


def _rms_f32(x):
    xf = x.astype(jnp.float32)
    return xf * lax.rsqrt(jnp.mean(xf * xf, axis=-1, keepdims=True) + EPS)


def rmsnorm(x, g):
    return (_rms_f32(x) * g.astype(jnp.float32)).astype(x.dtype)


def group_norm_concat(parts, g):
    y = jnp.concatenate([_rms_f32(p) for p in parts], axis=-1) * g.astype(jnp.float32)
    return y.astype(parts[0].dtype)


def alibi_slopes(n):
    return 2.0 ** (-8.0 * jnp.arange(1, n + 1, dtype=jnp.float32) / n)


def sink_attend(s, sinks, v, eq):
    sink_b = sinks[:, :, None, None]
    m = jnp.maximum(jnp.max(s, axis=-1, keepdims=True), sink_b)
    p = jnp.exp(s - m)
    denom = jnp.sum(p, axis=-1, keepdims=True) + jnp.exp(sink_b - m)
    return jnp.einsum(eq, (p / denom).astype(v.dtype), v)


def swa_prompt(q, k, v, sinks, slopes):
    B, S = q.shape[0], q.shape[1]
    L = ATTN_BLOCK
    nb = S // L
    qb = q.reshape(B, nb, L, N_KV_HEADS, Q_PER_KV, HEAD_DIM)
    kb = k.reshape(B, nb, L, N_KV_HEADS, HEAD_DIM)
    vb = v.reshape(B, nb, L, N_KV_HEADS, HEAD_DIM)
    pad = ((0, 0), (1, 0), (0, 0), (0, 0), (0, 0))
    kk = jnp.concatenate([jnp.pad(kb, pad)[:, :-1], kb], axis=2)
    vv = jnp.concatenate([jnp.pad(vb, pad)[:, :-1], vb], axis=2)
    dist = (L + jnp.arange(L))[:, None] - jnp.arange(2 * L)[None, :]
    key_pos = jnp.arange(nb)[:, None] * L - L + jnp.arange(2 * L)[None, :]
    mask = ((dist >= 0) & (dist < WINDOW))[None] & (key_pos >= 0)[:, None, :]
    s = jnp.einsum('bnqkgd,bnskd->bnkgqs', qb, kk).astype(jnp.float32) * ATTN_SCALE
    s = s - slopes[:, :, None, None] * dist.astype(jnp.float32)
    s = jnp.where(mask[None, :, None, None], s, -jnp.inf)
    out = sink_attend(s, sinks, vv, 'bnkgqs,bnskd->bnqkgd')
    return out.reshape(B, S, ATTN_WIDTH)


def swa_decode(q, k_all, v_all, sinks, slopes):
    B, T = q.shape[0], q.shape[1]
    W = k_all.shape[1] - T
    dist = (W + jnp.arange(T))[:, None] - jnp.arange(W + T)[None, :]
    mask = (dist >= 0) & (dist < WINDOW)
    s = jnp.einsum('btkgd,bskd->bkgts', q, k_all).astype(jnp.float32) * ATTN_SCALE
    s = s - slopes[:, :, None, None] * dist.astype(jnp.float32)
    s = jnp.where(mask[None, None, None], s, -jnp.inf)
    out = sink_attend(s, sinks, v_all, 'bkgts,bskd->btkgd')
    return out.reshape(B, T, ATTN_WIDTH)


def mem_kv(mem, g_mem_norm, w_mem_kv):
    B, M, _ = mem.shape
    kv = rmsnorm(mem, g_mem_norm) @ w_mem_kv
    mk, mv = jnp.split(kv, 2, axis=-1)
    return (mk.reshape(B, M, N_MEM_HEADS, HEAD_DIM), mv.reshape(B, M, N_MEM_HEADS, HEAD_DIM))


def mem_attend(q, mk, mv):
    B, T = q.shape[0], q.shape[1]
    s = jnp.einsum('bthd,bmhd->bhtm', q, mk).astype(jnp.float32) * ATTN_SCALE
    w = jax.nn.softmax(s, axis=-1).astype(q.dtype)
    return jnp.einsum('bhtm,bmhd->bthd', w, mv).reshape(B, T, MEM_WIDTH)


def moe(x, w_router, b_router, w_gate_up, b_gate_up, w_down, b_down):
    N, D = x.shape
    logits = (x @ w_router).astype(jnp.float32) + b_router.astype(jnp.float32)
    top_v, top_e = lax.top_k(logits, TOP_K)
    gates = jax.nn.softmax(top_v, axis=-1)
    NK = N * TOP_K
    flat_e = top_e.reshape(NK)
    flat_tok = jnp.arange(NK, dtype=jnp.int32) // TOP_K
    order = jnp.argsort(flat_e)
    sorted_e = flat_e[order]
    counts = jnp.bincount(flat_e, length=N_EXPERTS)
    starts = jnp.cumsum(counts) - counts
    padded = (counts + MOE_BLOCK - 1) // MOE_BLOCK * MOE_BLOCK
    pad_ends = jnp.cumsum(padded)
    pad_starts = pad_ends - padded
    dest_sorted = pad_starts[sorted_e] + jnp.arange(NK, dtype=jnp.int32) - starts[sorted_e]
    n_blocks = -(-NK // MOE_BLOCK) + N_EXPERTS
    rows = n_blocks * MOE_BLOCK
    row_tok = jnp.zeros((rows,), jnp.int32).at[dest_sorted].set(flat_tok[order])
    block_e = jnp.minimum(jnp.searchsorted(pad_ends, jnp.arange(n_blocks) * MOE_BLOCK, side='right'), N_EXPERTS - 1)
    xs = x[row_tok].reshape(n_blocks, MOE_BLOCK, D)

    def expert_block(args):
        xb, e = args
        h = xb @ w_gate_up[e] + b_gate_up[e]
        gate = jnp.minimum(h[:, 0::2], SWIGLU_LIMIT)
        up = jnp.clip(h[:, 1::2], -SWIGLU_LIMIT, SWIGLU_LIMIT)
        glu = gate * jax.nn.sigmoid(gate * SWIGLU_ALPHA)
        return ((up + 1.0) * glu) @ w_down[e] + b_down[e]

    outs = lax.map(expert_block, (xs, block_e)).reshape(rows, D)
    dest = jnp.zeros((NK,), jnp.int32).at[order].set(dest_sorted)
    picked = outs[dest].reshape(N, TOP_K, D)
    return jnp.sum(picked * gates[..., None].astype(x.dtype), axis=1)


def layer(x, mk, mv, win_k, win_v, conv_prev, g_attn_norm, w_in, conv_w, attn_sinks, g_mix_out, w_out,
          g_ffn_norm, w_router, b_router, w_gate_up, b_gate_up, w_down, b_down):
    B, T, _ = x.shape
    xn = rmsnorm(x, g_attn_norm)
    z = xn @ w_in
    c0 = ATTN_WIDTH
    c1 = c0 + KV_WIDTH
    c2 = c1 + KV_WIDTH
    c3 = c2 + CONV_WIDTH
    c4 = c3 + CONV_WIDTH
    c5 = c4 + CONV_WIDTH
    q, k, v, cb, cc, cv, mq = jnp.split(z, [c0, c1, c2, c3, c4, c5], axis=-1)
    q = q.reshape(B, T, N_KV_HEADS, Q_PER_KV, HEAD_DIM)
    k = k.reshape(B, T, N_KV_HEADS, HEAD_DIM)
    v = v.reshape(B, T, N_KV_HEADS, HEAD_DIM)
    slopes = alibi_slopes(N_Q_HEADS).reshape(N_KV_HEADS, Q_PER_KV)
    sinks = attn_sinks.astype(jnp.float32).reshape(N_KV_HEADS, Q_PER_KV)
    if win_k is None:
        attn = swa_prompt(q, k, v, sinks, slopes)
        k_all, v_all, keep = k, v, WINDOW
        conv_prev = jnp.zeros((B, CONV_K - 1, CONV_WIDTH), x.dtype)
    else:
        k_all = jnp.concatenate([win_k, k], axis=1)
        v_all = jnp.concatenate([win_v, v], axis=1)
        attn = swa_decode(q, k_all, v_all, sinks, slopes)
        keep = win_k.shape[1]
    new_wk = k_all[:, -keep:]
    new_wv = v_all[:, -keep:]
    u_all = jnp.concatenate([conv_prev, cc * cv], axis=1)
    conv_y = conv_w[0] * u_all[:, 0:T]
    for j in range(1, CONV_K):
        conv_y = conv_y + conv_w[j] * u_all[:, j:j + T]
    conv_out = cb * conv_y
    new_conv = u_all[:, -(CONV_K - 1):]
    cross = mem_attend(mq.reshape(B, T, N_MEM_HEADS, HEAD_DIM), mk, mv)
    mix = group_norm_concat([attn, conv_out, cross], g_mix_out)
    h = x + mix @ w_out
    hn = rmsnorm(h, g_ffn_norm).reshape(B * T, D_MODEL)
    y = h + moe(hn, w_router, b_router, w_gate_up, b_gate_up, w_down, b_down).reshape(B, T, D_MODEL)
    return y, new_wk, new_wv, new_conv


def setup_inputs(seed: int = 0) -> dict:
    key = jax.random.key(seed)
    ks = jax.random.split(key, 24)
    f32 = jnp.float32
    win_len = min(WINDOW, PAST_LEN)

    def nrm(k, shape, scale):
        return jax.random.normal(k, shape, f32) * scale

    return {
        'x_prompt': nrm(ks[0], (BATCH, SEQ, D_MODEL), 1.0),
        'x_sample': nrm(ks[1], (DEC_BATCH, DEC_SEQ, D_MODEL), 1.0),
        'mem_prompt': nrm(ks[2], (BATCH, N_MEM, D_MODEL), 1.0),
        'cache_win_k': nrm(ks[3], (DEPTH, DEC_BATCH, win_len, N_KV_HEADS, HEAD_DIM), 1.0),
        'cache_win_v': nrm(ks[4], (DEPTH, DEC_BATCH, win_len, N_KV_HEADS, HEAD_DIM), 1.0),
        'state_conv': nrm(ks[5], (DEPTH, DEC_BATCH, CONV_K - 1, CONV_WIDTH), 1.0),
        'cache_mem_k': nrm(ks[6], (DEPTH, DEC_BATCH, N_MEM, N_MEM_HEADS, HEAD_DIM), 1.0),
        'cache_mem_v': nrm(ks[7], (DEPTH, DEC_BATCH, N_MEM, N_MEM_HEADS, HEAD_DIM), 1.0),
        'g_attn_norm': 1.0 + nrm(ks[8], (DEPTH, D_MODEL), 0.01),
        'w_in': nrm(ks[9], (DEPTH, D_MODEL, IN_WIDTH), D_MODEL ** -0.5),
        'conv_w': nrm(ks[10], (DEPTH, CONV_K, CONV_WIDTH), CONV_K ** -0.5),
        'attn_sinks': nrm(ks[11], (DEPTH, N_Q_HEADS), 0.5),
        'g_mem_norm': 1.0 + nrm(ks[12], (DEPTH, D_MODEL), 0.01),
        'w_mem_kv': nrm(ks[13], (DEPTH, D_MODEL, 2 * MEM_WIDTH), D_MODEL ** -0.5),
        'g_mix_out': 1.0 + nrm(ks[14], (DEPTH, MIX_WIDTH), 0.01),
        'w_out': nrm(ks[15], (DEPTH, MIX_WIDTH, D_MODEL), MIX_WIDTH ** -0.5),
        'g_ffn_norm': 1.0 + nrm(ks[16], (DEPTH, D_MODEL), 0.01),
        'w_router': nrm(ks[17], (DEPTH, D_MODEL, N_EXPERTS), D_MODEL ** -0.5),
        'b_router': nrm(ks[18], (DEPTH, N_EXPERTS), 0.01),
        'w_gate_up': nrm(ks[19], (DEPTH, N_EXPERTS, D_MODEL, 2 * D_FF), D_MODEL ** -0.5),
        'b_gate_up': nrm(ks[20], (DEPTH, N_EXPERTS, 2 * D_FF), 0.01),
        'w_down': nrm(ks[21], (DEPTH, N_EXPERTS, D_FF, D_MODEL), D_FF ** -0.5),
        'b_down': nrm(ks[22], (DEPTH, N_EXPERTS, D_MODEL), 0.01),
        'g_final': 1.0 + nrm(ks[23], (D_MODEL,), 0.01),
    }


def reference(x_prompt, x_sample, mem_prompt, cache_win_k, cache_win_v, state_conv, cache_mem_k, cache_mem_v,
              g_attn_norm, w_in, conv_w, attn_sinks, g_mem_norm, w_mem_kv, g_mix_out, w_out, g_ffn_norm,
              w_router, b_router, w_gate_up, b_gate_up, w_down, b_down, g_final):
    yp, ys = x_prompt, x_sample
    wkp_l, wvp_l, cp_l, mkp_l, mvp_l, wks_l, wvs_l, cs_l = [], [], [], [], [], [], [], []
    for l in range(DEPTH):
        mk_p, mv_p = mem_kv(mem_prompt, g_mem_norm[l], w_mem_kv[l])
        yp, wkp, wvp, cp = layer(yp, mk_p, mv_p, None, None, None,
                                 g_attn_norm[l], w_in[l], conv_w[l], attn_sinks[l], g_mix_out[l], w_out[l],
                                 g_ffn_norm[l], w_router[l], b_router[l], w_gate_up[l], b_gate_up[l],
                                 w_down[l], b_down[l])
        ys, wks, wvs, cs = layer(ys, cache_mem_k[l], cache_mem_v[l], cache_win_k[l], cache_win_v[l], state_conv[l],
                                 g_attn_norm[l], w_in[l], conv_w[l], attn_sinks[l], g_mix_out[l], w_out[l],
                                 g_ffn_norm[l], w_router[l], b_router[l], w_gate_up[l], b_gate_up[l],
                                 w_down[l], b_down[l])
        wkp_l.append(wkp)
        wvp_l.append(wvp)
        cp_l.append(cp)
        mkp_l.append(mk_p)
        mvp_l.append(mv_p)
        wks_l.append(wks)
        wvs_l.append(wvs)
        cs_l.append(cs)
    y_prompt = rmsnorm(yp, g_final)
    y_sample = rmsnorm(ys, g_final)
    return (y_prompt, y_sample,
            jnp.stack(wkp_l), jnp.stack(wvp_l), jnp.stack(cp_l), jnp.stack(mkp_l), jnp.stack(mvp_l),
            jnp.stack(wks_l), jnp.stack(wvs_l), jnp.stack(cs_l))
```

```python
import functools

import jax
import jax.numpy as jnp
from jax import lax
from jax.experimental import pallas as pl
from jax.experimental.pallas import tpu as pltpu

F32 = jnp.float32
BF16 = jnp.bfloat16

HEAD_DIM = 64
N_Q_HEADS = 8
N_KV_HEADS = 2
WINDOW = 128
ATTN_WIDTH = N_Q_HEADS * HEAD_DIM
KV_WIDTH = N_KV_HEADS * HEAD_DIM
N_MEM_HEADS = 4
MEM_WIDTH = N_MEM_HEADS * HEAD_DIM
TOP_K = 4
SWIGLU_LIMIT = 7.0
SWIGLU_ALPHA = 1.702
EPS = 1e-5
ATTN_SCALE = HEAD_DIM ** -0.5
ALIBI_SLOPES = tuple(2.0 ** (-8.0 * (h + 1) / N_Q_HEADS) for h in range(N_Q_HEADS))

LANES = 128
SUBLANES = 8
VMEM_LIMIT = 56 * 1024 * 1024

TQ = 256
BB = 16
BM = 256
REP = 8


def _rms(x):
    return x * lax.rsqrt(jnp.mean(x * x, axis=-1, keepdims=True) + EPS)


def _dot(a, b):
    return jnp.dot(a, b, preferred_element_type=F32)


def _dot_nt(a, b):
    return lax.dot_general(a, b, (((1,), (1,)), ((), ())), preferred_element_type=F32)


def _iota(shape, axis):
    return lax.broadcasted_iota(jnp.int32, shape, axis)


def _memkv_kernel(mem_ref, g_ref, w_ref, mk_ref, mv_ref, mkt_ref, mvb_ref):
    xn = (_rms(mem_ref[0]) * g_ref[...]).astype(BF16)
    kv = _dot(xn, w_ref[...])
    mk = kv[:, :MEM_WIDTH]
    mv = kv[:, MEM_WIDTH:]
    mk_ref[0] = mk
    mv_ref[0] = mv
    mkt_ref[0] = mk.T.astype(BF16)
    mvb_ref[0] = mv.astype(BF16)


def _memkv(mem, g, w_bf):
    b, m, d = mem.shape
    out_f = jax.ShapeDtypeStruct((b, m, MEM_WIDTH), F32)
    out_b = jax.ShapeDtypeStruct((b, m, MEM_WIDTH), BF16)
    out_t = jax.ShapeDtypeStruct((b, MEM_WIDTH, m), BF16)
    blk = lambda r, c: pl.BlockSpec((1, r, c), lambda i: (i, 0, 0))
    return pl.pallas_call(
        _memkv_kernel,
        grid=(b,),
        in_specs=[blk(m, d), pl.BlockSpec((1, d), lambda i: (0, 0)),
                  pl.BlockSpec((d, 2 * MEM_WIDTH), lambda i: (0, 0))],
        out_specs=[blk(m, MEM_WIDTH), blk(m, MEM_WIDTH), blk(MEM_WIDTH, m), blk(m, MEM_WIDTH)],
        out_shape=[out_f, out_f, out_t, out_b],
        name="memkv",
    )(mem, g, w_bf)


def _router_topk(hn, w_r_ref, b_r_ref, tope_ref, gates_ref):
    n_e = w_r_ref.shape[1]
    logits = _dot(hn.astype(BF16), w_r_ref[...]) + b_r_ref[...]
    rows = logits.shape[0]
    col = _iota((rows, n_e), 1).astype(F32)
    vals, idxs = [], []
    cur = logits
    for _ in range(TOP_K):
        m = jnp.max(cur, axis=-1, keepdims=True)
        idx = jnp.min(jnp.where(cur == m, col, float(n_e)), axis=-1, keepdims=True)
        vals.append(m)
        idxs.append(idx)
        cur = jnp.where(col == idx, -jnp.inf, cur)
    exps = [jnp.exp(v - vals[0]) for v in vals]
    tot = exps[0] + exps[1] + exps[2] + exps[3]
    col4 = _iota((rows, TOP_K), 1)
    te = jnp.zeros((rows, TOP_K), F32)
    ga = jnp.zeros((rows, TOP_K), F32)
    for k in range(TOP_K):
        te = jnp.where(col4 == k, idxs[k], te)
        ga = jnp.where(col4 == k, exps[k] / tot, ga)
    tope_ref[...] = te.astype(jnp.int32)
    gates_ref[...] = ga


def _mix_out(x, attn, conv_out, cross, g_mix_ref, w_out_ref, g_ffn_ref, w_r_ref, b_r_ref,
             h_ref, hn_ref, tope_ref, gates_ref):
    mix = jnp.concatenate([_rms(attn), _rms(conv_out), _rms(cross)], axis=-1) * g_mix_ref[...]
    h = x + _dot(mix.astype(BF16), w_out_ref[...])
    hn = _rms(h) * g_ffn_ref[...]
    h_ref[...] = h
    hn_ref[...] = hn
    _router_topk(hn, w_r_ref, b_r_ref, tope_ref, gates_ref)


def _swa_block(q_blk, kk, vv, prev_lim, sinks_ref):
    blk = WINDOW
    lane = _iota((2 * blk, KV_WIDTH), 1)
    lo = lane < HEAD_DIM
    kk_r = pltpu.roll(kk, HEAD_DIM, axis=1)
    vv_r = pltpu.roll(vv, HEAD_DIM, axis=1)
    kdup = [jnp.where(lo, kk, kk_r).astype(BF16), jnp.where(lo, kk_r, kk).astype(BF16)]
    vlo = [jnp.where(lo, vv, 0.0).astype(BF16), jnp.where(lo, vv_r, 0.0).astype(BF16)]
    vhi = [jnp.where(lo, 0.0, vv_r).astype(BF16), jnp.where(lo, 0.0, vv).astype(BF16)]
    qi = _iota((blk, 2 * blk), 0)
    kj = _iota((blk, 2 * blk), 1)
    dist = blk + qi - kj
    mask = (dist >= 0) & (dist < WINDOW) & (kj >= prev_lim)
    distf = dist.astype(F32)
    qlo = _iota((blk, 2 * HEAD_DIM), 1) < HEAD_DIM
    outs = []
    for p in range(N_Q_HEADS // 2):
        kh = (2 * p) // (N_Q_HEADS // N_KV_HEADS)
        qp = q_blk[:, p * 2 * HEAD_DIM:(p + 1) * 2 * HEAD_DIM]
        acc = None
        for e in range(2):
            h = 2 * p + e
            qm = jnp.where(qlo if e == 0 else jnp.logical_not(qlo), qp, 0.0).astype(BF16)
            s = _dot_nt(qm, kdup[kh]) - ALIBI_SLOPES[h] * distf
            s = jnp.where(mask, s, -jnp.inf)
            sink = sinks_ref[h]
            m = jnp.maximum(jnp.max(s, axis=-1, keepdims=True), sink)
            pe = jnp.exp(s - m)
            denom = jnp.sum(pe, axis=-1, keepdims=True) + jnp.exp(sink - m)
            o = _dot(pe.astype(BF16), (vlo if e == 0 else vhi)[kh]) / denom
            acc = o if acc is None else acc + o
        outs.append(acc)
    return jnp.concatenate(outs, axis=1)


def _mem_attend_shared(mq, mkt, mvb):
    t = mq.shape[0]
    m_tok = mvb.shape[0]
    qhead = _iota((t, MEM_WIDTH), 1) // HEAD_DIM
    vhead = _iota((m_tok, MEM_WIDTH), 1) // HEAD_DIM
    cross = None
    for h in range(N_MEM_HEADS):
        qm = jnp.where(qhead == h, mq, 0.0).astype(BF16)
        s = _dot(qm, mkt)
        m = jnp.max(s, axis=-1, keepdims=True)
        pe = jnp.exp(s - m)
        denom = jnp.sum(pe, axis=-1, keepdims=True)
        vm = jnp.where(vhead == h, mvb, jnp.zeros_like(mvb))
        o = _dot(pe.astype(BF16), vm) / denom
        cross = o if cross is None else cross + o
    return cross


def _mixer_p_kernel(sinks_ref, x_ref, g_attn_ref, w_in_ref, conv_w_ref, g_mix_ref, w_out_ref, g_ffn_ref,
                    w_r_ref, b_r_ref, mkt_ref, mvb_ref,
                    h_ref, hn_ref, tope_ref, gates_ref, lastk_ref, lastv_ref, convst_ref,
                    ck_ref, cv_ref, cu_ref):
    j = pl.program_id(1)
    nj = pl.num_programs(1)

    @pl.when(j == 0)
    def _():
        ck_ref[...] = jnp.zeros_like(ck_ref)
        cv_ref[...] = jnp.zeros_like(cv_ref)
        cu_ref[...] = jnp.zeros_like(cu_ref)

    x = x_ref[0]
    xn = (_rms(x) * g_attn_ref[...]).astype(BF16)
    z = _dot(xn, w_in_ref[...])
    c0 = ATTN_WIDTH
    c1 = c0 + KV_WIDTH
    c2 = c1 + KV_WIDTH
    cw = conv_w_ref.shape[1]
    c3, c4, c5 = c2 + cw, c2 + 2 * cw, c2 + 3 * cw
    q = z[:, :c0] * ATTN_SCALE
    k = z[:, c0:c1]
    v = z[:, c1:c2]
    cb = z[:, c2:c3]
    cc = z[:, c3:c4]
    cvv = z[:, c4:c5]
    mq = z[:, c5:] * ATTN_SCALE

    blk = WINDOW
    attn_blocks = []
    for i in range(TQ // blk):
        if i == 0:
            pk, pv = ck_ref[...], cv_ref[...]
            prev_lim = jnp.where(j > 0, 0, blk)
        else:
            pk, pv = k[(i - 1) * blk:i * blk], v[(i - 1) * blk:i * blk]
            prev_lim = 0
        kk = jnp.concatenate([pk, k[i * blk:(i + 1) * blk]], axis=0)
        vv = jnp.concatenate([pv, v[i * blk:(i + 1) * blk]], axis=0)
        attn_blocks.append(_swa_block(q[i * blk:(i + 1) * blk], kk, vv, prev_lim, sinks_ref))
    attn = jnp.concatenate(attn_blocks, axis=0)
    ck_ref[...] = k[TQ - blk:]
    cv_ref[...] = v[TQ - blk:]

    u = cc * cvv
    row = _iota(u.shape, 0)
    u1 = jnp.where(row == 0, cu_ref[SUBLANES - 1:SUBLANES, :], pltpu.roll(u, 1, axis=0))
    u2 = jnp.where(row == 0, cu_ref[SUBLANES - 2:SUBLANES - 1, :],
                   jnp.where(row == 1, cu_ref[SUBLANES - 1:SUBLANES, :], pltpu.roll(u, 2, axis=0)))
    conv_out = cb * (conv_w_ref[0:1, :] * u2 + conv_w_ref[1:2, :] * u1 + conv_w_ref[2:3, :] * u)
    cu_ref[...] = u[TQ - SUBLANES:]

    cross = _mem_attend_shared(mq, mkt_ref[0], mvb_ref[0])

    @pl.when(j == nj - 1)
    def _():
        lastk_ref[0] = k[TQ - blk:]
        lastv_ref[0] = v[TQ - blk:]
        convst_ref[0] = u[TQ - 2:]

    _mix_out(x, attn, conv_out, cross, g_mix_ref, w_out_ref, g_ffn_ref, w_r_ref, b_r_ref,
             h_ref, hn_ref, tope_ref, gates_ref)


def _mixer_p(x, sinks, g_attn, w_in, conv_w, g_mix, w_out, g_ffn, w_r, b_r, mkt, mvb):
    b, s, d = x.shape
    nj = s // TQ
    n = b * s
    cw = conv_w.shape[1]
    full = lambda a: pl.BlockSpec(a.shape, lambda bi, ji, *_: (0,) * a.ndim)
    tok = lambda w: pl.BlockSpec((TQ, w), lambda bi, ji, *_: (bi * nj + ji, 0))
    per_b = lambda r, c: pl.BlockSpec((1, r, c), lambda bi, ji, *_: (bi, 0, 0))
    grid_spec = pltpu.PrefetchScalarGridSpec(
        num_scalar_prefetch=1,
        grid=(b, nj),
        in_specs=[pl.BlockSpec((1, TQ, d), lambda bi, ji, *_: (bi, ji, 0)),
                  full(g_attn), full(w_in), full(conv_w), full(g_mix), full(w_out), full(g_ffn),
                  full(w_r), full(b_r), per_b(MEM_WIDTH, mkt.shape[2]), per_b(mvb.shape[1], MEM_WIDTH)],
        out_specs=[tok(d), tok(d), tok(TOP_K), tok(TOP_K),
                   per_b(WINDOW, KV_WIDTH), per_b(WINDOW, KV_WIDTH), per_b(2, cw)],
        scratch_shapes=[pltpu.VMEM((WINDOW, KV_WIDTH), F32), pltpu.VMEM((WINDOW, KV_WIDTH), F32),
                        pltpu.VMEM((SUBLANES, cw), F32)],
    )
    return pl.pallas_call(
        _mixer_p_kernel,
        grid_spec=grid_spec,
        out_shape=[jax.ShapeDtypeStruct((n, d), F32), jax.ShapeDtypeStruct((n, d), F32),
                   jax.ShapeDtypeStruct((n, TOP_K), jnp.int32), jax.ShapeDtypeStruct((n, TOP_K), F32),
                   jax.ShapeDtypeStruct((b, WINDOW, KV_WIDTH), F32),
                   jax.ShapeDtypeStruct((b, WINDOW, KV_WIDTH), F32),
                   jax.ShapeDtypeStruct((b, 2, cw), F32)],
        compiler_params=pltpu.CompilerParams(dimension_semantics=("arbitrary", "arbitrary"),
                                             vmem_limit_bytes=VMEM_LIMIT),
        name="mixer_p",
    )(sinks, x, g_attn, w_in, conv_w, g_mix, w_out, g_ffn, w_r, b_r, mkt, mvb)


def _per_head_column(values, hrow):
    col = jnp.zeros(hrow.shape, F32)
    for h in range(N_Q_HEADS):
        col = jnp.where(hrow == h, values[h], col)
    return col


def _mixer_s_kernel(sinks_ref, x_ref, pm1_ref, pm2_ref, wk_ref, wv_ref, mk_ref, mv_ref,
                    g_attn_ref, w_in_ref, conv_w_ref, g_mix_ref, w_out_ref, g_ffn_ref, w_r_ref, b_r_ref,
                    h_ref, hn_ref, tope_ref, gates_ref, nwk_ref, nwv_ref, u_ref, *, t_dec):
    r_tok = BB * t_dec
    r_exp = r_tok * REP
    qrows = t_dec * REP
    x = x_ref[...]
    xn = (_rms(x) * g_attn_ref[...]).astype(BF16)
    z = _dot(xn, w_in_ref[...])
    c0 = ATTN_WIDTH
    c1 = c0 + KV_WIDTH
    c2 = c1 + KV_WIDTH
    cw = conv_w_ref.shape[1]
    c3, c4, c5 = c2 + cw, c2 + 2 * cw, c2 + 3 * cw
    q = z[:, :c0] * ATTN_SCALE
    k_new = z[:, c0:c1]
    v_new = z[:, c1:c2]
    cb = z[:, c2:c3]
    cc = z[:, c3:c4]
    cvv = z[:, c4:c5]
    mq = z[:, c5:] * ATTN_SCALE
    win = wk_ref.shape[1]

    xi = _iota((KV_WIDTH, ATTN_WIDTH), 0)
    xl = _iota((KV_WIDTH, ATTN_WIDTH), 1)
    q_per_kv = N_Q_HEADS // N_KV_HEADS
    expand = (xi == (xl // (q_per_kv * HEAD_DIM)) * HEAD_DIM + xl % HEAD_DIM).astype(BF16)
    rr = _iota((r_exp, r_tok), 0)
    rc = _iota((r_exp, r_tok), 1)
    rep = (rr // REP == rc).astype(BF16)

    hrow = _iota((r_exp, 1), 0) % REP
    trow = (_iota((r_exp, 1), 0) // REP) % t_dec
    slope_col = _per_head_column(ALIBI_SLOPES, hrow)
    sink_col = _per_head_column([sinks_ref[h] for h in range(N_Q_HEADS)], hrow)

    qexp = jnp.where(hrow == _iota((r_exp, ATTN_WIDTH), 1) // HEAD_DIM, _dot(rep, q.astype(BF16)), 0.0)
    kexp = _dot(wk_ref[...].reshape(BB * win, KV_WIDTH).astype(BF16), expand).astype(BF16)
    vexp = _dot(wv_ref[...].reshape(BB * win, KV_WIDTH).astype(BF16), expand).astype(BF16)
    s = jnp.einsum("bqc,bkc->bqk", qexp.astype(BF16).reshape(BB, qrows, ATTN_WIDTH),
                   kexp.reshape(BB, win, ATTN_WIDTH), preferred_element_type=F32).reshape(r_exp, win)
    scol = _iota((r_exp, win), 1)
    s = s - slope_col * (win + trow - scol).astype(F32)
    s = jnp.where(scol > trow, s, -jnp.inf)
    knew_exp = _dot(k_new.astype(BF16), expand).astype(BF16)
    vnew_exp = _dot(v_new.astype(BF16), expand).astype(BF16)
    s_new, v_rep = [], []
    for jn in range(t_dec):
        rep_j = (rc == (rr // qrows) * t_dec + jn).astype(BF16)
        k_rep = _dot(rep_j, knew_exp)
        v_rep.append(_dot(rep_j, vnew_exp))
        sj = jnp.sum(qexp * k_rep, axis=-1, keepdims=True) - slope_col * (trow - jn).astype(F32)
        s_new.append(jnp.where(trow >= jn, sj, -jnp.inf))
    m = jnp.maximum(jnp.max(s, axis=-1, keepdims=True), sink_col)
    for sj in s_new:
        m = jnp.maximum(m, sj)
    pe = jnp.exp(s - m)
    denom = jnp.sum(pe, axis=-1, keepdims=True) + jnp.exp(sink_col - m)
    o = jnp.einsum("bqk,bkc->bqc", pe.astype(BF16).reshape(BB, qrows, win),
                   vexp.reshape(BB, win, ATTN_WIDTH), preferred_element_type=F32).reshape(r_exp, ATTN_WIDTH)
    for jn in range(t_dec):
        pj = jnp.exp(s_new[jn] - m)
        denom = denom + pj
        o = o + pj * v_rep[jn]
    o = jnp.where(hrow == _iota((r_exp, ATTN_WIDTH), 1) // HEAD_DIM, o / denom, 0.0)
    attn = jnp.sum(o.reshape(r_tok, REP, ATTN_WIDTH), axis=1)

    m_tok = mk_ref.shape[1]
    mhead = _iota((r_exp, MEM_WIDTH), 1) // HEAD_DIM
    mqexp = jnp.where(hrow == mhead, _dot(rep, mq.astype(BF16)), 0.0).astype(BF16)
    sm = jnp.einsum("bqc,bmc->bqm", mqexp.reshape(BB, qrows, MEM_WIDTH), mk_ref[...].astype(BF16),
                    preferred_element_type=F32).reshape(r_exp, m_tok)
    mm = jnp.max(sm, axis=-1, keepdims=True)
    pm = jnp.exp(sm - mm)
    dm = jnp.sum(pm, axis=-1, keepdims=True)
    om = jnp.einsum("bqm,bmc->bqc", pm.astype(BF16).reshape(BB, qrows, m_tok), mv_ref[...].astype(BF16),
                    preferred_element_type=F32).reshape(r_exp, MEM_WIDTH)
    om = jnp.where(hrow == mhead, om / dm, 0.0)
    cross = jnp.sum(om.reshape(r_tok, REP, MEM_WIDTH), axis=1)

    u = cc * cvv
    tt = _iota(u.shape, 0) % t_dec
    u1 = jnp.where(tt >= 1, pltpu.roll(u, 1, axis=0), pm1_ref[...])
    u2 = jnp.where(tt >= 2, pltpu.roll(u, 2, axis=0), pm2_ref[...])
    conv_out = cb * (conv_w_ref[0:1, :] * u2 + conv_w_ref[1:2, :] * u1 + conv_w_ref[2:3, :] * u)
    u_ref[...] = u

    nwk_ref[:, 0:win - t_dec, :] = wk_ref[:, t_dec:win, :]
    nwv_ref[:, 0:win - t_dec, :] = wv_ref[:, t_dec:win, :]
    for b in range(BB):
        nwk_ref[b, win - t_dec:win, :] = k_new[b * t_dec:(b + 1) * t_dec, :]
        nwv_ref[b, win - t_dec:win, :] = v_new[b * t_dec:(b + 1) * t_dec, :]

    _mix_out(x, attn, conv_out, cross, g_mix_ref, w_out_ref, g_ffn_ref, w_r_ref, b_r_ref,
             h_ref, hn_ref, tope_ref, gates_ref)


def _mixer_s(x2, t_dec, pm1, pm2, wk, wv, mk, mv, sinks, g_attn, w_in, conv_w, g_mix, w_out, g_ffn, w_r, b_r):
    n, d = x2.shape
    nb = wk.shape[0]
    win = wk.shape[1]
    m_tok = mk.shape[1]
    cw = conv_w.shape[1]
    r_tok = BB * t_dec
    full = lambda a: pl.BlockSpec(a.shape, lambda i, *_: (0,) * a.ndim)
    tok = lambda w: pl.BlockSpec((r_tok, w), lambda i, *_: (i, 0))
    per_b = lambda r, c: pl.BlockSpec((BB, r, c), lambda i, *_: (i, 0, 0))
    grid_spec = pltpu.PrefetchScalarGridSpec(
        num_scalar_prefetch=1,
        grid=(nb // BB,),
        in_specs=[tok(d), tok(cw), tok(cw), per_b(win, KV_WIDTH), per_b(win, KV_WIDTH),
                  per_b(m_tok, MEM_WIDTH), per_b(m_tok, MEM_WIDTH),
                  full(g_attn), full(w_in), full(conv_w), full(g_mix), full(w_out), full(g_ffn),
                  full(w_r), full(b_r)],
        out_specs=[tok(d), tok(d), tok(TOP_K), tok(TOP_K), per_b(win, KV_WIDTH), per_b(win, KV_WIDTH), tok(cw)],
    )
    return pl.pallas_call(
        functools.partial(_mixer_s_kernel, t_dec=t_dec),
        grid_spec=grid_spec,
        out_shape=[jax.ShapeDtypeStruct((n, d), F32), jax.ShapeDtypeStruct((n, d), F32),
                   jax.ShapeDtypeStruct((n, TOP_K), jnp.int32), jax.ShapeDtypeStruct((n, TOP_K), F32),
                   jax.ShapeDtypeStruct((nb, win, KV_WIDTH), F32), jax.ShapeDtypeStruct((nb, win, KV_WIDTH), F32),
                   jax.ShapeDtypeStruct((n, cw), F32)],
        compiler_params=pltpu.CompilerParams(dimension_semantics=("arbitrary",), vmem_limit_bytes=VMEM_LIMIT),
        name="mixer_s",
    )(sinks, x2, pm1, pm2, wk, wv, mk, mv, g_attn, w_in, conv_w, g_mix, w_out, g_ffn, w_r, b_r)


def _rank_kernel(tope_ref, rank_ref, counts_ref, tri_ref, carry_ref, *, n_e):
    i = pl.program_id(0)
    tm = tope_ref.shape[0]

    @pl.when(i == 0)
    def _():
        tri_ref[...] = (_iota((tm, tm), 0) > _iota((tm, tm), 1)).astype(BF16)
        carry_ref[...] = jnp.zeros_like(carry_ref)

    te = tope_ref[...]
    col = _iota((tm, n_e), 1)
    hits = [te[:, k:k + 1] == col for k in range(TOP_K)]
    onehot = jnp.zeros((tm, n_e), F32)
    for hk in hits:
        onehot = onehot + hk.astype(F32)
    before = _dot(tri_ref[...], onehot.astype(BF16)) + carry_ref[...]
    col4 = _iota((tm, TOP_K), 1)
    rank = jnp.zeros((tm, TOP_K), F32)
    for k in range(TOP_K):
        rank = jnp.where(col4 == k, jnp.sum(jnp.where(hits[k], before, 0.0), axis=-1, keepdims=True), rank)
    rank_ref[...] = rank.astype(jnp.int32)
    carry_ref[...] += jnp.sum(onehot, axis=0, keepdims=True)
    counts_ref[...] = carry_ref[...]


def _rank(tope, tm, n_e):
    n = tope.shape[0]
    return pl.pallas_call(
        functools.partial(_rank_kernel, n_e=n_e),
        grid=(n // tm,),
        in_specs=[pl.BlockSpec((tm, TOP_K), lambda i: (i, 0))],
        out_specs=[pl.BlockSpec((tm, TOP_K), lambda i: (i, 0)), pl.BlockSpec((1, n_e), lambda i: (0, 0))],
        out_shape=[jax.ShapeDtypeStruct((n, TOP_K), jnp.int32), jax.ShapeDtypeStruct((1, n_e), F32)],
        scratch_shapes=[pltpu.VMEM((tm, tm), BF16), pltpu.VMEM((1, n_e), F32)],
        compiler_params=pltpu.CompilerParams(dimension_semantics=("arbitrary",)),
        name="rank",
    )(tope)


def _dest_kernel(tope_ref, rank_ref, start_ref, dest_ref, *, n_e):
    tm = tope_ref.shape[0]
    te = tope_ref[...]
    col = _iota((tm, n_e), 1)
    col4 = _iota((tm, TOP_K), 1)
    base = jnp.zeros((tm, TOP_K), F32)
    for k in range(TOP_K):
        sel = jnp.sum(jnp.where(te[:, k:k + 1] == col, start_ref[...], 0.0), axis=-1, keepdims=True)
        base = jnp.where(col4 == k, sel, base)
    dest_ref[...] = base.astype(jnp.int32) + rank_ref[...]


def _dest(tope, rank, pad_starts_f, tm, n_e):
    n = tope.shape[0]
    return pl.pallas_call(
        functools.partial(_dest_kernel, n_e=n_e),
        grid=(n // tm,),
        in_specs=[pl.BlockSpec((tm, TOP_K), lambda i: (i, 0)), pl.BlockSpec((tm, TOP_K), lambda i: (i, 0)),
                  pl.BlockSpec((1, n_e), lambda i: (0, 0))],
        out_specs=pl.BlockSpec((tm, TOP_K), lambda i: (i, 0)),
        out_shape=jax.ShapeDtypeStruct((n, TOP_K), jnp.int32),
        name="dest",
    )(tope, rank, pad_starts_f)


def _dispatch_kernel(zrow_ref, zok_ref, nused_ref, dest_ref, hn_ref, xs_ref, zbuf_ref, sem, zsem, *, n_blocks):
    i = pl.program_id(0)
    tm = hn_ref.shape[0]
    n_e = zrow_ref.shape[0]

    def zero_copy(row):
        return pltpu.make_async_copy(zbuf_ref, xs_ref.at[pl.ds(pl.multiple_of(row, BM), BM)], zsem)

    @pl.when(i == 0)
    def _():
        zbuf_ref[...] = jnp.zeros_like(zbuf_ref)

        def fill_expert(e, c):
            @pl.when(zok_ref[e] > 0)
            def _():
                zero_copy(zrow_ref[e]).start()
            return c

        def wait_expert(e, c):
            @pl.when(zok_ref[e] > 0)
            def _():
                zero_copy(zrow_ref[e]).wait()
            return c

        def fill_tail(b, c):
            zero_copy(b * BM).start()
            return c

        def wait_tail(b, c):
            zero_copy(b * BM).wait()
            return c

        lax.fori_loop(0, n_e, fill_expert, 0)
        lax.fori_loop(nused_ref[0], n_blocks, fill_tail, 0)
        lax.fori_loop(0, n_e, wait_expert, 0)
        lax.fori_loop(nused_ref[0], n_blocks, wait_tail, 0)

    def row_copy(r):
        return pltpu.make_async_copy(hn_ref.at[pl.ds(r // TOP_K, 1)], xs_ref.at[pl.ds(dest_ref[0, 0, r], 1)], sem)

    def issue(r, c):
        row_copy(r).start()
        return c

    def drain(r, c):
        row_copy(r).wait()
        return c

    lax.fori_loop(0, tm * TOP_K, issue, 0)
    lax.fori_loop(0, tm * TOP_K, drain, 0)


def _dispatch(hn_all, dest, zrow, zok, nused, tm, n_blocks):
    n, d = hn_all.shape
    nt = n // tm
    dest3 = dest.reshape(nt, 1, tm * TOP_K)
    grid_spec = pltpu.PrefetchScalarGridSpec(
        num_scalar_prefetch=3,
        grid=(nt,),
        in_specs=[pl.BlockSpec((1, 1, tm * TOP_K), lambda i, *_: (i, 0, 0), memory_space=pltpu.SMEM),
                  pl.BlockSpec((tm, d), lambda i, *_: (i, 0))],
        out_specs=pl.BlockSpec(memory_space=pl.ANY),
        scratch_shapes=[pltpu.VMEM((BM, d), F32), pltpu.SemaphoreType.DMA(()), pltpu.SemaphoreType.DMA(())],
    )
    return pl.pallas_call(
        functools.partial(_dispatch_kernel, n_blocks=n_blocks),
        grid_spec=grid_spec,
        out_shape=jax.ShapeDtypeStruct((n_blocks * BM, d), F32),
        compiler_params=pltpu.CompilerParams(dimension_semantics=("arbitrary",)),
        name="dispatch",
    )(zrow, zok, nused, dest3, hn_all)


def _experts_kernel(blk_e_ref, nused_ref, xs_ref, wgu_ref, bgu_ref, wd_ref, bd_ref, ys_ref, wgu_bf_ref, wd_bf_ref):
    i = pl.program_id(0)
    nused = nused_ref[0]
    d_ff2 = wgu_ref.shape[2]
    grp = 2 * LANES
    e = blk_e_ref[i]
    e_prev = blk_e_ref[jnp.maximum(i - 1, 0)]

    @pl.when((i < nused) & ((i == 0) | (e != e_prev)))
    def _():
        pr = _iota((grp, grp), 0)
        pc = _iota((grp, grp), 1)
        perm = (pr == jnp.where(pc < LANES, 2 * pc, 2 * (pc - LANES) + 1)).astype(BF16)
        for g in range(d_ff2 // grp):
            w = wgu_ref[0, :, g * grp:(g + 1) * grp].astype(BF16)
            wgu_bf_ref[:, g * grp:(g + 1) * grp] = _dot(w, perm).astype(BF16)
        wd_bf_ref[...] = wd_ref[0].astype(BF16)

    @pl.when(i < nused)
    def _():
        x = xs_ref[...].astype(BF16)
        hgu = _dot(x, wgu_bf_ref[...]) + bgu_ref[0]
        acts = []
        for g in range(d_ff2 // grp):
            gate = jnp.minimum(hgu[:, g * grp:g * grp + LANES], SWIGLU_LIMIT)
            up = jnp.clip(hgu[:, g * grp + LANES:(g + 1) * grp], -SWIGLU_LIMIT, SWIGLU_LIMIT)
            glu = gate * (1.0 / (1.0 + jnp.exp(-SWIGLU_ALPHA * gate)))
            acts.append(((up + 1.0) * glu).astype(BF16))
        act = jnp.concatenate(acts, axis=1)
        ys_ref[...] = _dot(act, wd_bf_ref[...]) + bd_ref[0]

    @pl.when(i >= nused)
    def _():
        ys_ref[...] = jnp.zeros_like(ys_ref)


def _experts(xs, blk_e, nused, w_gate_up, b_gu_perm, w_down, b_down):
    rows, d = xs.shape
    n_e, _, d_ff2 = w_gate_up.shape
    n_blocks = rows // BM
    last = lambda i, nu: jnp.minimum(i, nu[0] - 1)
    grid_spec = pltpu.PrefetchScalarGridSpec(
        num_scalar_prefetch=2,
        grid=(n_blocks,),
        in_specs=[pl.BlockSpec((BM, d), lambda i, be, nu: (i, 0)),
                  pl.BlockSpec((1, d, d_ff2), lambda i, be, nu: (be[last(i, nu)], 0, 0)),
                  pl.BlockSpec((1, 1, d_ff2), lambda i, be, nu: (be[last(i, nu)], 0, 0)),
                  pl.BlockSpec((1, d_ff2 // 2, d), lambda i, be, nu: (be[last(i, nu)], 0, 0)),
                  pl.BlockSpec((1, 1, d), lambda i, be, nu: (be[last(i, nu)], 0, 0))],
        out_specs=pl.BlockSpec((BM, d), lambda i, be, nu: (i, 0)),
        scratch_shapes=[pltpu.VMEM((d, d_ff2), BF16), pltpu.VMEM((d_ff2 // 2, d), BF16)],
    )
    return pl.pallas_call(
        _experts_kernel,
        grid_spec=grid_spec,
        out_shape=jax.ShapeDtypeStruct((rows, d), F32),
        compiler_params=pltpu.CompilerParams(dimension_semantics=("arbitrary",), vmem_limit_bytes=VMEM_LIMIT),
        name="experts",
    )(blk_e, nused, xs, w_gate_up, b_gu_perm, w_down, b_down)


def _combine_kernel(dest_ref, h_ref, gates_ref, g_ref, ys_ref, out_ref, buf_ref, sem):
    te = h_ref.shape[0]

    def row_copy(r):
        return pltpu.make_async_copy(ys_ref.at[pl.ds(dest_ref[0, 0, r], 1)],
                                     buf_ref.at[r % TOP_K, pl.ds(r // TOP_K, 1)], sem)

    def issue(r, c):
        row_copy(r).start()
        return c

    def drain(r, c):
        row_copy(r).wait()
        return c

    lax.fori_loop(0, te * TOP_K, issue, 0)
    lax.fori_loop(0, te * TOP_K, drain, 0)
    gates = gates_ref[...]
    acc = buf_ref[0] * gates[:, 0:1]
    for k in range(1, TOP_K):
        acc = acc + buf_ref[k] * gates[:, k:k + 1]
    out_ref[...] = _rms(h_ref[...] + acc) * g_ref[...]


def _combine(ys, dest, h_all, gates, g_final, te, tile0, n_tiles):
    d = h_all.shape[1]
    dest3 = dest.reshape(dest.shape[0] // te, 1, te * TOP_K)
    return pl.pallas_call(
        _combine_kernel,
        grid=(n_tiles,),
        in_specs=[pl.BlockSpec((1, 1, te * TOP_K), lambda i: (tile0 + i, 0, 0), memory_space=pltpu.SMEM),
                  pl.BlockSpec((te, d), lambda i: (tile0 + i, 0)),
                  pl.BlockSpec((te, TOP_K), lambda i: (tile0 + i, 0)),
                  pl.BlockSpec((1, d), lambda i: (0, 0)),
                  pl.BlockSpec(memory_space=pl.ANY)],
        out_specs=pl.BlockSpec((te, d), lambda i: (i, 0)),
        out_shape=jax.ShapeDtypeStruct((n_tiles * te, d), F32),
        scratch_shapes=[pltpu.VMEM((TOP_K, te, d), F32), pltpu.SemaphoreType.DMA(())],
        compiler_params=pltpu.CompilerParams(dimension_semantics=("arbitrary",), vmem_limit_bytes=VMEM_LIMIT),
        name="combine",
    )(dest3, h_all, gates, g_final, ys)


def _largest_tile(cands, *sizes):
    for c in cands:
        if all(s % c == 0 for s in sizes):
            return c
    raise ValueError(f"no tile in {cands} divides {sizes}")


def kernel(x_prompt, x_sample, mem_prompt, cache_win_k, cache_win_v, state_conv, cache_mem_k, cache_mem_v, g_attn_norm, w_in, conv_w, attn_sinks, g_mem_norm, w_mem_kv, g_mix_out, w_out, g_ffn_norm, w_router, b_router, w_gate_up, b_gate_up, w_down, b_down, g_final):
    depth = w_in.shape[0]
    assert depth == 1, "single-layer step"
    b, s, d = x_prompt.shape
    nb, t_dec, _ = x_sample.shape
    n_e = w_router.shape[2]
    d_ff2 = w_gate_up.shape[3]
    cw = conv_w.shape[2]
    win = cache_win_k.shape[2]
    m_tok = cache_mem_k.shape[2]
    assert s % TQ == 0 and nb % BB == 0 and win == WINDOW and t_dec <= SUBLANES and d_ff2 % (2 * LANES) == 0

    row = lambda a: a.reshape(1, -1)
    w_in_bf = w_in[0].astype(BF16)
    w_out_bf = w_out[0].astype(BF16)
    w_r_bf = w_router[0].astype(BF16)
    sinks = attn_sinks[0].astype(F32)
    shared = (row(g_attn_norm[0]), w_in_bf, conv_w[0], row(g_mix_out[0]), w_out_bf, row(g_ffn_norm[0]),
              w_r_bf, row(b_router[0]))

    mk_p, mv_p, mkt, mvb = _memkv(mem_prompt, row(g_mem_norm[0]), w_mem_kv[0].astype(BF16))
    h_p, hn_p, tope_p, gates_p, lastk, lastv, convst = _mixer_p(x_prompt, sinks, *shared, mkt, mvb)

    zeros = lambda r: jnp.zeros((nb, r, cw), F32)
    st = state_conv[0]
    pm1 = jnp.concatenate([st[:, 1:2], zeros(t_dec - 1)], axis=1).reshape(nb * t_dec, cw)
    pm2 = jnp.concatenate([st, zeros(t_dec - 2)], axis=1).reshape(nb * t_dec, cw)
    h_s, hn_s, tope_s, gates_s, nwk, nwv, u_s = _mixer_s(
        x_sample.reshape(nb * t_dec, d), t_dec, pm1, pm2,
        cache_win_k[0].reshape(nb, win, KV_WIDTH), cache_win_v[0].reshape(nb, win, KV_WIDTH),
        cache_mem_k[0].reshape(nb, m_tok, MEM_WIDTH), cache_mem_v[0].reshape(nb, m_tok, MEM_WIDTH),
        sinks, *shared)

    n_p, n_s = b * s, nb * t_dec
    n = n_p + n_s
    tm = _largest_tile((512, 256, 128, 64, 32, 16, 8), n_p, n_s)
    te = min(tm, 256)
    h_all = jnp.concatenate([h_p, h_s], axis=0)
    hn_all = jnp.concatenate([hn_p, hn_s], axis=0)
    tope = jnp.concatenate([tope_p, tope_s], axis=0)
    gates = jnp.concatenate([gates_p, gates_s], axis=0)

    rank, counts_f = _rank(tope, tm, n_e)
    counts = counts_f[0].astype(jnp.int32)
    padded = (counts + BM - 1) // BM * BM
    pad_ends = jnp.cumsum(padded)
    pad_starts = pad_ends - padded
    n_blocks = -(-(n * TOP_K) // BM) + n_e
    nused = (pad_ends[-1:] // BM).astype(jnp.int32)
    blk_e = jnp.minimum(jnp.searchsorted(pad_ends, jnp.arange(n_blocks, dtype=jnp.int32) * BM, side="right"),
                        n_e - 1).astype(jnp.int32)
    dest = _dest(tope, rank, pad_starts.astype(F32).reshape(1, n_e), tm, n_e)

    zrow = jnp.maximum(pad_ends - BM, 0).astype(jnp.int32)
    zok = (padded > 0).astype(jnp.int32)
    xs = _dispatch(hn_all, dest, zrow, zok, nused, tm, n_blocks)

    grp = 2 * LANES
    b_gu = b_gate_up[0].reshape(n_e, d_ff2 // grp, LANES, 2).transpose(0, 1, 3, 2).reshape(n_e, 1, d_ff2)
    ys = _experts(xs, blk_e, nused, w_gate_up[0], b_gu, w_down[0], b_down[0].reshape(n_e, 1, d))

    g_fin = row(g_final)
    y_p = _combine(ys, dest, h_all, gates, g_fin, te, 0, n_p // te)
    y_s = _combine(ys, dest, h_all, gates, g_fin, te, n_p // te, n_s // te)

    kv5 = lambda a, bsz, r, hds: a.reshape(1, bsz, r, hds, HEAD_DIM)
    return (y_p.reshape(b, s, d), y_s.reshape(nb, t_dec, d),
            kv5(lastk, b, WINDOW, N_KV_HEADS), kv5(lastv, b, WINDOW, N_KV_HEADS),
            convst.reshape(1, b, 2, cw),
            kv5(mk_p, b, m_tok, N_MEM_HEADS), kv5(mv_p, b, m_tok, N_MEM_HEADS),
            kv5(nwk, nb, win, N_KV_HEADS), kv5(nwv, nb, win, N_KV_HEADS),
            u_s.reshape(nb, t_dec, cw)[:, t_dec - 2:].reshape(1, nb, 2, cw))
```

```python
import functools

import jax
import jax.numpy as jnp
from jax import lax
from jax.experimental import pallas as pl
from jax.experimental.pallas import tpu as pltpu

F32 = jnp.float32
BF16 = jnp.bfloat16

HEAD_DIM = 64
N_Q_HEADS = 8
N_KV_HEADS = 2
WINDOW = 128
ATTN_WIDTH = N_Q_HEADS * HEAD_DIM
KV_WIDTH = N_KV_HEADS * HEAD_DIM
N_MEM_HEADS = 4
MEM_WIDTH = N_MEM_HEADS * HEAD_DIM
TOP_K = 4
SWIGLU_LIMIT = 7.0
SWIGLU_ALPHA = 1.702
EPS = 1e-5
ATTN_SCALE = HEAD_DIM ** -0.5
ALIBI_SLOPES = tuple(2.0 ** (-8.0 * (h + 1) / N_Q_HEADS) for h in range(N_Q_HEADS))

LANES = 128
SUBLANES = 8
VMEM_LIMIT = 56 * 1024 * 1024

TQ = 256
BB = 16
BM = 256
REP = 8


def _rms(x):
    return x * lax.rsqrt(jnp.mean(x * x, axis=-1, keepdims=True) + EPS)


def _dot(a, b):
    return jnp.dot(a, b, preferred_element_type=F32)


def _dot_nt(a, b):
    return lax.dot_general(a, b, (((1,), (1,)), ((), ())), preferred_element_type=F32)


def _iota(shape, axis):
    return lax.broadcasted_iota(jnp.int32, shape, axis)


def _memkv_kernel(mem_ref, g_ref, w_ref, mk_ref, mv_ref, mkt_ref, mvb_ref):
    xn = (_rms(mem_ref[0]) * g_ref[...]).astype(BF16)
    kv = _dot(xn, w_ref[...])
    mk = kv[:, :MEM_WIDTH]
    mv = kv[:, MEM_WIDTH:]
    mk_ref[0] = mk
    mv_ref[0] = mv
    mkt_ref[0] = mk.T.astype(BF16)
    mvb_ref[0] = mv.astype(BF16)


def _memkv(mem, g, w_bf):
    b, m, d = mem.shape
    out_f = jax.ShapeDtypeStruct((b, m, MEM_WIDTH), F32)
    out_b = jax.ShapeDtypeStruct((b, m, MEM_WIDTH), BF16)
    out_t = jax.ShapeDtypeStruct((b, MEM_WIDTH, m), BF16)
    blk = lambda r, c: pl.BlockSpec((1, r, c), lambda i: (i, 0, 0))
    return pl.pallas_call(
        _memkv_kernel,
        grid=(b,),
        in_specs=[blk(m, d), pl.BlockSpec((1, d), lambda i: (0, 0)),
                  pl.BlockSpec((d, 2 * MEM_WIDTH), lambda i: (0, 0))],
        out_specs=[blk(m, MEM_WIDTH), blk(m, MEM_WIDTH), blk(MEM_WIDTH, m), blk(m, MEM_WIDTH)],
        out_shape=[out_f, out_f, out_t, out_b],
        name="memkv",
    )(mem, g, w_bf)


def _router_topk(hn, w_r_ref, b_r_ref, tope_ref, gates_ref):
    n_e = w_r_ref.shape[1]
    logits = _dot(hn.astype(BF16), w_r_ref[...]) + b_r_ref[...]
    rows = logits.shape[0]
    col = _iota((rows, n_e), 1).astype(F32)
    vals, idxs = [], []
    cur = logits
    for _ in range(TOP_K):
        m = jnp.max(cur, axis=-1, keepdims=True)
        idx = jnp.min(jnp.where(cur == m, col, float(n_e)), axis=-1, keepdims=True)
        vals.append(m)
        idxs.append(idx)
        cur = jnp.where(col == idx, -jnp.inf, cur)
    exps = [jnp.exp(v - vals[0]) for v in vals]
    tot = exps[0] + exps[1] + exps[2] + exps[3]
    col4 = _iota((rows, TOP_K), 1)
    te = jnp.zeros((rows, TOP_K), F32)
    ga = jnp.zeros((rows, TOP_K), F32)
    for k in range(TOP_K):
        te = jnp.where(col4 == k, idxs[k], te)
        ga = jnp.where(col4 == k, exps[k] / tot, ga)
    tope_ref[...] = te.astype(jnp.int32)
    gates_ref[...] = ga


def _mix_out(x, attn, conv_out, cross, g_mix_ref, w_out_ref, g_ffn_ref, w_r_ref, b_r_ref,
             h_ref, hn_ref, tope_ref, gates_ref):
    mix = jnp.concatenate([_rms(attn), _rms(conv_out), _rms(cross)], axis=-1) * g_mix_ref[...]
    h = x + _dot(mix.astype(BF16), w_out_ref[...])
    hn = _rms(h) * g_ffn_ref[...]
    h_ref[...] = h
    hn_ref[...] = hn
    _router_topk(hn, w_r_ref, b_r_ref, tope_ref, gates_ref)


def _swa_block(q_blk, kk, vv, prev_lim, sinks_ref):
    blk = WINDOW
    lane = _iota((2 * blk, KV_WIDTH), 1)
    lo = lane < HEAD_DIM
    kk_r = pltpu.roll(kk, HEAD_DIM, axis=1)
    vv_r = pltpu.roll(vv, HEAD_DIM, axis=1)
    kdup = [jnp.where(lo, kk, kk_r).astype(BF16), jnp.where(lo, kk_r, kk).astype(BF16)]
    vlo = [jnp.where(lo, vv, 0.0).astype(BF16), jnp.where(lo, vv_r, 0.0).astype(BF16)]
    vhi = [jnp.where(lo, 0.0, vv_r).astype(BF16), jnp.where(lo, 0.0, vv).astype(BF16)]
    qi = _iota((blk, 2 * blk), 0)
    kj = _iota((blk, 2 * blk), 1)
    dist = blk + qi - kj
    mask = (dist >= 0) & (dist < WINDOW) & (kj >= prev_lim)
    distf = dist.astype(F32)
    qlo = _iota((blk, 2 * HEAD_DIM), 1) < HEAD_DIM
    outs = []
    for p in range(N_Q_HEADS // 2):
        kh = (2 * p) // (N_Q_HEADS // N_KV_HEADS)
        qp = q_blk[:, p * 2 * HEAD_DIM:(p + 1) * 2 * HEAD_DIM]
        acc = None
        for e in range(2):
            h = 2 * p + e
            qm = jnp.where(qlo if e == 0 else jnp.logical_not(qlo), qp, 0.0).astype(BF16)
            s = _dot_nt(qm, kdup[kh]) - ALIBI_SLOPES[h] * distf
            s = jnp.where(mask, s, -jnp.inf)
            sink = sinks_ref[h]
            m = jnp.maximum(jnp.max(s, axis=-1, keepdims=True), sink)
            pe = jnp.exp(s - m)
            denom = jnp.sum(pe, axis=-1, keepdims=True) + jnp.exp(sink - m)
            o = _dot(pe.astype(BF16), (vlo if e == 0 else vhi)[kh]) / denom
            acc = o if acc is None else acc + o
        outs.append(acc)
    return jnp.concatenate(outs, axis=1)


def _mem_attend_shared(mq, mkt, mvb):
    t = mq.shape[0]
    m_tok = mvb.shape[0]
    qhead = _iota((t, MEM_WIDTH), 1) // HEAD_DIM
    vhead = _iota((m_tok, MEM_WIDTH), 1) // HEAD_DIM
    cross = None
    for h in range(N_MEM_HEADS):
        qm = jnp.where(qhead == h, mq, 0.0).astype(BF16)
        s = _dot(qm, mkt)
        m = jnp.max(s, axis=-1, keepdims=True)
        pe = jnp.exp(s - m)
        denom = jnp.sum(pe, axis=-1, keepdims=True)
        vm = jnp.where(vhead == h, mvb, jnp.zeros_like(mvb))
        o = _dot(pe.astype(BF16), vm) / denom
        cross = o if cross is None else cross + o
    return cross


def _mixer_p_kernel(sinks_ref, x_ref, g_attn_ref, w_in_ref, conv_w_ref, g_mix_ref, w_out_ref, g_ffn_ref,
                    w_r_ref, b_r_ref, mkt_ref, mvb_ref,
                    h_ref, hn_ref, tope_ref, gates_ref, lastk_ref, lastv_ref, convst_ref,
                    ck_ref, cv_ref, cu_ref):
    j = pl.program_id(1)
    nj = pl.num_programs(1)

    @pl.when(j == 0)
    def _():
        ck_ref[...] = jnp.zeros_like(ck_ref)
        cv_ref[...] = jnp.zeros_like(cv_ref)
        cu_ref[...] = jnp.zeros_like(cu_ref)

    x = x_ref[0]
    xn = (_rms(x) * g_attn_ref[...]).astype(BF16)
    z = _dot(xn, w_in_ref[...])
    c0 = ATTN_WIDTH
    c1 = c0 + KV_WIDTH
    c2 = c1 + KV_WIDTH
    cw = conv_w_ref.shape[1]
    c3, c4, c5 = c2 + cw, c2 + 2 * cw, c2 + 3 * cw
    q = z[:, :c0] * ATTN_SCALE
    k = z[:, c0:c1]
    v = z[:, c1:c2]
    cb = z[:, c2:c3]
    cc = z[:, c3:c4]
    cvv = z[:, c4:c5]
    mq = z[:, c5:] * ATTN_SCALE

    blk = WINDOW
    attn_blocks = []
    for i in range(TQ // blk):
        if i == 0:
            pk, pv = ck_ref[...], cv_ref[...]
            prev_lim = jnp.where(j > 0, 0, blk)
        else:
            pk, pv = k[(i - 1) * blk:i * blk], v[(i - 1) * blk:i * blk]
            prev_lim = 0
        kk = jnp.concatenate([pk, k[i * blk:(i + 1) * blk]], axis=0)
        vv = jnp.concatenate([pv, v[i * blk:(i + 1) * blk]], axis=0)
        attn_blocks.append(_swa_block(q[i * blk:(i + 1) * blk], kk, vv, prev_lim, sinks_ref))
    attn = jnp.concatenate(attn_blocks, axis=0)
    ck_ref[...] = k[TQ - blk:]
    cv_ref[...] = v[TQ - blk:]

    u = cc * cvv
    row = _iota(u.shape, 0)
    u1 = jnp.where(row == 0, cu_ref[SUBLANES - 1:SUBLANES, :], pltpu.roll(u, 1, axis=0))
    u2 = jnp.where(row == 0, cu_ref[SUBLANES - 2:SUBLANES - 1, :],
                   jnp.where(row == 1, cu_ref[SUBLANES - 1:SUBLANES, :], pltpu.roll(u, 2, axis=0)))
    conv_out = cb * (conv_w_ref[0:1, :] * u2 + conv_w_ref[1:2, :] * u1 + conv_w_ref[2:3, :] * u)
    cu_ref[...] = u[TQ - SUBLANES:]

    cross = _mem_attend_shared(mq, mkt_ref[0], mvb_ref[0])

    @pl.when(j == nj - 1)
    def _():
        lastk_ref[0] = k[TQ - blk:]
        lastv_ref[0] = v[TQ - blk:]
        convst_ref[0] = u[TQ - 2:]

    _mix_out(x, attn, conv_out, cross, g_mix_ref, w_out_ref, g_ffn_ref, w_r_ref, b_r_ref,
             h_ref, hn_ref, tope_ref, gates_ref)


def _mixer_p(x, sinks, g_attn, w_in, conv_w, g_mix, w_out, g_ffn, w_r, b_r, mkt, mvb):
    b, s, d = x.shape
    nj = s // TQ
    n = b * s
    cw = conv_w.shape[1]
    full = lambda a: pl.BlockSpec(a.shape, lambda bi, ji, *_: (0,) * a.ndim)
    tok = lambda w: pl.BlockSpec((TQ, w), lambda bi, ji, *_: (bi * nj + ji, 0))
    per_b = lambda r, c: pl.BlockSpec((1, r, c), lambda bi, ji, *_: (bi, 0, 0))
    grid_spec = pltpu.PrefetchScalarGridSpec(
        num_scalar_prefetch=1,
        grid=(b, nj),
        in_specs=[pl.BlockSpec((1, TQ, d), lambda bi, ji, *_: (bi, ji, 0)),
                  full(g_attn), full(w_in), full(conv_w), full(g_mix), full(w_out), full(g_ffn),
                  full(w_r), full(b_r), per_b(MEM_WIDTH, mkt.shape[2]), per_b(mvb.shape[1], MEM_WIDTH)],
        out_specs=[tok(d), tok(d), tok(TOP_K), tok(TOP_K),
                   per_b(WINDOW, KV_WIDTH), per_b(WINDOW, KV_WIDTH), per_b(2, cw)],
        scratch_shapes=[pltpu.VMEM((WINDOW, KV_WIDTH), F32), pltpu.VMEM((WINDOW, KV_WIDTH), F32),
                        pltpu.VMEM((SUBLANES, cw), F32)],
    )
    return pl.pallas_call(
        _mixer_p_kernel,
        grid_spec=grid_spec,
        out_shape=[jax.ShapeDtypeStruct((n, d), F32), jax.ShapeDtypeStruct((n, d), F32),
                   jax.ShapeDtypeStruct((n, TOP_K), jnp.int32), jax.ShapeDtypeStruct((n, TOP_K), F32),
                   jax.ShapeDtypeStruct((b, WINDOW, KV_WIDTH), F32),
                   jax.ShapeDtypeStruct((b, WINDOW, KV_WIDTH), F32),
                   jax.ShapeDtypeStruct((b, 2, cw), F32)],
        compiler_params=pltpu.CompilerParams(dimension_semantics=("arbitrary", "arbitrary"),
                                             vmem_limit_bytes=VMEM_LIMIT),
        name="mixer_p",
    )(sinks, x, g_attn, w_in, conv_w, g_mix, w_out, g_ffn, w_r, b_r, mkt, mvb)


def _per_head_column(values, hrow):
    col = jnp.zeros(hrow.shape, F32)
    for h in range(N_Q_HEADS):
        col = jnp.where(hrow == h, values[h], col)
    return col


def _mixer_s_kernel(sinks_ref, x_ref, pm1_ref, pm2_ref, wk_ref, wv_ref, mk_ref, mv_ref,
                    g_attn_ref, w_in_ref, conv_w_ref, g_mix_ref, w_out_ref, g_ffn_ref, w_r_ref, b_r_ref,
                    h_ref, hn_ref, tope_ref, gates_ref, nwk_ref, nwv_ref, u_ref, *, t_dec):
    r_tok = BB * t_dec
    r_exp = r_tok * REP
    qrows = t_dec * REP
    x = x_ref[...]
    xn = (_rms(x) * g_attn_ref[...]).astype(BF16)
    z = _dot(xn, w_in_ref[...])
    c0 = ATTN_WIDTH
    c1 = c0 + KV_WIDTH
    c2 = c1 + KV_WIDTH
    cw = conv_w_ref.shape[1]
    c3, c4, c5 = c2 + cw, c2 + 2 * cw, c2 + 3 * cw
    q = z[:, :c0] * ATTN_SCALE
    k_new = z[:, c0:c1]
    v_new = z[:, c1:c2]
    cb = z[:, c2:c3]
    cc = z[:, c3:c4]
    cvv = z[:, c4:c5]
    mq = z[:, c5:] * ATTN_SCALE
    win = wk_ref.shape[1]

    xi = _iota((KV_WIDTH, ATTN_WIDTH), 0)
    xl = _iota((KV_WIDTH, ATTN_WIDTH), 1)
    q_per_kv = N_Q_HEADS // N_KV_HEADS
    expand = (xi == (xl // (q_per_kv * HEAD_DIM)) * HEAD_DIM + xl % HEAD_DIM).astype(BF16)
    rr = _iota((r_exp, r_tok), 0)
    rc = _iota((r_exp, r_tok), 1)
    rep = (rr // REP == rc).astype(BF16)

    hrow = _iota((r_exp, 1), 0) % REP
    trow = (_iota((r_exp, 1), 0) // REP) % t_dec
    slope_col = _per_head_column(ALIBI_SLOPES, hrow)
    sink_col = _per_head_column([sinks_ref[h] for h in range(N_Q_HEADS)], hrow)

    qexp = jnp.where(hrow == _iota((r_exp, ATTN_WIDTH), 1) // HEAD_DIM, _dot(rep, q.astype(BF16)), 0.0)
    kexp = _dot(wk_ref[...].reshape(BB * win, KV_WIDTH).astype(BF16), expand).astype(BF16)
    vexp = _dot(wv_ref[...].reshape(BB * win, KV_WIDTH).astype(BF16), expand).astype(BF16)
    s = jnp.einsum("bqc,bkc->bqk", qexp.astype(BF16).reshape(BB, qrows, ATTN_WIDTH),
                   kexp.reshape(BB, win, ATTN_WIDTH), preferred_element_type=F32).reshape(r_exp, win)
    scol = _iota((r_exp, win), 1)
    s = s - slope_col * (win + trow - scol).astype(F32)
    s = jnp.where(scol > trow, s, -jnp.inf)
    knew_exp = _dot(k_new.astype(BF16), expand).astype(BF16)
    vnew_exp = _dot(v_new.astype(BF16), expand).astype(BF16)
    s_new, v_rep = [], []
    for jn in range(t_dec):
        rep_j = (rc == (rr // qrows) * t_dec + jn).astype(BF16)
        k_rep = _dot(rep_j, knew_exp)
        v_rep.append(_dot(rep_j, vnew_exp))
        sj = jnp.sum(qexp * k_rep, axis=-1, keepdims=True) - slope_col * (trow - jn).astype(F32)
        s_new.append(jnp.where(trow >= jn, sj, -jnp.inf))
    m = jnp.maximum(jnp.max(s, axis=-1, keepdims=True), sink_col)
    for sj in s_new:
        m = jnp.maximum(m, sj)
    pe = jnp.exp(s - m)
    denom = jnp.sum(pe, axis=-1, keepdims=True) + jnp.exp(sink_col - m)
    o = jnp.einsum("bqk,bkc->bqc", pe.astype(BF16).reshape(BB, qrows, win),
                   vexp.reshape(BB, win, ATTN_WIDTH), preferred_element_type=F32).reshape(r_exp, ATTN_WIDTH)
    for jn in range(t_dec):
        pj = jnp.exp(s_new[jn] - m)
        denom = denom + pj
        o = o + pj * v_rep[jn]
    o = jnp.where(hrow == _iota((r_exp, ATTN_WIDTH), 1) // HEAD_DIM, o / denom, 0.0)
    attn = jnp.sum(o.reshape(r_tok, REP, ATTN_WIDTH), axis=1)

    m_tok = mk_ref.shape[1]
    mhead = _iota((r_exp, MEM_WIDTH), 1) // HEAD_DIM
    mqexp = jnp.where(hrow == mhead, _dot(rep, mq.astype(BF16)), 0.0).astype(BF16)
    sm = jnp.einsum("bqc,bmc->bqm", mqexp.reshape(BB, qrows, MEM_WIDTH), mk_ref[...].astype(BF16),
                    preferred_element_type=F32).reshape(r_exp, m_tok)
    mm = jnp.max(sm, axis=-1, keepdims=True)
    pm = jnp.exp(sm - mm)
    dm = jnp.sum(pm, axis=-1, keepdims=True)
    om = jnp.einsum("bqm,bmc->bqc", pm.astype(BF16).reshape(BB, qrows, m_tok), mv_ref[...].astype(BF16),
                    preferred_element_type=F32).reshape(r_exp, MEM_WIDTH)
    om = jnp.where(hrow == mhead, om / dm, 0.0)
    cross = jnp.sum(om.reshape(r_tok, REP, MEM_WIDTH), axis=1)

    u = cc * cvv
    tt = _iota(u.shape, 0) % t_dec
    u1 = jnp.where(tt >= 1, pltpu.roll(u, 1, axis=0), pm1_ref[...])
    u2 = jnp.where(tt >= 2, pltpu.roll(u, 2, axis=0), pm2_ref[...])
    conv_out = cb * (conv_w_ref[0:1, :] * u2 + conv_w_ref[1:2, :] * u1 + conv_w_ref[2:3, :] * u)
    u_ref[...] = u

    nwk_ref[:, 0:win - t_dec, :] = wk_ref[:, t_dec:win, :]
    nwv_ref[:, 0:win - t_dec, :] = wv_ref[:, t_dec:win, :]
    for b in range(BB):
        nwk_ref[b, win - t_dec:win, :] = k_new[b * t_dec:(b + 1) * t_dec, :]
        nwv_ref[b, win - t_dec:win, :] = v_new[b * t_dec:(b + 1) * t_dec, :]

    _mix_out(x, attn, conv_out, cross, g_mix_ref, w_out_ref, g_ffn_ref, w_r_ref, b_r_ref,
             h_ref, hn_ref, tope_ref, gates_ref)


def _mixer_s(x2, t_dec, pm1, pm2, wk, wv, mk, mv, sinks, g_attn, w_in, conv_w, g_mix, w_out, g_ffn, w_r, b_r):
    n, d = x2.shape
    nb = wk.shape[0]
    win = wk.shape[1]
    m_tok = mk.shape[1]
    cw = conv_w.shape[1]
    r_tok = BB * t_dec
    full = lambda a: pl.BlockSpec(a.shape, lambda i, *_: (0,) * a.ndim)
    tok = lambda w: pl.BlockSpec((r_tok, w), lambda i, *_: (i, 0))
    per_b = lambda r, c: pl.BlockSpec((BB, r, c), lambda i, *_: (i, 0, 0))
    grid_spec = pltpu.PrefetchScalarGridSpec(
        num_scalar_prefetch=1,
        grid=(nb // BB,),
        in_specs=[tok(d), tok(cw), tok(cw), per_b(win, KV_WIDTH), per_b(win, KV_WIDTH),
                  per_b(m_tok, MEM_WIDTH), per_b(m_tok, MEM_WIDTH),
                  full(g_attn), full(w_in), full(conv_w), full(g_mix), full(w_out), full(g_ffn),
                  full(w_r), full(b_r)],
        out_specs=[tok(d), tok(d), tok(TOP_K), tok(TOP_K), per_b(win, KV_WIDTH), per_b(win, KV_WIDTH), tok(cw)],
    )
    return pl.pallas_call(
        functools.partial(_mixer_s_kernel, t_dec=t_dec),
        grid_spec=grid_spec,
        out_shape=[jax.ShapeDtypeStruct((n, d), F32), jax.ShapeDtypeStruct((n, d), F32),
                   jax.ShapeDtypeStruct((n, TOP_K), jnp.int32), jax.ShapeDtypeStruct((n, TOP_K), F32),
                   jax.ShapeDtypeStruct((nb, win, KV_WIDTH), F32), jax.ShapeDtypeStruct((nb, win, KV_WIDTH), F32),
                   jax.ShapeDtypeStruct((n, cw), F32)],
        compiler_params=pltpu.CompilerParams(dimension_semantics=("arbitrary",), vmem_limit_bytes=VMEM_LIMIT),
        name="mixer_s",
    )(sinks, x2, pm1, pm2, wk, wv, mk, mv, g_attn, w_in, conv_w, g_mix, w_out, g_ffn, w_r, b_r)


def _rank_kernel(tope_ref, rank_ref, counts_ref, tri_ref, carry_ref, *, n_e):
    i = pl.program_id(0)
    tm = tope_ref.shape[0]

    @pl.when(i == 0)
    def _():
        tri_ref[...] = (_iota((tm, tm), 0) > _iota((tm, tm), 1)).astype(BF16)
        carry_ref[...] = jnp.zeros_like(carry_ref)

    te = tope_ref[...]
    col = _iota((tm, n_e), 1)
    hits = [te[:, k:k + 1] == col for k in range(TOP_K)]
    onehot = jnp.zeros((tm, n_e), F32)
    for hk in hits:
        onehot = onehot + hk.astype(F32)
    before = _dot(tri_ref[...], onehot.astype(BF16)) + carry_ref[...]
    col4 = _iota((tm, TOP_K), 1)
    rank = jnp.zeros((tm, TOP_K), F32)
    for k in range(TOP_K):
        rank = jnp.where(col4 == k, jnp.sum(jnp.where(hits[k], before, 0.0), axis=-1, keepdims=True), rank)
    rank_ref[...] = rank.astype(jnp.int32)
    carry_ref[...] += jnp.sum(onehot, axis=0, keepdims=True)
    counts_ref[...] = carry_ref[...]


def _rank(tope, tm, n_e):
    n = tope.shape[0]
    return pl.pallas_call(
        functools.partial(_rank_kernel, n_e=n_e),
        grid=(n // tm,),
        in_specs=[pl.BlockSpec((tm, TOP_K), lambda i: (i, 0))],
        out_specs=[pl.BlockSpec((tm, TOP_K), lambda i: (i, 0)), pl.BlockSpec((1, n_e), lambda i: (0, 0))],
        out_shape=[jax.ShapeDtypeStruct((n, TOP_K), jnp.int32), jax.ShapeDtypeStruct((1, n_e), F32)],
        scratch_shapes=[pltpu.VMEM((tm, tm), BF16), pltpu.VMEM((1, n_e), F32)],
        compiler_params=pltpu.CompilerParams(dimension_semantics=("arbitrary",)),
        name="rank",
    )(tope)


def _dest_kernel(tope_ref, rank_ref, start_ref, dest_ref, *, n_e):
    tm = tope_ref.shape[0]
    te = tope_ref[...]
    col = _iota((tm, n_e), 1)
    col4 = _iota((tm, TOP_K), 1)
    base = jnp.zeros((tm, TOP_K), F32)
    for k in range(TOP_K):
        sel = jnp.sum(jnp.where(te[:, k:k + 1] == col, start_ref[...], 0.0), axis=-1, keepdims=True)
        base = jnp.where(col4 == k, sel, base)
    dest_ref[...] = base.astype(jnp.int32) + rank_ref[...]


def _dest(tope, rank, pad_starts_f, tm, n_e):
    n = tope.shape[0]
    return pl.pallas_call(
        functools.partial(_dest_kernel, n_e=n_e),
        grid=(n // tm,),
        in_specs=[pl.BlockSpec((tm, TOP_K), lambda i: (i, 0)), pl.BlockSpec((tm, TOP_K), lambda i: (i, 0)),
                  pl.BlockSpec((1, n_e), lambda i: (0, 0))],
        out_specs=pl.BlockSpec((tm, TOP_K), lambda i: (i, 0)),
        out_shape=jax.ShapeDtypeStruct((n, TOP_K), jnp.int32),
        name="dest",
    )(tope, rank, pad_starts_f)


def _invmap_kernel(dest_ref, inv_ref):
    i = pl.program_id(0)
    per_tile = dest_ref.shape[2]

    @pl.when(i == 0)
    def _():
        def clear(r, c):
            inv_ref[r] = -1
            return c
        lax.fori_loop(0, inv_ref.shape[0], clear, 0, unroll=8)

    def place(r, c):
        inv_ref[dest_ref[0, 0, r]] = i * per_tile + r
        return c

    lax.fori_loop(0, per_tile, place, 0, unroll=8)


def _invmap(dest, tm, n_rows):
    nt = dest.shape[0] // tm
    return pl.pallas_call(
        _invmap_kernel,
        grid=(nt,),
        in_specs=[pl.BlockSpec((1, 1, tm * TOP_K), lambda i: (i, 0, 0), memory_space=pltpu.SMEM)],
        out_specs=pl.BlockSpec(memory_space=pltpu.SMEM),
        out_shape=jax.ShapeDtypeStruct((n_rows,), jnp.int32),
        compiler_params=pltpu.CompilerParams(dimension_semantics=("arbitrary",)),
        name="invmap",
    )(dest.reshape(nt, 1, tm * TOP_K))


def _experts_kernel(blk_e_ref, nused_ref, gcur_ref, gnext_ref, scur_ref, sprev_ref, hn_ref,
                    wgu_ref, bgu_ref, wd_ref, bd_ref, yk_ref,
                    xbuf_ref, ybuf_ref, wgu_bf_ref, wd_bf_ref, gsem, ssem):
    i = pl.program_id(0)
    nused = nused_ref[0]
    d_ff2 = wgu_ref.shape[2]
    grp = 2 * LANES

    def gather_copy(idx_ref, r, slot):
        return pltpu.make_async_copy(hn_ref.at[pl.ds(idx_ref[0, 0, r], 1)],
                                     xbuf_ref.at[slot, pl.ds(r, 1)], gsem.at[slot])

    def scatter_copy(idx_ref, r, slot):
        return pltpu.make_async_copy(ybuf_ref.at[slot, pl.ds(r, 1)],
                                     yk_ref.at[pl.ds(idx_ref[0, 0, r], 1)], ssem.at[slot])

    @pl.when(i <= nused)
    def _():
        slot = i % 2
        oslot = 1 - slot
        last = jnp.maximum(nused - 1, 0)
        e = blk_e_ref[jnp.minimum(i, last)]
        e_prev = blk_e_ref[jnp.minimum(jnp.maximum(i - 1, 0), last)]

        @pl.when(i == 0)
        def _():
            ybuf_ref[...] = jnp.zeros_like(ybuf_ref)
            for r in range(BM):
                gather_copy(gcur_ref, r, 0).start(priority=r % 2)

        @pl.when(i >= 1)
        def _():
            for r in range(BM):
                scatter_copy(sprev_ref, r, slot).wait()

        @pl.when((i == 0) | (e != e_prev))
        def _():
            pr = _iota((grp, grp), 0)
            pc = _iota((grp, grp), 1)
            perm = (pr == jnp.where(pc < LANES, 2 * pc, 2 * (pc - LANES) + 1)).astype(BF16)
            for g in range(d_ff2 // grp):
                w = wgu_ref[0, :, g * grp:(g + 1) * grp].astype(BF16)
                wgu_bf_ref[:, g * grp:(g + 1) * grp] = _dot(w, perm).astype(BF16)
            wd_bf_ref[...] = wd_ref[0].astype(BF16)

        for r in range(BM):
            gather_copy(gcur_ref, r, slot).wait()
        x = xbuf_ref[slot].astype(BF16)
        for r in range(BM):
            gather_copy(gnext_ref, r, oslot).start(priority=r % 2)
        for r in range(BM):
            scatter_copy(scur_ref, r, oslot).start(priority=r % 2)
        hgu = _dot(x, wgu_bf_ref[...]) + bgu_ref[0]
        acts = []
        for g in range(d_ff2 // grp):
            gate = jnp.minimum(hgu[:, g * grp:g * grp + LANES], SWIGLU_LIMIT)
            up = jnp.clip(hgu[:, g * grp + LANES:(g + 1) * grp], -SWIGLU_LIMIT, SWIGLU_LIMIT)
            glu = gate * (1.0 / (1.0 + jnp.exp(-SWIGLU_ALPHA * gate)))
            acts.append(((up + 1.0) * glu).astype(BF16))
        act = jnp.concatenate(acts, axis=1)
        ybuf_ref[slot] = _dot(act, wd_bf_ref[...]) + bd_ref[0]

        @pl.when(i == nused)
        def _():
            for r in range(BM):
                scatter_copy(scur_ref, r, oslot).wait()
            for r in range(BM):
                gather_copy(gnext_ref, r, oslot).wait()


def _experts(hn_all, g_tab, s_tab, blk_e, nused, w_gate_up, b_gu_perm, w_down, b_down, n_out_rows):
    d = hn_all.shape[1]
    n_e, _, d_ff2 = w_gate_up.shape
    n_tab = g_tab.shape[0]
    expert = lambda i, be, nu: be[jnp.minimum(i, jnp.maximum(nu[0] - 1, 0))]
    step = lambda i, nu: jnp.minimum(i, nu[0])
    idx_spec = lambda f: pl.BlockSpec((1, 1, BM), f, memory_space=pltpu.SMEM)
    grid_spec = pltpu.PrefetchScalarGridSpec(
        num_scalar_prefetch=2,
        grid=(n_tab - 1,),
        in_specs=[idx_spec(lambda i, be, nu: (step(i, nu), 0, 0)),
                  idx_spec(lambda i, be, nu: (step(i, nu) + 1, 0, 0)),
                  idx_spec(lambda i, be, nu: (step(i, nu), 0, 0)),
                  idx_spec(lambda i, be, nu: (jnp.maximum(step(i, nu) - 1, 0), 0, 0)),
                  pl.BlockSpec(memory_space=pl.ANY),
                  pl.BlockSpec((1, d, d_ff2), lambda i, be, nu: (expert(i, be, nu), 0, 0)),
                  pl.BlockSpec((1, 1, d_ff2), lambda i, be, nu: (expert(i, be, nu), 0, 0)),
                  pl.BlockSpec((1, d_ff2 // 2, d), lambda i, be, nu: (expert(i, be, nu), 0, 0)),
                  pl.BlockSpec((1, 1, d), lambda i, be, nu: (expert(i, be, nu), 0, 0))],
        out_specs=pl.BlockSpec(memory_space=pl.ANY),
        scratch_shapes=[pltpu.VMEM((2, BM, d), F32), pltpu.VMEM((2, BM, d), F32),
                        pltpu.VMEM((d, d_ff2), BF16), pltpu.VMEM((d_ff2 // 2, d), BF16),
                        pltpu.SemaphoreType.DMA((2,)), pltpu.SemaphoreType.DMA((2,))],
    )
    return pl.pallas_call(
        _experts_kernel,
        grid_spec=grid_spec,
        out_shape=jax.ShapeDtypeStruct((n_out_rows, d), F32),
        compiler_params=pltpu.CompilerParams(dimension_semantics=("arbitrary",), vmem_limit_bytes=VMEM_LIMIT),
        name="experts",
    )(blk_e, nused, g_tab, g_tab, s_tab, s_tab, hn_all, w_gate_up, b_gu_perm, w_down, b_down)


def _combine_kernel(h_ref, gates_ref, g_ref, yk_ref, out_ref):
    d = h_ref.shape[1]
    gates = gates_ref[...]
    acc = yk_ref[:, 0:d] * gates[:, 0:1]
    for k in range(1, TOP_K):
        acc = acc + yk_ref[:, k * d:(k + 1) * d] * gates[:, k:k + 1]
    out_ref[...] = _rms(h_ref[...] + acc) * g_ref[...]


def _combine(yk4, h, gates, g_final, te, tile0):
    n, d = h.shape
    return pl.pallas_call(
        _combine_kernel,
        grid=(n // te,),
        in_specs=[pl.BlockSpec((te, d), lambda i: (i, 0)),
                  pl.BlockSpec((te, TOP_K), lambda i: (i, 0)),
                  pl.BlockSpec((1, d), lambda i: (0, 0)),
                  pl.BlockSpec((te, TOP_K * d), lambda i: (tile0 + i, 0))],
        out_specs=pl.BlockSpec((te, d), lambda i: (i, 0)),
        out_shape=jax.ShapeDtypeStruct((n, d), F32),
        compiler_params=pltpu.CompilerParams(vmem_limit_bytes=VMEM_LIMIT),
        name="combine",
    )(h, gates, g_final, yk4)


def _largest_tile(cands, *sizes):
    for c in cands:
        if all(s % c == 0 for s in sizes):
            return c
    raise ValueError(f"no tile in {cands} divides {sizes}")


def kernel(x_prompt, x_sample, mem_prompt, cache_win_k, cache_win_v, state_conv, cache_mem_k, cache_mem_v, g_attn_norm, w_in, conv_w, attn_sinks, g_mem_norm, w_mem_kv, g_mix_out, w_out, g_ffn_norm, w_router, b_router, w_gate_up, b_gate_up, w_down, b_down, g_final):
    depth = w_in.shape[0]
    assert depth == 1, "single-layer step"
    b, s, d = x_prompt.shape
    nb, t_dec, _ = x_sample.shape
    n_e = w_router.shape[2]
    d_ff2 = w_gate_up.shape[3]
    cw = conv_w.shape[2]
    win = cache_win_k.shape[2]
    m_tok = cache_mem_k.shape[2]
    assert s % TQ == 0 and nb % BB == 0 and win == WINDOW and t_dec <= SUBLANES and d_ff2 % (2 * LANES) == 0

    row = lambda a: a.reshape(1, -1)
    w_in_bf = w_in[0].astype(BF16)
    w_out_bf = w_out[0].astype(BF16)
    w_r_bf = w_router[0].astype(BF16)
    sinks = attn_sinks[0].astype(F32)
    shared = (row(g_attn_norm[0]), w_in_bf, conv_w[0], row(g_mix_out[0]), w_out_bf, row(g_ffn_norm[0]),
              w_r_bf, row(b_router[0]))

    mk_p, mv_p, mkt, mvb = _memkv(mem_prompt, row(g_mem_norm[0]), w_mem_kv[0].astype(BF16))
    h_p, hn_p, tope_p, gates_p, lastk, lastv, convst = _mixer_p(x_prompt, sinks, *shared, mkt, mvb)

    zeros = lambda r: jnp.zeros((nb, r, cw), F32)
    st = state_conv[0]
    pm1 = jnp.concatenate([st[:, 1:2], zeros(t_dec - 1)], axis=1).reshape(nb * t_dec, cw)
    pm2 = jnp.concatenate([st, zeros(t_dec - 2)], axis=1).reshape(nb * t_dec, cw)
    h_s, hn_s, tope_s, gates_s, nwk, nwv, u_s = _mixer_s(
        x_sample.reshape(nb * t_dec, d), t_dec, pm1, pm2,
        cache_win_k[0].reshape(nb, win, KV_WIDTH), cache_win_v[0].reshape(nb, win, KV_WIDTH),
        cache_mem_k[0].reshape(nb, m_tok, MEM_WIDTH), cache_mem_v[0].reshape(nb, m_tok, MEM_WIDTH),
        sinks, *shared)

    n_p, n_s = b * s, nb * t_dec
    n = n_p + n_s
    tm = _largest_tile((512, 256, 128, 64, 32, 16, 8), n_p, n_s)
    hn_all = jnp.concatenate([hn_p, hn_s], axis=0)
    tope = jnp.concatenate([tope_p, tope_s], axis=0)

    rank, counts_f = _rank(tope, tm, n_e)
    counts = counts_f[0].astype(jnp.int32)
    padded = (counts + BM - 1) // BM * BM
    pad_ends = jnp.cumsum(padded)
    pad_starts = pad_ends - padded
    nk = n * TOP_K
    n_blocks = -(-nk // BM) + n_e
    nused = (pad_ends[-1:] // BM).astype(jnp.int32)
    blk_start = jnp.arange(n_blocks, dtype=jnp.int32) * BM
    blk_e = jnp.minimum(jnp.sum((pad_ends[None, :] <= blk_start[:, None]).astype(jnp.int32), axis=1), n_e - 1)
    dest = _dest(tope, rank, pad_starts.astype(F32).reshape(1, n_e), tm, n_e)
    inv = _invmap(dest, tm, n_blocks * BM)

    spare = nk + jnp.arange(BM, dtype=jnp.int32)
    gidx = jnp.maximum(inv, 0) // TOP_K
    sidx = jnp.where(inv >= 0, inv, jnp.tile(spare, n_blocks))
    g_tab = jnp.concatenate([gidx, jnp.zeros((2 * BM,), jnp.int32)]).reshape(n_blocks + 2, 1, BM)
    s_tab = jnp.concatenate([spare, sidx, spare]).reshape(n_blocks + 2, 1, BM)

    grp = 2 * LANES
    b_gu = b_gate_up[0].reshape(n_e, d_ff2 // grp, LANES, 2).transpose(0, 1, 3, 2).reshape(n_e, 1, d_ff2)
    yk = _experts(hn_all, g_tab, s_tab, blk_e, nused, w_gate_up[0], b_gu, w_down[0],
                  b_down[0].reshape(n_e, 1, d), nk + BM)
    yk4 = yk.reshape((nk + BM) // TOP_K, TOP_K * d)

    g_fin = row(g_final)
    y_p = _combine(yk4, h_p, gates_p, g_fin, tm, 0)
    y_s = _combine(yk4, h_s, gates_s, g_fin, tm, n_p // tm)

    kv5 = lambda a, bsz, r, hds: a.reshape(1, bsz, r, hds, HEAD_DIM)
    return (y_p.reshape(b, s, d), y_s.reshape(nb, t_dec, d),
            kv5(lastk, b, WINDOW, N_KV_HEADS), kv5(lastv, b, WINDOW, N_KV_HEADS),
            convst.reshape(1, b, 2, cw),
            kv5(mk_p, b, m_tok, N_MEM_HEADS), kv5(mv_p, b, m_tok, N_MEM_HEADS),
            kv5(nwk, nb, win, N_KV_HEADS), kv5(nwv, nb, win, N_KV_HEADS),
            u_s.reshape(nb, t_dec, cw)[:, t_dec - 2:].reshape(1, nb, 2, cw))
```

```python
import functools

import jax
import jax.numpy as jnp
from jax import lax
from jax.experimental import pallas as pl
from jax.experimental.pallas import tpu as pltpu

F32 = jnp.float32
BF16 = jnp.bfloat16

HEAD_DIM = 64
N_Q_HEADS = 8
N_KV_HEADS = 2
WINDOW = 128
ATTN_WIDTH = N_Q_HEADS * HEAD_DIM
KV_WIDTH = N_KV_HEADS * HEAD_DIM
N_MEM_HEADS = 4
MEM_WIDTH = N_MEM_HEADS * HEAD_DIM
TOP_K = 4
SWIGLU_LIMIT = 7.0
SWIGLU_ALPHA = 1.702
EPS = 1e-5
ATTN_SCALE = HEAD_DIM ** -0.5
ALIBI_SLOPES = tuple(2.0 ** (-8.0 * (h + 1) / N_Q_HEADS) for h in range(N_Q_HEADS))

LANES = 128
SUBLANES = 8
VMEM_LIMIT = 56 * 1024 * 1024

TQ = 256
BB = 16
BM = 256
REP = 8


def _rms(x):
    return x * lax.rsqrt(jnp.mean(x * x, axis=-1, keepdims=True) + EPS)


def _dot(a, b):
    return jnp.dot(a, b, preferred_element_type=F32)


def _dot_nt(a, b):
    return lax.dot_general(a, b, (((1,), (1,)), ((), ())), preferred_element_type=F32)


def _iota(shape, axis):
    return lax.broadcasted_iota(jnp.int32, shape, axis)


def _store_token_tiles(ref, x, base=0):
    t = x.shape[0]
    for s in range(x.shape[1] // LANES):
        ref[pl.ds(base + s, t, stride=SUBLANES), :] = x[:, s * LANES:(s + 1) * LANES]


def _load_token_tiles(ref, t, s, base=0):
    return ref[pl.ds(base + s, t, stride=SUBLANES), :]


def _memkv_kernel(mem_ref, g_ref, w_ref, mk_ref, mv_ref, mkt_ref, mvb_ref):
    xn = (_rms(mem_ref[0]) * g_ref[...]).astype(BF16)
    kv = _dot(xn, w_ref[...])
    mk = kv[:, :MEM_WIDTH]
    mv = kv[:, MEM_WIDTH:]
    mk_ref[0] = mk
    mv_ref[0] = mv
    mkt_ref[0] = mk.T.astype(BF16)
    mvb_ref[0] = mv.astype(BF16)


def _memkv(mem, g, w_bf):
    b, m, d = mem.shape
    out_f = jax.ShapeDtypeStruct((b, m, MEM_WIDTH), F32)
    out_b = jax.ShapeDtypeStruct((b, m, MEM_WIDTH), BF16)
    out_t = jax.ShapeDtypeStruct((b, MEM_WIDTH, m), BF16)
    blk = lambda r, c: pl.BlockSpec((1, r, c), lambda i: (i, 0, 0))
    return pl.pallas_call(
        _memkv_kernel,
        grid=(b,),
        in_specs=[blk(m, d), pl.BlockSpec((1, d), lambda i: (0, 0)),
                  pl.BlockSpec((d, 2 * MEM_WIDTH), lambda i: (0, 0))],
        out_specs=[blk(m, MEM_WIDTH), blk(m, MEM_WIDTH), blk(MEM_WIDTH, m), blk(m, MEM_WIDTH)],
        out_shape=[out_f, out_f, out_t, out_b],
        name="memkv",
    )(mem, g, w_bf)


def _router_topk(hn, w_r_ref, b_r_ref, tope_ref, gates_ref):
    n_e = w_r_ref.shape[1]
    logits = _dot(hn.astype(BF16), w_r_ref[...]) + b_r_ref[...]
    rows = logits.shape[0]
    col = _iota((rows, n_e), 1).astype(F32)
    vals, idxs = [], []
    cur = logits
    for _ in range(TOP_K):
        m = jnp.max(cur, axis=-1, keepdims=True)
        idx = jnp.min(jnp.where(cur == m, col, float(n_e)), axis=-1, keepdims=True)
        vals.append(m)
        idxs.append(idx)
        cur = jnp.where(col == idx, -jnp.inf, cur)
    exps = [jnp.exp(v - vals[0]) for v in vals]
    tot = exps[0] + exps[1] + exps[2] + exps[3]
    col4 = _iota((rows, TOP_K), 1)
    te = jnp.zeros((rows, TOP_K), F32)
    ga = jnp.zeros((rows, TOP_K), F32)
    for k in range(TOP_K):
        te = jnp.where(col4 == k, idxs[k], te)
        ga = jnp.where(col4 == k, exps[k] / tot, ga)
    tope_ref[...] = te.astype(jnp.int32)
    gates_ref[...] = ga


def _mix_out(x, attn, conv_out, cross, g_mix_ref, w_out_ref, g_ffn_ref, w_r_ref, b_r_ref,
             h_ref, hn_ref, tope_ref, gates_ref):
    mix = jnp.concatenate([_rms(attn), _rms(conv_out), _rms(cross)], axis=-1) * g_mix_ref[...]
    h = x + _dot(mix.astype(BF16), w_out_ref[...])
    hn = _rms(h) * g_ffn_ref[...]
    h_ref[...] = h
    _store_token_tiles(hn_ref, hn)
    _router_topk(hn, w_r_ref, b_r_ref, tope_ref, gates_ref)


def _swa_block(q_blk, kk, vv, prev_lim, sinks_ref):
    blk = WINDOW
    lane = _iota((2 * blk, KV_WIDTH), 1)
    lo = lane < HEAD_DIM
    kk_r = pltpu.roll(kk, HEAD_DIM, axis=1)
    vv_r = pltpu.roll(vv, HEAD_DIM, axis=1)
    kdup = [jnp.where(lo, kk, kk_r).astype(BF16), jnp.where(lo, kk_r, kk).astype(BF16)]
    vlo = [jnp.where(lo, vv, 0.0).astype(BF16), jnp.where(lo, vv_r, 0.0).astype(BF16)]
    vhi = [jnp.where(lo, 0.0, vv_r).astype(BF16), jnp.where(lo, 0.0, vv).astype(BF16)]
    qi = _iota((blk, 2 * blk), 0)
    kj = _iota((blk, 2 * blk), 1)
    dist = blk + qi - kj
    mask = (dist >= 0) & (dist < WINDOW) & (kj >= prev_lim)
    distf = dist.astype(F32)
    qlo = _iota((blk, 2 * HEAD_DIM), 1) < HEAD_DIM
    outs = []
    for p in range(N_Q_HEADS // 2):
        kh = (2 * p) // (N_Q_HEADS // N_KV_HEADS)
        qp = q_blk[:, p * 2 * HEAD_DIM:(p + 1) * 2 * HEAD_DIM]
        acc = None
        for e in range(2):
            h = 2 * p + e
            qm = jnp.where(qlo if e == 0 else jnp.logical_not(qlo), qp, 0.0).astype(BF16)
            s = _dot_nt(qm, kdup[kh]) - ALIBI_SLOPES[h] * distf
            s = jnp.where(mask, s, -jnp.inf)
            sink = sinks_ref[h]
            m = jnp.maximum(jnp.max(s, axis=-1, keepdims=True), sink)
            pe = jnp.exp(s - m)
            denom = jnp.sum(pe, axis=-1, keepdims=True) + jnp.exp(sink - m)
            o = _dot(pe.astype(BF16), (vlo if e == 0 else vhi)[kh]) / denom
            acc = o if acc is None else acc + o
        outs.append(acc)
    return jnp.concatenate(outs, axis=1)


def _mem_attend_shared(mq, mkt, mvb):
    t = mq.shape[0]
    m_tok = mvb.shape[0]
    qhead = _iota((t, MEM_WIDTH), 1) // HEAD_DIM
    vhead = _iota((m_tok, MEM_WIDTH), 1) // HEAD_DIM
    cross = None
    for h in range(N_MEM_HEADS):
        qm = jnp.where(qhead == h, mq, 0.0).astype(BF16)
        s = _dot(qm, mkt)
        m = jnp.max(s, axis=-1, keepdims=True)
        pe = jnp.exp(s - m)
        denom = jnp.sum(pe, axis=-1, keepdims=True)
        vm = jnp.where(vhead == h, mvb, jnp.zeros_like(mvb))
        o = _dot(pe.astype(BF16), vm) / denom
        cross = o if cross is None else cross + o
    return cross


def _mixer_p_kernel(sinks_ref, x_ref, g_attn_ref, w_in_ref, conv_w_ref, g_mix_ref, w_out_ref, g_ffn_ref,
                    w_r_ref, b_r_ref, mkt_ref, mvb_ref,
                    h_ref, hn_ref, tope_ref, gates_ref, lastk_ref, lastv_ref, convst_ref,
                    ck_ref, cv_ref, cu_ref):
    j = pl.program_id(1)
    nj = pl.num_programs(1)

    @pl.when(j == 0)
    def _():
        ck_ref[...] = jnp.zeros_like(ck_ref)
        cv_ref[...] = jnp.zeros_like(cv_ref)
        cu_ref[...] = jnp.zeros_like(cu_ref)

    x = x_ref[0]
    xn = (_rms(x) * g_attn_ref[...]).astype(BF16)
    z = _dot(xn, w_in_ref[...])
    c0 = ATTN_WIDTH
    c1 = c0 + KV_WIDTH
    c2 = c1 + KV_WIDTH
    cw = conv_w_ref.shape[1]
    c3, c4, c5 = c2 + cw, c2 + 2 * cw, c2 + 3 * cw
    q = z[:, :c0] * ATTN_SCALE
    k = z[:, c0:c1]
    v = z[:, c1:c2]
    cb = z[:, c2:c3]
    cc = z[:, c3:c4]
    cvv = z[:, c4:c5]
    mq = z[:, c5:] * ATTN_SCALE

    blk = WINDOW
    attn_blocks = []
    for i in range(TQ // blk):
        if i == 0:
            pk, pv = ck_ref[...], cv_ref[...]
            prev_lim = jnp.where(j > 0, 0, blk)
        else:
            pk, pv = k[(i - 1) * blk:i * blk], v[(i - 1) * blk:i * blk]
            prev_lim = 0
        kk = jnp.concatenate([pk, k[i * blk:(i + 1) * blk]], axis=0)
        vv = jnp.concatenate([pv, v[i * blk:(i + 1) * blk]], axis=0)
        attn_blocks.append(_swa_block(q[i * blk:(i + 1) * blk], kk, vv, prev_lim, sinks_ref))
    attn = jnp.concatenate(attn_blocks, axis=0)
    ck_ref[...] = k[TQ - blk:]
    cv_ref[...] = v[TQ - blk:]

    u = cc * cvv
    row = _iota(u.shape, 0)
    u1 = jnp.where(row == 0, cu_ref[SUBLANES - 1:SUBLANES, :], pltpu.roll(u, 1, axis=0))
    u2 = jnp.where(row == 0, cu_ref[SUBLANES - 2:SUBLANES - 1, :],
                   jnp.where(row == 1, cu_ref[SUBLANES - 1:SUBLANES, :], pltpu.roll(u, 2, axis=0)))
    conv_out = cb * (conv_w_ref[0:1, :] * u2 + conv_w_ref[1:2, :] * u1 + conv_w_ref[2:3, :] * u)
    cu_ref[...] = u[TQ - SUBLANES:]

    cross = _mem_attend_shared(mq, mkt_ref[0], mvb_ref[0])

    @pl.when(j == nj - 1)
    def _():
        lastk_ref[0] = k[TQ - blk:]
        lastv_ref[0] = v[TQ - blk:]
        convst_ref[0] = u[TQ - 2:]

    _mix_out(x, attn, conv_out, cross, g_mix_ref, w_out_ref, g_ffn_ref, w_r_ref, b_r_ref,
             h_ref, hn_ref, tope_ref, gates_ref)


def _mixer_p(x, sinks, g_attn, w_in, conv_w, g_mix, w_out, g_ffn, w_r, b_r, mkt, mvb):
    b, s, d = x.shape
    nj = s // TQ
    n = b * s
    cw = conv_w.shape[1]
    full = lambda a: pl.BlockSpec(a.shape, lambda bi, ji, *_: (0,) * a.ndim)
    tok = lambda w: pl.BlockSpec((TQ, w), lambda bi, ji, *_: (bi * nj + ji, 0))
    per_b = lambda r, c: pl.BlockSpec((1, r, c), lambda bi, ji, *_: (bi, 0, 0))
    grid_spec = pltpu.PrefetchScalarGridSpec(
        num_scalar_prefetch=1,
        grid=(b, nj),
        in_specs=[pl.BlockSpec((1, TQ, d), lambda bi, ji, *_: (bi, ji, 0)),
                  full(g_attn), full(w_in), full(conv_w), full(g_mix), full(w_out), full(g_ffn),
                  full(w_r), full(b_r), per_b(MEM_WIDTH, mkt.shape[2]), per_b(mvb.shape[1], MEM_WIDTH)],
        out_specs=[tok(d), pl.BlockSpec((TQ * SUBLANES, LANES), lambda bi, ji, *_: (bi * nj + ji, 0)),
                   tok(TOP_K), tok(TOP_K),
                   per_b(WINDOW, KV_WIDTH), per_b(WINDOW, KV_WIDTH), per_b(2, cw)],
        scratch_shapes=[pltpu.VMEM((WINDOW, KV_WIDTH), F32), pltpu.VMEM((WINDOW, KV_WIDTH), F32),
                        pltpu.VMEM((SUBLANES, cw), F32)],
    )
    return pl.pallas_call(
        _mixer_p_kernel,
        grid_spec=grid_spec,
        out_shape=[jax.ShapeDtypeStruct((n, d), F32), jax.ShapeDtypeStruct((n * SUBLANES, LANES), F32),
                   jax.ShapeDtypeStruct((n, TOP_K), jnp.int32), jax.ShapeDtypeStruct((n, TOP_K), F32),
                   jax.ShapeDtypeStruct((b, WINDOW, KV_WIDTH), F32),
                   jax.ShapeDtypeStruct((b, WINDOW, KV_WIDTH), F32),
                   jax.ShapeDtypeStruct((b, 2, cw), F32)],
        compiler_params=pltpu.CompilerParams(dimension_semantics=("arbitrary", "arbitrary"),
                                             vmem_limit_bytes=VMEM_LIMIT),
        name="mixer_p",
    )(sinks, x, g_attn, w_in, conv_w, g_mix, w_out, g_ffn, w_r, b_r, mkt, mvb)


def _per_head_column(values, hrow):
    col = jnp.zeros(hrow.shape, F32)
    for h in range(N_Q_HEADS):
        col = jnp.where(hrow == h, values[h], col)
    return col


def _mixer_s_kernel(sinks_ref, x_ref, pm1_ref, pm2_ref, wk_ref, wv_ref, mk_ref, mv_ref,
                    g_attn_ref, w_in_ref, conv_w_ref, g_mix_ref, w_out_ref, g_ffn_ref, w_r_ref, b_r_ref,
                    h_ref, hn_ref, tope_ref, gates_ref, nwk_ref, nwv_ref, u_ref, *, t_dec):
    r_tok = BB * t_dec
    r_exp = r_tok * REP
    qrows = t_dec * REP
    x = x_ref[...]
    xn = (_rms(x) * g_attn_ref[...]).astype(BF16)
    z = _dot(xn, w_in_ref[...])
    c0 = ATTN_WIDTH
    c1 = c0 + KV_WIDTH
    c2 = c1 + KV_WIDTH
    cw = conv_w_ref.shape[1]
    c3, c4, c5 = c2 + cw, c2 + 2 * cw, c2 + 3 * cw
    q = z[:, :c0] * ATTN_SCALE
    k_new = z[:, c0:c1]
    v_new = z[:, c1:c2]
    cb = z[:, c2:c3]
    cc = z[:, c3:c4]
    cvv = z[:, c4:c5]
    mq = z[:, c5:] * ATTN_SCALE
    win = wk_ref.shape[1]

    xi = _iota((KV_WIDTH, ATTN_WIDTH), 0)
    xl = _iota((KV_WIDTH, ATTN_WIDTH), 1)
    q_per_kv = N_Q_HEADS // N_KV_HEADS
    expand = (xi == (xl // (q_per_kv * HEAD_DIM)) * HEAD_DIM + xl % HEAD_DIM).astype(BF16)
    rr = _iota((r_exp, r_tok), 0)
    rc = _iota((r_exp, r_tok), 1)
    rep = (rr // REP == rc).astype(BF16)

    hrow = _iota((r_exp, 1), 0) % REP
    trow = (_iota((r_exp, 1), 0) // REP) % t_dec
    slope_col = _per_head_column(ALIBI_SLOPES, hrow)
    sink_col = _per_head_column([sinks_ref[h] for h in range(N_Q_HEADS)], hrow)

    qexp = jnp.where(hrow == _iota((r_exp, ATTN_WIDTH), 1) // HEAD_DIM, _dot(rep, q.astype(BF16)), 0.0)
    kexp = _dot(wk_ref[...].reshape(BB * win, KV_WIDTH).astype(BF16), expand).astype(BF16)
    vexp = _dot(wv_ref[...].reshape(BB * win, KV_WIDTH).astype(BF16), expand).astype(BF16)
    s = jnp.einsum("bqc,bkc->bqk", qexp.astype(BF16).reshape(BB, qrows, ATTN_WIDTH),
                   kexp.reshape(BB, win, ATTN_WIDTH), preferred_element_type=F32).reshape(r_exp, win)
    scol = _iota((r_exp, win), 1)
    s = s - slope_col * (win + trow - scol).astype(F32)
    s = jnp.where(scol > trow, s, -jnp.inf)
    knew_exp = _dot(k_new.astype(BF16), expand).astype(BF16)
    vnew_exp = _dot(v_new.astype(BF16), expand).astype(BF16)
    s_new, v_rep = [], []
    for jn in range(t_dec):
        rep_j = (rc == (rr // qrows) * t_dec + jn).astype(BF16)
        k_rep = _dot(rep_j, knew_exp)
        v_rep.append(_dot(rep_j, vnew_exp))
        sj = jnp.sum(qexp * k_rep, axis=-1, keepdims=True) - slope_col * (trow - jn).astype(F32)
        s_new.append(jnp.where(trow >= jn, sj, -jnp.inf))
    m = jnp.maximum(jnp.max(s, axis=-1, keepdims=True), sink_col)
    for sj in s_new:
        m = jnp.maximum(m, sj)
    pe = jnp.exp(s - m)
    denom = jnp.sum(pe, axis=-1, keepdims=True) + jnp.exp(sink_col - m)
    o = jnp.einsum("bqk,bkc->bqc", pe.astype(BF16).reshape(BB, qrows, win),
                   vexp.reshape(BB, win, ATTN_WIDTH), preferred_element_type=F32).reshape(r_exp, ATTN_WIDTH)
    for jn in range(t_dec):
        pj = jnp.exp(s_new[jn] - m)
        denom = denom + pj
        o = o + pj * v_rep[jn]
    o = jnp.where(hrow == _iota((r_exp, ATTN_WIDTH), 1) // HEAD_DIM, o / denom, 0.0)
    attn = jnp.sum(o.reshape(r_tok, REP, ATTN_WIDTH), axis=1)

    m_tok = mk_ref.shape[1]
    mhead = _iota((r_exp, MEM_WIDTH), 1) // HEAD_DIM
    mqexp = jnp.where(hrow == mhead, _dot(rep, mq.astype(BF16)), 0.0).astype(BF16)
    sm = jnp.einsum("bqc,bmc->bqm", mqexp.reshape(BB, qrows, MEM_WIDTH), mk_ref[...].astype(BF16),
                    preferred_element_type=F32).reshape(r_exp, m_tok)
    mm = jnp.max(sm, axis=-1, keepdims=True)
    pm = jnp.exp(sm - mm)
    dm = jnp.sum(pm, axis=-1, keepdims=True)
    om = jnp.einsum("bqm,bmc->bqc", pm.astype(BF16).reshape(BB, qrows, m_tok), mv_ref[...].astype(BF16),
                    preferred_element_type=F32).reshape(r_exp, MEM_WIDTH)
    om = jnp.where(hrow == mhead, om / dm, 0.0)
    cross = jnp.sum(om.reshape(r_tok, REP, MEM_WIDTH), axis=1)

    u = cc * cvv
    tt = _iota(u.shape, 0) % t_dec
    u1 = jnp.where(tt >= 1, pltpu.roll(u, 1, axis=0), pm1_ref[...])
    u2 = jnp.where(tt >= 2, pltpu.roll(u, 2, axis=0), pm2_ref[...])
    conv_out = cb * (conv_w_ref[0:1, :] * u2 + conv_w_ref[1:2, :] * u1 + conv_w_ref[2:3, :] * u)
    u_ref[...] = u

    nwk_ref[:, 0:win - t_dec, :] = wk_ref[:, t_dec:win, :]
    nwv_ref[:, 0:win - t_dec, :] = wv_ref[:, t_dec:win, :]
    for b in range(BB):
        nwk_ref[b, win - t_dec:win, :] = k_new[b * t_dec:(b + 1) * t_dec, :]
        nwv_ref[b, win - t_dec:win, :] = v_new[b * t_dec:(b + 1) * t_dec, :]

    _mix_out(x, attn, conv_out, cross, g_mix_ref, w_out_ref, g_ffn_ref, w_r_ref, b_r_ref,
             h_ref, hn_ref, tope_ref, gates_ref)


def _mixer_s(x2, t_dec, pm1, pm2, wk, wv, mk, mv, sinks, g_attn, w_in, conv_w, g_mix, w_out, g_ffn, w_r, b_r):
    n, d = x2.shape
    nb = wk.shape[0]
    win = wk.shape[1]
    m_tok = mk.shape[1]
    cw = conv_w.shape[1]
    r_tok = BB * t_dec
    full = lambda a: pl.BlockSpec(a.shape, lambda i, *_: (0,) * a.ndim)
    tok = lambda w: pl.BlockSpec((r_tok, w), lambda i, *_: (i, 0))
    per_b = lambda r, c: pl.BlockSpec((BB, r, c), lambda i, *_: (i, 0, 0))
    grid_spec = pltpu.PrefetchScalarGridSpec(
        num_scalar_prefetch=1,
        grid=(nb // BB,),
        in_specs=[tok(d), tok(cw), tok(cw), per_b(win, KV_WIDTH), per_b(win, KV_WIDTH),
                  per_b(m_tok, MEM_WIDTH), per_b(m_tok, MEM_WIDTH),
                  full(g_attn), full(w_in), full(conv_w), full(g_mix), full(w_out), full(g_ffn),
                  full(w_r), full(b_r)],
        out_specs=[tok(d), pl.BlockSpec((r_tok * SUBLANES, LANES), lambda i, *_: (i, 0)),
                   tok(TOP_K), tok(TOP_K), per_b(win, KV_WIDTH), per_b(win, KV_WIDTH), tok(cw)],
    )
    return pl.pallas_call(
        functools.partial(_mixer_s_kernel, t_dec=t_dec),
        grid_spec=grid_spec,
        out_shape=[jax.ShapeDtypeStruct((n, d), F32), jax.ShapeDtypeStruct((n * SUBLANES, LANES), F32),
                   jax.ShapeDtypeStruct((n, TOP_K), jnp.int32), jax.ShapeDtypeStruct((n, TOP_K), F32),
                   jax.ShapeDtypeStruct((nb, win, KV_WIDTH), F32), jax.ShapeDtypeStruct((nb, win, KV_WIDTH), F32),
                   jax.ShapeDtypeStruct((n, cw), F32)],
        compiler_params=pltpu.CompilerParams(dimension_semantics=("arbitrary",), vmem_limit_bytes=VMEM_LIMIT),
        name="mixer_s",
    )(sinks, x2, pm1, pm2, wk, wv, mk, mv, g_attn, w_in, conv_w, g_mix, w_out, g_ffn, w_r, b_r)


def _rank_kernel(tope_ref, rank_ref, counts_ref, tri_ref, carry_ref, *, n_e):
    i = pl.program_id(0)
    tm = tope_ref.shape[0]

    @pl.when(i == 0)
    def _():
        tri_ref[...] = (_iota((tm, tm), 0) > _iota((tm, tm), 1)).astype(BF16)
        carry_ref[...] = jnp.zeros_like(carry_ref)

    te = tope_ref[...]
    col = _iota((tm, n_e), 1)
    hits = [te[:, k:k + 1] == col for k in range(TOP_K)]
    onehot = jnp.zeros((tm, n_e), F32)
    for hk in hits:
        onehot = onehot + hk.astype(F32)
    before = _dot(tri_ref[...], onehot.astype(BF16)) + carry_ref[...]
    col4 = _iota((tm, TOP_K), 1)
    rank = jnp.zeros((tm, TOP_K), F32)
    for k in range(TOP_K):
        rank = jnp.where(col4 == k, jnp.sum(jnp.where(hits[k], before, 0.0), axis=-1, keepdims=True), rank)
    rank_ref[...] = rank.astype(jnp.int32)
    carry_ref[...] += jnp.sum(onehot, axis=0, keepdims=True)
    counts_ref[...] = carry_ref[...]


def _rank(tope, tm, n_e):
    n = tope.shape[0]
    return pl.pallas_call(
        functools.partial(_rank_kernel, n_e=n_e),
        grid=(n // tm,),
        in_specs=[pl.BlockSpec((tm, TOP_K), lambda i: (i, 0))],
        out_specs=[pl.BlockSpec((tm, TOP_K), lambda i: (i, 0)), pl.BlockSpec((1, n_e), lambda i: (0, 0))],
        out_shape=[jax.ShapeDtypeStruct((n, TOP_K), jnp.int32), jax.ShapeDtypeStruct((1, n_e), F32)],
        scratch_shapes=[pltpu.VMEM((tm, tm), BF16), pltpu.VMEM((1, n_e), F32)],
        compiler_params=pltpu.CompilerParams(dimension_semantics=("arbitrary",)),
        name="rank",
    )(tope)


def _dest_kernel(tope_ref, rank_ref, start_ref, dest_ref, *, n_e):
    tm = tope_ref.shape[0]
    te = tope_ref[...]
    col = _iota((tm, n_e), 1)
    col4 = _iota((tm, TOP_K), 1)
    base = jnp.zeros((tm, TOP_K), F32)
    for k in range(TOP_K):
        sel = jnp.sum(jnp.where(te[:, k:k + 1] == col, start_ref[...], 0.0), axis=-1, keepdims=True)
        base = jnp.where(col4 == k, sel, base)
    dest_ref[...] = base.astype(jnp.int32) + rank_ref[...]


def _dest(tope, rank, pad_starts_f, tm, n_e):
    n = tope.shape[0]
    return pl.pallas_call(
        functools.partial(_dest_kernel, n_e=n_e),
        grid=(n // tm,),
        in_specs=[pl.BlockSpec((tm, TOP_K), lambda i: (i, 0)), pl.BlockSpec((tm, TOP_K), lambda i: (i, 0)),
                  pl.BlockSpec((1, n_e), lambda i: (0, 0))],
        out_specs=pl.BlockSpec((tm, TOP_K), lambda i: (i, 0)),
        out_shape=jax.ShapeDtypeStruct((n, TOP_K), jnp.int32),
        name="dest",
    )(tope, rank, pad_starts_f)


def _invmap_kernel(lastblk_ref, nused_ref, dest_ref, inv_ref, *, n_blocks):
    i = pl.program_id(0)
    per_tile = dest_ref.shape[2]

    @pl.when(i == 0)
    def _():
        def clear_block(b, c):
            def clear(r, c2):
                inv_ref[b * BM + r] = -1
                return c2
            return lax.fori_loop(0, BM, clear, c, unroll=8)

        def clear_expert(e, c):
            @pl.when(lastblk_ref[e] >= 0)
            def _():
                clear_block(lastblk_ref[e], 0)
            return c

        lax.fori_loop(0, lastblk_ref.shape[0], clear_expert, 0)
        lax.fori_loop(nused_ref[0], n_blocks, clear_block, 0)

    def place(r, c):
        inv_ref[dest_ref[0, 0, r]] = i * per_tile + r
        return c

    lax.fori_loop(0, per_tile, place, 0, unroll=8)


def _invmap(dest, lastblk, nused, tm, n_blocks):
    nt = dest.shape[0] // tm
    grid_spec = pltpu.PrefetchScalarGridSpec(
        num_scalar_prefetch=2,
        grid=(nt,),
        in_specs=[pl.BlockSpec((1, 1, tm * TOP_K), lambda i, *_: (i, 0, 0), memory_space=pltpu.SMEM)],
        out_specs=pl.BlockSpec(memory_space=pltpu.SMEM),
    )
    return pl.pallas_call(
        functools.partial(_invmap_kernel, n_blocks=n_blocks),
        grid_spec=grid_spec,
        out_shape=jax.ShapeDtypeStruct((n_blocks * BM,), jnp.int32),
        compiler_params=pltpu.CompilerParams(dimension_semantics=("arbitrary",)),
        name="invmap",
    )(lastblk, nused, dest.reshape(nt, 1, tm * TOP_K))


def _experts_kernel(blk_e_ref, nused_ref, gcur_ref, gnext_ref, scur_ref, sprev_ref, hn_ref,
                    wgu_ref, bgu_ref, wd_ref, bd_ref, yk_ref,
                    xbuf_ref, ybuf_ref, wgu_bf_ref, wd_bf_ref, gsem, ssem):
    i = pl.program_id(0)
    nused = nused_ref[0]
    d_ff2 = wgu_ref.shape[2]
    grp = 2 * LANES

    def tile_at(ref, first_row):
        return ref.at[pl.ds(pl.multiple_of(first_row, SUBLANES), SUBLANES)]

    def gather_copy(idx_ref, r, slot):
        return pltpu.make_async_copy(tile_at(hn_ref, idx_ref[0, 0, r]),
                                     xbuf_ref.at[slot, pl.ds(r * SUBLANES, SUBLANES)], gsem.at[slot])

    def scatter_copy(idx_ref, r, slot):
        return pltpu.make_async_copy(ybuf_ref.at[slot, pl.ds(r * SUBLANES, SUBLANES)],
                                     tile_at(yk_ref, idx_ref[0, 0, r]), ssem.at[slot])

    @pl.when(i <= nused)
    def _():
        slot = i % 2
        oslot = 1 - slot
        last = jnp.maximum(nused - 1, 0)
        e = blk_e_ref[jnp.minimum(i, last)]
        e_prev = blk_e_ref[jnp.minimum(jnp.maximum(i - 1, 0), last)]

        @pl.when(i == 0)
        def _():
            ybuf_ref[...] = jnp.zeros_like(ybuf_ref)
            for r in range(BM):
                gather_copy(gcur_ref, r, 0).start(priority=r % 2)

        @pl.when(i >= 1)
        def _():
            for r in range(BM):
                scatter_copy(sprev_ref, r, slot).wait()

        @pl.when((i == 0) | (e != e_prev))
        def _():
            pr = _iota((grp, grp), 0)
            pc = _iota((grp, grp), 1)
            perm = (pr == jnp.where(pc < LANES, 2 * pc, 2 * (pc - LANES) + 1)).astype(BF16)
            for g in range(d_ff2 // grp):
                w = wgu_ref[0, :, g * grp:(g + 1) * grp].astype(BF16)
                wgu_bf_ref[:, g * grp:(g + 1) * grp] = _dot(w, perm).astype(BF16)
            wd_bf_ref[...] = wd_ref[0].astype(BF16)

        for r in range(BM):
            gather_copy(gcur_ref, r, slot).wait()
        x = jnp.concatenate([_load_token_tiles(xbuf_ref.at[slot], BM, s).astype(BF16)
                             for s in range(SUBLANES)], axis=1)
        for r in range(BM):
            gather_copy(gnext_ref, r, oslot).start(priority=r % 2)
        for r in range(BM):
            scatter_copy(scur_ref, r, oslot).start(priority=r % 2)
        hgu = _dot(x, wgu_bf_ref[...]) + bgu_ref[0]
        acts = []
        for g in range(d_ff2 // grp):
            gate = jnp.minimum(hgu[:, g * grp:g * grp + LANES], SWIGLU_LIMIT)
            up = jnp.clip(hgu[:, g * grp + LANES:(g + 1) * grp], -SWIGLU_LIMIT, SWIGLU_LIMIT)
            glu = gate * (1.0 / (1.0 + jnp.exp(-SWIGLU_ALPHA * gate)))
            acts.append(((up + 1.0) * glu).astype(BF16))
        act = jnp.concatenate(acts, axis=1)
        _store_token_tiles(ybuf_ref.at[slot], _dot(act, wd_bf_ref[...]) + bd_ref[0])

        @pl.when(i == nused)
        def _():
            for r in range(BM):
                scatter_copy(scur_ref, r, oslot).wait()
            for r in range(BM):
                gather_copy(gnext_ref, r, oslot).wait()


def _experts(hn_all, g_tab, s_tab, blk_e, nused, w_gate_up, b_gu_perm, w_down, b_down, n_out_rows):
    n_e, d, d_ff2 = w_gate_up.shape
    n_tab = g_tab.shape[0]
    expert = lambda i, be, nu: be[jnp.minimum(i, jnp.maximum(nu[0] - 1, 0))]
    step = lambda i, nu: jnp.minimum(i, nu[0])
    idx_spec = lambda f: pl.BlockSpec((1, 1, BM), f, memory_space=pltpu.SMEM)
    grid_spec = pltpu.PrefetchScalarGridSpec(
        num_scalar_prefetch=2,
        grid=(n_tab - 1,),
        in_specs=[idx_spec(lambda i, be, nu: (step(i, nu), 0, 0)),
                  idx_spec(lambda i, be, nu: (step(i, nu) + 1, 0, 0)),
                  idx_spec(lambda i, be, nu: (step(i, nu), 0, 0)),
                  idx_spec(lambda i, be, nu: (jnp.maximum(step(i, nu) - 1, 0), 0, 0)),
                  pl.BlockSpec(memory_space=pl.ANY),
                  pl.BlockSpec((1, d, d_ff2), lambda i, be, nu: (expert(i, be, nu), 0, 0)),
                  pl.BlockSpec((1, 1, d_ff2), lambda i, be, nu: (expert(i, be, nu), 0, 0)),
                  pl.BlockSpec((1, d_ff2 // 2, d), lambda i, be, nu: (expert(i, be, nu), 0, 0)),
                  pl.BlockSpec((1, 1, d), lambda i, be, nu: (expert(i, be, nu), 0, 0))],
        out_specs=pl.BlockSpec(memory_space=pl.ANY),
        scratch_shapes=[pltpu.VMEM((2, BM * SUBLANES, LANES), F32), pltpu.VMEM((2, BM * SUBLANES, LANES), F32),
                        pltpu.VMEM((d, d_ff2), BF16), pltpu.VMEM((d_ff2 // 2, d), BF16),
                        pltpu.SemaphoreType.DMA((2,)), pltpu.SemaphoreType.DMA((2,))],
    )
    return pl.pallas_call(
        _experts_kernel,
        grid_spec=grid_spec,
        out_shape=jax.ShapeDtypeStruct((n_out_rows * SUBLANES, LANES), F32),
        compiler_params=pltpu.CompilerParams(dimension_semantics=("arbitrary",), vmem_limit_bytes=VMEM_LIMIT),
        name="experts",
    )(blk_e, nused, g_tab, g_tab, s_tab, s_tab, hn_all, w_gate_up, b_gu_perm, w_down, b_down)


def _combine_kernel(h_ref, gates_ref, g_ref, *refs):
    slot_refs, out_ref = refs[:TOP_K], refs[TOP_K]
    te, d = h_ref.shape
    gates = gates_ref[...]
    h = h_ref[...]
    chunks = []
    ssq = jnp.zeros((te, 1), F32)
    for s in range(d // LANES):
        acc = _load_token_tiles(slot_refs[0], te, s) * gates[:, 0:1]
        for k in range(1, TOP_K):
            acc = acc + _load_token_tiles(slot_refs[k], te, s) * gates[:, k:k + 1]
        y = h[:, s * LANES:(s + 1) * LANES] + acc
        ssq = ssq + jnp.sum(y * y, axis=-1, keepdims=True)
        chunks.append(y)
    rinv = lax.rsqrt(ssq / d + EPS)
    out_ref[...] = jnp.concatenate(chunks, axis=1) * rinv * g_ref[...]


def _combine(yk, h, gates, g_final, te, tile0, tiles_per_slot):
    n, d = h.shape
    slot_spec = lambda k: pl.BlockSpec((te * SUBLANES, LANES), lambda i: (k * tiles_per_slot + tile0 + i, 0))
    return pl.pallas_call(
        _combine_kernel,
        grid=(n // te,),
        in_specs=[pl.BlockSpec((te, d), lambda i: (i, 0)),
                  pl.BlockSpec((te, TOP_K), lambda i: (i, 0)),
                  pl.BlockSpec((1, d), lambda i: (0, 0))] + [slot_spec(k) for k in range(TOP_K)],
        out_specs=pl.BlockSpec((te, d), lambda i: (i, 0)),
        out_shape=jax.ShapeDtypeStruct((n, d), F32),
        compiler_params=pltpu.CompilerParams(vmem_limit_bytes=VMEM_LIMIT),
        name="combine",
    )(h, gates, g_final, *([yk] * TOP_K))


def _largest_tile(cands, *sizes):
    for c in cands:
        if all(s % c == 0 for s in sizes):
            return c
    raise ValueError(f"no tile in {cands} divides {sizes}")


def kernel(x_prompt, x_sample, mem_prompt, cache_win_k, cache_win_v, state_conv, cache_mem_k, cache_mem_v, g_attn_norm, w_in, conv_w, attn_sinks, g_mem_norm, w_mem_kv, g_mix_out, w_out, g_ffn_norm, w_router, b_router, w_gate_up, b_gate_up, w_down, b_down, g_final):
    depth = w_in.shape[0]
    assert depth == 1, "single-layer step"
    b, s, d = x_prompt.shape
    nb, t_dec, _ = x_sample.shape
    n_e = w_router.shape[2]
    d_ff2 = w_gate_up.shape[3]
    cw = conv_w.shape[2]
    win = cache_win_k.shape[2]
    m_tok = cache_mem_k.shape[2]
    assert s % TQ == 0 and nb % BB == 0 and win == WINDOW and t_dec <= SUBLANES and d_ff2 % (2 * LANES) == 0
    assert d == SUBLANES * LANES, "token-tile layout: one token is one (8, 128) f32 tile"

    row = lambda a: a.reshape(1, -1)
    w_in_bf = w_in[0].astype(BF16)
    w_out_bf = w_out[0].astype(BF16)
    w_r_bf = w_router[0].astype(BF16)
    sinks = attn_sinks[0].astype(F32)
    shared = (row(g_attn_norm[0]), w_in_bf, conv_w[0], row(g_mix_out[0]), w_out_bf, row(g_ffn_norm[0]),
              w_r_bf, row(b_router[0]))

    mk_p, mv_p, mkt, mvb = _memkv(mem_prompt, row(g_mem_norm[0]), w_mem_kv[0].astype(BF16))
    h_p, hn_p, tope_p, gates_p, lastk, lastv, convst = _mixer_p(x_prompt, sinks, *shared, mkt, mvb)

    zeros = lambda r: jnp.zeros((nb, r, cw), F32)
    st = state_conv[0]
    pm1 = jnp.concatenate([st[:, 1:2], zeros(t_dec - 1)], axis=1).reshape(nb * t_dec, cw)
    pm2 = jnp.concatenate([st, zeros(t_dec - 2)], axis=1).reshape(nb * t_dec, cw)
    h_s, hn_s, tope_s, gates_s, nwk, nwv, u_s = _mixer_s(
        x_sample.reshape(nb * t_dec, d), t_dec, pm1, pm2,
        cache_win_k[0].reshape(nb, win, KV_WIDTH), cache_win_v[0].reshape(nb, win, KV_WIDTH),
        cache_mem_k[0].reshape(nb, m_tok, MEM_WIDTH), cache_mem_v[0].reshape(nb, m_tok, MEM_WIDTH),
        sinks, *shared)

    n_p, n_s = b * s, nb * t_dec
    n = n_p + n_s
    tm = _largest_tile((512, 256, 128, 64, 32, 16, 8), n_p, n_s)
    hn_all = jnp.concatenate([hn_p, hn_s], axis=0)
    tope = jnp.concatenate([tope_p, tope_s], axis=0)

    rank, counts_f = _rank(tope, tm, n_e)
    counts = counts_f[0].astype(jnp.int32)
    padded = (counts + BM - 1) // BM * BM
    pad_ends = jnp.cumsum(padded)
    pad_starts = pad_ends - padded
    nk = n * TOP_K
    n_blocks = -(-nk // BM) + n_e
    nused = (pad_ends[-1:] // BM).astype(jnp.int32)
    blk_start = jnp.arange(n_blocks, dtype=jnp.int32) * BM
    blk_e = jnp.minimum(jnp.sum((pad_ends[None, :] <= blk_start[:, None]).astype(jnp.int32), axis=1), n_e - 1)
    dest = _dest(tope, rank, pad_starts.astype(F32).reshape(1, n_e), tm, n_e)
    lastblk = jnp.where(padded > 0, pad_ends // BM - 1, -1).astype(jnp.int32)
    inv = _invmap(dest, lastblk, nused, tm, n_blocks)

    spare = nk + jnp.arange(BM, dtype=jnp.int32)
    gidx = jnp.maximum(inv, 0) // TOP_K
    sidx = jnp.where(inv >= 0, (inv % TOP_K) * n + inv // TOP_K, jnp.tile(spare, n_blocks))
    g_tab = (jnp.concatenate([gidx, jnp.zeros((2 * BM,), jnp.int32)]) * SUBLANES).reshape(n_blocks + 2, 1, BM)
    s_tab = (jnp.concatenate([spare, sidx, spare]) * SUBLANES).reshape(n_blocks + 2, 1, BM)

    grp = 2 * LANES
    b_gu = b_gate_up[0].reshape(n_e, d_ff2 // grp, LANES, 2).transpose(0, 1, 3, 2).reshape(n_e, 1, d_ff2)
    yk = _experts(hn_all, g_tab, s_tab, blk_e, nused, w_gate_up[0], b_gu, w_down[0],
                  b_down[0].reshape(n_e, 1, d), nk + BM)

    g_fin = row(g_final)
    y_p = _combine(yk, h_p, gates_p, g_fin, tm, 0, n // tm)
    y_s = _combine(yk, h_s, gates_s, g_fin, tm, n_p // tm, n // tm)

    kv5 = lambda a, bsz, r, hds: a.reshape(1, bsz, r, hds, HEAD_DIM)
    return (y_p.reshape(b, s, d), y_s.reshape(nb, t_dec, d),
            kv5(lastk, b, WINDOW, N_KV_HEADS), kv5(lastv, b, WINDOW, N_KV_HEADS),
            convst.reshape(1, b, 2, cw),
            kv5(mk_p, b, m_tok, N_MEM_HEADS), kv5(mv_p, b, m_tok, N_MEM_HEADS),
            kv5(nwk, nb, win, N_KV_HEADS), kv5(nwv, nb, win, N_KV_HEADS),
            u_s.reshape(nb, t_dec, cw)[:, t_dec - 2:].reshape(1, nb, 2, cw))
```

```python
import functools

import jax
import jax.numpy as jnp
from jax import lax
from jax.experimental import pallas as pl
from jax.experimental.pallas import tpu as pltpu

F32 = jnp.float32
BF16 = jnp.bfloat16

HEAD_DIM = 64
N_Q_HEADS = 8
N_KV_HEADS = 2
WINDOW = 128
ATTN_WIDTH = N_Q_HEADS * HEAD_DIM
KV_WIDTH = N_KV_HEADS * HEAD_DIM
N_MEM_HEADS = 4
MEM_WIDTH = N_MEM_HEADS * HEAD_DIM
TOP_K = 4
SWIGLU_LIMIT = 7.0
SWIGLU_ALPHA = 1.702
EPS = 1e-5
ATTN_SCALE = HEAD_DIM ** -0.5
ALIBI_SLOPES = tuple(2.0 ** (-8.0 * (h + 1) / N_Q_HEADS) for h in range(N_Q_HEADS))

LANES = 128
SUBLANES = 8
VMEM_LIMIT = 56 * 1024 * 1024

TQ = 256
BB = 16
BM = 256
REP = 8
NBUF = 3


def _rms(x):
    return x * lax.rsqrt(jnp.mean(x * x, axis=-1, keepdims=True) + EPS)


def _dot(a, b):
    return jnp.dot(a, b, preferred_element_type=F32)


def _dot_nt(a, b):
    return lax.dot_general(a, b, (((1,), (1,)), ((), ())), preferred_element_type=F32)


def _iota(shape, axis):
    return lax.broadcasted_iota(jnp.int32, shape, axis)


def _store_token_tiles(ref, x, base=0):
    t = x.shape[0]
    for s in range(x.shape[1] // LANES):
        ref[pl.ds(base + s, t, stride=SUBLANES), :] = x[:, s * LANES:(s + 1) * LANES]


def _load_token_tiles(ref, t, s, base=0):
    return ref[pl.ds(base + s, t, stride=SUBLANES), :]


def _memkv_kernel(mem_ref, g_ref, w_ref, mk_ref, mv_ref, mkt_ref, mvb_ref):
    xn = (_rms(mem_ref[0]) * g_ref[...]).astype(BF16)
    kv = _dot(xn, w_ref[...])
    mk = kv[:, :MEM_WIDTH]
    mv = kv[:, MEM_WIDTH:]
    mk_ref[0] = mk
    mv_ref[0] = mv
    mkt_ref[0] = mk.T.astype(BF16)
    mvb_ref[0] = mv.astype(BF16)


def _memkv(mem, g, w_bf):
    b, m, d = mem.shape
    out_f = jax.ShapeDtypeStruct((b, m, MEM_WIDTH), F32)
    out_b = jax.ShapeDtypeStruct((b, m, MEM_WIDTH), BF16)
    out_t = jax.ShapeDtypeStruct((b, MEM_WIDTH, m), BF16)
    blk = lambda r, c: pl.BlockSpec((1, r, c), lambda i: (i, 0, 0))
    return pl.pallas_call(
        _memkv_kernel,
        grid=(b,),
        in_specs=[blk(m, d), pl.BlockSpec((1, d), lambda i: (0, 0)),
                  pl.BlockSpec((d, 2 * MEM_WIDTH), lambda i: (0, 0))],
        out_specs=[blk(m, MEM_WIDTH), blk(m, MEM_WIDTH), blk(MEM_WIDTH, m), blk(m, MEM_WIDTH)],
        out_shape=[out_f, out_f, out_t, out_b],
        name="memkv",
    )(mem, g, w_bf)


def _router_topk(hn, w_r_ref, b_r_ref, tope_ref, gates_ref):
    n_e = w_r_ref.shape[1]
    logits = _dot(hn.astype(BF16), w_r_ref[...]) + b_r_ref[...]
    rows = logits.shape[0]
    col = _iota((rows, n_e), 1).astype(F32)
    vals, idxs = [], []
    cur = logits
    for _ in range(TOP_K):
        m = jnp.max(cur, axis=-1, keepdims=True)
        idx = jnp.min(jnp.where(cur == m, col, float(n_e)), axis=-1, keepdims=True)
        vals.append(m)
        idxs.append(idx)
        cur = jnp.where(col == idx, -jnp.inf, cur)
    exps = [jnp.exp(v - vals[0]) for v in vals]
    tot = exps[0] + exps[1] + exps[2] + exps[3]
    col4 = _iota((rows, TOP_K), 1)
    te = jnp.zeros((rows, TOP_K), F32)
    ga = jnp.zeros((rows, TOP_K), F32)
    for k in range(TOP_K):
        te = jnp.where(col4 == k, idxs[k], te)
        ga = jnp.where(col4 == k, exps[k] / tot, ga)
    tope_ref[...] = te.astype(jnp.int32)
    gates_ref[...] = ga


def _mix_out(x, attn, conv_out, cross, g_mix_ref, w_out_ref, g_ffn_ref, w_r_ref, b_r_ref,
             h_ref, hn_ref, tope_ref, gates_ref):
    mix = jnp.concatenate([_rms(attn), _rms(conv_out), _rms(cross)], axis=-1) * g_mix_ref[...]
    h = x + _dot(mix.astype(BF16), w_out_ref[...])
    hn = _rms(h) * g_ffn_ref[...]
    h_ref[...] = h
    _store_token_tiles(hn_ref, hn)
    _router_topk(hn, w_r_ref, b_r_ref, tope_ref, gates_ref)


def _swa_block(q_blk, kk, vv, prev_lim, sinks_ref):
    blk = WINDOW
    lane = _iota((2 * blk, KV_WIDTH), 1)
    lo = lane < HEAD_DIM
    kk_r = pltpu.roll(kk, HEAD_DIM, axis=1)
    vv_r = pltpu.roll(vv, HEAD_DIM, axis=1)
    kdup = [jnp.where(lo, kk, kk_r).astype(BF16), jnp.where(lo, kk_r, kk).astype(BF16)]
    vlo = [jnp.where(lo, vv, 0.0).astype(BF16), jnp.where(lo, vv_r, 0.0).astype(BF16)]
    vhi = [jnp.where(lo, 0.0, vv_r).astype(BF16), jnp.where(lo, 0.0, vv).astype(BF16)]
    qi = _iota((blk, 2 * blk), 0)
    kj = _iota((blk, 2 * blk), 1)
    dist = blk + qi - kj
    mask = (dist >= 0) & (dist < WINDOW) & (kj >= prev_lim)
    distf = dist.astype(F32)
    qlo = _iota((blk, 2 * HEAD_DIM), 1) < HEAD_DIM
    outs = []
    for p in range(N_Q_HEADS // 2):
        kh = (2 * p) // (N_Q_HEADS // N_KV_HEADS)
        qp = q_blk[:, p * 2 * HEAD_DIM:(p + 1) * 2 * HEAD_DIM]
        acc = None
        for e in range(2):
            h = 2 * p + e
            qm = jnp.where(qlo if e == 0 else jnp.logical_not(qlo), qp, 0.0).astype(BF16)
            s = _dot_nt(qm, kdup[kh]) - ALIBI_SLOPES[h] * distf
            s = jnp.where(mask, s, -jnp.inf)
            sink = sinks_ref[h]
            m = jnp.maximum(jnp.max(s, axis=-1, keepdims=True), sink)
            pe = jnp.exp(s - m)
            denom = jnp.sum(pe, axis=-1, keepdims=True) + jnp.exp(sink - m)
            o = _dot(pe.astype(BF16), (vlo if e == 0 else vhi)[kh]) / denom
            acc = o if acc is None else acc + o
        outs.append(acc)
    return jnp.concatenate(outs, axis=1)


def _mem_attend_shared(mq, mkt, mvb):
    t = mq.shape[0]
    m_tok = mvb.shape[0]
    qhead = _iota((t, MEM_WIDTH), 1) // HEAD_DIM
    vhead = _iota((m_tok, MEM_WIDTH), 1) // HEAD_DIM
    cross = None
    for h in range(N_MEM_HEADS):
        qm = jnp.where(qhead == h, mq, 0.0).astype(BF16)
        s = _dot(qm, mkt)
        m = jnp.max(s, axis=-1, keepdims=True)
        pe = jnp.exp(s - m)
        denom = jnp.sum(pe, axis=-1, keepdims=True)
        vm = jnp.where(vhead == h, mvb, jnp.zeros_like(mvb))
        o = _dot(pe.astype(BF16), vm) / denom
        cross = o if cross is None else cross + o
    return cross


def _mixer_p_kernel(sinks_ref, x_ref, g_attn_ref, w_in_ref, conv_w_ref, g_mix_ref, w_out_ref, g_ffn_ref,
                    w_r_ref, b_r_ref, mkt_ref, mvb_ref,
                    h_ref, hn_ref, tope_ref, gates_ref, lastk_ref, lastv_ref, convst_ref,
                    ck_ref, cv_ref, cu_ref):
    j = pl.program_id(1)
    nj = pl.num_programs(1)

    @pl.when(j == 0)
    def _():
        ck_ref[...] = jnp.zeros_like(ck_ref)
        cv_ref[...] = jnp.zeros_like(cv_ref)
        cu_ref[...] = jnp.zeros_like(cu_ref)

    x = x_ref[0]
    xn = (_rms(x) * g_attn_ref[...]).astype(BF16)
    z = _dot(xn, w_in_ref[...])
    c0 = ATTN_WIDTH
    c1 = c0 + KV_WIDTH
    c2 = c1 + KV_WIDTH
    cw = conv_w_ref.shape[1]
    c3, c4, c5 = c2 + cw, c2 + 2 * cw, c2 + 3 * cw
    q = z[:, :c0] * ATTN_SCALE
    k = z[:, c0:c1]
    v = z[:, c1:c2]
    cb = z[:, c2:c3]
    cc = z[:, c3:c4]
    cvv = z[:, c4:c5]
    mq = z[:, c5:] * ATTN_SCALE

    blk = WINDOW
    attn_blocks = []
    for i in range(TQ // blk):
        if i == 0:
            pk, pv = ck_ref[...], cv_ref[...]
            prev_lim = jnp.where(j > 0, 0, blk)
        else:
            pk, pv = k[(i - 1) * blk:i * blk], v[(i - 1) * blk:i * blk]
            prev_lim = 0
        kk = jnp.concatenate([pk, k[i * blk:(i + 1) * blk]], axis=0)
        vv = jnp.concatenate([pv, v[i * blk:(i + 1) * blk]], axis=0)
        attn_blocks.append(_swa_block(q[i * blk:(i + 1) * blk], kk, vv, prev_lim, sinks_ref))
    attn = jnp.concatenate(attn_blocks, axis=0)
    ck_ref[...] = k[TQ - blk:]
    cv_ref[...] = v[TQ - blk:]

    u = cc * cvv
    row = _iota(u.shape, 0)
    u1 = jnp.where(row == 0, cu_ref[SUBLANES - 1:SUBLANES, :], pltpu.roll(u, 1, axis=0))
    u2 = jnp.where(row == 0, cu_ref[SUBLANES - 2:SUBLANES - 1, :],
                   jnp.where(row == 1, cu_ref[SUBLANES - 1:SUBLANES, :], pltpu.roll(u, 2, axis=0)))
    conv_out = cb * (conv_w_ref[0:1, :] * u2 + conv_w_ref[1:2, :] * u1 + conv_w_ref[2:3, :] * u)
    cu_ref[...] = u[TQ - SUBLANES:]

    cross = _mem_attend_shared(mq, mkt_ref[0], mvb_ref[0])

    @pl.when(j == nj - 1)
    def _():
        lastk_ref[0] = k[TQ - blk:]
        lastv_ref[0] = v[TQ - blk:]
        convst_ref[0] = u[TQ - 2:]

    _mix_out(x, attn, conv_out, cross, g_mix_ref, w_out_ref, g_ffn_ref, w_r_ref, b_r_ref,
             h_ref, hn_ref, tope_ref, gates_ref)


def _mixer_p(x, sinks, g_attn, w_in, conv_w, g_mix, w_out, g_ffn, w_r, b_r, mkt, mvb):
    b, s, d = x.shape
    nj = s // TQ
    n = b * s
    cw = conv_w.shape[1]
    full = lambda a: pl.BlockSpec(a.shape, lambda bi, ji, *_: (0,) * a.ndim)
    tok = lambda w: pl.BlockSpec((TQ, w), lambda bi, ji, *_: (bi * nj + ji, 0))
    per_b = lambda r, c: pl.BlockSpec((1, r, c), lambda bi, ji, *_: (bi, 0, 0))
    grid_spec = pltpu.PrefetchScalarGridSpec(
        num_scalar_prefetch=1,
        grid=(b, nj),
        in_specs=[pl.BlockSpec((1, TQ, d), lambda bi, ji, *_: (bi, ji, 0)),
                  full(g_attn), full(w_in), full(conv_w), full(g_mix), full(w_out), full(g_ffn),
                  full(w_r), full(b_r), per_b(MEM_WIDTH, mkt.shape[2]), per_b(mvb.shape[1], MEM_WIDTH)],
        out_specs=[tok(d), pl.BlockSpec((TQ * SUBLANES, LANES), lambda bi, ji, *_: (bi * nj + ji, 0)),
                   tok(TOP_K), tok(TOP_K),
                   per_b(WINDOW, KV_WIDTH), per_b(WINDOW, KV_WIDTH), per_b(2, cw)],
        scratch_shapes=[pltpu.VMEM((WINDOW, KV_WIDTH), F32), pltpu.VMEM((WINDOW, KV_WIDTH), F32),
                        pltpu.VMEM((SUBLANES, cw), F32)],
    )
    return pl.pallas_call(
        _mixer_p_kernel,
        grid_spec=grid_spec,
        out_shape=[jax.ShapeDtypeStruct((n, d), F32), jax.ShapeDtypeStruct((n * SUBLANES, LANES), F32),
                   jax.ShapeDtypeStruct((n, TOP_K), jnp.int32), jax.ShapeDtypeStruct((n, TOP_K), F32),
                   jax.ShapeDtypeStruct((b, WINDOW, KV_WIDTH), F32),
                   jax.ShapeDtypeStruct((b, WINDOW, KV_WIDTH), F32),
                   jax.ShapeDtypeStruct((b, 2, cw), F32)],
        compiler_params=pltpu.CompilerParams(dimension_semantics=("arbitrary", "arbitrary"),
                                             vmem_limit_bytes=VMEM_LIMIT),
        name="mixer_p",
    )(sinks, x, g_attn, w_in, conv_w, g_mix, w_out, g_ffn, w_r, b_r, mkt, mvb)


def _per_head_column(values, hrow):
    col = jnp.zeros(hrow.shape, F32)
    for h in range(N_Q_HEADS):
        col = jnp.where(hrow == h, values[h], col)
    return col


def _mixer_s_kernel(sinks_ref, x_ref, pm1_ref, pm2_ref, wk_ref, wv_ref, mk_ref, mv_ref,
                    g_attn_ref, w_in_ref, conv_w_ref, g_mix_ref, w_out_ref, g_ffn_ref, w_r_ref, b_r_ref,
                    h_ref, hn_ref, tope_ref, gates_ref, nwk_ref, nwv_ref, u_ref, *, t_dec):
    r_tok = BB * t_dec
    r_exp = r_tok * REP
    qrows = t_dec * REP
    x = x_ref[...]
    xn = (_rms(x) * g_attn_ref[...]).astype(BF16)
    z = _dot(xn, w_in_ref[...])
    c0 = ATTN_WIDTH
    c1 = c0 + KV_WIDTH
    c2 = c1 + KV_WIDTH
    cw = conv_w_ref.shape[1]
    c3, c4, c5 = c2 + cw, c2 + 2 * cw, c2 + 3 * cw
    q = z[:, :c0] * ATTN_SCALE
    k_new = z[:, c0:c1]
    v_new = z[:, c1:c2]
    cb = z[:, c2:c3]
    cc = z[:, c3:c4]
    cvv = z[:, c4:c5]
    mq = z[:, c5:] * ATTN_SCALE
    win = wk_ref.shape[1]

    xi = _iota((KV_WIDTH, ATTN_WIDTH), 0)
    xl = _iota((KV_WIDTH, ATTN_WIDTH), 1)
    q_per_kv = N_Q_HEADS // N_KV_HEADS
    expand = (xi == (xl // (q_per_kv * HEAD_DIM)) * HEAD_DIM + xl % HEAD_DIM).astype(BF16)
    rr = _iota((r_exp, r_tok), 0)
    rc = _iota((r_exp, r_tok), 1)
    rep = (rr // REP == rc).astype(BF16)

    hrow = _iota((r_exp, 1), 0) % REP
    trow = (_iota((r_exp, 1), 0) // REP) % t_dec
    slope_col = _per_head_column(ALIBI_SLOPES, hrow)
    sink_col = _per_head_column([sinks_ref[h] for h in range(N_Q_HEADS)], hrow)

    qexp = jnp.where(hrow == _iota((r_exp, ATTN_WIDTH), 1) // HEAD_DIM, _dot(rep, q.astype(BF16)), 0.0)
    kexp = _dot(wk_ref[...].reshape(BB * win, KV_WIDTH).astype(BF16), expand).astype(BF16)
    vexp = _dot(wv_ref[...].reshape(BB * win, KV_WIDTH).astype(BF16), expand).astype(BF16)
    s = jnp.einsum("bqc,bkc->bqk", qexp.astype(BF16).reshape(BB, qrows, ATTN_WIDTH),
                   kexp.reshape(BB, win, ATTN_WIDTH), preferred_element_type=F32).reshape(r_exp, win)
    scol = _iota((r_exp, win), 1)
    s = s - slope_col * (win + trow - scol).astype(F32)
    s = jnp.where(scol > trow, s, -jnp.inf)
    knew_exp = _dot(k_new.astype(BF16), expand).astype(BF16)
    vnew_exp = _dot(v_new.astype(BF16), expand).astype(BF16)
    s_new, v_rep = [], []
    for jn in range(t_dec):
        rep_j = (rc == (rr // qrows) * t_dec + jn).astype(BF16)
        k_rep = _dot(rep_j, knew_exp)
        v_rep.append(_dot(rep_j, vnew_exp))
        sj = jnp.sum(qexp * k_rep, axis=-1, keepdims=True) - slope_col * (trow - jn).astype(F32)
        s_new.append(jnp.where(trow >= jn, sj, -jnp.inf))
    m = jnp.maximum(jnp.max(s, axis=-1, keepdims=True), sink_col)
    for sj in s_new:
        m = jnp.maximum(m, sj)
    pe = jnp.exp(s - m)
    denom = jnp.sum(pe, axis=-1, keepdims=True) + jnp.exp(sink_col - m)
    o = jnp.einsum("bqk,bkc->bqc", pe.astype(BF16).reshape(BB, qrows, win),
                   vexp.reshape(BB, win, ATTN_WIDTH), preferred_element_type=F32).reshape(r_exp, ATTN_WIDTH)
    for jn in range(t_dec):
        pj = jnp.exp(s_new[jn] - m)
        denom = denom + pj
        o = o + pj * v_rep[jn]
    o = jnp.where(hrow == _iota((r_exp, ATTN_WIDTH), 1) // HEAD_DIM, o / denom, 0.0)
    attn = jnp.sum(o.reshape(r_tok, REP, ATTN_WIDTH), axis=1)

    m_tok = mk_ref.shape[1]
    mhead = _iota((r_exp, MEM_WIDTH), 1) // HEAD_DIM
    mqexp = jnp.where(hrow == mhead, _dot(rep, mq.astype(BF16)), 0.0).astype(BF16)
    sm = jnp.einsum("bqc,bmc->bqm", mqexp.reshape(BB, qrows, MEM_WIDTH), mk_ref[...].astype(BF16),
                    preferred_element_type=F32).reshape(r_exp, m_tok)
    mm = jnp.max(sm, axis=-1, keepdims=True)
    pm = jnp.exp(sm - mm)
    dm = jnp.sum(pm, axis=-1, keepdims=True)
    om = jnp.einsum("bqm,bmc->bqc", pm.astype(BF16).reshape(BB, qrows, m_tok), mv_ref[...].astype(BF16),
                    preferred_element_type=F32).reshape(r_exp, MEM_WIDTH)
    om = jnp.where(hrow == mhead, om / dm, 0.0)
    cross = jnp.sum(om.reshape(r_tok, REP, MEM_WIDTH), axis=1)

    u = cc * cvv
    tt = _iota(u.shape, 0) % t_dec
    u1 = jnp.where(tt >= 1, pltpu.roll(u, 1, axis=0), pm1_ref[...])
    u2 = jnp.where(tt >= 2, pltpu.roll(u, 2, axis=0), pm2_ref[...])
    conv_out = cb * (conv_w_ref[0:1, :] * u2 + conv_w_ref[1:2, :] * u1 + conv_w_ref[2:3, :] * u)
    u_ref[...] = u

    nwk_ref[:, 0:win - t_dec, :] = wk_ref[:, t_dec:win, :]
    nwv_ref[:, 0:win - t_dec, :] = wv_ref[:, t_dec:win, :]
    for b in range(BB):
        nwk_ref[b, win - t_dec:win, :] = k_new[b * t_dec:(b + 1) * t_dec, :]
        nwv_ref[b, win - t_dec:win, :] = v_new[b * t_dec:(b + 1) * t_dec, :]

    _mix_out(x, attn, conv_out, cross, g_mix_ref, w_out_ref, g_ffn_ref, w_r_ref, b_r_ref,
             h_ref, hn_ref, tope_ref, gates_ref)


def _mixer_s(x2, t_dec, pm1, pm2, wk, wv, mk, mv, sinks, g_attn, w_in, conv_w, g_mix, w_out, g_ffn, w_r, b_r):
    n, d = x2.shape
    nb = wk.shape[0]
    win = wk.shape[1]
    m_tok = mk.shape[1]
    cw = conv_w.shape[1]
    r_tok = BB * t_dec
    full = lambda a: pl.BlockSpec(a.shape, lambda i, *_: (0,) * a.ndim)
    tok = lambda w: pl.BlockSpec((r_tok, w), lambda i, *_: (i, 0))
    per_b = lambda r, c: pl.BlockSpec((BB, r, c), lambda i, *_: (i, 0, 0))
    grid_spec = pltpu.PrefetchScalarGridSpec(
        num_scalar_prefetch=1,
        grid=(nb // BB,),
        in_specs=[tok(d), tok(cw), tok(cw), per_b(win, KV_WIDTH), per_b(win, KV_WIDTH),
                  per_b(m_tok, MEM_WIDTH), per_b(m_tok, MEM_WIDTH),
                  full(g_attn), full(w_in), full(conv_w), full(g_mix), full(w_out), full(g_ffn),
                  full(w_r), full(b_r)],
        out_specs=[tok(d), pl.BlockSpec((r_tok * SUBLANES, LANES), lambda i, *_: (i, 0)),
                   tok(TOP_K), tok(TOP_K), per_b(win, KV_WIDTH), per_b(win, KV_WIDTH), tok(cw)],
    )
    return pl.pallas_call(
        functools.partial(_mixer_s_kernel, t_dec=t_dec),
        grid_spec=grid_spec,
        out_shape=[jax.ShapeDtypeStruct((n, d), F32), jax.ShapeDtypeStruct((n * SUBLANES, LANES), F32),
                   jax.ShapeDtypeStruct((n, TOP_K), jnp.int32), jax.ShapeDtypeStruct((n, TOP_K), F32),
                   jax.ShapeDtypeStruct((nb, win, KV_WIDTH), F32), jax.ShapeDtypeStruct((nb, win, KV_WIDTH), F32),
                   jax.ShapeDtypeStruct((n, cw), F32)],
        compiler_params=pltpu.CompilerParams(dimension_semantics=("arbitrary",), vmem_limit_bytes=VMEM_LIMIT),
        name="mixer_s",
    )(sinks, x2, pm1, pm2, wk, wv, mk, mv, g_attn, w_in, conv_w, g_mix, w_out, g_ffn, w_r, b_r)


def _rank_kernel(tope_ref, rank_ref, counts_ref, tri_ref, carry_ref, *, n_e):
    i = pl.program_id(0)
    tm = tope_ref.shape[0]

    @pl.when(i == 0)
    def _():
        tri_ref[...] = (_iota((tm, tm), 0) > _iota((tm, tm), 1)).astype(BF16)
        carry_ref[...] = jnp.zeros_like(carry_ref)

    te = tope_ref[...]
    col = _iota((tm, n_e), 1)
    hits = [te[:, k:k + 1] == col for k in range(TOP_K)]
    onehot = jnp.zeros((tm, n_e), F32)
    for hk in hits:
        onehot = onehot + hk.astype(F32)
    before = _dot(tri_ref[...], onehot.astype(BF16)) + carry_ref[...]
    col4 = _iota((tm, TOP_K), 1)
    rank = jnp.zeros((tm, TOP_K), F32)
    for k in range(TOP_K):
        rank = jnp.where(col4 == k, jnp.sum(jnp.where(hits[k], before, 0.0), axis=-1, keepdims=True), rank)
    rank_ref[...] = rank.astype(jnp.int32)
    carry_ref[...] += jnp.sum(onehot, axis=0, keepdims=True)
    counts_ref[...] = carry_ref[...]


def _rank(tope, tm, n_e):
    n = tope.shape[0]
    return pl.pallas_call(
        functools.partial(_rank_kernel, n_e=n_e),
        grid=(n // tm,),
        in_specs=[pl.BlockSpec((tm, TOP_K), lambda i: (i, 0))],
        out_specs=[pl.BlockSpec((tm, TOP_K), lambda i: (i, 0)), pl.BlockSpec((1, n_e), lambda i: (0, 0))],
        out_shape=[jax.ShapeDtypeStruct((n, TOP_K), jnp.int32), jax.ShapeDtypeStruct((1, n_e), F32)],
        scratch_shapes=[pltpu.VMEM((tm, tm), BF16), pltpu.VMEM((1, n_e), F32)],
        compiler_params=pltpu.CompilerParams(dimension_semantics=("arbitrary",)),
        name="rank",
    )(tope)


def _dest_kernel(tope_ref, rank_ref, start_ref, dest_ref, *, n_e):
    tm = tope_ref.shape[0]
    te = tope_ref[...]
    col = _iota((tm, n_e), 1)
    col4 = _iota((tm, TOP_K), 1)
    base = jnp.zeros((tm, TOP_K), F32)
    for k in range(TOP_K):
        sel = jnp.sum(jnp.where(te[:, k:k + 1] == col, start_ref[...], 0.0), axis=-1, keepdims=True)
        base = jnp.where(col4 == k, sel, base)
    dest_ref[...] = base.astype(jnp.int32) + rank_ref[...]


def _dest(tope, rank, pad_starts_f, tm, n_e):
    n = tope.shape[0]
    return pl.pallas_call(
        functools.partial(_dest_kernel, n_e=n_e),
        grid=(n // tm,),
        in_specs=[pl.BlockSpec((tm, TOP_K), lambda i: (i, 0)), pl.BlockSpec((tm, TOP_K), lambda i: (i, 0)),
                  pl.BlockSpec((1, n_e), lambda i: (0, 0))],
        out_specs=pl.BlockSpec((tm, TOP_K), lambda i: (i, 0)),
        out_shape=jax.ShapeDtypeStruct((n, TOP_K), jnp.int32),
        name="dest",
    )(tope, rank, pad_starts_f)


def _invmap_kernel(lastblk_ref, nused_ref, dest_ref, inv_ref, *, n_blocks):
    i = pl.program_id(0)
    per_tile = dest_ref.shape[2]

    @pl.when(i == 0)
    def _():
        def clear_block(b, c):
            def clear(r, c2):
                inv_ref[b * BM + r] = -1
                return c2
            return lax.fori_loop(0, BM, clear, c, unroll=8)

        def clear_expert(e, c):
            @pl.when(lastblk_ref[e] >= 0)
            def _():
                clear_block(lastblk_ref[e], 0)
            return c

        lax.fori_loop(0, lastblk_ref.shape[0], clear_expert, 0)
        lax.fori_loop(nused_ref[0], n_blocks, clear_block, 0)

    def place(r, c):
        inv_ref[dest_ref[0, 0, r]] = i * per_tile + r
        return c

    lax.fori_loop(0, per_tile, place, 0, unroll=8)


def _invmap(dest, lastblk, nused, tm, n_blocks):
    nt = dest.shape[0] // tm
    grid_spec = pltpu.PrefetchScalarGridSpec(
        num_scalar_prefetch=2,
        grid=(nt,),
        in_specs=[pl.BlockSpec((1, 1, tm * TOP_K), lambda i, *_: (i, 0, 0), memory_space=pltpu.SMEM)],
        out_specs=pl.BlockSpec(memory_space=pltpu.SMEM),
    )
    return pl.pallas_call(
        functools.partial(_invmap_kernel, n_blocks=n_blocks),
        grid_spec=grid_spec,
        out_shape=jax.ShapeDtypeStruct((n_blocks * BM,), jnp.int32),
        compiler_params=pltpu.CompilerParams(dimension_semantics=("arbitrary",)),
        name="invmap",
    )(lastblk, nused, dest.reshape(nt, 1, tm * TOP_K))


def _experts_kernel(blk_e_ref, nused_ref, g0_ref, g1_ref, g2_ref, s0_ref, sm1_ref, sm2_ref, hn_ref,
                    wgu_ref, bgu_ref, wd_ref, bd_ref, yk_ref,
                    xbuf_ref, ybuf_ref, wgu_bf_ref, wd_bf_ref, gsem, ssem):
    i = pl.program_id(0)
    nused = nused_ref[0]
    d_ff2 = wgu_ref.shape[2]
    grp = 2 * LANES

    def tile_at(ref, first_row):
        return ref.at[pl.ds(pl.multiple_of(first_row, SUBLANES), SUBLANES)]

    def gather_copy(idx_ref, r, slot):
        return pltpu.make_async_copy(tile_at(hn_ref, idx_ref[0, 0, r]),
                                     xbuf_ref.at[slot, pl.ds(r * SUBLANES, SUBLANES)], gsem.at[slot])

    def scatter_copy(idx_ref, r, slot):
        return pltpu.make_async_copy(ybuf_ref.at[slot, pl.ds(r * SUBLANES, SUBLANES)],
                                     tile_at(yk_ref, idx_ref[0, 0, r]), ssem.at[slot])

    @pl.when(i <= nused)
    def _():
        slot = lax.rem(i, NBUF)
        slot_m1 = lax.rem(i + NBUF - 1, NBUF)
        slot_p1 = lax.rem(i + 1, NBUF)
        last = jnp.maximum(nused - 1, 0)
        e = blk_e_ref[jnp.minimum(i, last)]
        e_prev = blk_e_ref[jnp.minimum(jnp.maximum(i - 1, 0), last)]

        @pl.when(i == 0)
        def _():
            ybuf_ref[...] = jnp.zeros_like(ybuf_ref)
            for r in range(BM):
                gather_copy(g0_ref, r, slot).start(priority=1)
            for r in range(BM):
                gather_copy(g1_ref, r, slot_p1).start(priority=1)

        @pl.when(i >= 2)
        def _():
            for r in range(BM):
                scatter_copy(sm2_ref, r, slot).wait()

        @pl.when((i == 0) | (e != e_prev))
        def _():
            pr = _iota((grp, grp), 0)
            pc = _iota((grp, grp), 1)
            perm = (pr == jnp.where(pc < LANES, 2 * pc, 2 * (pc - LANES) + 1)).astype(BF16)
            for g in range(d_ff2 // grp):
                w = wgu_ref[0, :, g * grp:(g + 1) * grp].astype(BF16)
                wgu_bf_ref[:, g * grp:(g + 1) * grp] = _dot(w, perm).astype(BF16)
            wd_bf_ref[...] = wd_ref[0].astype(BF16)

        for r in range(BM):
            gather_copy(g0_ref, r, slot).wait()
        x = jnp.concatenate([_load_token_tiles(xbuf_ref.at[slot], BM, s).astype(BF16)
                             for s in range(SUBLANES)], axis=1)
        for r in range(BM):
            gather_copy(g2_ref, r, slot_m1).start(priority=1)
            scatter_copy(s0_ref, r, slot_m1).start(priority=r % 2)
        hgu = _dot(x, wgu_bf_ref[...]) + bgu_ref[0]
        acts = []
        for g in range(d_ff2 // grp):
            gate = jnp.minimum(hgu[:, g * grp:g * grp + LANES], SWIGLU_LIMIT)
            up = jnp.clip(hgu[:, g * grp + LANES:(g + 1) * grp], -SWIGLU_LIMIT, SWIGLU_LIMIT)
            glu = gate * (1.0 / (1.0 + jnp.exp(-SWIGLU_ALPHA * gate)))
            acts.append(((up + 1.0) * glu).astype(BF16))
        act = jnp.concatenate(acts, axis=1)
        _store_token_tiles(ybuf_ref.at[slot], _dot(act, wd_bf_ref[...]) + bd_ref[0])

        @pl.when(i == nused)
        def _():
            @pl.when(i >= 1)
            def _():
                for r in range(BM):
                    scatter_copy(sm1_ref, r, slot_p1).wait()
            for r in range(BM):
                scatter_copy(s0_ref, r, slot_m1).wait()
            for r in range(BM):
                gather_copy(g1_ref, r, slot_p1).wait()
            for r in range(BM):
                gather_copy(g2_ref, r, slot_m1).wait()


def _experts(hn_all, g_tab, s_tab, blk_e, nused, w_gate_up, b_gu_perm, w_down, b_down, n_out_rows):
    n_e, d, d_ff2 = w_gate_up.shape
    n_steps = s_tab.shape[0]
    assert g_tab.shape[0] == n_steps + 2
    expert = lambda i, be, nu: be[jnp.minimum(i, jnp.maximum(nu[0] - 1, 0))]
    idx_spec = lambda off: pl.BlockSpec(
        (1, 1, BM), lambda i, be, nu: (jnp.maximum(jnp.minimum(i, nu[0]) + off, 0), 0, 0), memory_space=pltpu.SMEM)
    grid_spec = pltpu.PrefetchScalarGridSpec(
        num_scalar_prefetch=2,
        grid=(n_steps,),
        in_specs=[idx_spec(0), idx_spec(1), idx_spec(2), idx_spec(0), idx_spec(-1), idx_spec(-2),
                  pl.BlockSpec(memory_space=pl.ANY),
                  pl.BlockSpec((1, d, d_ff2), lambda i, be, nu: (expert(i, be, nu), 0, 0)),
                  pl.BlockSpec((1, 1, d_ff2), lambda i, be, nu: (expert(i, be, nu), 0, 0)),
                  pl.BlockSpec((1, d_ff2 // 2, d), lambda i, be, nu: (expert(i, be, nu), 0, 0)),
                  pl.BlockSpec((1, 1, d), lambda i, be, nu: (expert(i, be, nu), 0, 0))],
        out_specs=pl.BlockSpec(memory_space=pl.ANY),
        scratch_shapes=[pltpu.VMEM((NBUF, BM * SUBLANES, LANES), F32), pltpu.VMEM((NBUF, BM * SUBLANES, LANES), F32),
                        pltpu.VMEM((d, d_ff2), BF16), pltpu.VMEM((d_ff2 // 2, d), BF16),
                        pltpu.SemaphoreType.DMA((NBUF,)), pltpu.SemaphoreType.DMA((NBUF,))],
    )
    return pl.pallas_call(
        _experts_kernel,
        grid_spec=grid_spec,
        out_shape=jax.ShapeDtypeStruct((n_out_rows * SUBLANES, LANES), F32),
        compiler_params=pltpu.CompilerParams(dimension_semantics=("arbitrary",), vmem_limit_bytes=VMEM_LIMIT),
        name="experts",
    )(blk_e, nused, g_tab, g_tab, g_tab, s_tab, s_tab, s_tab, hn_all, w_gate_up, b_gu_perm, w_down, b_down)


def _combine_kernel(h_ref, gates_ref, g_ref, *refs):
    slot_refs, out_ref = refs[:TOP_K], refs[TOP_K]
    te, d = h_ref.shape
    gates = gates_ref[...]
    h = h_ref[...]
    chunks = []
    ssq = jnp.zeros((te, 1), F32)
    for s in range(d // LANES):
        acc = _load_token_tiles(slot_refs[0], te, s) * gates[:, 0:1]
        for k in range(1, TOP_K):
            acc = acc + _load_token_tiles(slot_refs[k], te, s) * gates[:, k:k + 1]
        y = h[:, s * LANES:(s + 1) * LANES] + acc
        ssq = ssq + jnp.sum(y * y, axis=-1, keepdims=True)
        chunks.append(y)
    rinv = lax.rsqrt(ssq / d + EPS)
    out_ref[...] = jnp.concatenate(chunks, axis=1) * rinv * g_ref[...]


def _combine(yk, h, gates, g_final, te, tile0, tiles_per_slot):
    n, d = h.shape
    slot_spec = lambda k: pl.BlockSpec((te * SUBLANES, LANES), lambda i: (k * tiles_per_slot + tile0 + i, 0))
    return pl.pallas_call(
        _combine_kernel,
        grid=(n // te,),
        in_specs=[pl.BlockSpec((te, d), lambda i: (i, 0)),
                  pl.BlockSpec((te, TOP_K), lambda i: (i, 0)),
                  pl.BlockSpec((1, d), lambda i: (0, 0))] + [slot_spec(k) for k in range(TOP_K)],
        out_specs=pl.BlockSpec((te, d), lambda i: (i, 0)),
        out_shape=jax.ShapeDtypeStruct((n, d), F32),
        compiler_params=pltpu.CompilerParams(vmem_limit_bytes=VMEM_LIMIT),
        name="combine",
    )(h, gates, g_final, *([yk] * TOP_K))


def _largest_tile(cands, *sizes):
    for c in cands:
        if all(s % c == 0 for s in sizes):
            return c
    raise ValueError(f"no tile in {cands} divides {sizes}")


def kernel(x_prompt, x_sample, mem_prompt, cache_win_k, cache_win_v, state_conv, cache_mem_k, cache_mem_v, g_attn_norm, w_in, conv_w, attn_sinks, g_mem_norm, w_mem_kv, g_mix_out, w_out, g_ffn_norm, w_router, b_router, w_gate_up, b_gate_up, w_down, b_down, g_final):
    depth = w_in.shape[0]
    assert depth == 1, "single-layer step"
    b, s, d = x_prompt.shape
    nb, t_dec, _ = x_sample.shape
    n_e = w_router.shape[2]
    d_ff2 = w_gate_up.shape[3]
    cw = conv_w.shape[2]
    win = cache_win_k.shape[2]
    m_tok = cache_mem_k.shape[2]
    assert s % TQ == 0 and nb % BB == 0 and win == WINDOW and t_dec <= SUBLANES and d_ff2 % (2 * LANES) == 0
    assert d == SUBLANES * LANES, "token-tile layout: one token is one (8, 128) f32 tile"

    row = lambda a: a.reshape(1, -1)
    w_in_bf = w_in[0].astype(BF16)
    w_out_bf = w_out[0].astype(BF16)
    w_r_bf = w_router[0].astype(BF16)
    sinks = attn_sinks[0].astype(F32)
    shared = (row(g_attn_norm[0]), w_in_bf, conv_w[0], row(g_mix_out[0]), w_out_bf, row(g_ffn_norm[0]),
              w_r_bf, row(b_router[0]))

    mk_p, mv_p, mkt, mvb = _memkv(mem_prompt, row(g_mem_norm[0]), w_mem_kv[0].astype(BF16))
    h_p, hn_p, tope_p, gates_p, lastk, lastv, convst = _mixer_p(x_prompt, sinks, *shared, mkt, mvb)

    zeros = lambda r: jnp.zeros((nb, r, cw), F32)
    st = state_conv[0]
    pm1 = jnp.concatenate([st[:, 1:2], zeros(t_dec - 1)], axis=1).reshape(nb * t_dec, cw)
    pm2 = jnp.concatenate([st, zeros(t_dec - 2)], axis=1).reshape(nb * t_dec, cw)
    h_s, hn_s, tope_s, gates_s, nwk, nwv, u_s = _mixer_s(
        x_sample.reshape(nb * t_dec, d), t_dec, pm1, pm2,
        cache_win_k[0].reshape(nb, win, KV_WIDTH), cache_win_v[0].reshape(nb, win, KV_WIDTH),
        cache_mem_k[0].reshape(nb, m_tok, MEM_WIDTH), cache_mem_v[0].reshape(nb, m_tok, MEM_WIDTH),
        sinks, *shared)

    n_p, n_s = b * s, nb * t_dec
    n = n_p + n_s
    tm = _largest_tile((512, 256, 128, 64, 32, 16, 8), n_p, n_s)
    hn_all = jnp.concatenate([hn_p, hn_s], axis=0)
    tope = jnp.concatenate([tope_p, tope_s], axis=0)

    rank, counts_f = _rank(tope, tm, n_e)
    counts = counts_f[0].astype(jnp.int32)
    padded = (counts + BM - 1) // BM * BM
    pad_ends = jnp.cumsum(padded)
    pad_starts = pad_ends - padded
    nk = n * TOP_K
    n_blocks = -(-nk // BM) + n_e
    nused = (pad_ends[-1:] // BM).astype(jnp.int32)
    blk_start = jnp.arange(n_blocks, dtype=jnp.int32) * BM
    blk_e = jnp.minimum(jnp.sum((pad_ends[None, :] <= blk_start[:, None]).astype(jnp.int32), axis=1), n_e - 1)
    dest = _dest(tope, rank, pad_starts.astype(F32).reshape(1, n_e), tm, n_e)
    lastblk = jnp.where(padded > 0, pad_ends // BM - 1, -1).astype(jnp.int32)
    inv = _invmap(dest, lastblk, nused, tm, n_blocks)

    spare = nk + jnp.arange(BM, dtype=jnp.int32)
    gidx = jnp.maximum(inv, 0) // TOP_K
    sidx = jnp.where(inv >= 0, (inv % TOP_K) * n + inv // TOP_K, jnp.tile(spare, n_blocks))
    g_tab = (jnp.concatenate([gidx, jnp.zeros((3 * BM,), jnp.int32)]) * SUBLANES).reshape(n_blocks + 3, 1, BM)
    s_tab = (jnp.concatenate([spare, sidx]) * SUBLANES).reshape(n_blocks + 1, 1, BM)

    grp = 2 * LANES
    b_gu = b_gate_up[0].reshape(n_e, d_ff2 // grp, LANES, 2).transpose(0, 1, 3, 2).reshape(n_e, 1, d_ff2)
    yk = _experts(hn_all, g_tab, s_tab, blk_e, nused, w_gate_up[0], b_gu, w_down[0],
                  b_down[0].reshape(n_e, 1, d), nk + BM)

    g_fin = row(g_final)
    y_p = _combine(yk, h_p, gates_p, g_fin, tm, 0, n // tm)
    y_s = _combine(yk, h_s, gates_s, g_fin, tm, n_p // tm, n // tm)

    kv5 = lambda a, bsz, r, hds: a.reshape(1, bsz, r, hds, HEAD_DIM)
    return (y_p.reshape(b, s, d), y_s.reshape(nb, t_dec, d),
            kv5(lastk, b, WINDOW, N_KV_HEADS), kv5(lastv, b, WINDOW, N_KV_HEADS),
            convst.reshape(1, b, 2, cw),
            kv5(mk_p, b, m_tok, N_MEM_HEADS), kv5(mv_p, b, m_tok, N_MEM_HEADS),
            kv5(nwk, nb, win, N_KV_HEADS), kv5(nwv, nb, win, N_KV_HEADS),
            u_s.reshape(nb, t_dec, cw)[:, t_dec - 2:].reshape(1, nb, 2, cw))
```

```python
import functools

import jax
import jax.numpy as jnp
from jax import lax
from jax.experimental import pallas as pl
from jax.experimental.pallas import tpu as pltpu

F32 = jnp.float32
BF16 = jnp.bfloat16

HEAD_DIM = 64
N_Q_HEADS = 8
N_KV_HEADS = 2
WINDOW = 128
ATTN_WIDTH = N_Q_HEADS * HEAD_DIM
KV_WIDTH = N_KV_HEADS * HEAD_DIM
N_MEM_HEADS = 4
MEM_WIDTH = N_MEM_HEADS * HEAD_DIM
TOP_K = 4
SWIGLU_LIMIT = 7.0
SWIGLU_ALPHA = 1.702
EPS = 1e-5
ATTN_SCALE = HEAD_DIM ** -0.5
ALIBI_SLOPES = tuple(2.0 ** (-8.0 * (h + 1) / N_Q_HEADS) for h in range(N_Q_HEADS))

LANES = 128
SUBLANES = 8
VMEM_LIMIT = 56 * 1024 * 1024

TQ = 256
BB = 16
BM = 256
REP = 8


def _rms(x):
    return x * lax.rsqrt(jnp.mean(x * x, axis=-1, keepdims=True) + EPS)


def _dot(a, b):
    return jnp.dot(a, b, preferred_element_type=F32)


def _dot_nt(a, b):
    return lax.dot_general(a, b, (((1,), (1,)), ((), ())), preferred_element_type=F32)


def _iota(shape, axis):
    return lax.broadcasted_iota(jnp.int32, shape, axis)


def _store_token_tiles(ref, x, base=0):
    t = x.shape[0]
    for s in range(x.shape[1] // LANES):
        ref[pl.ds(base + s, t, stride=SUBLANES), :] = x[:, s * LANES:(s + 1) * LANES]


def _load_token_tiles(ref, t, s, base=0):
    return ref[pl.ds(base + s, t, stride=SUBLANES), :]


def _memkv_kernel(mem_ref, g_ref, w_ref, mk_ref, mv_ref, mkt_ref, mvb_ref):
    xn = (_rms(mem_ref[0]) * g_ref[...]).astype(BF16)
    kv = _dot(xn, w_ref[...])
    mk = kv[:, :MEM_WIDTH]
    mv = kv[:, MEM_WIDTH:]
    mk_ref[0] = mk
    mv_ref[0] = mv
    mkt_ref[0] = mk.T.astype(BF16)
    mvb_ref[0] = mv.astype(BF16)


def _memkv(mem, g, w_bf):
    b, m, d = mem.shape
    out_f = jax.ShapeDtypeStruct((b, m, MEM_WIDTH), F32)
    out_b = jax.ShapeDtypeStruct((b, m, MEM_WIDTH), BF16)
    out_t = jax.ShapeDtypeStruct((b, MEM_WIDTH, m), BF16)
    blk = lambda r, c: pl.BlockSpec((1, r, c), lambda i: (i, 0, 0))
    return pl.pallas_call(
        _memkv_kernel,
        grid=(b,),
        in_specs=[blk(m, d), pl.BlockSpec((1, d), lambda i: (0, 0)),
                  pl.BlockSpec((d, 2 * MEM_WIDTH), lambda i: (0, 0))],
        out_specs=[blk(m, MEM_WIDTH), blk(m, MEM_WIDTH), blk(MEM_WIDTH, m), blk(m, MEM_WIDTH)],
        out_shape=[out_f, out_f, out_t, out_b],
        name="memkv",
    )(mem, g, w_bf)


def _router_topk(hn, w_r_ref, b_r_ref, tope_ref, gates_ref):
    n_e = w_r_ref.shape[1]
    logits = _dot(hn.astype(BF16), w_r_ref[...]) + b_r_ref[...]
    rows = logits.shape[0]
    col = _iota((rows, n_e), 1).astype(F32)
    vals, idxs = [], []
    cur = logits
    for _ in range(TOP_K):
        m = jnp.max(cur, axis=-1, keepdims=True)
        idx = jnp.min(jnp.where(cur == m, col, float(n_e)), axis=-1, keepdims=True)
        vals.append(m)
        idxs.append(idx)
        cur = jnp.where(col == idx, -jnp.inf, cur)
    exps = [jnp.exp(v - vals[0]) for v in vals]
    tot = exps[0] + exps[1] + exps[2] + exps[3]
    col4 = _iota((rows, TOP_K), 1)
    te = jnp.zeros((rows, TOP_K), F32)
    ga = jnp.zeros((rows, TOP_K), F32)
    for k in range(TOP_K):
        te = jnp.where(col4 == k, idxs[k], te)
        ga = jnp.where(col4 == k, exps[k] / tot, ga)
    tope_ref[...] = te.astype(jnp.int32)
    gates_ref[...] = ga


def _mix_out(x, attn, conv_out, cross, g_mix_ref, w_out_ref, g_ffn_ref, w_r_ref, b_r_ref,
             h_ref, hn_ref, tope_ref, gates_ref):
    mix = jnp.concatenate([_rms(attn), _rms(conv_out), _rms(cross)], axis=-1) * g_mix_ref[...]
    h = x + _dot(mix.astype(BF16), w_out_ref[...])
    hn = _rms(h) * g_ffn_ref[...]
    h_ref[...] = h
    _store_token_tiles(hn_ref, hn)
    _router_topk(hn, w_r_ref, b_r_ref, tope_ref, gates_ref)


def _swa_block(q_blk, kk, vv, prev_lim, sinks_ref):
    blk = WINDOW
    lane = _iota((2 * blk, KV_WIDTH), 1)
    lo = lane < HEAD_DIM
    kk_r = pltpu.roll(kk, HEAD_DIM, axis=1)
    vv_r = pltpu.roll(vv, HEAD_DIM, axis=1)
    kdup = [jnp.where(lo, kk, kk_r).astype(BF16), jnp.where(lo, kk_r, kk).astype(BF16)]
    vlo = [jnp.where(lo, vv, 0.0).astype(BF16), jnp.where(lo, vv_r, 0.0).astype(BF16)]
    vhi = [jnp.where(lo, 0.0, vv_r).astype(BF16), jnp.where(lo, 0.0, vv).astype(BF16)]
    qi = _iota((blk, 2 * blk), 0)
    kj = _iota((blk, 2 * blk), 1)
    dist = blk + qi - kj
    mask = (dist >= 0) & (dist < WINDOW) & (kj >= prev_lim)
    distf = dist.astype(F32)
    qlo = _iota((blk, 2 * HEAD_DIM), 1) < HEAD_DIM
    outs = []
    for p in range(N_Q_HEADS // 2):
        kh = (2 * p) // (N_Q_HEADS // N_KV_HEADS)
        qp = q_blk[:, p * 2 * HEAD_DIM:(p + 1) * 2 * HEAD_DIM]
        acc = None
        for e in range(2):
            h = 2 * p + e
            qm = jnp.where(qlo if e == 0 else jnp.logical_not(qlo), qp, 0.0).astype(BF16)
            s = _dot_nt(qm, kdup[kh]) - ALIBI_SLOPES[h] * distf
            s = jnp.where(mask, s, -jnp.inf)
            sink = sinks_ref[h]
            m = jnp.maximum(jnp.max(s, axis=-1, keepdims=True), sink)
            pe = jnp.exp(s - m)
            denom = jnp.sum(pe, axis=-1, keepdims=True) + jnp.exp(sink - m)
            o = _dot(pe.astype(BF16), (vlo if e == 0 else vhi)[kh]) / denom
            acc = o if acc is None else acc + o
        outs.append(acc)
    return jnp.concatenate(outs, axis=1)


def _mem_attend_shared(mq, mkt, mvb):
    t = mq.shape[0]
    m_tok = mvb.shape[0]
    qhead = _iota((t, MEM_WIDTH), 1) // HEAD_DIM
    vhead = _iota((m_tok, MEM_WIDTH), 1) // HEAD_DIM
    cross = None
    for h in range(N_MEM_HEADS):
        qm = jnp.where(qhead == h, mq, 0.0).astype(BF16)
        s = _dot(qm, mkt)
        m = jnp.max(s, axis=-1, keepdims=True)
        pe = jnp.exp(s - m)
        denom = jnp.sum(pe, axis=-1, keepdims=True)
        vm = jnp.where(vhead == h, mvb, jnp.zeros_like(mvb))
        o = _dot(pe.astype(BF16), vm) / denom
        cross = o if cross is None else cross + o
    return cross


def _mixer_p_kernel(sinks_ref, x_ref, g_attn_ref, w_in_ref, conv_w_ref, g_mix_ref, w_out_ref, g_ffn_ref,
                    w_r_ref, b_r_ref, mkt_ref, mvb_ref,
                    h_ref, hn_ref, tope_ref, gates_ref, lastk_ref, lastv_ref, convst_ref,
                    ck_ref, cv_ref, cu_ref):
    j = pl.program_id(1)
    nj = pl.num_programs(1)

    @pl.when(j == 0)
    def _():
        ck_ref[...] = jnp.zeros_like(ck_ref)
        cv_ref[...] = jnp.zeros_like(cv_ref)
        cu_ref[...] = jnp.zeros_like(cu_ref)

    x = x_ref[0]
    xn = (_rms(x) * g_attn_ref[...]).astype(BF16)
    z = _dot(xn, w_in_ref[...])
    c0 = ATTN_WIDTH
    c1 = c0 + KV_WIDTH
    c2 = c1 + KV_WIDTH
    cw = conv_w_ref.shape[1]
    c3, c4, c5 = c2 + cw, c2 + 2 * cw, c2 + 3 * cw
    q = z[:, :c0] * ATTN_SCALE
    k = z[:, c0:c1]
    v = z[:, c1:c2]
    cb = z[:, c2:c3]
    cc = z[:, c3:c4]
    cvv = z[:, c4:c5]
    mq = z[:, c5:] * ATTN_SCALE

    blk = WINDOW
    attn_blocks = []
    for i in range(TQ // blk):
        if i == 0:
            pk, pv = ck_ref[...], cv_ref[...]
            prev_lim = jnp.where(j > 0, 0, blk)
        else:
            pk, pv = k[(i - 1) * blk:i * blk], v[(i - 1) * blk:i * blk]
            prev_lim = 0
        kk = jnp.concatenate([pk, k[i * blk:(i + 1) * blk]], axis=0)
        vv = jnp.concatenate([pv, v[i * blk:(i + 1) * blk]], axis=0)
        attn_blocks.append(_swa_block(q[i * blk:(i + 1) * blk], kk, vv, prev_lim, sinks_ref))
    attn = jnp.concatenate(attn_blocks, axis=0)
    ck_ref[...] = k[TQ - blk:]
    cv_ref[...] = v[TQ - blk:]

    u = cc * cvv
    row = _iota(u.shape, 0)
    u1 = jnp.where(row == 0, cu_ref[SUBLANES - 1:SUBLANES, :], pltpu.roll(u, 1, axis=0))
    u2 = jnp.where(row == 0, cu_ref[SUBLANES - 2:SUBLANES - 1, :],
                   jnp.where(row == 1, cu_ref[SUBLANES - 1:SUBLANES, :], pltpu.roll(u, 2, axis=0)))
    conv_out = cb * (conv_w_ref[0:1, :] * u2 + conv_w_ref[1:2, :] * u1 + conv_w_ref[2:3, :] * u)
    cu_ref[...] = u[TQ - SUBLANES:]

    cross = _mem_attend_shared(mq, mkt_ref[0], mvb_ref[0])

    @pl.when(j == nj - 1)
    def _():
        lastk_ref[0] = k[TQ - blk:]
        lastv_ref[0] = v[TQ - blk:]
        convst_ref[0] = u[TQ - 2:]

    _mix_out(x, attn, conv_out, cross, g_mix_ref, w_out_ref, g_ffn_ref, w_r_ref, b_r_ref,
             h_ref, hn_ref, tope_ref, gates_ref)


def _mixer_p(x, sinks, g_attn, w_in, conv_w, g_mix, w_out, g_ffn, w_r, b_r, mkt, mvb):
    b, s, d = x.shape
    nj = s // TQ
    n = b * s
    cw = conv_w.shape[1]
    full = lambda a: pl.BlockSpec(a.shape, lambda bi, ji, *_: (0,) * a.ndim)
    tok = lambda w: pl.BlockSpec((TQ, w), lambda bi, ji, *_: (bi * nj + ji, 0))
    per_b = lambda r, c: pl.BlockSpec((1, r, c), lambda bi, ji, *_: (bi, 0, 0))
    grid_spec = pltpu.PrefetchScalarGridSpec(
        num_scalar_prefetch=1,
        grid=(b, nj),
        in_specs=[pl.BlockSpec((1, TQ, d), lambda bi, ji, *_: (bi, ji, 0)),
                  full(g_attn), full(w_in), full(conv_w), full(g_mix), full(w_out), full(g_ffn),
                  full(w_r), full(b_r), per_b(MEM_WIDTH, mkt.shape[2]), per_b(mvb.shape[1], MEM_WIDTH)],
        out_specs=[tok(d), pl.BlockSpec((TQ * SUBLANES, LANES), lambda bi, ji, *_: (bi * nj + ji, 0)),
                   tok(TOP_K), tok(TOP_K),
                   per_b(WINDOW, KV_WIDTH), per_b(WINDOW, KV_WIDTH), per_b(2, cw)],
        scratch_shapes=[pltpu.VMEM((WINDOW, KV_WIDTH), F32), pltpu.VMEM((WINDOW, KV_WIDTH), F32),
                        pltpu.VMEM((SUBLANES, cw), F32)],
    )
    return pl.pallas_call(
        _mixer_p_kernel,
        grid_spec=grid_spec,
        out_shape=[jax.ShapeDtypeStruct((n, d), F32), jax.ShapeDtypeStruct((n * SUBLANES, LANES), F32),
                   jax.ShapeDtypeStruct((n, TOP_K), jnp.int32), jax.ShapeDtypeStruct((n, TOP_K), F32),
                   jax.ShapeDtypeStruct((b, WINDOW, KV_WIDTH), F32),
                   jax.ShapeDtypeStruct((b, WINDOW, KV_WIDTH), F32),
                   jax.ShapeDtypeStruct((b, 2, cw), F32)],
        compiler_params=pltpu.CompilerParams(dimension_semantics=("arbitrary", "arbitrary"),
                                             vmem_limit_bytes=VMEM_LIMIT),
        name="mixer_p",
    )(sinks, x, g_attn, w_in, conv_w, g_mix, w_out, g_ffn, w_r, b_r, mkt, mvb)


def _per_head_column(values, hrow):
    col = jnp.zeros(hrow.shape, F32)
    for h in range(N_Q_HEADS):
        col = jnp.where(hrow == h, values[h], col)
    return col


def _mixer_s_kernel(sinks_ref, x_ref, pm1_ref, pm2_ref, wk_ref, wv_ref, mk_ref, mv_ref,
                    g_attn_ref, w_in_ref, conv_w_ref, g_mix_ref, w_out_ref, g_ffn_ref, w_r_ref, b_r_ref,
                    h_ref, hn_ref, tope_ref, gates_ref, nwk_ref, nwv_ref, u_ref, *, t_dec):
    r_tok = BB * t_dec
    r_exp = r_tok * REP
    qrows = t_dec * REP
    x = x_ref[...]
    xn = (_rms(x) * g_attn_ref[...]).astype(BF16)
    z = _dot(xn, w_in_ref[...])
    c0 = ATTN_WIDTH
    c1 = c0 + KV_WIDTH
    c2 = c1 + KV_WIDTH
    cw = conv_w_ref.shape[1]
    c3, c4, c5 = c2 + cw, c2 + 2 * cw, c2 + 3 * cw
    q = z[:, :c0] * ATTN_SCALE
    k_new = z[:, c0:c1]
    v_new = z[:, c1:c2]
    cb = z[:, c2:c3]
    cc = z[:, c3:c4]
    cvv = z[:, c4:c5]
    mq = z[:, c5:] * ATTN_SCALE
    win = wk_ref.shape[1]

    xi = _iota((KV_WIDTH, ATTN_WIDTH), 0)
    xl = _iota((KV_WIDTH, ATTN_WIDTH), 1)
    q_per_kv = N_Q_HEADS // N_KV_HEADS
    expand = (xi == (xl // (q_per_kv * HEAD_DIM)) * HEAD_DIM + xl % HEAD_DIM).astype(BF16)
    rr = _iota((r_exp, r_tok), 0)
    rc = _iota((r_exp, r_tok), 1)
    rep = (rr // REP == rc).astype(BF16)

    hrow = _iota((r_exp, 1), 0) % REP
    trow = (_iota((r_exp, 1), 0) // REP) % t_dec
    slope_col = _per_head_column(ALIBI_SLOPES, hrow)
    sink_col = _per_head_column([sinks_ref[h] for h in range(N_Q_HEADS)], hrow)

    qexp = jnp.where(hrow == _iota((r_exp, ATTN_WIDTH), 1) // HEAD_DIM, _dot(rep, q.astype(BF16)), 0.0)
    kexp = _dot(wk_ref[...].reshape(BB * win, KV_WIDTH).astype(BF16), expand).astype(BF16)
    vexp = _dot(wv_ref[...].reshape(BB * win, KV_WIDTH).astype(BF16), expand).astype(BF16)
    s = jnp.einsum("bqc,bkc->bqk", qexp.astype(BF16).reshape(BB, qrows, ATTN_WIDTH),
                   kexp.reshape(BB, win, ATTN_WIDTH), preferred_element_type=F32).reshape(r_exp, win)
    scol = _iota((r_exp, win), 1)
    s = s - slope_col * (win + trow - scol).astype(F32)
    s = jnp.where(scol > trow, s, -jnp.inf)
    knew_exp = _dot(k_new.astype(BF16), expand).astype(BF16)
    vnew_exp = _dot(v_new.astype(BF16), expand).astype(BF16)
    s_new, v_rep = [], []
    for jn in range(t_dec):
        rep_j = (rc == (rr // qrows) * t_dec + jn).astype(BF16)
        k_rep = _dot(rep_j, knew_exp)
        v_rep.append(_dot(rep_j, vnew_exp))
        sj = jnp.sum(qexp * k_rep, axis=-1, keepdims=True) - slope_col * (trow - jn).astype(F32)
        s_new.append(jnp.where(trow >= jn, sj, -jnp.inf))
    m = jnp.maximum(jnp.max(s, axis=-1, keepdims=True), sink_col)
    for sj in s_new:
        m = jnp.maximum(m, sj)
    pe = jnp.exp(s - m)
    denom = jnp.sum(pe, axis=-1, keepdims=True) + jnp.exp(sink_col - m)
    o = jnp.einsum("bqk,bkc->bqc", pe.astype(BF16).reshape(BB, qrows, win),
                   vexp.reshape(BB, win, ATTN_WIDTH), preferred_element_type=F32).reshape(r_exp, ATTN_WIDTH)
    for jn in range(t_dec):
        pj = jnp.exp(s_new[jn] - m)
        denom = denom + pj
        o = o + pj * v_rep[jn]
    o = jnp.where(hrow == _iota((r_exp, ATTN_WIDTH), 1) // HEAD_DIM, o / denom, 0.0)
    attn = jnp.sum(o.reshape(r_tok, REP, ATTN_WIDTH), axis=1)

    m_tok = mk_ref.shape[1]
    mhead = _iota((r_exp, MEM_WIDTH), 1) // HEAD_DIM
    mqexp = jnp.where(hrow == mhead, _dot(rep, mq.astype(BF16)), 0.0).astype(BF16)
    sm = jnp.einsum("bqc,bmc->bqm", mqexp.reshape(BB, qrows, MEM_WIDTH), mk_ref[...].astype(BF16),
                    preferred_element_type=F32).reshape(r_exp, m_tok)
    mm = jnp.max(sm, axis=-1, keepdims=True)
    pm = jnp.exp(sm - mm)
    dm = jnp.sum(pm, axis=-1, keepdims=True)
    om = jnp.einsum("bqm,bmc->bqc", pm.astype(BF16).reshape(BB, qrows, m_tok), mv_ref[...].astype(BF16),
                    preferred_element_type=F32).reshape(r_exp, MEM_WIDTH)
    om = jnp.where(hrow == mhead, om / dm, 0.0)
    cross = jnp.sum(om.reshape(r_tok, REP, MEM_WIDTH), axis=1)

    u = cc * cvv
    tt = _iota(u.shape, 0) % t_dec
    u1 = jnp.where(tt >= 1, pltpu.roll(u, 1, axis=0), pm1_ref[...])
    u2 = jnp.where(tt >= 2, pltpu.roll(u, 2, axis=0), pm2_ref[...])
    conv_out = cb * (conv_w_ref[0:1, :] * u2 + conv_w_ref[1:2, :] * u1 + conv_w_ref[2:3, :] * u)
    u_ref[...] = u

    nwk_ref[:, 0:win - t_dec, :] = wk_ref[:, t_dec:win, :]
    nwv_ref[:, 0:win - t_dec, :] = wv_ref[:, t_dec:win, :]
    for b in range(BB):
        nwk_ref[b, win - t_dec:win, :] = k_new[b * t_dec:(b + 1) * t_dec, :]
        nwv_ref[b, win - t_dec:win, :] = v_new[b * t_dec:(b + 1) * t_dec, :]

    _mix_out(x, attn, conv_out, cross, g_mix_ref, w_out_ref, g_ffn_ref, w_r_ref, b_r_ref,
             h_ref, hn_ref, tope_ref, gates_ref)


def _mixer_s(x2, t_dec, pm1, pm2, wk, wv, mk, mv, sinks, g_attn, w_in, conv_w, g_mix, w_out, g_ffn, w_r, b_r):
    n, d = x2.shape
    nb = wk.shape[0]
    win = wk.shape[1]
    m_tok = mk.shape[1]
    cw = conv_w.shape[1]
    r_tok = BB * t_dec
    full = lambda a: pl.BlockSpec(a.shape, lambda i, *_: (0,) * a.ndim)
    tok = lambda w: pl.BlockSpec((r_tok, w), lambda i, *_: (i, 0))
    per_b = lambda r, c: pl.BlockSpec((BB, r, c), lambda i, *_: (i, 0, 0))
    grid_spec = pltpu.PrefetchScalarGridSpec(
        num_scalar_prefetch=1,
        grid=(nb // BB,),
        in_specs=[tok(d), tok(cw), tok(cw), per_b(win, KV_WIDTH), per_b(win, KV_WIDTH),
                  per_b(m_tok, MEM_WIDTH), per_b(m_tok, MEM_WIDTH),
                  full(g_attn), full(w_in), full(conv_w), full(g_mix), full(w_out), full(g_ffn),
                  full(w_r), full(b_r)],
        out_specs=[tok(d), pl.BlockSpec((r_tok * SUBLANES, LANES), lambda i, *_: (i, 0)),
                   tok(TOP_K), tok(TOP_K), per_b(win, KV_WIDTH), per_b(win, KV_WIDTH), tok(cw)],
    )
    return pl.pallas_call(
        functools.partial(_mixer_s_kernel, t_dec=t_dec),
        grid_spec=grid_spec,
        out_shape=[jax.ShapeDtypeStruct((n, d), F32), jax.ShapeDtypeStruct((n * SUBLANES, LANES), F32),
                   jax.ShapeDtypeStruct((n, TOP_K), jnp.int32), jax.ShapeDtypeStruct((n, TOP_K), F32),
                   jax.ShapeDtypeStruct((nb, win, KV_WIDTH), F32), jax.ShapeDtypeStruct((nb, win, KV_WIDTH), F32),
                   jax.ShapeDtypeStruct((n, cw), F32)],
        compiler_params=pltpu.CompilerParams(dimension_semantics=("arbitrary",), vmem_limit_bytes=VMEM_LIMIT),
        name="mixer_s",
    )(sinks, x2, pm1, pm2, wk, wv, mk, mv, g_attn, w_in, conv_w, g_mix, w_out, g_ffn, w_r, b_r)


def _rank_kernel(tope_ref, rank_ref, counts_ref, tri_ref, *, n_e):
    i = pl.program_id(0)
    tm = tope_ref.shape[0]

    @pl.when(i == 0)
    def _():
        tri_ref[...] = (_iota((tm, tm), 0) > _iota((tm, tm), 1)).astype(BF16)

    te = tope_ref[...]
    col = _iota((tm, n_e), 1)
    hits = [te[:, k:k + 1] == col for k in range(TOP_K)]
    onehot = jnp.zeros((tm, n_e), F32)
    for hk in hits:
        onehot = onehot + hk.astype(F32)
    before = _dot(tri_ref[...], onehot.astype(BF16))
    col4 = _iota((tm, TOP_K), 1)
    rank = jnp.zeros((tm, TOP_K), F32)
    for k in range(TOP_K):
        rank = jnp.where(col4 == k, jnp.sum(jnp.where(hits[k], before, 0.0), axis=-1, keepdims=True), rank)
    rank_ref[...] = rank.astype(jnp.int32)
    counts_ref[0] = jnp.sum(onehot, axis=0, keepdims=True)


def _rank(tope, tm, n_e):
    n = tope.shape[0]
    return pl.pallas_call(
        functools.partial(_rank_kernel, n_e=n_e),
        grid=(n // tm,),
        in_specs=[pl.BlockSpec((tm, TOP_K), lambda i: (i, 0))],
        out_specs=[pl.BlockSpec((tm, TOP_K), lambda i: (i, 0)), pl.BlockSpec((1, 1, n_e), lambda i: (i, 0, 0))],
        out_shape=[jax.ShapeDtypeStruct((n, TOP_K), jnp.int32), jax.ShapeDtypeStruct((n // tm, 1, n_e), F32)],
        scratch_shapes=[pltpu.VMEM((tm, tm), BF16)],
        compiler_params=pltpu.CompilerParams(dimension_semantics=("arbitrary",)),
        name="rank",
    )(tope)


def _lpos_kernel(tope_ref, rank_ref, start_ref, lpos_ref, *, n_e):
    tm = tope_ref.shape[0]
    te = tope_ref[...]
    col = _iota((tm, n_e), 1)
    col4 = _iota((tm, TOP_K), 1)
    base = jnp.zeros((tm, TOP_K), F32)
    for k in range(TOP_K):
        sel = jnp.sum(jnp.where(te[:, k:k + 1] == col, start_ref[0], 0.0), axis=-1, keepdims=True)
        base = jnp.where(col4 == k, sel, base)
    lpos_ref[...] = (base.astype(jnp.int32) + rank_ref[...]) * SUBLANES


def _lpos(tope, rank, lstart_f, tm, n_e):
    n = tope.shape[0]
    return pl.pallas_call(
        functools.partial(_lpos_kernel, n_e=n_e),
        grid=(n // tm,),
        in_specs=[pl.BlockSpec((tm, TOP_K), lambda i: (i, 0)), pl.BlockSpec((tm, TOP_K), lambda i: (i, 0)),
                  pl.BlockSpec((1, 1, n_e), lambda i: (i, 0, 0))],
        out_specs=pl.BlockSpec((tm, TOP_K), lambda i: (i, 0)),
        out_shape=jax.ShapeDtypeStruct((n, TOP_K), jnp.int32),
        name="lpos",
    )(tope, rank, lstart_f)


def _rows(ref, first_row, n_rows):
    return ref.at[pl.ds(pl.multiple_of(first_row * SUBLANES, SUBLANES), n_rows * SUBLANES)]


def _group_chunks(meta, g, n_e, max_rows, make_copy, wait):
    off_ref, cnt_ref, lst_ref = meta
    sizes = [1 << b for b in range(max_rows.bit_length() - 1, -1, -1)]

    def per_expert(e, c):
        cnt = cnt_ref[g * n_e + e]
        off = off_ref[g * n_e + e]
        lst = lst_ref[g * n_e + e]
        for size in sizes:
            @pl.when((cnt & size) != 0)
            def _():
                done = (cnt // (2 * size)) * (2 * size)
                cp = make_copy(lst + done, off + done, size)
                if wait:
                    cp.wait()
                else:
                    cp.start()
        return c

    lax.fori_loop(0, n_e, per_expert, 0)


def _dispatch_kernel(off_ref, cnt_ref, lst_ref, lastblk_ref, nused_ref, lpos_ref, hn_ref, xs_ref,
                     stage_ref, zbuf_ref, sem, zsem, *, n_e, n_blocks):
    j = pl.program_id(0)
    nj = pl.num_programs(0)
    tm = hn_ref.shape[0] // SUBLANES
    slot = j % 2
    meta = (off_ref, cnt_ref, lst_ref)

    def chunks(g, s, wait):
        _group_chunks(meta, g, n_e, tm,
                      lambda lrow, grow, size: pltpu.make_async_copy(
                          _rows(stage_ref.at[s], lrow, size), _rows(xs_ref, grow, size), sem.at[s]), wait)

    def zero_copy(blk):
        return pltpu.make_async_copy(zbuf_ref, _rows(xs_ref, blk * BM, BM), zsem)

    @pl.when(j == 0)
    def _():
        zbuf_ref[...] = jnp.zeros_like(zbuf_ref)

        def expert_fill(wait):
            def body(e, c):
                @pl.when(lastblk_ref[e] >= 0)
                def _():
                    cp = zero_copy(lastblk_ref[e])
                    cp.wait() if wait else cp.start()
                return c
            return body

        def tail_fill(wait):
            def body(b, c):
                cp = zero_copy(b)
                cp.wait() if wait else cp.start()
                return c
            return body

        lax.fori_loop(0, n_e, expert_fill(False), 0)
        lax.fori_loop(nused_ref[0], n_blocks, tail_fill(False), 0)
        lax.fori_loop(0, n_e, expert_fill(True), 0)
        lax.fori_loop(nused_ref[0], n_blocks, tail_fill(True), 0)

    @pl.when(j >= 2)
    def _():
        chunks(j - 2, slot, True)

    def place(t, c):
        tile = hn_ref[pl.ds(pl.multiple_of(t * SUBLANES, SUBLANES), SUBLANES), :]
        for k in range(TOP_K):
            pos = pl.multiple_of(lpos_ref[0, 0, t * TOP_K + k], SUBLANES)
            stage_ref[slot, pl.ds(pos, SUBLANES), :] = tile
        return c

    lax.fori_loop(0, tm, place, 0, unroll=8)
    chunks(j, slot, False)

    @pl.when(j == nj - 1)
    def _():
        @pl.when(j >= 1)
        def _():
            chunks(j - 1, 1 - slot, True)
        chunks(j, slot, True)


def _dispatch(hn_all, lpos, meta, lastblk, nused, tm, n_e, n_blocks):
    nt = lpos.shape[0] // tm
    grid_spec = pltpu.PrefetchScalarGridSpec(
        num_scalar_prefetch=5,
        grid=(nt,),
        in_specs=[pl.BlockSpec((1, 1, tm * TOP_K), lambda j, *_: (j, 0, 0), memory_space=pltpu.SMEM),
                  pl.BlockSpec((tm * SUBLANES, LANES), lambda j, *_: (j, 0))],
        out_specs=pl.BlockSpec(memory_space=pl.ANY),
        scratch_shapes=[pltpu.VMEM((2, tm * TOP_K * SUBLANES, LANES), F32), pltpu.VMEM((BM * SUBLANES, LANES), F32),
                        pltpu.SemaphoreType.DMA((2,)), pltpu.SemaphoreType.DMA(())],
    )
    return pl.pallas_call(
        functools.partial(_dispatch_kernel, n_e=n_e, n_blocks=n_blocks),
        grid_spec=grid_spec,
        out_shape=jax.ShapeDtypeStruct((n_blocks * BM * SUBLANES, LANES), F32),
        compiler_params=pltpu.CompilerParams(dimension_semantics=("arbitrary",), vmem_limit_bytes=VMEM_LIMIT),
        name="dispatch",
    )(*meta, lastblk, nused, lpos.reshape(nt, 1, tm * TOP_K), hn_all)


def _experts_kernel(blk_e_ref, nused_ref, xs_ref, wgu_ref, bgu_ref, wd_ref, bd_ref, ys_ref, wgu_bf_ref, wd_bf_ref):
    i = pl.program_id(0)
    nused = nused_ref[0]
    d_ff2 = wgu_ref.shape[2]
    grp = 2 * LANES

    @pl.when(i < nused)
    def _():
        e = blk_e_ref[i]
        e_prev = blk_e_ref[jnp.maximum(i - 1, 0)]

        @pl.when((i == 0) | (e != e_prev))
        def _():
            pr = _iota((grp, grp), 0)
            pc = _iota((grp, grp), 1)
            perm = (pr == jnp.where(pc < LANES, 2 * pc, 2 * (pc - LANES) + 1)).astype(BF16)
            for g in range(d_ff2 // grp):
                w = wgu_ref[0, :, g * grp:(g + 1) * grp].astype(BF16)
                wgu_bf_ref[:, g * grp:(g + 1) * grp] = _dot(w, perm).astype(BF16)
            wd_bf_ref[...] = wd_ref[0].astype(BF16)

        x = jnp.concatenate([_load_token_tiles(xs_ref, BM, s).astype(BF16) for s in range(SUBLANES)], axis=1)
        hgu = _dot(x, wgu_bf_ref[...]) + bgu_ref[0]
        acts = []
        for g in range(d_ff2 // grp):
            gate = jnp.minimum(hgu[:, g * grp:g * grp + LANES], SWIGLU_LIMIT)
            up = jnp.clip(hgu[:, g * grp + LANES:(g + 1) * grp], -SWIGLU_LIMIT, SWIGLU_LIMIT)
            glu = gate * (1.0 / (1.0 + jnp.exp(-SWIGLU_ALPHA * gate)))
            acts.append(((up + 1.0) * glu).astype(BF16))
        act = jnp.concatenate(acts, axis=1)
        _store_token_tiles(ys_ref, _dot(act, wd_bf_ref[...]) + bd_ref[0])

    @pl.when(i >= nused)
    def _():
        ys_ref[...] = jnp.zeros_like(ys_ref)


def _experts(xs, blk_e, nused, w_gate_up, b_gu_perm, w_down, b_down):
    n_e, d, d_ff2 = w_gate_up.shape
    n_blocks = xs.shape[0] // (BM * SUBLANES)
    expert = lambda i, be, nu: be[jnp.minimum(i, jnp.maximum(nu[0] - 1, 0))]
    rows_spec = pl.BlockSpec((BM * SUBLANES, LANES), lambda i, be, nu: (i, 0))
    grid_spec = pltpu.PrefetchScalarGridSpec(
        num_scalar_prefetch=2,
        grid=(n_blocks,),
        in_specs=[rows_spec,
                  pl.BlockSpec((1, d, d_ff2), lambda i, be, nu: (expert(i, be, nu), 0, 0)),
                  pl.BlockSpec((1, 1, d_ff2), lambda i, be, nu: (expert(i, be, nu), 0, 0)),
                  pl.BlockSpec((1, d_ff2 // 2, d), lambda i, be, nu: (expert(i, be, nu), 0, 0)),
                  pl.BlockSpec((1, 1, d), lambda i, be, nu: (expert(i, be, nu), 0, 0))],
        out_specs=rows_spec,
        scratch_shapes=[pltpu.VMEM((d, d_ff2), BF16), pltpu.VMEM((d_ff2 // 2, d), BF16)],
    )
    return pl.pallas_call(
        _experts_kernel,
        grid_spec=grid_spec,
        out_shape=jax.ShapeDtypeStruct(xs.shape, F32),
        compiler_params=pltpu.CompilerParams(dimension_semantics=("arbitrary",), vmem_limit_bytes=VMEM_LIMIT),
        name="experts",
    )(blk_e, nused, xs, w_gate_up, b_gu_perm, w_down, b_down)


def _combine_kernel(off_ref, cnt_ref, lst_ref, lpos_ref, gates_ref, h_ref, g_ref, ys_ref, out_ref,
                    stage_ref, acc_ref, sem, *, n_e, g0):
    i = pl.program_id(0)
    ng = pl.num_programs(0)
    tm, d = h_ref.shape
    slot = i % 2
    meta = (off_ref, cnt_ref, lst_ref)

    def chunks(g, s, wait):
        _group_chunks(meta, g, n_e, tm,
                      lambda lrow, grow, size: pltpu.make_async_copy(
                          _rows(ys_ref, grow, size), _rows(stage_ref.at[s], lrow, size), sem.at[s]), wait)

    @pl.when(i == 0)
    def _():
        chunks(g0, slot, False)

    @pl.when(i + 1 < ng)
    def _():
        chunks(g0 + i + 1, 1 - slot, False)

    chunks(g0 + i, slot, True)

    def mix(t, c):
        acc = None
        for k in range(TOP_K):
            pos = pl.multiple_of(lpos_ref[0, 0, t * TOP_K + k], SUBLANES)
            term = stage_ref[slot, pl.ds(pos, SUBLANES), :] * gates_ref[0, 0, t * TOP_K + k]
            acc = term if acc is None else acc + term
        acc_ref[pl.ds(pl.multiple_of(t * SUBLANES, SUBLANES), SUBLANES), :] = acc
        return c

    lax.fori_loop(0, tm, mix, 0, unroll=8)

    h = h_ref[...]
    parts = []
    ssq = jnp.zeros((tm, 1), F32)
    for s in range(d // LANES):
        y = h[:, s * LANES:(s + 1) * LANES] + _load_token_tiles(acc_ref, tm, s)
        ssq = ssq + jnp.sum(y * y, axis=-1, keepdims=True)
        parts.append(y)
    rinv = lax.rsqrt(ssq / d + EPS)
    out_ref[...] = jnp.concatenate(parts, axis=1) * rinv * g_ref[...]


def _combine(ys, h, lpos, gates, meta, g_final, tm, n_e, g0):
    n, d = h.shape
    grid_spec = pltpu.PrefetchScalarGridSpec(
        num_scalar_prefetch=3,
        grid=(n // tm,),
        in_specs=[pl.BlockSpec((1, 1, tm * TOP_K), lambda i, *_: (g0 + i, 0, 0), memory_space=pltpu.SMEM),
                  pl.BlockSpec((1, 1, tm * TOP_K), lambda i, *_: (g0 + i, 0, 0), memory_space=pltpu.SMEM),
                  pl.BlockSpec((tm, d), lambda i, *_: (i, 0)),
                  pl.BlockSpec((1, d), lambda i, *_: (0, 0)),
                  pl.BlockSpec(memory_space=pl.ANY)],
        out_specs=pl.BlockSpec((tm, d), lambda i, *_: (i, 0)),
        scratch_shapes=[pltpu.VMEM((2, tm * TOP_K * SUBLANES, LANES), F32), pltpu.VMEM((tm * SUBLANES, LANES), F32),
                        pltpu.SemaphoreType.DMA((2,))],
    )
    ngroups = lpos.shape[0] // tm
    return pl.pallas_call(
        functools.partial(_combine_kernel, n_e=n_e, g0=g0),
        grid_spec=grid_spec,
        out_shape=jax.ShapeDtypeStruct((n, d), F32),
        compiler_params=pltpu.CompilerParams(dimension_semantics=("arbitrary",), vmem_limit_bytes=VMEM_LIMIT),
        name="combine",
    )(*meta, lpos.reshape(ngroups, 1, tm * TOP_K), gates.reshape(ngroups, 1, tm * TOP_K), h, g_final, ys)


def _largest_tile(cands, *sizes):
    for c in cands:
        if all(s % c == 0 for s in sizes):
            return c
    raise ValueError(f"no tile in {cands} divides {sizes}")


def kernel(x_prompt, x_sample, mem_prompt, cache_win_k, cache_win_v, state_conv, cache_mem_k, cache_mem_v, g_attn_norm, w_in, conv_w, attn_sinks, g_mem_norm, w_mem_kv, g_mix_out, w_out, g_ffn_norm, w_router, b_router, w_gate_up, b_gate_up, w_down, b_down, g_final):
    depth = w_in.shape[0]
    assert depth == 1, "single-layer step"
    b, s, d = x_prompt.shape
    nb, t_dec, _ = x_sample.shape
    n_e = w_router.shape[2]
    d_ff2 = w_gate_up.shape[3]
    cw = conv_w.shape[2]
    win = cache_win_k.shape[2]
    m_tok = cache_mem_k.shape[2]
    assert s % TQ == 0 and nb % BB == 0 and win == WINDOW and t_dec <= SUBLANES and d_ff2 % (2 * LANES) == 0
    assert d == SUBLANES * LANES, "token-tile layout: one token is one (8, 128) f32 tile"

    row = lambda a: a.reshape(1, -1)
    w_in_bf = w_in[0].astype(BF16)
    w_out_bf = w_out[0].astype(BF16)
    w_r_bf = w_router[0].astype(BF16)
    sinks = attn_sinks[0].astype(F32)
    shared = (row(g_attn_norm[0]), w_in_bf, conv_w[0], row(g_mix_out[0]), w_out_bf, row(g_ffn_norm[0]),
              w_r_bf, row(b_router[0]))

    mk_p, mv_p, mkt, mvb = _memkv(mem_prompt, row(g_mem_norm[0]), w_mem_kv[0].astype(BF16))
    h_p, hn_p, tope_p, gates_p, lastk, lastv, convst = _mixer_p(x_prompt, sinks, *shared, mkt, mvb)

    zeros = lambda r: jnp.zeros((nb, r, cw), F32)
    st = state_conv[0]
    pm1 = jnp.concatenate([st[:, 1:2], zeros(t_dec - 1)], axis=1).reshape(nb * t_dec, cw)
    pm2 = jnp.concatenate([st, zeros(t_dec - 2)], axis=1).reshape(nb * t_dec, cw)
    h_s, hn_s, tope_s, gates_s, nwk, nwv, u_s = _mixer_s(
        x_sample.reshape(nb * t_dec, d), t_dec, pm1, pm2,
        cache_win_k[0].reshape(nb, win, KV_WIDTH), cache_win_v[0].reshape(nb, win, KV_WIDTH),
        cache_mem_k[0].reshape(nb, m_tok, MEM_WIDTH), cache_mem_v[0].reshape(nb, m_tok, MEM_WIDTH),
        sinks, *shared)

    n_p, n_s = b * s, nb * t_dec
    n = n_p + n_s
    tm = _largest_tile((512, 256, 128, 64, 32, 16, 8), n_p, n_s)
    hn_all = jnp.concatenate([hn_p, hn_s], axis=0)
    tope = jnp.concatenate([tope_p, tope_s], axis=0)
    gates = jnp.concatenate([gates_p, gates_s], axis=0)

    rank, cnt_f = _rank(tope, tm, n_e)
    cnt = cnt_f[:, 0, :].astype(jnp.int32)
    counts = jnp.sum(cnt, axis=0)
    padded = (counts + BM - 1) // BM * BM
    pad_ends = jnp.cumsum(padded)
    pad_starts = pad_ends - padded
    nk = n * TOP_K
    n_blocks = -(-nk // BM) + n_e
    nused = (pad_ends[-1:] // BM).astype(jnp.int32)
    blk_start = jnp.arange(n_blocks, dtype=jnp.int32) * BM
    blk_e = jnp.minimum(jnp.sum((pad_ends[None, :] <= blk_start[:, None]).astype(jnp.int32), axis=1), n_e - 1)
    lastblk = jnp.where(padded > 0, pad_ends // BM - 1, -1).astype(jnp.int32)
    off = pad_starts[None, :] + jnp.cumsum(cnt, axis=0) - cnt
    lstart = jnp.cumsum(cnt, axis=1) - cnt
    meta = (off.reshape(-1).astype(jnp.int32), cnt.reshape(-1), lstart.reshape(-1).astype(jnp.int32))
    lpos = _lpos(tope, rank, lstart.astype(F32).reshape(n // tm, 1, n_e), tm, n_e)

    xs = _dispatch(hn_all, lpos, meta, lastblk, nused, tm, n_e, n_blocks)

    grp = 2 * LANES
    b_gu = b_gate_up[0].reshape(n_e, d_ff2 // grp, LANES, 2).transpose(0, 1, 3, 2).reshape(n_e, 1, d_ff2)
    ys = _experts(xs, blk_e, nused, w_gate_up[0], b_gu, w_down[0], b_down[0].reshape(n_e, 1, d))

    g_fin = row(g_final)
    y_p = _combine(ys, h_p, lpos, gates, meta, g_fin, tm, n_e, 0)
    y_s = _combine(ys, h_s, lpos, gates, meta, g_fin, tm, n_e, n_p // tm)

    kv5 = lambda a, bsz, r, hds: a.reshape(1, bsz, r, hds, HEAD_DIM)
    return (y_p.reshape(b, s, d), y_s.reshape(nb, t_dec, d),
            kv5(lastk, b, WINDOW, N_KV_HEADS), kv5(lastv, b, WINDOW, N_KV_HEADS),
            convst.reshape(1, b, 2, cw),
            kv5(mk_p, b, m_tok, N_MEM_HEADS), kv5(mv_p, b, m_tok, N_MEM_HEADS),
            kv5(nwk, nb, win, N_KV_HEADS), kv5(nwv, nb, win, N_KV_HEADS),
            u_s.reshape(nb, t_dec, cw)[:, t_dec - 2:].reshape(1, nb, 2, cw))
```

```python
import functools

import jax
import jax.numpy as jnp
from jax import lax
from jax.experimental import pallas as pl
from jax.experimental.pallas import tpu as pltpu

F32 = jnp.float32
BF16 = jnp.bfloat16

HEAD_DIM = 64
N_Q_HEADS = 8
N_KV_HEADS = 2
WINDOW = 128
ATTN_WIDTH = N_Q_HEADS * HEAD_DIM
KV_WIDTH = N_KV_HEADS * HEAD_DIM
N_MEM_HEADS = 4
MEM_WIDTH = N_MEM_HEADS * HEAD_DIM
TOP_K = 4
SWIGLU_LIMIT = 7.0
SWIGLU_ALPHA = 1.702
EPS = 1e-5
ATTN_SCALE = HEAD_DIM ** -0.5
ALIBI_SLOPES = tuple(2.0 ** (-8.0 * (h + 1) / N_Q_HEADS) for h in range(N_Q_HEADS))

LANES = 128
SUBLANES = 8
VMEM_LIMIT = 56 * 1024 * 1024

TQ = 256
BB = 16
BM = 256
REP = 8


def _rms(x):
    return x * lax.rsqrt(jnp.mean(x * x, axis=-1, keepdims=True) + EPS)


def _dot(a, b):
    return jnp.dot(a, b, preferred_element_type=F32)


def _dot_nt(a, b):
    return lax.dot_general(a, b, (((1,), (1,)), ((), ())), preferred_element_type=F32)


def _iota(shape, axis):
    return lax.broadcasted_iota(jnp.int32, shape, axis)


def _store_token_tiles(ref, x, base=0):
    t = x.shape[0]
    for s in range(x.shape[1] // LANES):
        ref[pl.ds(base + s, t, stride=SUBLANES), :] = x[:, s * LANES:(s + 1) * LANES]


def _load_token_tiles(ref, t, s, base=0):
    return ref[pl.ds(base + s, t, stride=SUBLANES), :]


def _memkv_kernel(mem_ref, g_ref, w_ref, mk_ref, mv_ref, mkt_ref, mvb_ref):
    xn = (_rms(mem_ref[0]) * g_ref[...]).astype(BF16)
    kv = _dot(xn, w_ref[...])
    mk = kv[:, :MEM_WIDTH]
    mv = kv[:, MEM_WIDTH:]
    mk_ref[0] = mk
    mv_ref[0] = mv
    mkt_ref[0] = mk.T.astype(BF16)
    mvb_ref[0] = mv.astype(BF16)


def _memkv(mem, g, w_bf):
    b, m, d = mem.shape
    out_f = jax.ShapeDtypeStruct((b, m, MEM_WIDTH), F32)
    out_b = jax.ShapeDtypeStruct((b, m, MEM_WIDTH), BF16)
    out_t = jax.ShapeDtypeStruct((b, MEM_WIDTH, m), BF16)
    blk = lambda r, c: pl.BlockSpec((1, r, c), lambda i: (i, 0, 0))
    return pl.pallas_call(
        _memkv_kernel,
        grid=(b,),
        in_specs=[blk(m, d), pl.BlockSpec((1, d), lambda i: (0, 0)),
                  pl.BlockSpec((d, 2 * MEM_WIDTH), lambda i: (0, 0))],
        out_specs=[blk(m, MEM_WIDTH), blk(m, MEM_WIDTH), blk(MEM_WIDTH, m), blk(m, MEM_WIDTH)],
        out_shape=[out_f, out_f, out_t, out_b],
        name="memkv",
    )(mem, g, w_bf)


def _router_topk(hn, w_r_ref, b_r_ref, tope_ref, gates_ref):
    n_e = w_r_ref.shape[1]
    logits = _dot(hn.astype(BF16), w_r_ref[...]) + b_r_ref[...]
    rows = logits.shape[0]
    col = _iota((rows, n_e), 1).astype(F32)
    vals, idxs = [], []
    cur = logits
    for _ in range(TOP_K):
        m = jnp.max(cur, axis=-1, keepdims=True)
        idx = jnp.min(jnp.where(cur == m, col, float(n_e)), axis=-1, keepdims=True)
        vals.append(m)
        idxs.append(idx)
        cur = jnp.where(col == idx, -jnp.inf, cur)
    exps = [jnp.exp(v - vals[0]) for v in vals]
    tot = exps[0] + exps[1] + exps[2] + exps[3]
    col4 = _iota((rows, TOP_K), 1)
    te = jnp.zeros((rows, TOP_K), F32)
    ga = jnp.zeros((rows, TOP_K), F32)
    for k in range(TOP_K):
        te = jnp.where(col4 == k, idxs[k], te)
        ga = jnp.where(col4 == k, exps[k] / tot, ga)
    tope_ref[...] = te.astype(jnp.int32)
    gates_ref[...] = ga


def _mix_out(x, attn, conv_out, cross, g_mix_ref, w_out_ref, g_ffn_ref, w_r_ref, b_r_ref,
             h_ref, hn_ref, tope_ref, gates_ref):
    mix = jnp.concatenate([_rms(attn), _rms(conv_out), _rms(cross)], axis=-1) * g_mix_ref[...]
    h = x + _dot(mix.astype(BF16), w_out_ref[...])
    hn = _rms(h) * g_ffn_ref[...]
    h_ref[...] = h
    _store_token_tiles(hn_ref, hn)
    _router_topk(hn, w_r_ref, b_r_ref, tope_ref, gates_ref)


def _swa_block(q_blk, kk, vv, prev_lim, sinks_ref):
    blk = WINDOW
    lane = _iota((2 * blk, KV_WIDTH), 1)
    lo = lane < HEAD_DIM
    kk_r = pltpu.roll(kk, HEAD_DIM, axis=1)
    vv_r = pltpu.roll(vv, HEAD_DIM, axis=1)
    kdup = [jnp.where(lo, kk, kk_r).astype(BF16), jnp.where(lo, kk_r, kk).astype(BF16)]
    vlo = [jnp.where(lo, vv, 0.0).astype(BF16), jnp.where(lo, vv_r, 0.0).astype(BF16)]
    vhi = [jnp.where(lo, 0.0, vv_r).astype(BF16), jnp.where(lo, 0.0, vv).astype(BF16)]
    qi = _iota((blk, 2 * blk), 0)
    kj = _iota((blk, 2 * blk), 1)
    dist = blk + qi - kj
    mask = (dist >= 0) & (dist < WINDOW) & (kj >= prev_lim)
    distf = dist.astype(F32)
    qlo = _iota((blk, 2 * HEAD_DIM), 1) < HEAD_DIM
    outs = []
    for p in range(N_Q_HEADS // 2):
        kh = (2 * p) // (N_Q_HEADS // N_KV_HEADS)
        qp = q_blk[:, p * 2 * HEAD_DIM:(p + 1) * 2 * HEAD_DIM]
        acc = None
        for e in range(2):
            h = 2 * p + e
            qm = jnp.where(qlo if e == 0 else jnp.logical_not(qlo), qp, 0.0).astype(BF16)
            s = _dot_nt(qm, kdup[kh]) - ALIBI_SLOPES[h] * distf
            s = jnp.where(mask, s, -jnp.inf)
            sink = sinks_ref[h]
            m = jnp.maximum(jnp.max(s, axis=-1, keepdims=True), sink)
            pe = jnp.exp(s - m)
            denom = jnp.sum(pe, axis=-1, keepdims=True) + jnp.exp(sink - m)
            o = _dot(pe.astype(BF16), (vlo if e == 0 else vhi)[kh]) / denom
            acc = o if acc is None else acc + o
        outs.append(acc)
    return jnp.concatenate(outs, axis=1)


def _mem_attend_shared(mq, mkt, mvb):
    t = mq.shape[0]
    m_tok = mvb.shape[0]
    qhead = _iota((t, MEM_WIDTH), 1) // HEAD_DIM
    vhead = _iota((m_tok, MEM_WIDTH), 1) // HEAD_DIM
    cross = None
    for h in range(N_MEM_HEADS):
        qm = jnp.where(qhead == h, mq, 0.0).astype(BF16)
        s = _dot(qm, mkt)
        m = jnp.max(s, axis=-1, keepdims=True)
        pe = jnp.exp(s - m)
        denom = jnp.sum(pe, axis=-1, keepdims=True)
        vm = jnp.where(vhead == h, mvb, jnp.zeros_like(mvb))
        o = _dot(pe.astype(BF16), vm) / denom
        cross = o if cross is None else cross + o
    return cross


def _mixer_p_kernel(sinks_ref, x_ref, g_attn_ref, w_in_ref, conv_w_ref, g_mix_ref, w_out_ref, g_ffn_ref,
                    w_r_ref, b_r_ref, mkt_ref, mvb_ref,
                    h_ref, hn_ref, tope_ref, gates_ref, lastk_ref, lastv_ref, convst_ref,
                    ck_ref, cv_ref, cu_ref):
    j = pl.program_id(1)
    nj = pl.num_programs(1)

    @pl.when(j == 0)
    def _():
        ck_ref[...] = jnp.zeros_like(ck_ref)
        cv_ref[...] = jnp.zeros_like(cv_ref)
        cu_ref[...] = jnp.zeros_like(cu_ref)

    x = x_ref[0]
    xn = (_rms(x) * g_attn_ref[...]).astype(BF16)
    z = _dot(xn, w_in_ref[...])
    c0 = ATTN_WIDTH
    c1 = c0 + KV_WIDTH
    c2 = c1 + KV_WIDTH
    cw = conv_w_ref.shape[1]
    c3, c4, c5 = c2 + cw, c2 + 2 * cw, c2 + 3 * cw
    q = z[:, :c0] * ATTN_SCALE
    k = z[:, c0:c1]
    v = z[:, c1:c2]
    cb = z[:, c2:c3]
    cc = z[:, c3:c4]
    cvv = z[:, c4:c5]
    mq = z[:, c5:] * ATTN_SCALE

    blk = WINDOW
    attn_blocks = []
    for i in range(TQ // blk):
        if i == 0:
            pk, pv = ck_ref[...], cv_ref[...]
            prev_lim = jnp.where(j > 0, 0, blk)
        else:
            pk, pv = k[(i - 1) * blk:i * blk], v[(i - 1) * blk:i * blk]
            prev_lim = 0
        kk = jnp.concatenate([pk, k[i * blk:(i + 1) * blk]], axis=0)
        vv = jnp.concatenate([pv, v[i * blk:(i + 1) * blk]], axis=0)
        attn_blocks.append(_swa_block(q[i * blk:(i + 1) * blk], kk, vv, prev_lim, sinks_ref))
    attn = jnp.concatenate(attn_blocks, axis=0)
    ck_ref[...] = k[TQ - blk:]
    cv_ref[...] = v[TQ - blk:]

    u = cc * cvv
    row = _iota(u.shape, 0)
    u1 = jnp.where(row == 0, cu_ref[SUBLANES - 1:SUBLANES, :], pltpu.roll(u, 1, axis=0))
    u2 = jnp.where(row == 0, cu_ref[SUBLANES - 2:SUBLANES - 1, :],
                   jnp.where(row == 1, cu_ref[SUBLANES - 1:SUBLANES, :], pltpu.roll(u, 2, axis=0)))
    conv_out = cb * (conv_w_ref[0:1, :] * u2 + conv_w_ref[1:2, :] * u1 + conv_w_ref[2:3, :] * u)
    cu_ref[...] = u[TQ - SUBLANES:]

    cross = _mem_attend_shared(mq, mkt_ref[0], mvb_ref[0])

    @pl.when(j == nj - 1)
    def _():
        lastk_ref[0] = k[TQ - blk:]
        lastv_ref[0] = v[TQ - blk:]
        convst_ref[0] = u[TQ - 2:]

    _mix_out(x, attn, conv_out, cross, g_mix_ref, w_out_ref, g_ffn_ref, w_r_ref, b_r_ref,
             h_ref, hn_ref, tope_ref, gates_ref)


def _mixer_p(x, sinks, g_attn, w_in, conv_w, g_mix, w_out, g_ffn, w_r, b_r, mkt, mvb):
    b, s, d = x.shape
    nj = s // TQ
    n = b * s
    cw = conv_w.shape[1]
    full = lambda a: pl.BlockSpec(a.shape, lambda bi, ji, *_: (0,) * a.ndim)
    tok = lambda w: pl.BlockSpec((TQ, w), lambda bi, ji, *_: (bi * nj + ji, 0))
    per_b = lambda r, c: pl.BlockSpec((1, r, c), lambda bi, ji, *_: (bi, 0, 0))
    grid_spec = pltpu.PrefetchScalarGridSpec(
        num_scalar_prefetch=1,
        grid=(b, nj),
        in_specs=[pl.BlockSpec((1, TQ, d), lambda bi, ji, *_: (bi, ji, 0)),
                  full(g_attn), full(w_in), full(conv_w), full(g_mix), full(w_out), full(g_ffn),
                  full(w_r), full(b_r), per_b(MEM_WIDTH, mkt.shape[2]), per_b(mvb.shape[1], MEM_WIDTH)],
        out_specs=[tok(d), pl.BlockSpec((TQ * SUBLANES, LANES), lambda bi, ji, *_: (bi * nj + ji, 0)),
                   tok(TOP_K), tok(TOP_K),
                   per_b(WINDOW, KV_WIDTH), per_b(WINDOW, KV_WIDTH), per_b(2, cw)],
        scratch_shapes=[pltpu.VMEM((WINDOW, KV_WIDTH), F32), pltpu.VMEM((WINDOW, KV_WIDTH), F32),
                        pltpu.VMEM((SUBLANES, cw), F32)],
    )
    return pl.pallas_call(
        _mixer_p_kernel,
        grid_spec=grid_spec,
        out_shape=[jax.ShapeDtypeStruct((n, d), F32), jax.ShapeDtypeStruct((n * SUBLANES, LANES), F32),
                   jax.ShapeDtypeStruct((n, TOP_K), jnp.int32), jax.ShapeDtypeStruct((n, TOP_K), F32),
                   jax.ShapeDtypeStruct((b, WINDOW, KV_WIDTH), F32),
                   jax.ShapeDtypeStruct((b, WINDOW, KV_WIDTH), F32),
                   jax.ShapeDtypeStruct((b, 2, cw), F32)],
        compiler_params=pltpu.CompilerParams(dimension_semantics=("arbitrary", "arbitrary"),
                                             vmem_limit_bytes=VMEM_LIMIT),
        name="mixer_p",
    )(sinks, x, g_attn, w_in, conv_w, g_mix, w_out, g_ffn, w_r, b_r, mkt, mvb)


def _per_head_column(values, hrow):
    col = jnp.zeros(hrow.shape, F32)
    for h in range(N_Q_HEADS):
        col = jnp.where(hrow == h, values[h], col)
    return col


def _mixer_s_kernel(sinks_ref, x_ref, pm1_ref, pm2_ref, wk_ref, wv_ref, mk_ref, mv_ref,
                    g_attn_ref, w_in_ref, conv_w_ref, g_mix_ref, w_out_ref, g_ffn_ref, w_r_ref, b_r_ref,
                    h_ref, hn_ref, tope_ref, gates_ref, nwk_ref, nwv_ref, u_ref, *, t_dec):
    r_tok = BB * t_dec
    r_exp = r_tok * REP
    qrows = t_dec * REP
    x = x_ref[...]
    xn = (_rms(x) * g_attn_ref[...]).astype(BF16)
    z = _dot(xn, w_in_ref[...])
    c0 = ATTN_WIDTH
    c1 = c0 + KV_WIDTH
    c2 = c1 + KV_WIDTH
    cw = conv_w_ref.shape[1]
    c3, c4, c5 = c2 + cw, c2 + 2 * cw, c2 + 3 * cw
    q = z[:, :c0] * ATTN_SCALE
    k_new = z[:, c0:c1]
    v_new = z[:, c1:c2]
    cb = z[:, c2:c3]
    cc = z[:, c3:c4]
    cvv = z[:, c4:c5]
    mq = z[:, c5:] * ATTN_SCALE
    win = wk_ref.shape[1]

    xi = _iota((KV_WIDTH, ATTN_WIDTH), 0)
    xl = _iota((KV_WIDTH, ATTN_WIDTH), 1)
    q_per_kv = N_Q_HEADS // N_KV_HEADS
    expand = (xi == (xl // (q_per_kv * HEAD_DIM)) * HEAD_DIM + xl % HEAD_DIM).astype(BF16)
    rr = _iota((r_exp, r_tok), 0)
    rc = _iota((r_exp, r_tok), 1)
    rep = (rr // REP == rc).astype(BF16)

    hrow = _iota((r_exp, 1), 0) % REP
    trow = (_iota((r_exp, 1), 0) // REP) % t_dec
    slope_col = _per_head_column(ALIBI_SLOPES, hrow)
    sink_col = _per_head_column([sinks_ref[h] for h in range(N_Q_HEADS)], hrow)

    qexp = jnp.where(hrow == _iota((r_exp, ATTN_WIDTH), 1) // HEAD_DIM, _dot(rep, q.astype(BF16)), 0.0)
    kexp = _dot(wk_ref[...].reshape(BB * win, KV_WIDTH).astype(BF16), expand).astype(BF16)
    vexp = _dot(wv_ref[...].reshape(BB * win, KV_WIDTH).astype(BF16), expand).astype(BF16)
    s = jnp.einsum("bqc,bkc->bqk", qexp.astype(BF16).reshape(BB, qrows, ATTN_WIDTH),
                   kexp.reshape(BB, win, ATTN_WIDTH), preferred_element_type=F32).reshape(r_exp, win)
    scol = _iota((r_exp, win), 1)
    s = s - slope_col * (win + trow - scol).astype(F32)
    s = jnp.where(scol > trow, s, -jnp.inf)
    knew_exp = _dot(k_new.astype(BF16), expand).astype(BF16)
    vnew_exp = _dot(v_new.astype(BF16), expand).astype(BF16)
    s_new, v_rep = [], []
    for jn in range(t_dec):
        rep_j = (rc == (rr // qrows) * t_dec + jn).astype(BF16)
        k_rep = _dot(rep_j, knew_exp)
        v_rep.append(_dot(rep_j, vnew_exp))
        sj = jnp.sum(qexp * k_rep, axis=-1, keepdims=True) - slope_col * (trow - jn).astype(F32)
        s_new.append(jnp.where(trow >= jn, sj, -jnp.inf))
    m = jnp.maximum(jnp.max(s, axis=-1, keepdims=True), sink_col)
    for sj in s_new:
        m = jnp.maximum(m, sj)
    pe = jnp.exp(s - m)
    denom = jnp.sum(pe, axis=-1, keepdims=True) + jnp.exp(sink_col - m)
    o = jnp.einsum("bqk,bkc->bqc", pe.astype(BF16).reshape(BB, qrows, win),
                   vexp.reshape(BB, win, ATTN_WIDTH), preferred_element_type=F32).reshape(r_exp, ATTN_WIDTH)
    for jn in range(t_dec):
        pj = jnp.exp(s_new[jn] - m)
        denom = denom + pj
        o = o + pj * v_rep[jn]
    o = jnp.where(hrow == _iota((r_exp, ATTN_WIDTH), 1) // HEAD_DIM, o / denom, 0.0)
    attn = jnp.sum(o.reshape(r_tok, REP, ATTN_WIDTH), axis=1)

    m_tok = mk_ref.shape[1]
    mhead = _iota((r_exp, MEM_WIDTH), 1) // HEAD_DIM
    mqexp = jnp.where(hrow == mhead, _dot(rep, mq.astype(BF16)), 0.0).astype(BF16)
    sm = jnp.einsum("bqc,bmc->bqm", mqexp.reshape(BB, qrows, MEM_WIDTH), mk_ref[...].astype(BF16),
                    preferred_element_type=F32).reshape(r_exp, m_tok)
    mm = jnp.max(sm, axis=-1, keepdims=True)
    pm = jnp.exp(sm - mm)
    dm = jnp.sum(pm, axis=-1, keepdims=True)
    om = jnp.einsum("bqm,bmc->bqc", pm.astype(BF16).reshape(BB, qrows, m_tok), mv_ref[...].astype(BF16),
                    preferred_element_type=F32).reshape(r_exp, MEM_WIDTH)
    om = jnp.where(hrow == mhead, om / dm, 0.0)
    cross = jnp.sum(om.reshape(r_tok, REP, MEM_WIDTH), axis=1)

    u = cc * cvv
    tt = _iota(u.shape, 0) % t_dec
    u1 = jnp.where(tt >= 1, pltpu.roll(u, 1, axis=0), pm1_ref[...])
    u2 = jnp.where(tt >= 2, pltpu.roll(u, 2, axis=0), pm2_ref[...])
    conv_out = cb * (conv_w_ref[0:1, :] * u2 + conv_w_ref[1:2, :] * u1 + conv_w_ref[2:3, :] * u)
    u_ref[...] = u

    nwk_ref[:, 0:win - t_dec, :] = wk_ref[:, t_dec:win, :]
    nwv_ref[:, 0:win - t_dec, :] = wv_ref[:, t_dec:win, :]
    for b in range(BB):
        nwk_ref[b, win - t_dec:win, :] = k_new[b * t_dec:(b + 1) * t_dec, :]
        nwv_ref[b, win - t_dec:win, :] = v_new[b * t_dec:(b + 1) * t_dec, :]

    _mix_out(x, attn, conv_out, cross, g_mix_ref, w_out_ref, g_ffn_ref, w_r_ref, b_r_ref,
             h_ref, hn_ref, tope_ref, gates_ref)


def _mixer_s(x2, t_dec, pm1, pm2, wk, wv, mk, mv, sinks, g_attn, w_in, conv_w, g_mix, w_out, g_ffn, w_r, b_r):
    n, d = x2.shape
    nb = wk.shape[0]
    win = wk.shape[1]
    m_tok = mk.shape[1]
    cw = conv_w.shape[1]
    r_tok = BB * t_dec
    full = lambda a: pl.BlockSpec(a.shape, lambda i, *_: (0,) * a.ndim)
    tok = lambda w: pl.BlockSpec((r_tok, w), lambda i, *_: (i, 0))
    per_b = lambda r, c: pl.BlockSpec((BB, r, c), lambda i, *_: (i, 0, 0))
    grid_spec = pltpu.PrefetchScalarGridSpec(
        num_scalar_prefetch=1,
        grid=(nb // BB,),
        in_specs=[tok(d), tok(cw), tok(cw), per_b(win, KV_WIDTH), per_b(win, KV_WIDTH),
                  per_b(m_tok, MEM_WIDTH), per_b(m_tok, MEM_WIDTH),
                  full(g_attn), full(w_in), full(conv_w), full(g_mix), full(w_out), full(g_ffn),
                  full(w_r), full(b_r)],
        out_specs=[tok(d), pl.BlockSpec((r_tok * SUBLANES, LANES), lambda i, *_: (i, 0)),
                   tok(TOP_K), tok(TOP_K), per_b(win, KV_WIDTH), per_b(win, KV_WIDTH), tok(cw)],
    )
    return pl.pallas_call(
        functools.partial(_mixer_s_kernel, t_dec=t_dec),
        grid_spec=grid_spec,
        out_shape=[jax.ShapeDtypeStruct((n, d), F32), jax.ShapeDtypeStruct((n * SUBLANES, LANES), F32),
                   jax.ShapeDtypeStruct((n, TOP_K), jnp.int32), jax.ShapeDtypeStruct((n, TOP_K), F32),
                   jax.ShapeDtypeStruct((nb, win, KV_WIDTH), F32), jax.ShapeDtypeStruct((nb, win, KV_WIDTH), F32),
                   jax.ShapeDtypeStruct((n, cw), F32)],
        compiler_params=pltpu.CompilerParams(dimension_semantics=("arbitrary",), vmem_limit_bytes=VMEM_LIMIT),
        name="mixer_s",
    )(sinks, x2, pm1, pm2, wk, wv, mk, mv, g_attn, w_in, conv_w, g_mix, w_out, g_ffn, w_r, b_r)


def _rank_kernel(tope_ref, rank_ref, counts_ref, tri_ref, *, n_e):
    i = pl.program_id(0)
    tm = tope_ref.shape[0]

    @pl.when(i == 0)
    def _():
        tri_ref[...] = (_iota((tm, tm), 0) > _iota((tm, tm), 1)).astype(BF16)

    te = tope_ref[...]
    col = _iota((tm, n_e), 1)
    hits = [te[:, k:k + 1] == col for k in range(TOP_K)]
    onehot = jnp.zeros((tm, n_e), F32)
    for hk in hits:
        onehot = onehot + hk.astype(F32)
    before = _dot(tri_ref[...], onehot.astype(BF16))
    col4 = _iota((tm, TOP_K), 1)
    rank = jnp.zeros((tm, TOP_K), F32)
    for k in range(TOP_K):
        rank = jnp.where(col4 == k, jnp.sum(jnp.where(hits[k], before, 0.0), axis=-1, keepdims=True), rank)
    rank_ref[...] = rank.astype(jnp.int32)
    counts_ref[0] = jnp.sum(onehot, axis=0, keepdims=True)


def _rank(tope, tm, n_e):
    n = tope.shape[0]
    return pl.pallas_call(
        functools.partial(_rank_kernel, n_e=n_e),
        grid=(n // tm,),
        in_specs=[pl.BlockSpec((tm, TOP_K), lambda i: (i, 0))],
        out_specs=[pl.BlockSpec((tm, TOP_K), lambda i: (i, 0)), pl.BlockSpec((1, 1, n_e), lambda i: (i, 0, 0))],
        out_shape=[jax.ShapeDtypeStruct((n, TOP_K), jnp.int32), jax.ShapeDtypeStruct((n // tm, 1, n_e), F32)],
        scratch_shapes=[pltpu.VMEM((tm, tm), BF16)],
        compiler_params=pltpu.CompilerParams(dimension_semantics=("arbitrary",)),
        name="rank",
    )(tope)


def _lpos_kernel(tope_ref, rank_ref, start_ref, lpos_ref, *, n_e):
    tm = tope_ref.shape[0]
    te = tope_ref[...]
    col = _iota((tm, n_e), 1)
    col4 = _iota((tm, TOP_K), 1)
    base = jnp.zeros((tm, TOP_K), F32)
    for k in range(TOP_K):
        sel = jnp.sum(jnp.where(te[:, k:k + 1] == col, start_ref[0], 0.0), axis=-1, keepdims=True)
        base = jnp.where(col4 == k, sel, base)
    half = (pl.program_id(0) % 2) * (tm * TOP_K)
    lpos_ref[...] = (base.astype(jnp.int32) + rank_ref[...] + half) * SUBLANES


def _lpos(tope, rank, lstart_f, tm, n_e):
    n = tope.shape[0]
    return pl.pallas_call(
        functools.partial(_lpos_kernel, n_e=n_e),
        grid=(n // tm,),
        in_specs=[pl.BlockSpec((tm, TOP_K), lambda i: (i, 0)), pl.BlockSpec((tm, TOP_K), lambda i: (i, 0)),
                  pl.BlockSpec((1, 1, n_e), lambda i: (i, 0, 0))],
        out_specs=pl.BlockSpec((tm, TOP_K), lambda i: (i, 0)),
        out_shape=jax.ShapeDtypeStruct((n, TOP_K), jnp.int32),
        name="lpos",
    )(tope, rank, lstart_f)


def _rows(ref, first_row, n_rows):
    return ref.at[pl.ds(pl.multiple_of(first_row * SUBLANES, SUBLANES), n_rows * SUBLANES)]


def _group_chunks(meta, g, n_e, max_rows, make_copy, wait):
    off_ref, cnt_ref, lst_ref = meta
    sizes = [1 << b for b in range(max_rows.bit_length() - 1, -1, -1)]

    def per_expert(e, c):
        cnt = cnt_ref[g * n_e + e]
        off = off_ref[g * n_e + e]
        lst = lst_ref[g * n_e + e]
        done = 0
        for size in sizes:
            bit = cnt & size

            @pl.when(bit != 0)
            def _():
                cp = make_copy(lst + done, off + done, size)
                if wait:
                    cp.wait()
                else:
                    cp.start()
            done = done + bit
        return c

    lax.fori_loop(0, n_e, per_expert, 0)


def _dispatch_kernel(off_ref, cnt_ref, lst_ref, lastblk_ref, nused_ref, lpos_ref, hn_ref, xs_ref,
                     stage_ref, zbuf_ref, sem, zsem, *, n_e, n_blocks):
    j = pl.program_id(0)
    nj = pl.num_programs(0)
    tm = hn_ref.shape[0] // SUBLANES
    slot = j % 2
    meta = (off_ref, cnt_ref, lst_ref)

    def chunks(g, s, wait):
        _group_chunks(meta, g, n_e, tm,
                      lambda lrow, grow, size: pltpu.make_async_copy(
                          _rows(stage_ref, s * (tm * TOP_K) + lrow, size), _rows(xs_ref, grow, size),
                          sem.at[s]), wait)

    def zero_copy(blk):
        return pltpu.make_async_copy(zbuf_ref, _rows(xs_ref, blk * BM, BM), zsem)

    @pl.when(j == 0)
    def _():
        zbuf_ref[...] = jnp.zeros_like(zbuf_ref)

        def expert_fill(wait):
            def body(e, c):
                @pl.when(lastblk_ref[e] >= 0)
                def _():
                    cp = zero_copy(lastblk_ref[e])
                    cp.wait() if wait else cp.start()
                return c
            return body

        def tail_fill(wait):
            def body(b, c):
                cp = zero_copy(b)
                cp.wait() if wait else cp.start()
                return c
            return body

        lax.fori_loop(0, n_e, expert_fill(False), 0)
        lax.fori_loop(nused_ref[0], n_blocks, tail_fill(False), 0)
        lax.fori_loop(0, n_e, expert_fill(True), 0)
        lax.fori_loop(nused_ref[0], n_blocks, tail_fill(True), 0)

    @pl.when(j >= 2)
    def _():
        chunks(j - 2, slot, True)

    def place(t, c):
        tile = hn_ref[pl.ds(pl.multiple_of(t * SUBLANES, SUBLANES), SUBLANES), :]
        for k in range(TOP_K):
            pos = pl.multiple_of(lpos_ref[0, 0, t * TOP_K + k], SUBLANES)
            stage_ref[pl.ds(pos, SUBLANES), :] = tile
        return c

    lax.fori_loop(0, tm, place, 0, unroll=8)
    chunks(j, slot, False)

    @pl.when(j == nj - 1)
    def _():
        @pl.when(j >= 1)
        def _():
            chunks(j - 1, 1 - slot, True)
        chunks(j, slot, True)


def _dispatch(hn_all, lpos, meta, lastblk, nused, tm, n_e, n_blocks):
    nt = lpos.shape[0] // tm
    grid_spec = pltpu.PrefetchScalarGridSpec(
        num_scalar_prefetch=5,
        grid=(nt,),
        in_specs=[pl.BlockSpec((1, 1, tm * TOP_K), lambda j, *_: (j, 0, 0), memory_space=pltpu.SMEM),
                  pl.BlockSpec((tm * SUBLANES, LANES), lambda j, *_: (j, 0))],
        out_specs=pl.BlockSpec(memory_space=pl.ANY),
        scratch_shapes=[pltpu.VMEM((2 * tm * TOP_K * SUBLANES, LANES), F32), pltpu.VMEM((BM * SUBLANES, LANES), F32),
                        pltpu.SemaphoreType.DMA((2,)), pltpu.SemaphoreType.DMA(())],
    )
    return pl.pallas_call(
        functools.partial(_dispatch_kernel, n_e=n_e, n_blocks=n_blocks),
        grid_spec=grid_spec,
        out_shape=jax.ShapeDtypeStruct((n_blocks * BM * SUBLANES, LANES), F32),
        compiler_params=pltpu.CompilerParams(dimension_semantics=("arbitrary",), vmem_limit_bytes=VMEM_LIMIT),
        name="dispatch",
    )(*meta, lastblk, nused, lpos.reshape(nt, 1, tm * TOP_K), hn_all)


def _experts_kernel(blk_e_ref, nused_ref, xs_ref, wgu_ref, bgu_ref, wd_ref, bd_ref, ys_ref, wgu_bf_ref, wd_bf_ref):
    i = pl.program_id(0)
    nused = nused_ref[0]
    d_ff2 = wgu_ref.shape[2]
    grp = 2 * LANES

    @pl.when(i < nused)
    def _():
        e = blk_e_ref[i]
        e_prev = blk_e_ref[jnp.maximum(i - 1, 0)]

        @pl.when((i == 0) | (e != e_prev))
        def _():
            pr = _iota((grp, grp), 0)
            pc = _iota((grp, grp), 1)
            perm = (pr == jnp.where(pc < LANES, 2 * pc, 2 * (pc - LANES) + 1)).astype(BF16)
            for g in range(d_ff2 // grp):
                w = wgu_ref[0, :, g * grp:(g + 1) * grp].astype(BF16)
                wgu_bf_ref[:, g * grp:(g + 1) * grp] = _dot(w, perm).astype(BF16)
            wd_bf_ref[...] = wd_ref[0].astype(BF16)

        x = jnp.concatenate([_load_token_tiles(xs_ref, BM, s).astype(BF16) for s in range(SUBLANES)], axis=1)
        hgu = _dot(x, wgu_bf_ref[...]) + bgu_ref[0]
        acts = []
        for g in range(d_ff2 // grp):
            gate = jnp.minimum(hgu[:, g * grp:g * grp + LANES], SWIGLU_LIMIT)
            up = jnp.clip(hgu[:, g * grp + LANES:(g + 1) * grp], -SWIGLU_LIMIT, SWIGLU_LIMIT)
            glu = gate * (1.0 / (1.0 + jnp.exp(-SWIGLU_ALPHA * gate)))
            acts.append(((up + 1.0) * glu).astype(BF16))
        act = jnp.concatenate(acts, axis=1)
        _store_token_tiles(ys_ref, _dot(act, wd_bf_ref[...]) + bd_ref[0])

    @pl.when(i >= nused)
    def _():
        ys_ref[...] = jnp.zeros_like(ys_ref)


def _experts(xs, blk_e, nused, w_gate_up, b_gu_perm, w_down, b_down):
    n_e, d, d_ff2 = w_gate_up.shape
    n_blocks = xs.shape[0] // (BM * SUBLANES)
    expert = lambda i, be, nu: be[jnp.minimum(i, jnp.maximum(nu[0] - 1, 0))]
    rows_spec = pl.BlockSpec((BM * SUBLANES, LANES), lambda i, be, nu: (i, 0))
    grid_spec = pltpu.PrefetchScalarGridSpec(
        num_scalar_prefetch=2,
        grid=(n_blocks,),
        in_specs=[rows_spec,
                  pl.BlockSpec((1, d, d_ff2), lambda i, be, nu: (expert(i, be, nu), 0, 0)),
                  pl.BlockSpec((1, 1, d_ff2), lambda i, be, nu: (expert(i, be, nu), 0, 0)),
                  pl.BlockSpec((1, d_ff2 // 2, d), lambda i, be, nu: (expert(i, be, nu), 0, 0)),
                  pl.BlockSpec((1, 1, d), lambda i, be, nu: (expert(i, be, nu), 0, 0))],
        out_specs=rows_spec,
        scratch_shapes=[pltpu.VMEM((d, d_ff2), BF16), pltpu.VMEM((d_ff2 // 2, d), BF16)],
    )
    return pl.pallas_call(
        _experts_kernel,
        grid_spec=grid_spec,
        out_shape=jax.ShapeDtypeStruct(xs.shape, F32),
        compiler_params=pltpu.CompilerParams(dimension_semantics=("arbitrary",), vmem_limit_bytes=VMEM_LIMIT),
        name="experts",
    )(blk_e, nused, xs, w_gate_up, b_gu_perm, w_down, b_down)


def _combine_kernel(off_ref, cnt_ref, lst_ref, lpos_ref, gates_ref, h_ref, g_ref, ys_ref, out_ref,
                    stage_ref, acc_ref, sem, *, n_e, g0):
    i = pl.program_id(0)
    ng = pl.num_programs(0)
    tm, d = h_ref.shape
    slot = (g0 + i) % 2
    meta = (off_ref, cnt_ref, lst_ref)

    def chunks(g, s, wait):
        _group_chunks(meta, g, n_e, tm,
                      lambda lrow, grow, size: pltpu.make_async_copy(
                          _rows(ys_ref, grow, size), _rows(stage_ref, s * (tm * TOP_K) + lrow, size),
                          sem.at[s]), wait)

    @pl.when(i == 0)
    def _():
        chunks(g0, slot, False)

    @pl.when(i + 1 < ng)
    def _():
        chunks(g0 + i + 1, 1 - slot, False)

    chunks(g0 + i, slot, True)

    def mix(t, c):
        acc = None
        for k in range(TOP_K):
            pos = pl.multiple_of(lpos_ref[0, 0, t * TOP_K + k], SUBLANES)
            term = stage_ref[pl.ds(pos, SUBLANES), :] * gates_ref[0, 0, t * TOP_K + k]
            acc = term if acc is None else acc + term
        acc_ref[pl.ds(pl.multiple_of(t * SUBLANES, SUBLANES), SUBLANES), :] = acc
        return c

    lax.fori_loop(0, tm, mix, 0, unroll=8)

    h = h_ref[...]
    parts = []
    sq = jnp.zeros((tm, LANES), F32)
    for s in range(d // LANES):
        y = h[:, s * LANES:(s + 1) * LANES] + _load_token_tiles(acc_ref, tm, s)
        sq = sq + y * y
        parts.append(y)
    rinv = lax.rsqrt(jnp.sum(sq, axis=-1, keepdims=True) / d + EPS)
    out_ref[...] = jnp.concatenate(parts, axis=1) * rinv * g_ref[...]


def _combine(ys, h, lpos, gates, meta, g_final, tm, n_e, g0):
    n, d = h.shape
    grid_spec = pltpu.PrefetchScalarGridSpec(
        num_scalar_prefetch=3,
        grid=(n // tm,),
        in_specs=[pl.BlockSpec((1, 1, tm * TOP_K), lambda i, *_: (g0 + i, 0, 0), memory_space=pltpu.SMEM),
                  pl.BlockSpec((1, 1, tm * TOP_K), lambda i, *_: (g0 + i, 0, 0), memory_space=pltpu.SMEM),
                  pl.BlockSpec((tm, d), lambda i, *_: (i, 0)),
                  pl.BlockSpec((1, d), lambda i, *_: (0, 0)),
                  pl.BlockSpec(memory_space=pl.ANY)],
        out_specs=pl.BlockSpec((tm, d), lambda i, *_: (i, 0)),
        scratch_shapes=[pltpu.VMEM((2 * tm * TOP_K * SUBLANES, LANES), F32), pltpu.VMEM((tm * SUBLANES, LANES), F32),
                        pltpu.SemaphoreType.DMA((2,))],
    )
    ngroups = lpos.shape[0] // tm
    return pl.pallas_call(
        functools.partial(_combine_kernel, n_e=n_e, g0=g0),
        grid_spec=grid_spec,
        out_shape=jax.ShapeDtypeStruct((n, d), F32),
        compiler_params=pltpu.CompilerParams(dimension_semantics=("arbitrary",), vmem_limit_bytes=VMEM_LIMIT),
        name="combine",
    )(*meta, lpos.reshape(ngroups, 1, tm * TOP_K), gates.reshape(ngroups, 1, tm * TOP_K), h, g_final, ys)


def _largest_tile(cands, *sizes):
    for c in cands:
        if all(s % c == 0 for s in sizes):
            return c
    raise ValueError(f"no tile in {cands} divides {sizes}")


def kernel(x_prompt, x_sample, mem_prompt, cache_win_k, cache_win_v, state_conv, cache_mem_k, cache_mem_v, g_attn_norm, w_in, conv_w, attn_sinks, g_mem_norm, w_mem_kv, g_mix_out, w_out, g_ffn_norm, w_router, b_router, w_gate_up, b_gate_up, w_down, b_down, g_final):
    depth = w_in.shape[0]
    assert depth == 1, "single-layer step"
    b, s, d = x_prompt.shape
    nb, t_dec, _ = x_sample.shape
    n_e = w_router.shape[2]
    d_ff2 = w_gate_up.shape[3]
    cw = conv_w.shape[2]
    win = cache_win_k.shape[2]
    m_tok = cache_mem_k.shape[2]
    assert s % TQ == 0 and nb % BB == 0 and win == WINDOW and t_dec <= SUBLANES and d_ff2 % (2 * LANES) == 0
    assert d == SUBLANES * LANES, "token-tile layout: one token is one (8, 128) f32 tile"

    row = lambda a: a.reshape(1, -1)
    w_in_bf = w_in[0].astype(BF16)
    w_out_bf = w_out[0].astype(BF16)
    w_r_bf = w_router[0].astype(BF16)
    sinks = attn_sinks[0].astype(F32)
    shared = (row(g_attn_norm[0]), w_in_bf, conv_w[0], row(g_mix_out[0]), w_out_bf, row(g_ffn_norm[0]),
              w_r_bf, row(b_router[0]))

    mk_p, mv_p, mkt, mvb = _memkv(mem_prompt, row(g_mem_norm[0]), w_mem_kv[0].astype(BF16))
    h_p, hn_p, tope_p, gates_p, lastk, lastv, convst = _mixer_p(x_prompt, sinks, *shared, mkt, mvb)

    zeros = lambda r: jnp.zeros((nb, r, cw), F32)
    st = state_conv[0]
    pm1 = jnp.concatenate([st[:, 1:2], zeros(t_dec - 1)], axis=1).reshape(nb * t_dec, cw)
    pm2 = jnp.concatenate([st, zeros(t_dec - 2)], axis=1).reshape(nb * t_dec, cw)
    h_s, hn_s, tope_s, gates_s, nwk, nwv, u_s = _mixer_s(
        x_sample.reshape(nb * t_dec, d), t_dec, pm1, pm2,
        cache_win_k[0].reshape(nb, win, KV_WIDTH), cache_win_v[0].reshape(nb, win, KV_WIDTH),
        cache_mem_k[0].reshape(nb, m_tok, MEM_WIDTH), cache_mem_v[0].reshape(nb, m_tok, MEM_WIDTH),
        sinks, *shared)

    n_p, n_s = b * s, nb * t_dec
    n = n_p + n_s
    tm = _largest_tile((512, 256, 128, 64, 32, 16, 8), n_p, n_s)
    hn_all = jnp.concatenate([hn_p, hn_s], axis=0)
    tope = jnp.concatenate([tope_p, tope_s], axis=0)
    gates = jnp.concatenate([gates_p, gates_s], axis=0)

    rank, cnt_f = _rank(tope, tm, n_e)
    cnt = cnt_f[:, 0, :].astype(jnp.int32)
    counts = jnp.sum(cnt, axis=0)
    padded = (counts + BM - 1) // BM * BM
    pad_ends = jnp.cumsum(padded)
    pad_starts = pad_ends - padded
    nk = n * TOP_K
    n_blocks = -(-nk // BM) + n_e
    nused = (pad_ends[-1:] // BM).astype(jnp.int32)
    blk_start = jnp.arange(n_blocks, dtype=jnp.int32) * BM
    blk_e = jnp.minimum(jnp.sum((pad_ends[None, :] <= blk_start[:, None]).astype(jnp.int32), axis=1), n_e - 1)
    lastblk = jnp.where(padded > 0, pad_ends // BM - 1, -1).astype(jnp.int32)
    off = pad_starts[None, :] + jnp.cumsum(cnt, axis=0) - cnt
    lstart = jnp.cumsum(cnt, axis=1) - cnt
    meta = (off.reshape(-1).astype(jnp.int32), cnt.reshape(-1), lstart.reshape(-1).astype(jnp.int32))
    lpos = _lpos(tope, rank, lstart.astype(F32).reshape(n // tm, 1, n_e), tm, n_e)

    xs = _dispatch(hn_all, lpos, meta, lastblk, nused, tm, n_e, n_blocks)

    grp = 2 * LANES
    b_gu = b_gate_up[0].reshape(n_e, d_ff2 // grp, LANES, 2).transpose(0, 1, 3, 2).reshape(n_e, 1, d_ff2)
    ys = _experts(xs, blk_e, nused, w_gate_up[0], b_gu, w_down[0], b_down[0].reshape(n_e, 1, d))

    g_fin = row(g_final)
    y_p = _combine(ys, h_p, lpos, gates, meta, g_fin, tm, n_e, 0)
    y_s = _combine(ys, h_s, lpos, gates, meta, g_fin, tm, n_e, n_p // tm)

    kv5 = lambda a, bsz, r, hds: a.reshape(1, bsz, r, hds, HEAD_DIM)
    return (y_p.reshape(b, s, d), y_s.reshape(nb, t_dec, d),
            kv5(lastk, b, WINDOW, N_KV_HEADS), kv5(lastv, b, WINDOW, N_KV_HEADS),
            convst.reshape(1, b, 2, cw),
            kv5(mk_p, b, m_tok, N_MEM_HEADS), kv5(mv_p, b, m_tok, N_MEM_HEADS),
            kv5(nwk, nb, win, N_KV_HEADS), kv5(nwv, nb, win, N_KV_HEADS),
            u_s.reshape(nb, t_dec, cw)[:, t_dec - 2:].reshape(1, nb, 2, cw))
```

```python
import functools

import jax
import jax.numpy as jnp
from jax import lax
from jax.experimental import pallas as pl
from jax.experimental.pallas import tpu as pltpu

F32 = jnp.float32
BF16 = jnp.bfloat16

HEAD_DIM = 64
N_Q_HEADS = 8
N_KV_HEADS = 2
WINDOW = 128
ATTN_WIDTH = N_Q_HEADS * HEAD_DIM
KV_WIDTH = N_KV_HEADS * HEAD_DIM
N_MEM_HEADS = 4
MEM_WIDTH = N_MEM_HEADS * HEAD_DIM
TOP_K = 4
SWIGLU_LIMIT = 7.0
SWIGLU_ALPHA = 1.702
EPS = 1e-5
ATTN_SCALE = HEAD_DIM ** -0.5
ALIBI_SLOPES = tuple(2.0 ** (-8.0 * (h + 1) / N_Q_HEADS) for h in range(N_Q_HEADS))

LANES = 128
SUBLANES = 8
VMEM_LIMIT = 56 * 1024 * 1024

TQ = 256
BB = 16
BM = 512
REP = 8


def _rms(x):
    return x * lax.rsqrt(jnp.mean(x * x, axis=-1, keepdims=True) + EPS)


def _dot(a, b):
    return jnp.dot(a, b, preferred_element_type=F32)


def _dot_nt(a, b):
    return lax.dot_general(a, b, (((1,), (1,)), ((), ())), preferred_element_type=F32)


def _iota(shape, axis):
    return lax.broadcasted_iota(jnp.int32, shape, axis)


def _store_token_tiles(ref, x, base=0):
    t = x.shape[0]
    for s in range(x.shape[1] // LANES):
        ref[pl.ds(base + s, t, stride=SUBLANES), :] = x[:, s * LANES:(s + 1) * LANES]


def _load_token_tiles(ref, t, s, base=0):
    return ref[pl.ds(base + s, t, stride=SUBLANES), :]


def _memkv_kernel(mem_ref, g_ref, w_ref, mk_ref, mv_ref, mkt_ref, mvb_ref):
    xn = (_rms(mem_ref[0]) * g_ref[...]).astype(BF16)
    kv = _dot(xn, w_ref[...])
    mk = kv[:, :MEM_WIDTH]
    mv = kv[:, MEM_WIDTH:]
    mk_ref[0] = mk
    mv_ref[0] = mv
    mkt_ref[0] = mk.T.astype(BF16)
    mvb_ref[0] = mv.astype(BF16)


def _memkv(mem, g, w_bf):
    b, m, d = mem.shape
    out_f = jax.ShapeDtypeStruct((b, m, MEM_WIDTH), F32)
    out_b = jax.ShapeDtypeStruct((b, m, MEM_WIDTH), BF16)
    out_t = jax.ShapeDtypeStruct((b, MEM_WIDTH, m), BF16)
    blk = lambda r, c: pl.BlockSpec((1, r, c), lambda i: (i, 0, 0))
    return pl.pallas_call(
        _memkv_kernel,
        grid=(b,),
        in_specs=[blk(m, d), pl.BlockSpec((1, d), lambda i: (0, 0)),
                  pl.BlockSpec((d, 2 * MEM_WIDTH), lambda i: (0, 0))],
        out_specs=[blk(m, MEM_WIDTH), blk(m, MEM_WIDTH), blk(MEM_WIDTH, m), blk(m, MEM_WIDTH)],
        out_shape=[out_f, out_f, out_t, out_b],
        name="memkv",
    )(mem, g, w_bf)


def _router_topk(hn, w_r_ref, b_r_ref, tope_ref, gates_ref):
    n_e = w_r_ref.shape[1]
    logits = _dot(hn.astype(BF16), w_r_ref[...]) + b_r_ref[...]
    rows = logits.shape[0]
    col = _iota((rows, n_e), 1).astype(F32)
    vals, idxs = [], []
    cur = logits
    for _ in range(TOP_K):
        m = jnp.max(cur, axis=-1, keepdims=True)
        idx = jnp.min(jnp.where(cur == m, col, float(n_e)), axis=-1, keepdims=True)
        vals.append(m)
        idxs.append(idx)
        cur = jnp.where(col == idx, -jnp.inf, cur)
    exps = [jnp.exp(v - vals[0]) for v in vals]
    tot = exps[0] + exps[1] + exps[2] + exps[3]
    col4 = _iota((rows, TOP_K), 1)
    te = jnp.zeros((rows, TOP_K), F32)
    ga = jnp.zeros((rows, TOP_K), F32)
    for k in range(TOP_K):
        te = jnp.where(col4 == k, idxs[k], te)
        ga = jnp.where(col4 == k, exps[k] / tot, ga)
    tope_ref[...] = te.astype(jnp.int32)
    gates_ref[...] = ga


def _mix_out(x, attn, conv_out, cross, g_mix_ref, w_out_ref, g_ffn_ref, w_r_ref, b_r_ref,
             h_ref, hn_ref, tope_ref, gates_ref):
    mix = jnp.concatenate([_rms(attn), _rms(conv_out), _rms(cross)], axis=-1) * g_mix_ref[...]
    h = x + _dot(mix.astype(BF16), w_out_ref[...])
    hn = _rms(h) * g_ffn_ref[...]
    h_ref[...] = h
    _store_token_tiles(hn_ref, hn)
    _router_topk(hn, w_r_ref, b_r_ref, tope_ref, gates_ref)


def _swa_block(q_blk, kk, vv, prev_lim, sinks_ref):
    blk = WINDOW
    lane = _iota((2 * blk, KV_WIDTH), 1)
    lo = lane < HEAD_DIM
    kk_r = pltpu.roll(kk, HEAD_DIM, axis=1)
    vv_r = pltpu.roll(vv, HEAD_DIM, axis=1)
    kdup = [jnp.where(lo, kk, kk_r).astype(BF16), jnp.where(lo, kk_r, kk).astype(BF16)]
    vlo = [jnp.where(lo, vv, 0.0).astype(BF16), jnp.where(lo, vv_r, 0.0).astype(BF16)]
    vhi = [jnp.where(lo, 0.0, vv_r).astype(BF16), jnp.where(lo, 0.0, vv).astype(BF16)]
    qi = _iota((blk, 2 * blk), 0)
    kj = _iota((blk, 2 * blk), 1)
    dist = blk + qi - kj
    mask = (dist >= 0) & (dist < WINDOW) & (kj >= prev_lim)
    distf = dist.astype(F32)
    qlo = _iota((blk, 2 * HEAD_DIM), 1) < HEAD_DIM
    outs = []
    for p in range(N_Q_HEADS // 2):
        kh = (2 * p) // (N_Q_HEADS // N_KV_HEADS)
        qp = q_blk[:, p * 2 * HEAD_DIM:(p + 1) * 2 * HEAD_DIM]
        acc = None
        for e in range(2):
            h = 2 * p + e
            qm = jnp.where(qlo if e == 0 else jnp.logical_not(qlo), qp, 0.0).astype(BF16)
            s = _dot_nt(qm, kdup[kh]) - ALIBI_SLOPES[h] * distf
            s = jnp.where(mask, s, -jnp.inf)
            sink = sinks_ref[h]
            m = jnp.maximum(jnp.max(s, axis=-1, keepdims=True), sink)
            pe = jnp.exp(s - m)
            denom = jnp.sum(pe, axis=-1, keepdims=True) + jnp.exp(sink - m)
            o = _dot(pe.astype(BF16), (vlo if e == 0 else vhi)[kh]) / denom
            acc = o if acc is None else acc + o
        outs.append(acc)
    return jnp.concatenate(outs, axis=1)


def _mem_attend_shared(mq, mkt, mvb):
    t = mq.shape[0]
    m_tok = mvb.shape[0]
    qhead = _iota((t, MEM_WIDTH), 1) // HEAD_DIM
    vhead = _iota((m_tok, MEM_WIDTH), 1) // HEAD_DIM
    cross = None
    for h in range(N_MEM_HEADS):
        qm = jnp.where(qhead == h, mq, 0.0).astype(BF16)
        s = _dot(qm, mkt)
        m = jnp.max(s, axis=-1, keepdims=True)
        pe = jnp.exp(s - m)
        denom = jnp.sum(pe, axis=-1, keepdims=True)
        vm = jnp.where(vhead == h, mvb, jnp.zeros_like(mvb))
        o = _dot(pe.astype(BF16), vm) / denom
        cross = o if cross is None else cross + o
    return cross


def _mixer_p_kernel(sinks_ref, x_ref, g_attn_ref, w_in_ref, conv_w_ref, g_mix_ref, w_out_ref, g_ffn_ref,
                    w_r_ref, b_r_ref, mkt_ref, mvb_ref,
                    h_ref, hn_ref, tope_ref, gates_ref, lastk_ref, lastv_ref, convst_ref,
                    ck_ref, cv_ref, cu_ref):
    j = pl.program_id(1)
    nj = pl.num_programs(1)

    @pl.when(j == 0)
    def _():
        ck_ref[...] = jnp.zeros_like(ck_ref)
        cv_ref[...] = jnp.zeros_like(cv_ref)
        cu_ref[...] = jnp.zeros_like(cu_ref)

    x = x_ref[0]
    xn = (_rms(x) * g_attn_ref[...]).astype(BF16)
    z = _dot(xn, w_in_ref[...])
    c0 = ATTN_WIDTH
    c1 = c0 + KV_WIDTH
    c2 = c1 + KV_WIDTH
    cw = conv_w_ref.shape[1]
    c3, c4, c5 = c2 + cw, c2 + 2 * cw, c2 + 3 * cw
    q = z[:, :c0] * ATTN_SCALE
    k = z[:, c0:c1]
    v = z[:, c1:c2]
    cb = z[:, c2:c3]
    cc = z[:, c3:c4]
    cvv = z[:, c4:c5]
    mq = z[:, c5:] * ATTN_SCALE

    blk = WINDOW
    attn_blocks = []
    for i in range(TQ // blk):
        if i == 0:
            pk, pv = ck_ref[...], cv_ref[...]
            prev_lim = jnp.where(j > 0, 0, blk)
        else:
            pk, pv = k[(i - 1) * blk:i * blk], v[(i - 1) * blk:i * blk]
            prev_lim = 0
        kk = jnp.concatenate([pk, k[i * blk:(i + 1) * blk]], axis=0)
        vv = jnp.concatenate([pv, v[i * blk:(i + 1) * blk]], axis=0)
        attn_blocks.append(_swa_block(q[i * blk:(i + 1) * blk], kk, vv, prev_lim, sinks_ref))
    attn = jnp.concatenate(attn_blocks, axis=0)
    ck_ref[...] = k[TQ - blk:]
    cv_ref[...] = v[TQ - blk:]

    u = cc * cvv
    row = _iota(u.shape, 0)
    u1 = jnp.where(row == 0, cu_ref[SUBLANES - 1:SUBLANES, :], pltpu.roll(u, 1, axis=0))
    u2 = jnp.where(row == 0, cu_ref[SUBLANES - 2:SUBLANES - 1, :],
                   jnp.where(row == 1, cu_ref[SUBLANES - 1:SUBLANES, :], pltpu.roll(u, 2, axis=0)))
    conv_out = cb * (conv_w_ref[0:1, :] * u2 + conv_w_ref[1:2, :] * u1 + conv_w_ref[2:3, :] * u)
    cu_ref[...] = u[TQ - SUBLANES:]

    cross = _mem_attend_shared(mq, mkt_ref[0], mvb_ref[0])

    @pl.when(j == nj - 1)
    def _():
        lastk_ref[0] = k[TQ - blk:]
        lastv_ref[0] = v[TQ - blk:]
        convst_ref[0] = u[TQ - 2:]

    _mix_out(x, attn, conv_out, cross, g_mix_ref, w_out_ref, g_ffn_ref, w_r_ref, b_r_ref,
             h_ref, hn_ref, tope_ref, gates_ref)


def _mixer_p(x, sinks, g_attn, w_in, conv_w, g_mix, w_out, g_ffn, w_r, b_r, mkt, mvb):
    b, s, d = x.shape
    nj = s // TQ
    n = b * s
    cw = conv_w.shape[1]
    full = lambda a: pl.BlockSpec(a.shape, lambda bi, ji, *_: (0,) * a.ndim)
    tok = lambda w: pl.BlockSpec((TQ, w), lambda bi, ji, *_: (bi * nj + ji, 0))
    per_b = lambda r, c: pl.BlockSpec((1, r, c), lambda bi, ji, *_: (bi, 0, 0))
    grid_spec = pltpu.PrefetchScalarGridSpec(
        num_scalar_prefetch=1,
        grid=(b, nj),
        in_specs=[pl.BlockSpec((1, TQ, d), lambda bi, ji, *_: (bi, ji, 0)),
                  full(g_attn), full(w_in), full(conv_w), full(g_mix), full(w_out), full(g_ffn),
                  full(w_r), full(b_r), per_b(MEM_WIDTH, mkt.shape[2]), per_b(mvb.shape[1], MEM_WIDTH)],
        out_specs=[tok(d), pl.BlockSpec((TQ * SUBLANES, LANES), lambda bi, ji, *_: (bi * nj + ji, 0)),
                   tok(TOP_K), tok(TOP_K),
                   per_b(WINDOW, KV_WIDTH), per_b(WINDOW, KV_WIDTH), per_b(2, cw)],
        scratch_shapes=[pltpu.VMEM((WINDOW, KV_WIDTH), F32), pltpu.VMEM((WINDOW, KV_WIDTH), F32),
                        pltpu.VMEM((SUBLANES, cw), F32)],
    )
    return pl.pallas_call(
        _mixer_p_kernel,
        grid_spec=grid_spec,
        out_shape=[jax.ShapeDtypeStruct((n, d), F32), jax.ShapeDtypeStruct((n * SUBLANES, LANES), F32),
                   jax.ShapeDtypeStruct((n, TOP_K), jnp.int32), jax.ShapeDtypeStruct((n, TOP_K), F32),
                   jax.ShapeDtypeStruct((b, WINDOW, KV_WIDTH), F32),
                   jax.ShapeDtypeStruct((b, WINDOW, KV_WIDTH), F32),
                   jax.ShapeDtypeStruct((b, 2, cw), F32)],
        compiler_params=pltpu.CompilerParams(dimension_semantics=("arbitrary", "arbitrary"),
                                             vmem_limit_bytes=VMEM_LIMIT),
        name="mixer_p",
    )(sinks, x, g_attn, w_in, conv_w, g_mix, w_out, g_ffn, w_r, b_r, mkt, mvb)


def _per_head_column(values, hrow):
    col = jnp.zeros(hrow.shape, F32)
    for h in range(N_Q_HEADS):
        col = jnp.where(hrow == h, values[h], col)
    return col


def _mixer_s_kernel(sinks_ref, x_ref, pm1_ref, pm2_ref, wk_ref, wv_ref, mk_ref, mv_ref,
                    g_attn_ref, w_in_ref, conv_w_ref, g_mix_ref, w_out_ref, g_ffn_ref, w_r_ref, b_r_ref,
                    h_ref, hn_ref, tope_ref, gates_ref, nwk_ref, nwv_ref, u_ref, *, t_dec):
    r_tok = BB * t_dec
    r_exp = r_tok * REP
    qrows = t_dec * REP
    x = x_ref[...]
    xn = (_rms(x) * g_attn_ref[...]).astype(BF16)
    z = _dot(xn, w_in_ref[...])
    c0 = ATTN_WIDTH
    c1 = c0 + KV_WIDTH
    c2 = c1 + KV_WIDTH
    cw = conv_w_ref.shape[1]
    c3, c4, c5 = c2 + cw, c2 + 2 * cw, c2 + 3 * cw
    q = z[:, :c0] * ATTN_SCALE
    k_new = z[:, c0:c1]
    v_new = z[:, c1:c2]
    cb = z[:, c2:c3]
    cc = z[:, c3:c4]
    cvv = z[:, c4:c5]
    mq = z[:, c5:] * ATTN_SCALE
    win = wk_ref.shape[1]

    xi = _iota((KV_WIDTH, ATTN_WIDTH), 0)
    xl = _iota((KV_WIDTH, ATTN_WIDTH), 1)
    q_per_kv = N_Q_HEADS // N_KV_HEADS
    expand = (xi == (xl // (q_per_kv * HEAD_DIM)) * HEAD_DIM + xl % HEAD_DIM).astype(BF16)
    rr = _iota((r_exp, r_tok), 0)
    rc = _iota((r_exp, r_tok), 1)
    rep = (rr // REP == rc).astype(BF16)

    hrow = _iota((r_exp, 1), 0) % REP
    trow = (_iota((r_exp, 1), 0) // REP) % t_dec
    slope_col = _per_head_column(ALIBI_SLOPES, hrow)
    sink_col = _per_head_column([sinks_ref[h] for h in range(N_Q_HEADS)], hrow)

    qexp = jnp.where(hrow == _iota((r_exp, ATTN_WIDTH), 1) // HEAD_DIM, _dot(rep, q.astype(BF16)), 0.0)
    kexp = _dot(wk_ref[...].reshape(BB * win, KV_WIDTH).astype(BF16), expand).astype(BF16)
    vexp = _dot(wv_ref[...].reshape(BB * win, KV_WIDTH).astype(BF16), expand).astype(BF16)
    s = jnp.einsum("bqc,bkc->bqk", qexp.astype(BF16).reshape(BB, qrows, ATTN_WIDTH),
                   kexp.reshape(BB, win, ATTN_WIDTH), preferred_element_type=F32).reshape(r_exp, win)
    scol = _iota((r_exp, win), 1)
    s = s - slope_col * (win + trow - scol).astype(F32)
    s = jnp.where(scol > trow, s, -jnp.inf)
    knew_exp = _dot(k_new.astype(BF16), expand).astype(BF16)
    vnew_exp = _dot(v_new.astype(BF16), expand).astype(BF16)
    s_new, v_rep = [], []
    for jn in range(t_dec):
        rep_j = (rc == (rr // qrows) * t_dec + jn).astype(BF16)
        k_rep = _dot(rep_j, knew_exp)
        v_rep.append(_dot(rep_j, vnew_exp))
        sj = jnp.sum(qexp * k_rep, axis=-1, keepdims=True) - slope_col * (trow - jn).astype(F32)
        s_new.append(jnp.where(trow >= jn, sj, -jnp.inf))
    m = jnp.maximum(jnp.max(s, axis=-1, keepdims=True), sink_col)
    for sj in s_new:
        m = jnp.maximum(m, sj)
    pe = jnp.exp(s - m)
    denom = jnp.sum(pe, axis=-1, keepdims=True) + jnp.exp(sink_col - m)
    o = jnp.einsum("bqk,bkc->bqc", pe.astype(BF16).reshape(BB, qrows, win),
                   vexp.reshape(BB, win, ATTN_WIDTH), preferred_element_type=F32).reshape(r_exp, ATTN_WIDTH)
    for jn in range(t_dec):
        pj = jnp.exp(s_new[jn] - m)
        denom = denom + pj
        o = o + pj * v_rep[jn]
    o = jnp.where(hrow == _iota((r_exp, ATTN_WIDTH), 1) // HEAD_DIM, o / denom, 0.0)
    attn = jnp.sum(o.reshape(r_tok, REP, ATTN_WIDTH), axis=1)

    m_tok = mk_ref.shape[1]
    mhead = _iota((r_exp, MEM_WIDTH), 1) // HEAD_DIM
    mqexp = jnp.where(hrow == mhead, _dot(rep, mq.astype(BF16)), 0.0).astype(BF16)
    sm = jnp.einsum("bqc,bmc->bqm", mqexp.reshape(BB, qrows, MEM_WIDTH), mk_ref[...].astype(BF16),
                    preferred_element_type=F32).reshape(r_exp, m_tok)
    mm = jnp.max(sm, axis=-1, keepdims=True)
    pm = jnp.exp(sm - mm)
    dm = jnp.sum(pm, axis=-1, keepdims=True)
    om = jnp.einsum("bqm,bmc->bqc", pm.astype(BF16).reshape(BB, qrows, m_tok), mv_ref[...].astype(BF16),
                    preferred_element_type=F32).reshape(r_exp, MEM_WIDTH)
    om = jnp.where(hrow == mhead, om / dm, 0.0)
    cross = jnp.sum(om.reshape(r_tok, REP, MEM_WIDTH), axis=1)

    u = cc * cvv
    tt = _iota(u.shape, 0) % t_dec
    u1 = jnp.where(tt >= 1, pltpu.roll(u, 1, axis=0), pm1_ref[...])
    u2 = jnp.where(tt >= 2, pltpu.roll(u, 2, axis=0), pm2_ref[...])
    conv_out = cb * (conv_w_ref[0:1, :] * u2 + conv_w_ref[1:2, :] * u1 + conv_w_ref[2:3, :] * u)
    u_ref[...] = u

    nwk_ref[:, 0:win - t_dec, :] = wk_ref[:, t_dec:win, :]
    nwv_ref[:, 0:win - t_dec, :] = wv_ref[:, t_dec:win, :]
    for b in range(BB):
        nwk_ref[b, win - t_dec:win, :] = k_new[b * t_dec:(b + 1) * t_dec, :]
        nwv_ref[b, win - t_dec:win, :] = v_new[b * t_dec:(b + 1) * t_dec, :]

    _mix_out(x, attn, conv_out, cross, g_mix_ref, w_out_ref, g_ffn_ref, w_r_ref, b_r_ref,
             h_ref, hn_ref, tope_ref, gates_ref)


def _mixer_s(x2, t_dec, pm1, pm2, wk, wv, mk, mv, sinks, g_attn, w_in, conv_w, g_mix, w_out, g_ffn, w_r, b_r):
    n, d = x2.shape
    nb = wk.shape[0]
    win = wk.shape[1]
    m_tok = mk.shape[1]
    cw = conv_w.shape[1]
    r_tok = BB * t_dec
    full = lambda a: pl.BlockSpec(a.shape, lambda i, *_: (0,) * a.ndim)
    tok = lambda w: pl.BlockSpec((r_tok, w), lambda i, *_: (i, 0))
    per_b = lambda r, c: pl.BlockSpec((BB, r, c), lambda i, *_: (i, 0, 0))
    grid_spec = pltpu.PrefetchScalarGridSpec(
        num_scalar_prefetch=1,
        grid=(nb // BB,),
        in_specs=[tok(d), tok(cw), tok(cw), per_b(win, KV_WIDTH), per_b(win, KV_WIDTH),
                  per_b(m_tok, MEM_WIDTH), per_b(m_tok, MEM_WIDTH),
                  full(g_attn), full(w_in), full(conv_w), full(g_mix), full(w_out), full(g_ffn),
                  full(w_r), full(b_r)],
        out_specs=[tok(d), pl.BlockSpec((r_tok * SUBLANES, LANES), lambda i, *_: (i, 0)),
                   tok(TOP_K), tok(TOP_K), per_b(win, KV_WIDTH), per_b(win, KV_WIDTH), tok(cw)],
    )
    return pl.pallas_call(
        functools.partial(_mixer_s_kernel, t_dec=t_dec),
        grid_spec=grid_spec,
        out_shape=[jax.ShapeDtypeStruct((n, d), F32), jax.ShapeDtypeStruct((n * SUBLANES, LANES), F32),
                   jax.ShapeDtypeStruct((n, TOP_K), jnp.int32), jax.ShapeDtypeStruct((n, TOP_K), F32),
                   jax.ShapeDtypeStruct((nb, win, KV_WIDTH), F32), jax.ShapeDtypeStruct((nb, win, KV_WIDTH), F32),
                   jax.ShapeDtypeStruct((n, cw), F32)],
        compiler_params=pltpu.CompilerParams(dimension_semantics=("arbitrary",), vmem_limit_bytes=VMEM_LIMIT),
        name="mixer_s",
    )(sinks, x2, pm1, pm2, wk, wv, mk, mv, g_attn, w_in, conv_w, g_mix, w_out, g_ffn, w_r, b_r)


def _lpos_kernel(tope_ref, lpos_ref, counts_ref, tri_ref, *, n_e):
    i = pl.program_id(0)
    tm = tope_ref.shape[0]

    @pl.when(i == 0)
    def _():
        tri_ref[...] = (_iota((tm, tm), 0) > _iota((tm, tm), 1)).astype(BF16)

    te = tope_ref[...]
    col = _iota((tm, n_e), 1)
    hits = [te[:, k:k + 1] == col for k in range(TOP_K)]
    onehot = jnp.zeros((tm, n_e), F32)
    lower = jnp.zeros((tm, n_e), F32)
    for k in range(TOP_K):
        onehot = onehot + hits[k].astype(F32)
        lower = lower + (te[:, k:k + 1] < col).astype(F32)
    ahead = _dot(tri_ref[...], onehot.astype(BF16)) + jnp.sum(lower, axis=0, keepdims=True)
    col4 = _iota((tm, TOP_K), 1)
    pos = jnp.zeros((tm, TOP_K), F32)
    for k in range(TOP_K):
        pos = jnp.where(col4 == k, jnp.sum(jnp.where(hits[k], ahead, 0.0), axis=-1, keepdims=True), pos)
    half = (i % 2) * (tm * TOP_K)
    lpos_ref[...] = (pos.astype(jnp.int32) + half) * SUBLANES
    counts_ref[0] = jnp.sum(onehot, axis=0, keepdims=True)


def _lpos(tope, tm, n_e):
    n = tope.shape[0]
    return pl.pallas_call(
        functools.partial(_lpos_kernel, n_e=n_e),
        grid=(n // tm,),
        in_specs=[pl.BlockSpec((tm, TOP_K), lambda i: (i, 0))],
        out_specs=[pl.BlockSpec((tm, TOP_K), lambda i: (i, 0)), pl.BlockSpec((1, 1, n_e), lambda i: (i, 0, 0))],
        out_shape=[jax.ShapeDtypeStruct((n, TOP_K), jnp.int32), jax.ShapeDtypeStruct((n // tm, 1, n_e), F32)],
        scratch_shapes=[pltpu.VMEM((tm, tm), BF16)],
        compiler_params=pltpu.CompilerParams(dimension_semantics=("arbitrary",)),
        name="lpos",
    )(tope)


def _rows(ref, first_row, n_rows):
    return ref.at[pl.ds(pl.multiple_of(first_row * SUBLANES, SUBLANES), n_rows * SUBLANES)]


def _group_chunks(meta, g, n_e, max_rows, make_copy, wait):
    off_ref, cnt_ref, lst_ref = meta
    sizes = [1 << b for b in range(max_rows.bit_length() - 1, -1, -1)]

    def per_expert(e, c):
        cnt = cnt_ref[g * n_e + e]
        off = off_ref[g * n_e + e]
        lst = lst_ref[g * n_e + e]
        done = 0
        for size in sizes:
            bit = cnt & size

            @pl.when(bit != 0)
            def _():
                cp = make_copy(lst + done, off + done, size)
                if wait:
                    cp.wait()
                else:
                    cp.start()
            done = done + bit
        return c

    lax.fori_loop(0, n_e, per_expert, 0)


def _dispatch_kernel(off_ref, cnt_ref, lst_ref, lastblk_ref, nused_ref, lpos_ref, hn_a_ref, hn_b_ref, xs_ref,
                     stage_ref, zbuf_ref, sem, zsem, *, n_e, n_blocks, groups_a):
    j = pl.program_id(0)
    nj = pl.num_programs(0)
    tm = hn_a_ref.shape[0] // SUBLANES
    slot = j % 2
    meta = (off_ref, cnt_ref, lst_ref)

    def chunks(g, s, wait):
        _group_chunks(meta, g, n_e, tm,
                      lambda lrow, grow, size: pltpu.make_async_copy(
                          _rows(stage_ref, s * (tm * TOP_K) + lrow, size), _rows(xs_ref, grow, size),
                          sem.at[s]), wait)

    def zero_copy(blk):
        return pltpu.make_async_copy(zbuf_ref, _rows(xs_ref, blk * BM, BM), zsem)

    @pl.when(j == 0)
    def _():
        zbuf_ref[...] = jnp.zeros_like(zbuf_ref)

        def expert_fill(wait):
            def body(e, c):
                @pl.when(lastblk_ref[e] >= 0)
                def _():
                    cp = zero_copy(lastblk_ref[e])
                    cp.wait() if wait else cp.start()
                return c
            return body

        def tail_fill(wait):
            def body(b, c):
                cp = zero_copy(b)
                cp.wait() if wait else cp.start()
                return c
            return body

        lax.fori_loop(0, n_e, expert_fill(False), 0)
        lax.fori_loop(nused_ref[0], n_blocks, tail_fill(False), 0)
        lax.fori_loop(0, n_e, expert_fill(True), 0)
        lax.fori_loop(nused_ref[0], n_blocks, tail_fill(True), 0)

    @pl.when(j >= 2)
    def _():
        chunks(j - 2, slot, True)

    def place_from(hn_ref):
        def place(t, c):
            tile = hn_ref[pl.ds(pl.multiple_of(t * SUBLANES, SUBLANES), SUBLANES), :]
            for k in range(TOP_K):
                pos = pl.multiple_of(lpos_ref[0, 0, t * TOP_K + k], SUBLANES)
                stage_ref[pl.ds(pos, SUBLANES), :] = tile
            return c
        lax.fori_loop(0, tm, place, 0, unroll=8)

    @pl.when(j < groups_a)
    def _():
        place_from(hn_a_ref)

    @pl.when(j >= groups_a)
    def _():
        place_from(hn_b_ref)

    chunks(j, slot, False)

    @pl.when(j == nj - 1)
    def _():
        @pl.when(j >= 1)
        def _():
            chunks(j - 1, 1 - slot, True)
        chunks(j, slot, True)


def _dispatch(hn_a, hn_b, lpos, meta, lastblk, nused, tm, n_e, n_blocks):
    nt = lpos.shape[0] // tm
    groups_a = hn_a.shape[0] // (tm * SUBLANES)
    assert groups_a >= 1 and groups_a + hn_b.shape[0] // (tm * SUBLANES) == nt
    grid_spec = pltpu.PrefetchScalarGridSpec(
        num_scalar_prefetch=5,
        grid=(nt,),
        in_specs=[pl.BlockSpec((1, 1, tm * TOP_K), lambda j, *_: (j, 0, 0), memory_space=pltpu.SMEM),
                  pl.BlockSpec((tm * SUBLANES, LANES), lambda j, *_: (jnp.minimum(j, groups_a - 1), 0)),
                  pl.BlockSpec((tm * SUBLANES, LANES), lambda j, *_: (jnp.maximum(j - groups_a, 0), 0))],
        out_specs=pl.BlockSpec(memory_space=pl.ANY),
        scratch_shapes=[pltpu.VMEM((2 * tm * TOP_K * SUBLANES, LANES), F32), pltpu.VMEM((BM * SUBLANES, LANES), F32),
                        pltpu.SemaphoreType.DMA((2,)), pltpu.SemaphoreType.DMA(())],
    )
    return pl.pallas_call(
        functools.partial(_dispatch_kernel, n_e=n_e, n_blocks=n_blocks, groups_a=groups_a),
        grid_spec=grid_spec,
        out_shape=jax.ShapeDtypeStruct((n_blocks * BM * SUBLANES, LANES), F32),
        compiler_params=pltpu.CompilerParams(dimension_semantics=("arbitrary",), vmem_limit_bytes=VMEM_LIMIT),
        name="dispatch",
    )(*meta, lastblk, nused, lpos.reshape(nt, 1, tm * TOP_K), hn_a, hn_b)


def _experts_kernel(blk_e_ref, nused_ref, xs_ref, wgu_ref, bgu_ref, wd_ref, bd_ref, ys_ref, wgu_bf_ref, wd_bf_ref):
    i = pl.program_id(0)
    nused = nused_ref[0]
    d_ff2 = wgu_ref.shape[2]
    grp = 2 * LANES

    @pl.when(i < nused)
    def _():
        e = blk_e_ref[i]
        e_prev = blk_e_ref[jnp.maximum(i - 1, 0)]

        @pl.when((i == 0) | (e != e_prev))
        def _():
            pr = _iota((grp, grp), 0)
            pc = _iota((grp, grp), 1)
            perm = (pr == jnp.where(pc < LANES, 2 * pc, 2 * (pc - LANES) + 1)).astype(BF16)
            for g in range(d_ff2 // grp):
                w = wgu_ref[0, :, g * grp:(g + 1) * grp].astype(BF16)
                wgu_bf_ref[:, g * grp:(g + 1) * grp] = _dot(w, perm).astype(BF16)
            wd_bf_ref[...] = wd_ref[0].astype(BF16)

        x = jnp.concatenate([_load_token_tiles(xs_ref, BM, s).astype(BF16) for s in range(SUBLANES)], axis=1)
        hgu = _dot(x, wgu_bf_ref[...]) + bgu_ref[0]
        acts = []
        for g in range(d_ff2 // grp):
            gate = jnp.minimum(hgu[:, g * grp:g * grp + LANES], SWIGLU_LIMIT)
            up = jnp.clip(hgu[:, g * grp + LANES:(g + 1) * grp], -SWIGLU_LIMIT, SWIGLU_LIMIT)
            glu = gate * (1.0 / (1.0 + jnp.exp(-SWIGLU_ALPHA * gate)))
            acts.append(((up + 1.0) * glu).astype(BF16))
        act = jnp.concatenate(acts, axis=1)
        _store_token_tiles(ys_ref, _dot(act, wd_bf_ref[...]) + bd_ref[0])

    @pl.when(i >= nused)
    def _():
        ys_ref[...] = jnp.zeros_like(ys_ref)


def _experts(xs, blk_e, nused, w_gate_up, b_gu_perm, w_down, b_down):
    n_e, d, d_ff2 = w_gate_up.shape
    n_blocks = xs.shape[0] // (BM * SUBLANES)
    expert = lambda i, be, nu: be[jnp.minimum(i, jnp.maximum(nu[0] - 1, 0))]
    rows_spec = pl.BlockSpec((BM * SUBLANES, LANES), lambda i, be, nu: (i, 0))
    grid_spec = pltpu.PrefetchScalarGridSpec(
        num_scalar_prefetch=2,
        grid=(n_blocks,),
        in_specs=[rows_spec,
                  pl.BlockSpec((1, d, d_ff2), lambda i, be, nu: (expert(i, be, nu), 0, 0)),
                  pl.BlockSpec((1, 1, d_ff2), lambda i, be, nu: (expert(i, be, nu), 0, 0)),
                  pl.BlockSpec((1, d_ff2 // 2, d), lambda i, be, nu: (expert(i, be, nu), 0, 0)),
                  pl.BlockSpec((1, 1, d), lambda i, be, nu: (expert(i, be, nu), 0, 0))],
        out_specs=rows_spec,
        scratch_shapes=[pltpu.VMEM((d, d_ff2), BF16), pltpu.VMEM((d_ff2 // 2, d), BF16)],
    )
    return pl.pallas_call(
        _experts_kernel,
        grid_spec=grid_spec,
        out_shape=jax.ShapeDtypeStruct(xs.shape, F32),
        compiler_params=pltpu.CompilerParams(dimension_semantics=("arbitrary",), vmem_limit_bytes=VMEM_LIMIT),
        name="experts",
    )(blk_e, nused, xs, w_gate_up, b_gu_perm, w_down, b_down)


def _combine_kernel(off_ref, cnt_ref, lst_ref, lpos_ref, gates_ref, h_ref, g_ref, ys_ref, out_ref,
                    stage_ref, acc_ref, sem, *, n_e, g0):
    i = pl.program_id(0)
    ng = pl.num_programs(0)
    tm, d = h_ref.shape
    slot = (g0 + i) % 2
    meta = (off_ref, cnt_ref, lst_ref)

    def chunks(g, s, wait):
        _group_chunks(meta, g, n_e, tm,
                      lambda lrow, grow, size: pltpu.make_async_copy(
                          _rows(ys_ref, grow, size), _rows(stage_ref, s * (tm * TOP_K) + lrow, size),
                          sem.at[s]), wait)

    @pl.when(i == 0)
    def _():
        chunks(g0, slot, False)

    @pl.when(i + 1 < ng)
    def _():
        chunks(g0 + i + 1, 1 - slot, False)

    chunks(g0 + i, slot, True)

    def mix(t, c):
        acc = None
        for k in range(TOP_K):
            pos = pl.multiple_of(lpos_ref[0, 0, t * TOP_K + k], SUBLANES)
            term = stage_ref[pl.ds(pos, SUBLANES), :] * gates_ref[0, 0, t * TOP_K + k]
            acc = term if acc is None else acc + term
        acc_ref[pl.ds(pl.multiple_of(t * SUBLANES, SUBLANES), SUBLANES), :] = acc
        return c

    lax.fori_loop(0, tm, mix, 0, unroll=8)

    h = h_ref[...]
    parts = []
    sq = jnp.zeros((tm, LANES), F32)
    for s in range(d // LANES):
        y = h[:, s * LANES:(s + 1) * LANES] + _load_token_tiles(acc_ref, tm, s)
        sq = sq + y * y
        parts.append(y)
    rinv = lax.rsqrt(jnp.sum(sq, axis=-1, keepdims=True) / d + EPS)
    out_ref[...] = jnp.concatenate(parts, axis=1) * rinv * g_ref[...]


def _combine(ys, h, lpos, gates, meta, g_final, tm, n_e, g0):
    n, d = h.shape
    grid_spec = pltpu.PrefetchScalarGridSpec(
        num_scalar_prefetch=3,
        grid=(n // tm,),
        in_specs=[pl.BlockSpec((1, 1, tm * TOP_K), lambda i, *_: (g0 + i, 0, 0), memory_space=pltpu.SMEM),
                  pl.BlockSpec((1, 1, tm * TOP_K), lambda i, *_: (g0 + i, 0, 0), memory_space=pltpu.SMEM),
                  pl.BlockSpec((tm, d), lambda i, *_: (i, 0)),
                  pl.BlockSpec((1, d), lambda i, *_: (0, 0)),
                  pl.BlockSpec(memory_space=pl.ANY)],
        out_specs=pl.BlockSpec((tm, d), lambda i, *_: (i, 0)),
        scratch_shapes=[pltpu.VMEM((2 * tm * TOP_K * SUBLANES, LANES), F32), pltpu.VMEM((tm * SUBLANES, LANES), F32),
                        pltpu.SemaphoreType.DMA((2,))],
    )
    ngroups = lpos.shape[0] // tm
    return pl.pallas_call(
        functools.partial(_combine_kernel, n_e=n_e, g0=g0),
        grid_spec=grid_spec,
        out_shape=jax.ShapeDtypeStruct((n, d), F32),
        compiler_params=pltpu.CompilerParams(dimension_semantics=("arbitrary",), vmem_limit_bytes=VMEM_LIMIT),
        name="combine",
    )(*meta, lpos.reshape(ngroups, 1, tm * TOP_K), gates.reshape(ngroups, 1, tm * TOP_K), h, g_final, ys)


def _largest_tile(cands, *sizes):
    for c in cands:
        if all(s % c == 0 for s in sizes):
            return c
    raise ValueError(f"no tile in {cands} divides {sizes}")


def kernel(x_prompt, x_sample, mem_prompt, cache_win_k, cache_win_v, state_conv, cache_mem_k, cache_mem_v, g_attn_norm, w_in, conv_w, attn_sinks, g_mem_norm, w_mem_kv, g_mix_out, w_out, g_ffn_norm, w_router, b_router, w_gate_up, b_gate_up, w_down, b_down, g_final):
    depth = w_in.shape[0]
    assert depth == 1, "single-layer step"
    b, s, d = x_prompt.shape
    nb, t_dec, _ = x_sample.shape
    n_e = w_router.shape[2]
    d_ff2 = w_gate_up.shape[3]
    cw = conv_w.shape[2]
    win = cache_win_k.shape[2]
    m_tok = cache_mem_k.shape[2]
    assert s % TQ == 0 and nb % BB == 0 and win == WINDOW and t_dec <= SUBLANES and d_ff2 % (2 * LANES) == 0
    assert d == SUBLANES * LANES, "token-tile layout: one token is one (8, 128) f32 tile"

    row = lambda a: a.reshape(1, -1)
    w_in_bf = w_in[0].astype(BF16)
    w_out_bf = w_out[0].astype(BF16)
    w_r_bf = w_router[0].astype(BF16)
    sinks = attn_sinks[0].astype(F32)
    shared = (row(g_attn_norm[0]), w_in_bf, conv_w[0], row(g_mix_out[0]), w_out_bf, row(g_ffn_norm[0]),
              w_r_bf, row(b_router[0]))

    mk_p, mv_p, mkt, mvb = _memkv(mem_prompt, row(g_mem_norm[0]), w_mem_kv[0].astype(BF16))
    h_p, hn_p, tope_p, gates_p, lastk, lastv, convst = _mixer_p(x_prompt, sinks, *shared, mkt, mvb)

    zeros = lambda r: jnp.zeros((nb, r, cw), F32)
    st = state_conv[0]
    pm1 = jnp.concatenate([st[:, 1:2], zeros(t_dec - 1)], axis=1).reshape(nb * t_dec, cw)
    pm2 = jnp.concatenate([st, zeros(t_dec - 2)], axis=1).reshape(nb * t_dec, cw)
    h_s, hn_s, tope_s, gates_s, nwk, nwv, u_s = _mixer_s(
        x_sample.reshape(nb * t_dec, d), t_dec, pm1, pm2,
        cache_win_k[0].reshape(nb, win, KV_WIDTH), cache_win_v[0].reshape(nb, win, KV_WIDTH),
        cache_mem_k[0].reshape(nb, m_tok, MEM_WIDTH), cache_mem_v[0].reshape(nb, m_tok, MEM_WIDTH),
        sinks, *shared)

    n_p, n_s = b * s, nb * t_dec
    n = n_p + n_s
    tm = _largest_tile((512, 256, 128, 64, 32, 16, 8), n_p, n_s)
    tope = jnp.concatenate([tope_p, tope_s], axis=0)
    gates = jnp.concatenate([gates_p, gates_s], axis=0)

    lpos, cnt_f = _lpos(tope, tm, n_e)
    cnt = cnt_f[:, 0, :].astype(jnp.int32)
    counts = jnp.sum(cnt, axis=0)
    padded = (counts + BM - 1) // BM * BM
    pad_ends = jnp.cumsum(padded)
    pad_starts = pad_ends - padded
    nk = n * TOP_K
    n_blocks = -(-nk // BM) + n_e
    nused = (pad_ends[-1:] // BM).astype(jnp.int32)
    blk_start = jnp.arange(n_blocks, dtype=jnp.int32) * BM
    blk_e = jnp.minimum(jnp.sum((pad_ends[None, :] <= blk_start[:, None]).astype(jnp.int32), axis=1), n_e - 1)
    lastblk = jnp.where(padded > 0, pad_ends // BM - 1, -1).astype(jnp.int32)
    off = pad_starts[None, :] + jnp.cumsum(cnt, axis=0) - cnt
    lstart = jnp.cumsum(cnt, axis=1) - cnt
    meta = (off.reshape(-1).astype(jnp.int32), cnt.reshape(-1), lstart.reshape(-1).astype(jnp.int32))

    xs = _dispatch(hn_p, hn_s, lpos, meta, lastblk, nused, tm, n_e, n_blocks)

    grp = 2 * LANES
    b_gu = b_gate_up[0].reshape(n_e, d_ff2 // grp, LANES, 2).transpose(0, 1, 3, 2).reshape(n_e, 1, d_ff2)
    ys = _experts(xs, blk_e, nused, w_gate_up[0], b_gu, w_down[0], b_down[0].reshape(n_e, 1, d))

    g_fin = row(g_final)
    y_p = _combine(ys, h_p, lpos, gates, meta, g_fin, tm, n_e, 0)
    y_s = _combine(ys, h_s, lpos, gates, meta, g_fin, tm, n_e, n_p // tm)

    kv5 = lambda a, bsz, r, hds: a.reshape(1, bsz, r, hds, HEAD_DIM)
    return (y_p.reshape(b, s, d), y_s.reshape(nb, t_dec, d),
            kv5(lastk, b, WINDOW, N_KV_HEADS), kv5(lastv, b, WINDOW, N_KV_HEADS),
            convst.reshape(1, b, 2, cw),
            kv5(mk_p, b, m_tok, N_MEM_HEADS), kv5(mv_p, b, m_tok, N_MEM_HEADS),
            kv5(nwk, nb, win, N_KV_HEADS), kv5(nwv, nb, win, N_KV_HEADS),
            u_s.reshape(nb, t_dec, cw)[:, t_dec - 2:].reshape(1, nb, 2, cw))
```

```python
import functools

import jax
import jax.numpy as jnp
from jax import lax
from jax.experimental import pallas as pl
from jax.experimental.pallas import tpu as pltpu

F32 = jnp.float32
BF16 = jnp.bfloat16

HEAD_DIM = 64
N_Q_HEADS = 8
N_KV_HEADS = 2
WINDOW = 128
ATTN_WIDTH = N_Q_HEADS * HEAD_DIM
KV_WIDTH = N_KV_HEADS * HEAD_DIM
N_MEM_HEADS = 4
MEM_WIDTH = N_MEM_HEADS * HEAD_DIM
TOP_K = 4
SWIGLU_LIMIT = 7.0
SWIGLU_ALPHA = 1.702
EPS = 1e-5
ATTN_SCALE = HEAD_DIM ** -0.5
ALIBI_SLOPES = tuple(2.0 ** (-8.0 * (h + 1) / N_Q_HEADS) for h in range(N_Q_HEADS))

LANES = 128
SUBLANES = 8
VMEM_LIMIT = 56 * 1024 * 1024

TQ = 256
BB = 16
BM = 512
REP = 8
W_PARTS = 4


def _rms(x):
    return x * lax.rsqrt(jnp.mean(x * x, axis=-1, keepdims=True) + EPS)


def _dot(a, b):
    return jnp.dot(a, b, preferred_element_type=F32)


def _dot_nt(a, b):
    return lax.dot_general(a, b, (((1,), (1,)), ((), ())), preferred_element_type=F32)


def _iota(shape, axis):
    return lax.broadcasted_iota(jnp.int32, shape, axis)


def _store_token_tiles(ref, x, base=0):
    t = x.shape[0]
    for s in range(x.shape[1] // LANES):
        ref[pl.ds(base + s, t, stride=SUBLANES), :] = x[:, s * LANES:(s + 1) * LANES]


def _load_token_tiles(ref, t, s, base=0):
    return ref[pl.ds(base + s, t, stride=SUBLANES), :]


def _memkv_kernel(mem_ref, g_ref, w_ref, mk_ref, mv_ref, mkt_ref, mvb_ref):
    xn = (_rms(mem_ref[0]) * g_ref[...]).astype(BF16)
    kv = _dot(xn, w_ref[...])
    mk = kv[:, :MEM_WIDTH]
    mv = kv[:, MEM_WIDTH:]
    mk_ref[0] = mk
    mv_ref[0] = mv
    mkt_ref[0] = mk.T.astype(BF16)
    mvb_ref[0] = mv.astype(BF16)


def _memkv(mem, g, w_bf):
    b, m, d = mem.shape
    out_f = jax.ShapeDtypeStruct((b, m, MEM_WIDTH), F32)
    out_b = jax.ShapeDtypeStruct((b, m, MEM_WIDTH), BF16)
    out_t = jax.ShapeDtypeStruct((b, MEM_WIDTH, m), BF16)
    blk = lambda r, c: pl.BlockSpec((1, r, c), lambda i: (i, 0, 0))
    return pl.pallas_call(
        _memkv_kernel,
        grid=(b,),
        in_specs=[blk(m, d), pl.BlockSpec((1, d), lambda i: (0, 0)),
                  pl.BlockSpec((d, 2 * MEM_WIDTH), lambda i: (0, 0))],
        out_specs=[blk(m, MEM_WIDTH), blk(m, MEM_WIDTH), blk(MEM_WIDTH, m), blk(m, MEM_WIDTH)],
        out_shape=[out_f, out_f, out_t, out_b],
        name="memkv",
    )(mem, g, w_bf)


def _router_topk(hn, w_r_ref, b_r_ref, tope_ref, gates_ref):
    n_e = w_r_ref.shape[1]
    logits = _dot(hn.astype(BF16), w_r_ref[...]) + b_r_ref[...]
    rows = logits.shape[0]
    col = _iota((rows, n_e), 1).astype(F32)
    vals, idxs = [], []
    cur = logits
    for _ in range(TOP_K):
        m = jnp.max(cur, axis=-1, keepdims=True)
        idx = jnp.min(jnp.where(cur == m, col, float(n_e)), axis=-1, keepdims=True)
        vals.append(m)
        idxs.append(idx)
        cur = jnp.where(col == idx, -jnp.inf, cur)
    exps = [jnp.exp(v - vals[0]) for v in vals]
    tot = exps[0] + exps[1] + exps[2] + exps[3]
    col4 = _iota((rows, TOP_K), 1)
    te = jnp.zeros((rows, TOP_K), F32)
    ga = jnp.zeros((rows, TOP_K), F32)
    for k in range(TOP_K):
        te = jnp.where(col4 == k, idxs[k], te)
        ga = jnp.where(col4 == k, exps[k] / tot, ga)
    tope_ref[...] = te.astype(jnp.int32)
    gates_ref[...] = ga


def _mix_out(x, attn, conv_out, cross, g_mix_ref, w_out_ref, g_ffn_ref, w_r_ref, b_r_ref,
             h_ref, hn_ref, tope_ref, gates_ref):
    mix = jnp.concatenate([_rms(attn), _rms(conv_out), _rms(cross)], axis=-1) * g_mix_ref[...]
    h = x + _dot(mix.astype(BF16), w_out_ref[...])
    hn = _rms(h) * g_ffn_ref[...]
    h_ref[...] = h
    _store_token_tiles(hn_ref, hn)
    _router_topk(hn, w_r_ref, b_r_ref, tope_ref, gates_ref)


def _swa_block(q_blk, kk, vv, prev_lim, sinks_ref):
    blk = WINDOW
    lane = _iota((2 * blk, KV_WIDTH), 1)
    lo = lane < HEAD_DIM
    kk_r = pltpu.roll(kk, HEAD_DIM, axis=1)
    vv_r = pltpu.roll(vv, HEAD_DIM, axis=1)
    kdup = [jnp.where(lo, kk, kk_r).astype(BF16), jnp.where(lo, kk_r, kk).astype(BF16)]
    vlo = [jnp.where(lo, vv, 0.0).astype(BF16), jnp.where(lo, vv_r, 0.0).astype(BF16)]
    vhi = [jnp.where(lo, 0.0, vv_r).astype(BF16), jnp.where(lo, 0.0, vv).astype(BF16)]
    qi = _iota((blk, 2 * blk), 0)
    kj = _iota((blk, 2 * blk), 1)
    dist = blk + qi - kj
    mask = (dist >= 0) & (dist < WINDOW) & (kj >= prev_lim)
    distf = dist.astype(F32)
    qlo = _iota((blk, 2 * HEAD_DIM), 1) < HEAD_DIM
    outs = []
    for p in range(N_Q_HEADS // 2):
        kh = (2 * p) // (N_Q_HEADS // N_KV_HEADS)
        qp = q_blk[:, p * 2 * HEAD_DIM:(p + 1) * 2 * HEAD_DIM]
        acc = None
        for e in range(2):
            h = 2 * p + e
            qm = jnp.where(qlo if e == 0 else jnp.logical_not(qlo), qp, 0.0).astype(BF16)
            s = _dot_nt(qm, kdup[kh]) - ALIBI_SLOPES[h] * distf
            s = jnp.where(mask, s, -jnp.inf)
            sink = sinks_ref[h]
            m = jnp.maximum(jnp.max(s, axis=-1, keepdims=True), sink)
            pe = jnp.exp(s - m)
            denom = jnp.sum(pe, axis=-1, keepdims=True) + jnp.exp(sink - m)
            o = _dot(pe.astype(BF16), (vlo if e == 0 else vhi)[kh]) / denom
            acc = o if acc is None else acc + o
        outs.append(acc)
    return jnp.concatenate(outs, axis=1)


def _mem_attend_shared(mq, mkt, mvb):
    t = mq.shape[0]
    m_tok = mvb.shape[0]
    qhead = _iota((t, MEM_WIDTH), 1) // HEAD_DIM
    vhead = _iota((m_tok, MEM_WIDTH), 1) // HEAD_DIM
    cross = None
    for h in range(N_MEM_HEADS):
        qm = jnp.where(qhead == h, mq, 0.0).astype(BF16)
        s = _dot(qm, mkt)
        m = jnp.max(s, axis=-1, keepdims=True)
        pe = jnp.exp(s - m)
        denom = jnp.sum(pe, axis=-1, keepdims=True)
        vm = jnp.where(vhead == h, mvb, jnp.zeros_like(mvb))
        o = _dot(pe.astype(BF16), vm) / denom
        cross = o if cross is None else cross + o
    return cross


def _mixer_p_kernel(sinks_ref, x_ref, g_attn_ref, w_in_ref, conv_w_ref, g_mix_ref, w_out_ref, g_ffn_ref,
                    w_r_ref, b_r_ref, mkt_ref, mvb_ref,
                    h_ref, hn_ref, tope_ref, gates_ref, lastk_ref, lastv_ref, convst_ref,
                    ck_ref, cv_ref, cu_ref):
    j = pl.program_id(1)
    nj = pl.num_programs(1)

    @pl.when(j == 0)
    def _():
        ck_ref[...] = jnp.zeros_like(ck_ref)
        cv_ref[...] = jnp.zeros_like(cv_ref)
        cu_ref[...] = jnp.zeros_like(cu_ref)

    x = x_ref[0]
    xn = (_rms(x) * g_attn_ref[...]).astype(BF16)
    z = _dot(xn, w_in_ref[...])
    c0 = ATTN_WIDTH
    c1 = c0 + KV_WIDTH
    c2 = c1 + KV_WIDTH
    cw = conv_w_ref.shape[1]
    c3, c4, c5 = c2 + cw, c2 + 2 * cw, c2 + 3 * cw
    q = z[:, :c0] * ATTN_SCALE
    k = z[:, c0:c1]
    v = z[:, c1:c2]
    cb = z[:, c2:c3]
    cc = z[:, c3:c4]
    cvv = z[:, c4:c5]
    mq = z[:, c5:] * ATTN_SCALE

    blk = WINDOW
    attn_blocks = []
    for i in range(TQ // blk):
        if i == 0:
            pk, pv = ck_ref[...], cv_ref[...]
            prev_lim = jnp.where(j > 0, 0, blk)
        else:
            pk, pv = k[(i - 1) * blk:i * blk], v[(i - 1) * blk:i * blk]
            prev_lim = 0
        kk = jnp.concatenate([pk, k[i * blk:(i + 1) * blk]], axis=0)
        vv = jnp.concatenate([pv, v[i * blk:(i + 1) * blk]], axis=0)
        attn_blocks.append(_swa_block(q[i * blk:(i + 1) * blk], kk, vv, prev_lim, sinks_ref))
    attn = jnp.concatenate(attn_blocks, axis=0)
    ck_ref[...] = k[TQ - blk:]
    cv_ref[...] = v[TQ - blk:]

    u = cc * cvv
    row = _iota(u.shape, 0)
    u1 = jnp.where(row == 0, cu_ref[SUBLANES - 1:SUBLANES, :], pltpu.roll(u, 1, axis=0))
    u2 = jnp.where(row == 0, cu_ref[SUBLANES - 2:SUBLANES - 1, :],
                   jnp.where(row == 1, cu_ref[SUBLANES - 1:SUBLANES, :], pltpu.roll(u, 2, axis=0)))
    conv_out = cb * (conv_w_ref[0:1, :] * u2 + conv_w_ref[1:2, :] * u1 + conv_w_ref[2:3, :] * u)
    cu_ref[...] = u[TQ - SUBLANES:]

    cross = _mem_attend_shared(mq, mkt_ref[0], mvb_ref[0])

    @pl.when(j == nj - 1)
    def _():
        lastk_ref[0] = k[TQ - blk:]
        lastv_ref[0] = v[TQ - blk:]
        convst_ref[0] = u[TQ - 2:]

    _mix_out(x, attn, conv_out, cross, g_mix_ref, w_out_ref, g_ffn_ref, w_r_ref, b_r_ref,
             h_ref, hn_ref, tope_ref, gates_ref)


def _mixer_p(x, sinks, g_attn, w_in, conv_w, g_mix, w_out, g_ffn, w_r, b_r, mkt, mvb):
    b, s, d = x.shape
    nj = s // TQ
    n = b * s
    cw = conv_w.shape[1]
    full = lambda a: pl.BlockSpec(a.shape, lambda bi, ji, *_: (0,) * a.ndim)
    tok = lambda w: pl.BlockSpec((TQ, w), lambda bi, ji, *_: (bi * nj + ji, 0))
    per_b = lambda r, c: pl.BlockSpec((1, r, c), lambda bi, ji, *_: (bi, 0, 0))
    grid_spec = pltpu.PrefetchScalarGridSpec(
        num_scalar_prefetch=1,
        grid=(b, nj),
        in_specs=[pl.BlockSpec((1, TQ, d), lambda bi, ji, *_: (bi, ji, 0)),
                  full(g_attn), full(w_in), full(conv_w), full(g_mix), full(w_out), full(g_ffn),
                  full(w_r), full(b_r), per_b(MEM_WIDTH, mkt.shape[2]), per_b(mvb.shape[1], MEM_WIDTH)],
        out_specs=[tok(d), pl.BlockSpec((TQ * SUBLANES, LANES), lambda bi, ji, *_: (bi * nj + ji, 0)),
                   tok(TOP_K), tok(TOP_K),
                   per_b(WINDOW, KV_WIDTH), per_b(WINDOW, KV_WIDTH), per_b(2, cw)],
        scratch_shapes=[pltpu.VMEM((WINDOW, KV_WIDTH), F32), pltpu.VMEM((WINDOW, KV_WIDTH), F32),
                        pltpu.VMEM((SUBLANES, cw), F32)],
    )
    return pl.pallas_call(
        _mixer_p_kernel,
        grid_spec=grid_spec,
        out_shape=[jax.ShapeDtypeStruct((n, d), F32), jax.ShapeDtypeStruct((n * SUBLANES, LANES), F32),
                   jax.ShapeDtypeStruct((n, TOP_K), jnp.int32), jax.ShapeDtypeStruct((n, TOP_K), F32),
                   jax.ShapeDtypeStruct((b, WINDOW, KV_WIDTH), F32),
                   jax.ShapeDtypeStruct((b, WINDOW, KV_WIDTH), F32),
                   jax.ShapeDtypeStruct((b, 2, cw), F32)],
        compiler_params=pltpu.CompilerParams(dimension_semantics=("arbitrary", "arbitrary"),
                                             vmem_limit_bytes=VMEM_LIMIT),
        name="mixer_p",
    )(sinks, x, g_attn, w_in, conv_w, g_mix, w_out, g_ffn, w_r, b_r, mkt, mvb)


def _per_head_column(values, hrow):
    col = jnp.zeros(hrow.shape, F32)
    for h in range(N_Q_HEADS):
        col = jnp.where(hrow == h, values[h], col)
    return col


def _mixer_s_kernel(sinks_ref, x_ref, pm1_ref, pm2_ref, wk_ref, wv_ref, mk_ref, mv_ref,
                    g_attn_ref, w_in_ref, conv_w_ref, g_mix_ref, w_out_ref, g_ffn_ref, w_r_ref, b_r_ref,
                    h_ref, hn_ref, tope_ref, gates_ref, nwk_ref, nwv_ref, u_ref, *, t_dec):
    r_tok = BB * t_dec
    r_exp = r_tok * REP
    qrows = t_dec * REP
    x = x_ref[...]
    xn = (_rms(x) * g_attn_ref[...]).astype(BF16)
    z = _dot(xn, w_in_ref[...])
    c0 = ATTN_WIDTH
    c1 = c0 + KV_WIDTH
    c2 = c1 + KV_WIDTH
    cw = conv_w_ref.shape[1]
    c3, c4, c5 = c2 + cw, c2 + 2 * cw, c2 + 3 * cw
    q = z[:, :c0] * ATTN_SCALE
    k_new = z[:, c0:c1]
    v_new = z[:, c1:c2]
    cb = z[:, c2:c3]
    cc = z[:, c3:c4]
    cvv = z[:, c4:c5]
    mq = z[:, c5:] * ATTN_SCALE
    win = wk_ref.shape[1]

    xi = _iota((KV_WIDTH, ATTN_WIDTH), 0)
    xl = _iota((KV_WIDTH, ATTN_WIDTH), 1)
    q_per_kv = N_Q_HEADS // N_KV_HEADS
    expand = (xi == (xl // (q_per_kv * HEAD_DIM)) * HEAD_DIM + xl % HEAD_DIM).astype(BF16)
    rr = _iota((r_exp, r_tok), 0)
    rc = _iota((r_exp, r_tok), 1)
    rep = (rr // REP == rc).astype(BF16)

    hrow = _iota((r_exp, 1), 0) % REP
    trow = (_iota((r_exp, 1), 0) // REP) % t_dec
    slope_col = _per_head_column(ALIBI_SLOPES, hrow)
    sink_col = _per_head_column([sinks_ref[h] for h in range(N_Q_HEADS)], hrow)

    qexp = jnp.where(hrow == _iota((r_exp, ATTN_WIDTH), 1) // HEAD_DIM, _dot(rep, q.astype(BF16)), 0.0)
    kexp = _dot(wk_ref[...].reshape(BB * win, KV_WIDTH).astype(BF16), expand).astype(BF16)
    vexp = _dot(wv_ref[...].reshape(BB * win, KV_WIDTH).astype(BF16), expand).astype(BF16)
    s = jnp.einsum("bqc,bkc->bqk", qexp.astype(BF16).reshape(BB, qrows, ATTN_WIDTH),
                   kexp.reshape(BB, win, ATTN_WIDTH), preferred_element_type=F32).reshape(r_exp, win)
    scol = _iota((r_exp, win), 1)
    s = s - slope_col * (win + trow - scol).astype(F32)
    s = jnp.where(scol > trow, s, -jnp.inf)
    knew_exp = _dot(k_new.astype(BF16), expand).astype(BF16)
    vnew_exp = _dot(v_new.astype(BF16), expand).astype(BF16)
    s_new, v_rep = [], []
    for jn in range(t_dec):
        rep_j = (rc == (rr // qrows) * t_dec + jn).astype(BF16)
        k_rep = _dot(rep_j, knew_exp)
        v_rep.append(_dot(rep_j, vnew_exp))
        sj = jnp.sum(qexp * k_rep, axis=-1, keepdims=True) - slope_col * (trow - jn).astype(F32)
        s_new.append(jnp.where(trow >= jn, sj, -jnp.inf))
    m = jnp.maximum(jnp.max(s, axis=-1, keepdims=True), sink_col)
    for sj in s_new:
        m = jnp.maximum(m, sj)
    pe = jnp.exp(s - m)
    denom = jnp.sum(pe, axis=-1, keepdims=True) + jnp.exp(sink_col - m)
    o = jnp.einsum("bqk,bkc->bqc", pe.astype(BF16).reshape(BB, qrows, win),
                   vexp.reshape(BB, win, ATTN_WIDTH), preferred_element_type=F32).reshape(r_exp, ATTN_WIDTH)
    for jn in range(t_dec):
        pj = jnp.exp(s_new[jn] - m)
        denom = denom + pj
        o = o + pj * v_rep[jn]
    o = jnp.where(hrow == _iota((r_exp, ATTN_WIDTH), 1) // HEAD_DIM, o / denom, 0.0)
    attn = jnp.sum(o.reshape(r_tok, REP, ATTN_WIDTH), axis=1)

    m_tok = mk_ref.shape[1]
    mhead = _iota((r_exp, MEM_WIDTH), 1) // HEAD_DIM
    mqexp = jnp.where(hrow == mhead, _dot(rep, mq.astype(BF16)), 0.0).astype(BF16)
    sm = jnp.einsum("bqc,bmc->bqm", mqexp.reshape(BB, qrows, MEM_WIDTH), mk_ref[...].astype(BF16),
                    preferred_element_type=F32).reshape(r_exp, m_tok)
    mm = jnp.max(sm, axis=-1, keepdims=True)
    pm = jnp.exp(sm - mm)
    dm = jnp.sum(pm, axis=-1, keepdims=True)
    om = jnp.einsum("bqm,bmc->bqc", pm.astype(BF16).reshape(BB, qrows, m_tok), mv_ref[...].astype(BF16),
                    preferred_element_type=F32).reshape(r_exp, MEM_WIDTH)
    om = jnp.where(hrow == mhead, om / dm, 0.0)
    cross = jnp.sum(om.reshape(r_tok, REP, MEM_WIDTH), axis=1)

    u = cc * cvv
    tt = _iota(u.shape, 0) % t_dec
    u1 = jnp.where(tt >= 1, pltpu.roll(u, 1, axis=0), pm1_ref[...])
    u2 = jnp.where(tt >= 2, pltpu.roll(u, 2, axis=0), pm2_ref[...])
    conv_out = cb * (conv_w_ref[0:1, :] * u2 + conv_w_ref[1:2, :] * u1 + conv_w_ref[2:3, :] * u)
    u_ref[...] = u

    nwk_ref[:, 0:win - t_dec, :] = wk_ref[:, t_dec:win, :]
    nwv_ref[:, 0:win - t_dec, :] = wv_ref[:, t_dec:win, :]
    for b in range(BB):
        nwk_ref[b, win - t_dec:win, :] = k_new[b * t_dec:(b + 1) * t_dec, :]
        nwv_ref[b, win - t_dec:win, :] = v_new[b * t_dec:(b + 1) * t_dec, :]

    _mix_out(x, attn, conv_out, cross, g_mix_ref, w_out_ref, g_ffn_ref, w_r_ref, b_r_ref,
             h_ref, hn_ref, tope_ref, gates_ref)


def _mixer_s(x2, t_dec, pm1, pm2, wk, wv, mk, mv, sinks, g_attn, w_in, conv_w, g_mix, w_out, g_ffn, w_r, b_r):
    n, d = x2.shape
    nb = wk.shape[0]
    win = wk.shape[1]
    m_tok = mk.shape[1]
    cw = conv_w.shape[1]
    r_tok = BB * t_dec
    full = lambda a: pl.BlockSpec(a.shape, lambda i, *_: (0,) * a.ndim)
    tok = lambda w: pl.BlockSpec((r_tok, w), lambda i, *_: (i, 0))
    per_b = lambda r, c: pl.BlockSpec((BB, r, c), lambda i, *_: (i, 0, 0))
    grid_spec = pltpu.PrefetchScalarGridSpec(
        num_scalar_prefetch=1,
        grid=(nb // BB,),
        in_specs=[tok(d), tok(cw), tok(cw), per_b(win, KV_WIDTH), per_b(win, KV_WIDTH),
                  per_b(m_tok, MEM_WIDTH), per_b(m_tok, MEM_WIDTH),
                  full(g_attn), full(w_in), full(conv_w), full(g_mix), full(w_out), full(g_ffn),
                  full(w_r), full(b_r)],
        out_specs=[tok(d), pl.BlockSpec((r_tok * SUBLANES, LANES), lambda i, *_: (i, 0)),
                   tok(TOP_K), tok(TOP_K), per_b(win, KV_WIDTH), per_b(win, KV_WIDTH), tok(cw)],
    )
    return pl.pallas_call(
        functools.partial(_mixer_s_kernel, t_dec=t_dec),
        grid_spec=grid_spec,
        out_shape=[jax.ShapeDtypeStruct((n, d), F32), jax.ShapeDtypeStruct((n * SUBLANES, LANES), F32),
                   jax.ShapeDtypeStruct((n, TOP_K), jnp.int32), jax.ShapeDtypeStruct((n, TOP_K), F32),
                   jax.ShapeDtypeStruct((nb, win, KV_WIDTH), F32), jax.ShapeDtypeStruct((nb, win, KV_WIDTH), F32),
                   jax.ShapeDtypeStruct((n, cw), F32)],
        compiler_params=pltpu.CompilerParams(dimension_semantics=("arbitrary",), vmem_limit_bytes=VMEM_LIMIT),
        name="mixer_s",
    )(sinks, x2, pm1, pm2, wk, wv, mk, mv, g_attn, w_in, conv_w, g_mix, w_out, g_ffn, w_r, b_r)


def _lpos_kernel(tope_ref, lpos_ref, counts_ref, tri_ref, *, n_e):
    i = pl.program_id(0)
    tm = tope_ref.shape[0]

    @pl.when(i == 0)
    def _():
        tri_ref[...] = (_iota((tm, tm), 0) > _iota((tm, tm), 1)).astype(BF16)

    te = tope_ref[...]
    col = _iota((tm, n_e), 1)
    hits = [te[:, k:k + 1] == col for k in range(TOP_K)]
    onehot = jnp.zeros((tm, n_e), F32)
    lower = jnp.zeros((tm, n_e), F32)
    for k in range(TOP_K):
        onehot = onehot + hits[k].astype(F32)
        lower = lower + (te[:, k:k + 1] < col).astype(F32)
    ahead = _dot(tri_ref[...], onehot.astype(BF16)) + jnp.sum(lower, axis=0, keepdims=True)
    col4 = _iota((tm, TOP_K), 1)
    pos = jnp.zeros((tm, TOP_K), F32)
    for k in range(TOP_K):
        pos = jnp.where(col4 == k, jnp.sum(jnp.where(hits[k], ahead, 0.0), axis=-1, keepdims=True), pos)
    half = (i % 2) * (tm * TOP_K)
    lpos_ref[...] = (pos.astype(jnp.int32) + half) * SUBLANES
    counts_ref[0] = jnp.sum(onehot, axis=0, keepdims=True)


def _lpos(tope, tm, n_e):
    n = tope.shape[0]
    return pl.pallas_call(
        functools.partial(_lpos_kernel, n_e=n_e),
        grid=(n // tm,),
        in_specs=[pl.BlockSpec((tm, TOP_K), lambda i: (i, 0))],
        out_specs=[pl.BlockSpec((tm, TOP_K), lambda i: (i, 0)), pl.BlockSpec((1, 1, n_e), lambda i: (i, 0, 0))],
        out_shape=[jax.ShapeDtypeStruct((n, TOP_K), jnp.int32), jax.ShapeDtypeStruct((n // tm, 1, n_e), F32)],
        scratch_shapes=[pltpu.VMEM((tm, tm), BF16)],
        compiler_params=pltpu.CompilerParams(dimension_semantics=("arbitrary",)),
        name="lpos",
    )(tope)


def _rows(ref, first_row, n_rows):
    return ref.at[pl.ds(pl.multiple_of(first_row * SUBLANES, SUBLANES), n_rows * SUBLANES)]


def _group_chunks(meta, g, n_e, max_rows, make_copy, wait):
    off_ref, cnt_ref, lst_ref = meta
    sizes = [1 << b for b in range(max_rows.bit_length() - 1, -1, -1)]

    def per_expert(e, c):
        cnt = cnt_ref[g * n_e + e]
        off = off_ref[g * n_e + e]
        lst = lst_ref[g * n_e + e]
        done = 0
        for size in sizes:
            bit = cnt & size

            @pl.when(bit != 0)
            def _():
                cp = make_copy(lst + done, off + done, size)
                if wait:
                    cp.wait()
                else:
                    cp.start()
            done = done + bit
        return c

    lax.fori_loop(0, n_e, per_expert, 0)


def _dispatch_kernel(off_ref, cnt_ref, lst_ref, lastblk_ref, nused_ref, lpos_ref, hn_a_ref, hn_b_ref, xs_ref,
                     stage_ref, zbuf_ref, sem, zsem, *, n_e, n_blocks, groups_a):
    j = pl.program_id(0)
    nj = pl.num_programs(0)
    tm = hn_a_ref.shape[0] // SUBLANES
    slot = j % 2
    meta = (off_ref, cnt_ref, lst_ref)

    def chunks(g, s, wait):
        _group_chunks(meta, g, n_e, tm,
                      lambda lrow, grow, size: pltpu.make_async_copy(
                          _rows(stage_ref, s * (tm * TOP_K) + lrow, size), _rows(xs_ref, grow, size),
                          sem.at[s]), wait)

    def zero_copy(blk):
        return pltpu.make_async_copy(zbuf_ref, _rows(xs_ref, blk * BM, BM), zsem)

    @pl.when(j == 0)
    def _():
        zbuf_ref[...] = jnp.zeros_like(zbuf_ref)

        def expert_fill(wait):
            def body(e, c):
                @pl.when(lastblk_ref[e] >= 0)
                def _():
                    cp = zero_copy(lastblk_ref[e])
                    cp.wait() if wait else cp.start()
                return c
            return body

        def tail_fill(wait):
            def body(b, c):
                cp = zero_copy(b)
                cp.wait() if wait else cp.start()
                return c
            return body

        lax.fori_loop(0, n_e, expert_fill(False), 0)
        lax.fori_loop(nused_ref[0], n_blocks, tail_fill(False), 0)
        lax.fori_loop(0, n_e, expert_fill(True), 0)
        lax.fori_loop(nused_ref[0], n_blocks, tail_fill(True), 0)

    @pl.when(j >= 2)
    def _():
        chunks(j - 2, slot, True)

    def place_from(hn_ref):
        def place(t, c):
            tile = hn_ref[pl.ds(pl.multiple_of(t * SUBLANES, SUBLANES), SUBLANES), :]
            for k in range(TOP_K):
                pos = pl.multiple_of(lpos_ref[0, 0, t * TOP_K + k], SUBLANES)
                stage_ref[pl.ds(pos, SUBLANES), :] = tile
            return c
        lax.fori_loop(0, tm, place, 0, unroll=8)

    @pl.when(j < groups_a)
    def _():
        place_from(hn_a_ref)

    @pl.when(j >= groups_a)
    def _():
        place_from(hn_b_ref)

    chunks(j, slot, False)

    @pl.when(j == nj - 1)
    def _():
        @pl.when(j >= 1)
        def _():
            chunks(j - 1, 1 - slot, True)
        chunks(j, slot, True)


def _dispatch(hn_a, hn_b, lpos, meta, lastblk, nused, tm, n_e, n_blocks):
    nt = lpos.shape[0] // tm
    groups_a = hn_a.shape[0] // (tm * SUBLANES)
    assert groups_a >= 1 and groups_a + hn_b.shape[0] // (tm * SUBLANES) == nt
    grid_spec = pltpu.PrefetchScalarGridSpec(
        num_scalar_prefetch=5,
        grid=(nt,),
        in_specs=[pl.BlockSpec((1, 1, tm * TOP_K), lambda j, *_: (j, 0, 0), memory_space=pltpu.SMEM),
                  pl.BlockSpec((tm * SUBLANES, LANES), lambda j, *_: (jnp.minimum(j, groups_a - 1), 0)),
                  pl.BlockSpec((tm * SUBLANES, LANES), lambda j, *_: (jnp.maximum(j - groups_a, 0), 0))],
        out_specs=pl.BlockSpec(memory_space=pl.ANY),
        scratch_shapes=[pltpu.VMEM((2 * tm * TOP_K * SUBLANES, LANES), F32), pltpu.VMEM((BM * SUBLANES, LANES), F32),
                        pltpu.SemaphoreType.DMA((2,)), pltpu.SemaphoreType.DMA(())],
    )
    return pl.pallas_call(
        functools.partial(_dispatch_kernel, n_e=n_e, n_blocks=n_blocks, groups_a=groups_a),
        grid_spec=grid_spec,
        out_shape=jax.ShapeDtypeStruct((n_blocks * BM * SUBLANES, LANES), F32),
        compiler_params=pltpu.CompilerParams(dimension_semantics=("arbitrary",), vmem_limit_bytes=VMEM_LIMIT),
        name="dispatch",
    )(*meta, lastblk, nused, lpos.reshape(nt, 1, tm * TOP_K), hn_a, hn_b)


def _experts_kernel(blk_e_ref, nused_ref, next_e_ref, xs_ref, bgu_ref, bd_ref, wgu_hbm_ref, wd_hbm_ref, ys_ref,
                    wgu_ref, wd_ref, wgu_bf_ref, wd_bf_ref, wsem):
    i = pl.program_id(0)
    nused = nused_ref[0]
    d, d_ff2 = wgu_ref.shape
    grp = 2 * LANES

    def weight_copies(e):
        gu_cols = d_ff2 // W_PARTS
        dn_rows = (d_ff2 // 2) // W_PARTS
        cps = []
        for p in range(W_PARTS):
            cps.append(pltpu.make_async_copy(wgu_hbm_ref.at[e, :, pl.ds(p * gu_cols, gu_cols)],
                                             wgu_ref.at[:, pl.ds(p * gu_cols, gu_cols)], wsem.at[p]))
            cps.append(pltpu.make_async_copy(wd_hbm_ref.at[e, pl.ds(p * dn_rows, dn_rows), :],
                                             wd_ref.at[pl.ds(p * dn_rows, dn_rows), :], wsem.at[W_PARTS + p]))
        return cps

    @pl.when(i < nused)
    def _():
        e = blk_e_ref[i]
        e_prev = blk_e_ref[jnp.maximum(i - 1, 0)]

        @pl.when(i == 0)
        def _():
            for cp in weight_copies(e):
                cp.start()

        @pl.when((i == 0) | (e != e_prev))
        def _():
            for cp in weight_copies(e):
                cp.wait()
            pr = _iota((grp, grp), 0)
            pc = _iota((grp, grp), 1)
            perm = (pr == jnp.where(pc < LANES, 2 * pc, 2 * (pc - LANES) + 1)).astype(BF16)
            for g in range(d_ff2 // grp):
                w = wgu_ref[:, g * grp:(g + 1) * grp].astype(BF16)
                wgu_bf_ref[:, g * grp:(g + 1) * grp] = _dot(w, perm).astype(BF16)
            wd_bf_ref[...] = wd_ref[...].astype(BF16)

            @pl.when(next_e_ref[e] >= 0)
            def _():
                for cp in weight_copies(next_e_ref[e]):
                    cp.start()

        x = jnp.concatenate([_load_token_tiles(xs_ref, BM, s).astype(BF16) for s in range(SUBLANES)], axis=1)
        hgu = _dot(x, wgu_bf_ref[...]) + bgu_ref[0]
        acts = []
        for g in range(d_ff2 // grp):
            gate = jnp.minimum(hgu[:, g * grp:g * grp + LANES], SWIGLU_LIMIT)
            up = jnp.clip(hgu[:, g * grp + LANES:(g + 1) * grp], -SWIGLU_LIMIT, SWIGLU_LIMIT)
            glu = gate * (1.0 / (1.0 + jnp.exp(-SWIGLU_ALPHA * gate)))
            acts.append(((up + 1.0) * glu).astype(BF16))
        act = jnp.concatenate(acts, axis=1)
        _store_token_tiles(ys_ref, _dot(act, wd_bf_ref[...]) + bd_ref[0])

    @pl.when(i >= nused)
    def _():
        ys_ref[...] = jnp.zeros_like(ys_ref)


def _experts(xs, blk_e, nused, next_e, w_gate_up, b_gu_perm, w_down, b_down):
    n_e, d, d_ff2 = w_gate_up.shape
    n_blocks = xs.shape[0] // (BM * SUBLANES)
    expert = lambda i, be, nu, ne: be[jnp.minimum(i, jnp.maximum(nu[0] - 1, 0))]
    rows_spec = pl.BlockSpec((BM * SUBLANES, LANES), lambda i, *_: (i, 0))
    grid_spec = pltpu.PrefetchScalarGridSpec(
        num_scalar_prefetch=3,
        grid=(n_blocks,),
        in_specs=[rows_spec,
                  pl.BlockSpec((1, 1, d_ff2), lambda i, *s: (expert(i, *s), 0, 0)),
                  pl.BlockSpec((1, 1, d), lambda i, *s: (expert(i, *s), 0, 0)),
                  pl.BlockSpec(memory_space=pl.ANY), pl.BlockSpec(memory_space=pl.ANY)],
        out_specs=rows_spec,
        scratch_shapes=[pltpu.VMEM((d, d_ff2), F32), pltpu.VMEM((d_ff2 // 2, d), F32),
                        pltpu.VMEM((d, d_ff2), BF16), pltpu.VMEM((d_ff2 // 2, d), BF16),
                        pltpu.SemaphoreType.DMA((2 * W_PARTS,))],
    )
    return pl.pallas_call(
        _experts_kernel,
        grid_spec=grid_spec,
        out_shape=jax.ShapeDtypeStruct(xs.shape, F32),
        compiler_params=pltpu.CompilerParams(dimension_semantics=("arbitrary",), vmem_limit_bytes=VMEM_LIMIT),
        name="experts",
    )(blk_e, nused, next_e, xs, b_gu_perm, b_down, w_gate_up, w_down)


def _combine_kernel(off_ref, cnt_ref, lst_ref, lpos_ref, gates_ref, h_ref, g_ref, ys_ref, out_ref,
                    stage_ref, acc_ref, sem, *, n_e, g0):
    i = pl.program_id(0)
    ng = pl.num_programs(0)
    tm, d = h_ref.shape
    slot = (g0 + i) % 2
    meta = (off_ref, cnt_ref, lst_ref)

    def chunks(g, s, wait):
        _group_chunks(meta, g, n_e, tm,
                      lambda lrow, grow, size: pltpu.make_async_copy(
                          _rows(ys_ref, grow, size), _rows(stage_ref, s * (tm * TOP_K) + lrow, size),
                          sem.at[s]), wait)

    @pl.when(i == 0)
    def _():
        chunks(g0, slot, False)

    @pl.when(i + 1 < ng)
    def _():
        chunks(g0 + i + 1, 1 - slot, False)

    chunks(g0 + i, slot, True)

    def mix(t, c):
        acc = None
        for k in range(TOP_K):
            pos = pl.multiple_of(lpos_ref[0, 0, t * TOP_K + k], SUBLANES)
            term = stage_ref[pl.ds(pos, SUBLANES), :] * gates_ref[0, 0, t * TOP_K + k]
            acc = term if acc is None else acc + term
        acc_ref[pl.ds(pl.multiple_of(t * SUBLANES, SUBLANES), SUBLANES), :] = acc
        return c

    lax.fori_loop(0, tm, mix, 0, unroll=8)

    h = h_ref[...]
    parts = []
    sq = jnp.zeros((tm, LANES), F32)
    for s in range(d // LANES):
        y = h[:, s * LANES:(s + 1) * LANES] + _load_token_tiles(acc_ref, tm, s)
        sq = sq + y * y
        parts.append(y)
    rinv = lax.rsqrt(jnp.sum(sq, axis=-1, keepdims=True) / d + EPS)
    out_ref[...] = jnp.concatenate(parts, axis=1) * rinv * g_ref[...]


def _combine(ys, h, lpos, gates, meta, g_final, tm, n_e, g0):
    n, d = h.shape
    grid_spec = pltpu.PrefetchScalarGridSpec(
        num_scalar_prefetch=3,
        grid=(n // tm,),
        in_specs=[pl.BlockSpec((1, 1, tm * TOP_K), lambda i, *_: (g0 + i, 0, 0), memory_space=pltpu.SMEM),
                  pl.BlockSpec((1, 1, tm * TOP_K), lambda i, *_: (g0 + i, 0, 0), memory_space=pltpu.SMEM),
                  pl.BlockSpec((tm, d), lambda i, *_: (i, 0)),
                  pl.BlockSpec((1, d), lambda i, *_: (0, 0)),
                  pl.BlockSpec(memory_space=pl.ANY)],
        out_specs=pl.BlockSpec((tm, d), lambda i, *_: (i, 0)),
        scratch_shapes=[pltpu.VMEM((2 * tm * TOP_K * SUBLANES, LANES), F32), pltpu.VMEM((tm * SUBLANES, LANES), F32),
                        pltpu.SemaphoreType.DMA((2,))],
    )
    ngroups = lpos.shape[0] // tm
    return pl.pallas_call(
        functools.partial(_combine_kernel, n_e=n_e, g0=g0),
        grid_spec=grid_spec,
        out_shape=jax.ShapeDtypeStruct((n, d), F32),
        compiler_params=pltpu.CompilerParams(dimension_semantics=("arbitrary",), vmem_limit_bytes=VMEM_LIMIT),
        name="combine",
    )(*meta, lpos.reshape(ngroups, 1, tm * TOP_K), gates.reshape(ngroups, 1, tm * TOP_K), h, g_final, ys)


def _largest_tile(cands, *sizes):
    for c in cands:
        if all(s % c == 0 for s in sizes):
            return c
    raise ValueError(f"no tile in {cands} divides {sizes}")


def kernel(x_prompt, x_sample, mem_prompt, cache_win_k, cache_win_v, state_conv, cache_mem_k, cache_mem_v, g_attn_norm, w_in, conv_w, attn_sinks, g_mem_norm, w_mem_kv, g_mix_out, w_out, g_ffn_norm, w_router, b_router, w_gate_up, b_gate_up, w_down, b_down, g_final):
    depth = w_in.shape[0]
    assert depth == 1, "single-layer step"
    b, s, d = x_prompt.shape
    nb, t_dec, _ = x_sample.shape
    n_e = w_router.shape[2]
    d_ff2 = w_gate_up.shape[3]
    cw = conv_w.shape[2]
    win = cache_win_k.shape[2]
    m_tok = cache_mem_k.shape[2]
    assert s % TQ == 0 and nb % BB == 0 and win == WINDOW and t_dec <= SUBLANES and d_ff2 % (2 * LANES) == 0
    assert d == SUBLANES * LANES, "token-tile layout: one token is one (8, 128) f32 tile"

    row = lambda a: a.reshape(1, -1)
    w_in_bf = w_in[0].astype(BF16)
    w_out_bf = w_out[0].astype(BF16)
    w_r_bf = w_router[0].astype(BF16)
    sinks = attn_sinks[0].astype(F32)
    shared = (row(g_attn_norm[0]), w_in_bf, conv_w[0], row(g_mix_out[0]), w_out_bf, row(g_ffn_norm[0]),
              w_r_bf, row(b_router[0]))

    mk_p, mv_p, mkt, mvb = _memkv(mem_prompt, row(g_mem_norm[0]), w_mem_kv[0].astype(BF16))
    h_p, hn_p, tope_p, gates_p, lastk, lastv, convst = _mixer_p(x_prompt, sinks, *shared, mkt, mvb)

    zeros = lambda r: jnp.zeros((nb, r, cw), F32)
    st = state_conv[0]
    pm1 = jnp.concatenate([st[:, 1:2], zeros(t_dec - 1)], axis=1).reshape(nb * t_dec, cw)
    pm2 = jnp.concatenate([st, zeros(t_dec - 2)], axis=1).reshape(nb * t_dec, cw)
    h_s, hn_s, tope_s, gates_s, nwk, nwv, u_s = _mixer_s(
        x_sample.reshape(nb * t_dec, d), t_dec, pm1, pm2,
        cache_win_k[0].reshape(nb, win, KV_WIDTH), cache_win_v[0].reshape(nb, win, KV_WIDTH),
        cache_mem_k[0].reshape(nb, m_tok, MEM_WIDTH), cache_mem_v[0].reshape(nb, m_tok, MEM_WIDTH),
        sinks, *shared)

    n_p, n_s = b * s, nb * t_dec
    n = n_p + n_s
    tm = _largest_tile((512, 256, 128, 64, 32, 16, 8), n_p, n_s)
    tope = jnp.concatenate([tope_p, tope_s], axis=0)
    gates = jnp.concatenate([gates_p, gates_s], axis=0)

    lpos, cnt_f = _lpos(tope, tm, n_e)
    cnt = cnt_f[:, 0, :].astype(jnp.int32)
    counts = jnp.sum(cnt, axis=0)
    padded = (counts + BM - 1) // BM * BM
    pad_ends = jnp.cumsum(padded)
    pad_starts = pad_ends - padded
    nk = n * TOP_K
    n_blocks = -(-nk // BM) + n_e
    nused = (pad_ends[-1:] // BM).astype(jnp.int32)
    blk_start = jnp.arange(n_blocks, dtype=jnp.int32) * BM
    blk_e = jnp.minimum(jnp.sum((pad_ends[None, :] <= blk_start[:, None]).astype(jnp.int32), axis=1), n_e - 1)
    lastblk = jnp.where(padded > 0, pad_ends // BM - 1, -1).astype(jnp.int32)
    off = pad_starts[None, :] + jnp.cumsum(cnt, axis=0) - cnt
    lstart = jnp.cumsum(cnt, axis=1) - cnt
    meta = (off.reshape(-1).astype(jnp.int32), cnt.reshape(-1), lstart.reshape(-1).astype(jnp.int32))

    xs = _dispatch(hn_p, hn_s, lpos, meta, lastblk, nused, tm, n_e, n_blocks)

    grp = 2 * LANES
    b_gu = b_gate_up[0].reshape(n_e, d_ff2 // grp, LANES, 2).transpose(0, 1, 3, 2).reshape(n_e, 1, d_ff2)
    owner = jnp.where(padded > 0, jnp.arange(n_e, dtype=jnp.int32), n_e)
    following = jnp.concatenate([lax.cummin(owner, reverse=True)[1:], jnp.full((1,), n_e, jnp.int32)])
    next_e = jnp.where(following < n_e, following, -1).astype(jnp.int32)
    ys = _experts(xs, blk_e, nused, next_e, w_gate_up[0], b_gu, w_down[0], b_down[0].reshape(n_e, 1, d))

    g_fin = row(g_final)
    y_p = _combine(ys, h_p, lpos, gates, meta, g_fin, tm, n_e, 0)
    y_s = _combine(ys, h_s, lpos, gates, meta, g_fin, tm, n_e, n_p // tm)

    kv5 = lambda a, bsz, r, hds: a.reshape(1, bsz, r, hds, HEAD_DIM)
    return (y_p.reshape(b, s, d), y_s.reshape(nb, t_dec, d),
            kv5(lastk, b, WINDOW, N_KV_HEADS), kv5(lastv, b, WINDOW, N_KV_HEADS),
            convst.reshape(1, b, 2, cw),
            kv5(mk_p, b, m_tok, N_MEM_HEADS), kv5(mv_p, b, m_tok, N_MEM_HEADS),
            kv5(nwk, nb, win, N_KV_HEADS), kv5(nwv, nb, win, N_KV_HEADS),
            u_s.reshape(nb, t_dec, cw)[:, t_dec - 2:].reshape(1, nb, 2, cw))
```

```python
import functools

import jax
import jax.numpy as jnp
from jax import lax
from jax.experimental import pallas as pl
from jax.experimental.pallas import tpu as pltpu

F32 = jnp.float32
BF16 = jnp.bfloat16

HEAD_DIM = 64
N_Q_HEADS = 8
N_KV_HEADS = 2
WINDOW = 128
ATTN_WIDTH = N_Q_HEADS * HEAD_DIM
KV_WIDTH = N_KV_HEADS * HEAD_DIM
N_MEM_HEADS = 4
MEM_WIDTH = N_MEM_HEADS * HEAD_DIM
TOP_K = 4
SWIGLU_LIMIT = 7.0
SWIGLU_ALPHA = 1.702
EPS = 1e-5
ATTN_SCALE = HEAD_DIM ** -0.5
ALIBI_SLOPES = tuple(2.0 ** (-8.0 * (h + 1) / N_Q_HEADS) for h in range(N_Q_HEADS))

LANES = 128
SUBLANES = 8
VMEM_LIMIT = 56 * 1024 * 1024

TQ = 512
BB = 16
BM = 512
REP = 8
W_PARTS = 4


def _rms(x):
    return x * lax.rsqrt(jnp.mean(x * x, axis=-1, keepdims=True) + EPS)


def _dot(a, b):
    return jnp.dot(a, b, preferred_element_type=F32)


def _dot_nt(a, b):
    return lax.dot_general(a, b, (((1,), (1,)), ((), ())), preferred_element_type=F32)


def _iota(shape, axis):
    return lax.broadcasted_iota(jnp.int32, shape, axis)


def _store_token_tiles(ref, x, base=0):
    t = x.shape[0]
    for s in range(x.shape[1] // LANES):
        ref[pl.ds(base + s, t, stride=SUBLANES), :] = x[:, s * LANES:(s + 1) * LANES]


def _load_token_tiles(ref, t, s, base=0):
    return ref[pl.ds(base + s, t, stride=SUBLANES), :]


def _memkv_kernel(mem_ref, g_ref, w_ref, mk_ref, mv_ref, mkt_ref, mvb_ref):
    xn = (_rms(mem_ref[0]) * g_ref[...]).astype(BF16)
    kv = _dot(xn, w_ref[...])
    mk = kv[:, :MEM_WIDTH]
    mv = kv[:, MEM_WIDTH:]
    mk_ref[0] = mk
    mv_ref[0] = mv
    mkt_ref[0] = mk.T.astype(BF16)
    mvb_ref[0] = mv.astype(BF16)


def _memkv(mem, g, w_bf):
    b, m, d = mem.shape
    out_f = jax.ShapeDtypeStruct((b, m, MEM_WIDTH), F32)
    out_b = jax.ShapeDtypeStruct((b, m, MEM_WIDTH), BF16)
    out_t = jax.ShapeDtypeStruct((b, MEM_WIDTH, m), BF16)
    blk = lambda r, c: pl.BlockSpec((1, r, c), lambda i: (i, 0, 0))
    return pl.pallas_call(
        _memkv_kernel,
        grid=(b,),
        in_specs=[blk(m, d), pl.BlockSpec((1, d), lambda i: (0, 0)),
                  pl.BlockSpec((d, 2 * MEM_WIDTH), lambda i: (0, 0))],
        out_specs=[blk(m, MEM_WIDTH), blk(m, MEM_WIDTH), blk(MEM_WIDTH, m), blk(m, MEM_WIDTH)],
        out_shape=[out_f, out_f, out_t, out_b],
        name="memkv",
    )(mem, g, w_bf)


def _router_topk(hn, w_r_ref, b_r_ref, tope_ref, gates_ref):
    n_e = w_r_ref.shape[1]
    logits = _dot(hn.astype(BF16), w_r_ref[...]) + b_r_ref[...]
    rows = logits.shape[0]
    col = _iota((rows, n_e), 1).astype(F32)
    vals, idxs = [], []
    cur = logits
    for _ in range(TOP_K):
        m = jnp.max(cur, axis=-1, keepdims=True)
        idx = jnp.min(jnp.where(cur == m, col, float(n_e)), axis=-1, keepdims=True)
        vals.append(m)
        idxs.append(idx)
        cur = jnp.where(col == idx, -jnp.inf, cur)
    exps = [jnp.exp(v - vals[0]) for v in vals]
    tot = exps[0] + exps[1] + exps[2] + exps[3]
    col4 = _iota((rows, TOP_K), 1)
    te = jnp.zeros((rows, TOP_K), F32)
    ga = jnp.zeros((rows, TOP_K), F32)
    for k in range(TOP_K):
        te = jnp.where(col4 == k, idxs[k], te)
        ga = jnp.where(col4 == k, exps[k] / tot, ga)
    tope_ref[...] = te.astype(jnp.int32)
    gates_ref[...] = ga


def _mix_out(x, attn, conv_out, cross, g_mix_ref, w_out_ref, g_ffn_ref, w_r_ref, b_r_ref,
             h_ref, hn_ref, tope_ref, gates_ref):
    mix = jnp.concatenate([_rms(attn), _rms(conv_out), _rms(cross)], axis=-1) * g_mix_ref[...]
    h = x + _dot(mix.astype(BF16), w_out_ref[...])
    hn = _rms(h) * g_ffn_ref[...]
    h_ref[...] = h
    _store_token_tiles(hn_ref, hn)
    _router_topk(hn, w_r_ref, b_r_ref, tope_ref, gates_ref)


def _swa_block(q_blk, kk, vv, prev_lim, sinks_ref):
    blk = WINDOW
    lane = _iota((2 * blk, KV_WIDTH), 1)
    lo = lane < HEAD_DIM
    kk_r = pltpu.roll(kk, HEAD_DIM, axis=1)
    vv_r = pltpu.roll(vv, HEAD_DIM, axis=1)
    kdup = [jnp.where(lo, kk, kk_r).astype(BF16), jnp.where(lo, kk_r, kk).astype(BF16)]
    vlo = [jnp.where(lo, vv, 0.0).astype(BF16), jnp.where(lo, vv_r, 0.0).astype(BF16)]
    vhi = [jnp.where(lo, 0.0, vv_r).astype(BF16), jnp.where(lo, 0.0, vv).astype(BF16)]
    qi = _iota((blk, 2 * blk), 0)
    kj = _iota((blk, 2 * blk), 1)
    dist = blk + qi - kj
    mask = (dist >= 0) & (dist < WINDOW) & (kj >= prev_lim)
    distf = dist.astype(F32)
    qlo = _iota((blk, 2 * HEAD_DIM), 1) < HEAD_DIM
    outs = []
    for p in range(N_Q_HEADS // 2):
        kh = (2 * p) // (N_Q_HEADS // N_KV_HEADS)
        qp = q_blk[:, p * 2 * HEAD_DIM:(p + 1) * 2 * HEAD_DIM]
        acc = None
        for e in range(2):
            h = 2 * p + e
            qm = jnp.where(qlo if e == 0 else jnp.logical_not(qlo), qp, 0.0).astype(BF16)
            s = _dot_nt(qm, kdup[kh]) - ALIBI_SLOPES[h] * distf
            s = jnp.where(mask, s, -jnp.inf)
            sink = sinks_ref[h]
            m = jnp.maximum(jnp.max(s, axis=-1, keepdims=True), sink)
            pe = jnp.exp(s - m)
            denom = jnp.sum(pe, axis=-1, keepdims=True) + jnp.exp(sink - m)
            o = _dot(pe.astype(BF16), (vlo if e == 0 else vhi)[kh]) / denom
            acc = o if acc is None else acc + o
        outs.append(acc)
    return jnp.concatenate(outs, axis=1)


def _mem_attend_shared(mq, mkt, mvb):
    t = mq.shape[0]
    m_tok = mvb.shape[0]
    qhead = _iota((t, MEM_WIDTH), 1) // HEAD_DIM
    vhead = _iota((m_tok, MEM_WIDTH), 1) // HEAD_DIM
    cross = None
    for h in range(N_MEM_HEADS):
        qm = jnp.where(qhead == h, mq, 0.0).astype(BF16)
        s = _dot(qm, mkt)
        m = jnp.max(s, axis=-1, keepdims=True)
        pe = jnp.exp(s - m)
        denom = jnp.sum(pe, axis=-1, keepdims=True)
        vm = jnp.where(vhead == h, mvb, jnp.zeros_like(mvb))
        o = _dot(pe.astype(BF16), vm) / denom
        cross = o if cross is None else cross + o
    return cross


def _mixer_p_kernel(sinks_ref, x_ref, g_attn_ref, w_in_ref, conv_w_ref, g_mix_ref, w_out_ref, g_ffn_ref,
                    w_r_ref, b_r_ref, mkt_ref, mvb_ref,
                    h_ref, hn_ref, tope_ref, gates_ref, lastk_ref, lastv_ref, convst_ref,
                    ck_ref, cv_ref, cu_ref):
    j = pl.program_id(1)
    nj = pl.num_programs(1)

    @pl.when(j == 0)
    def _():
        ck_ref[...] = jnp.zeros_like(ck_ref)
        cv_ref[...] = jnp.zeros_like(cv_ref)
        cu_ref[...] = jnp.zeros_like(cu_ref)

    x = x_ref[0]
    xn = (_rms(x) * g_attn_ref[...]).astype(BF16)
    z = _dot(xn, w_in_ref[...])
    c0 = ATTN_WIDTH
    c1 = c0 + KV_WIDTH
    c2 = c1 + KV_WIDTH
    cw = conv_w_ref.shape[1]
    c3, c4, c5 = c2 + cw, c2 + 2 * cw, c2 + 3 * cw
    q = z[:, :c0] * ATTN_SCALE
    k = z[:, c0:c1]
    v = z[:, c1:c2]
    cb = z[:, c2:c3]
    cc = z[:, c3:c4]
    cvv = z[:, c4:c5]
    mq = z[:, c5:] * ATTN_SCALE

    blk = WINDOW
    attn_blocks = []
    for i in range(TQ // blk):
        if i == 0:
            pk, pv = ck_ref[...], cv_ref[...]
            prev_lim = jnp.where(j > 0, 0, blk)
        else:
            pk, pv = k[(i - 1) * blk:i * blk], v[(i - 1) * blk:i * blk]
            prev_lim = 0
        kk = jnp.concatenate([pk, k[i * blk:(i + 1) * blk]], axis=0)
        vv = jnp.concatenate([pv, v[i * blk:(i + 1) * blk]], axis=0)
        attn_blocks.append(_swa_block(q[i * blk:(i + 1) * blk], kk, vv, prev_lim, sinks_ref))
    attn = jnp.concatenate(attn_blocks, axis=0)
    ck_ref[...] = k[TQ - blk:]
    cv_ref[...] = v[TQ - blk:]

    u = cc * cvv
    row = _iota(u.shape, 0)
    u1 = jnp.where(row == 0, cu_ref[SUBLANES - 1:SUBLANES, :], pltpu.roll(u, 1, axis=0))
    u2 = jnp.where(row == 0, cu_ref[SUBLANES - 2:SUBLANES - 1, :],
                   jnp.where(row == 1, cu_ref[SUBLANES - 1:SUBLANES, :], pltpu.roll(u, 2, axis=0)))
    conv_out = cb * (conv_w_ref[0:1, :] * u2 + conv_w_ref[1:2, :] * u1 + conv_w_ref[2:3, :] * u)
    cu_ref[...] = u[TQ - SUBLANES:]

    cross = _mem_attend_shared(mq, mkt_ref[0], mvb_ref[0])

    @pl.when(j == nj - 1)
    def _():
        lastk_ref[0] = k[TQ - blk:]
        lastv_ref[0] = v[TQ - blk:]
        convst_ref[0] = u[TQ - 2:]

    _mix_out(x, attn, conv_out, cross, g_mix_ref, w_out_ref, g_ffn_ref, w_r_ref, b_r_ref,
             h_ref, hn_ref, tope_ref, gates_ref)


def _mixer_p(x, sinks, g_attn, w_in, conv_w, g_mix, w_out, g_ffn, w_r, b_r, mkt, mvb):
    b, s, d = x.shape
    nj = s // TQ
    n = b * s
    cw = conv_w.shape[1]
    full = lambda a: pl.BlockSpec(a.shape, lambda bi, ji, *_: (0,) * a.ndim)
    tok = lambda w: pl.BlockSpec((TQ, w), lambda bi, ji, *_: (bi * nj + ji, 0))
    per_b = lambda r, c: pl.BlockSpec((1, r, c), lambda bi, ji, *_: (bi, 0, 0))
    grid_spec = pltpu.PrefetchScalarGridSpec(
        num_scalar_prefetch=1,
        grid=(b, nj),
        in_specs=[pl.BlockSpec((1, TQ, d), lambda bi, ji, *_: (bi, ji, 0)),
                  full(g_attn), full(w_in), full(conv_w), full(g_mix), full(w_out), full(g_ffn),
                  full(w_r), full(b_r), per_b(MEM_WIDTH, mkt.shape[2]), per_b(mvb.shape[1], MEM_WIDTH)],
        out_specs=[tok(d), pl.BlockSpec((TQ * SUBLANES, LANES), lambda bi, ji, *_: (bi * nj + ji, 0)),
                   tok(TOP_K), tok(TOP_K),
                   per_b(WINDOW, KV_WIDTH), per_b(WINDOW, KV_WIDTH), per_b(2, cw)],
        scratch_shapes=[pltpu.VMEM((WINDOW, KV_WIDTH), F32), pltpu.VMEM((WINDOW, KV_WIDTH), F32),
                        pltpu.VMEM((SUBLANES, cw), F32)],
    )
    return pl.pallas_call(
        _mixer_p_kernel,
        grid_spec=grid_spec,
        out_shape=[jax.ShapeDtypeStruct((n, d), F32), jax.ShapeDtypeStruct((n * SUBLANES, LANES), F32),
                   jax.ShapeDtypeStruct((n, TOP_K), jnp.int32), jax.ShapeDtypeStruct((n, TOP_K), F32),
                   jax.ShapeDtypeStruct((b, WINDOW, KV_WIDTH), F32),
                   jax.ShapeDtypeStruct((b, WINDOW, KV_WIDTH), F32),
                   jax.ShapeDtypeStruct((b, 2, cw), F32)],
        compiler_params=pltpu.CompilerParams(dimension_semantics=("arbitrary", "arbitrary"),
                                             vmem_limit_bytes=VMEM_LIMIT),
        name="mixer_p",
    )(sinks, x, g_attn, w_in, conv_w, g_mix, w_out, g_ffn, w_r, b_r, mkt, mvb)


def _per_head_column(values, hrow):
    col = jnp.zeros(hrow.shape, F32)
    for h in range(N_Q_HEADS):
        col = jnp.where(hrow == h, values[h], col)
    return col


def _mixer_s_kernel(sinks_ref, x_ref, pm1_ref, pm2_ref, wk_ref, wv_ref, mk_ref, mv_ref,
                    g_attn_ref, w_in_ref, conv_w_ref, g_mix_ref, w_out_ref, g_ffn_ref, w_r_ref, b_r_ref,
                    h_ref, hn_ref, tope_ref, gates_ref, nwk_ref, nwv_ref, u_ref, *, t_dec):
    r_tok = BB * t_dec
    r_exp = r_tok * REP
    qrows = t_dec * REP
    x = x_ref[...]
    xn = (_rms(x) * g_attn_ref[...]).astype(BF16)
    z = _dot(xn, w_in_ref[...])
    c0 = ATTN_WIDTH
    c1 = c0 + KV_WIDTH
    c2 = c1 + KV_WIDTH
    cw = conv_w_ref.shape[1]
    c3, c4, c5 = c2 + cw, c2 + 2 * cw, c2 + 3 * cw
    q = z[:, :c0] * ATTN_SCALE
    k_new = z[:, c0:c1]
    v_new = z[:, c1:c2]
    cb = z[:, c2:c3]
    cc = z[:, c3:c4]
    cvv = z[:, c4:c5]
    mq = z[:, c5:] * ATTN_SCALE
    win = wk_ref.shape[1]

    xi = _iota((KV_WIDTH, ATTN_WIDTH), 0)
    xl = _iota((KV_WIDTH, ATTN_WIDTH), 1)
    q_per_kv = N_Q_HEADS // N_KV_HEADS
    expand = (xi == (xl // (q_per_kv * HEAD_DIM)) * HEAD_DIM + xl % HEAD_DIM).astype(BF16)
    rr = _iota((r_exp, r_tok), 0)
    rc = _iota((r_exp, r_tok), 1)
    rep = (rr // REP == rc).astype(BF16)

    hrow = _iota((r_exp, 1), 0) % REP
    trow = (_iota((r_exp, 1), 0) // REP) % t_dec
    slope_col = _per_head_column(ALIBI_SLOPES, hrow)
    sink_col = _per_head_column([sinks_ref[h] for h in range(N_Q_HEADS)], hrow)

    qexp = jnp.where(hrow == _iota((r_exp, ATTN_WIDTH), 1) // HEAD_DIM, _dot(rep, q.astype(BF16)), 0.0)
    kexp = _dot(wk_ref[...].reshape(BB * win, KV_WIDTH).astype(BF16), expand).astype(BF16)
    vexp = _dot(wv_ref[...].reshape(BB * win, KV_WIDTH).astype(BF16), expand).astype(BF16)
    s = jnp.einsum("bqc,bkc->bqk", qexp.astype(BF16).reshape(BB, qrows, ATTN_WIDTH),
                   kexp.reshape(BB, win, ATTN_WIDTH), preferred_element_type=F32).reshape(r_exp, win)
    scol = _iota((r_exp, win), 1)
    s = s - slope_col * (win + trow - scol).astype(F32)
    s = jnp.where(scol > trow, s, -jnp.inf)
    knew_exp = _dot(k_new.astype(BF16), expand).astype(BF16)
    vnew_exp = _dot(v_new.astype(BF16), expand).astype(BF16)
    s_new, v_rep = [], []
    for jn in range(t_dec):
        rep_j = (rc == (rr // qrows) * t_dec + jn).astype(BF16)
        k_rep = _dot(rep_j, knew_exp)
        v_rep.append(_dot(rep_j, vnew_exp))
        sj = jnp.sum(qexp * k_rep, axis=-1, keepdims=True) - slope_col * (trow - jn).astype(F32)
        s_new.append(jnp.where(trow >= jn, sj, -jnp.inf))
    m = jnp.maximum(jnp.max(s, axis=-1, keepdims=True), sink_col)
    for sj in s_new:
        m = jnp.maximum(m, sj)
    pe = jnp.exp(s - m)
    denom = jnp.sum(pe, axis=-1, keepdims=True) + jnp.exp(sink_col - m)
    o = jnp.einsum("bqk,bkc->bqc", pe.astype(BF16).reshape(BB, qrows, win),
                   vexp.reshape(BB, win, ATTN_WIDTH), preferred_element_type=F32).reshape(r_exp, ATTN_WIDTH)
    for jn in range(t_dec):
        pj = jnp.exp(s_new[jn] - m)
        denom = denom + pj
        o = o + pj * v_rep[jn]
    o = jnp.where(hrow == _iota((r_exp, ATTN_WIDTH), 1) // HEAD_DIM, o / denom, 0.0)
    attn = jnp.sum(o.reshape(r_tok, REP, ATTN_WIDTH), axis=1)

    m_tok = mk_ref.shape[1]
    mhead = _iota((r_exp, MEM_WIDTH), 1) // HEAD_DIM
    mqexp = jnp.where(hrow == mhead, _dot(rep, mq.astype(BF16)), 0.0).astype(BF16)
    sm = jnp.einsum("bqc,bmc->bqm", mqexp.reshape(BB, qrows, MEM_WIDTH), mk_ref[...].astype(BF16),
                    preferred_element_type=F32).reshape(r_exp, m_tok)
    mm = jnp.max(sm, axis=-1, keepdims=True)
    pm = jnp.exp(sm - mm)
    dm = jnp.sum(pm, axis=-1, keepdims=True)
    om = jnp.einsum("bqm,bmc->bqc", pm.astype(BF16).reshape(BB, qrows, m_tok), mv_ref[...].astype(BF16),
                    preferred_element_type=F32).reshape(r_exp, MEM_WIDTH)
    om = jnp.where(hrow == mhead, om / dm, 0.0)
    cross = jnp.sum(om.reshape(r_tok, REP, MEM_WIDTH), axis=1)

    u = cc * cvv
    tt = _iota(u.shape, 0) % t_dec
    u1 = jnp.where(tt >= 1, pltpu.roll(u, 1, axis=0), pm1_ref[...])
    u2 = jnp.where(tt >= 2, pltpu.roll(u, 2, axis=0), pm2_ref[...])
    conv_out = cb * (conv_w_ref[0:1, :] * u2 + conv_w_ref[1:2, :] * u1 + conv_w_ref[2:3, :] * u)
    u_ref[...] = u

    nwk_ref[:, 0:win - t_dec, :] = wk_ref[:, t_dec:win, :]
    nwv_ref[:, 0:win - t_dec, :] = wv_ref[:, t_dec:win, :]
    for b in range(BB):
        nwk_ref[b, win - t_dec:win, :] = k_new[b * t_dec:(b + 1) * t_dec, :]
        nwv_ref[b, win - t_dec:win, :] = v_new[b * t_dec:(b + 1) * t_dec, :]

    _mix_out(x, attn, conv_out, cross, g_mix_ref, w_out_ref, g_ffn_ref, w_r_ref, b_r_ref,
             h_ref, hn_ref, tope_ref, gates_ref)


def _mixer_s(x2, t_dec, pm1, pm2, wk, wv, mk, mv, sinks, g_attn, w_in, conv_w, g_mix, w_out, g_ffn, w_r, b_r):
    n, d = x2.shape
    nb = wk.shape[0]
    win = wk.shape[1]
    m_tok = mk.shape[1]
    cw = conv_w.shape[1]
    r_tok = BB * t_dec
    full = lambda a: pl.BlockSpec(a.shape, lambda i, *_: (0,) * a.ndim)
    tok = lambda w: pl.BlockSpec((r_tok, w), lambda i, *_: (i, 0))
    per_b = lambda r, c: pl.BlockSpec((BB, r, c), lambda i, *_: (i, 0, 0))
    grid_spec = pltpu.PrefetchScalarGridSpec(
        num_scalar_prefetch=1,
        grid=(nb // BB,),
        in_specs=[tok(d), tok(cw), tok(cw), per_b(win, KV_WIDTH), per_b(win, KV_WIDTH),
                  per_b(m_tok, MEM_WIDTH), per_b(m_tok, MEM_WIDTH),
                  full(g_attn), full(w_in), full(conv_w), full(g_mix), full(w_out), full(g_ffn),
                  full(w_r), full(b_r)],
        out_specs=[tok(d), pl.BlockSpec((r_tok * SUBLANES, LANES), lambda i, *_: (i, 0)),
                   tok(TOP_K), tok(TOP_K), per_b(win, KV_WIDTH), per_b(win, KV_WIDTH), tok(cw)],
    )
    return pl.pallas_call(
        functools.partial(_mixer_s_kernel, t_dec=t_dec),
        grid_spec=grid_spec,
        out_shape=[jax.ShapeDtypeStruct((n, d), F32), jax.ShapeDtypeStruct((n * SUBLANES, LANES), F32),
                   jax.ShapeDtypeStruct((n, TOP_K), jnp.int32), jax.ShapeDtypeStruct((n, TOP_K), F32),
                   jax.ShapeDtypeStruct((nb, win, KV_WIDTH), F32), jax.ShapeDtypeStruct((nb, win, KV_WIDTH), F32),
                   jax.ShapeDtypeStruct((n, cw), F32)],
        compiler_params=pltpu.CompilerParams(dimension_semantics=("arbitrary",), vmem_limit_bytes=VMEM_LIMIT),
        name="mixer_s",
    )(sinks, x2, pm1, pm2, wk, wv, mk, mv, g_attn, w_in, conv_w, g_mix, w_out, g_ffn, w_r, b_r)


def _lpos_kernel(tope_ref, lpos_ref, counts_ref, tri_ref, *, n_e):
    i = pl.program_id(0)
    tm = tope_ref.shape[0]

    @pl.when(i == 0)
    def _():
        tri_ref[...] = (_iota((tm, tm), 0) > _iota((tm, tm), 1)).astype(BF16)

    te = tope_ref[...]
    col = _iota((tm, n_e), 1)
    hits = [te[:, k:k + 1] == col for k in range(TOP_K)]
    onehot = jnp.zeros((tm, n_e), F32)
    lower = jnp.zeros((tm, n_e), F32)
    for k in range(TOP_K):
        onehot = onehot + hits[k].astype(F32)
        lower = lower + (te[:, k:k + 1] < col).astype(F32)
    ahead = _dot(tri_ref[...], onehot.astype(BF16)) + jnp.sum(lower, axis=0, keepdims=True)
    col4 = _iota((tm, TOP_K), 1)
    pos = jnp.zeros((tm, TOP_K), F32)
    for k in range(TOP_K):
        pos = jnp.where(col4 == k, jnp.sum(jnp.where(hits[k], ahead, 0.0), axis=-1, keepdims=True), pos)
    half = (i % 2) * (tm * TOP_K)
    lpos_ref[...] = (pos.astype(jnp.int32) + half) * SUBLANES
    counts_ref[0] = jnp.sum(onehot, axis=0, keepdims=True)


def _lpos(tope, tm, n_e):
    n = tope.shape[0]
    return pl.pallas_call(
        functools.partial(_lpos_kernel, n_e=n_e),
        grid=(n // tm,),
        in_specs=[pl.BlockSpec((tm, TOP_K), lambda i: (i, 0))],
        out_specs=[pl.BlockSpec((tm, TOP_K), lambda i: (i, 0)), pl.BlockSpec((1, 1, n_e), lambda i: (i, 0, 0))],
        out_shape=[jax.ShapeDtypeStruct((n, TOP_K), jnp.int32), jax.ShapeDtypeStruct((n // tm, 1, n_e), F32)],
        scratch_shapes=[pltpu.VMEM((tm, tm), BF16)],
        compiler_params=pltpu.CompilerParams(dimension_semantics=("arbitrary",)),
        name="lpos",
    )(tope)


def _rows(ref, first_row, n_rows):
    return ref.at[pl.ds(pl.multiple_of(first_row * SUBLANES, SUBLANES), n_rows * SUBLANES)]


def _group_chunks(meta, g, n_e, max_rows, make_copy, wait):
    off_ref, cnt_ref, lst_ref = meta
    if wait:
        make_copy(0, 0, max_rows * TOP_K).wait()
        return
    sizes = [1 << b for b in range(max_rows.bit_length() - 1, -1, -1)]

    def per_expert(e, c):
        cnt = cnt_ref[g * n_e + e]
        off = off_ref[g * n_e + e]
        lst = lst_ref[g * n_e + e]
        done = 0
        for size in sizes:
            bit = cnt & size

            @pl.when(bit != 0)
            def _():
                cp = make_copy(lst + done, off + done, size)
                if wait:
                    cp.wait()
                else:
                    cp.start()
            done = done + bit
        return c

    lax.fori_loop(0, n_e, per_expert, 0)


def _dispatch_kernel(off_ref, cnt_ref, lst_ref, lastblk_ref, nused_ref, lpos_ref, hn_a_ref, hn_b_ref, xs_ref,
                     stage_ref, zbuf_ref, sem, zsem, *, n_e, n_blocks, groups_a):
    j = pl.program_id(0)
    nj = pl.num_programs(0)
    tm = hn_a_ref.shape[0] // SUBLANES
    slot = j % 2
    meta = (off_ref, cnt_ref, lst_ref)

    def chunks(g, s, wait):
        _group_chunks(meta, g, n_e, tm,
                      lambda lrow, grow, size: pltpu.make_async_copy(
                          _rows(stage_ref, s * (tm * TOP_K) + lrow, size), _rows(xs_ref, grow, size),
                          sem.at[s]), wait)

    def zero_copy(blk):
        return pltpu.make_async_copy(zbuf_ref, _rows(xs_ref, blk * BM, BM), zsem)

    @pl.when(j == 0)
    def _():
        zbuf_ref[...] = jnp.zeros_like(zbuf_ref)

        def expert_fill(wait):
            def body(e, c):
                @pl.when(lastblk_ref[e] >= 0)
                def _():
                    cp = zero_copy(lastblk_ref[e])
                    cp.wait() if wait else cp.start()
                return c
            return body

        def tail_fill(wait):
            def body(b, c):
                cp = zero_copy(b)
                cp.wait() if wait else cp.start()
                return c
            return body

        lax.fori_loop(0, n_e, expert_fill(False), 0)
        lax.fori_loop(nused_ref[0], n_blocks, tail_fill(False), 0)
        lax.fori_loop(0, n_e, expert_fill(True), 0)
        lax.fori_loop(nused_ref[0], n_blocks, tail_fill(True), 0)

    @pl.when(j >= 2)
    def _():
        chunks(j - 2, slot, True)

    def place_from(hn_ref):
        def place(t, c):
            tile = hn_ref[pl.ds(pl.multiple_of(t * SUBLANES, SUBLANES), SUBLANES), :]
            for k in range(TOP_K):
                pos = pl.multiple_of(lpos_ref[0, 0, t * TOP_K + k], SUBLANES)
                stage_ref[pl.ds(pos, SUBLANES), :] = tile
            return c
        lax.fori_loop(0, tm, place, 0, unroll=8)

    @pl.when(j < groups_a)
    def _():
        place_from(hn_a_ref)

    @pl.when(j >= groups_a)
    def _():
        place_from(hn_b_ref)

    chunks(j, slot, False)

    @pl.when(j == nj - 1)
    def _():
        @pl.when(j >= 1)
        def _():
            chunks(j - 1, 1 - slot, True)
        chunks(j, slot, True)


def _dispatch(hn_a, hn_b, lpos, meta, lastblk, nused, tm, n_e, n_blocks):
    nt = lpos.shape[0] // tm
    groups_a = hn_a.shape[0] // (tm * SUBLANES)
    assert groups_a >= 1 and groups_a + hn_b.shape[0] // (tm * SUBLANES) == nt
    grid_spec = pltpu.PrefetchScalarGridSpec(
        num_scalar_prefetch=5,
        grid=(nt,),
        in_specs=[pl.BlockSpec((1, 1, tm * TOP_K), lambda j, *_: (j, 0, 0), memory_space=pltpu.SMEM),
                  pl.BlockSpec((tm * SUBLANES, LANES), lambda j, *_: (jnp.minimum(j, groups_a - 1), 0)),
                  pl.BlockSpec((tm * SUBLANES, LANES), lambda j, *_: (jnp.maximum(j - groups_a, 0), 0))],
        out_specs=pl.BlockSpec(memory_space=pl.ANY),
        scratch_shapes=[pltpu.VMEM((2 * tm * TOP_K * SUBLANES, LANES), F32), pltpu.VMEM((BM * SUBLANES, LANES), F32),
                        pltpu.SemaphoreType.DMA((2,)), pltpu.SemaphoreType.DMA(())],
    )
    return pl.pallas_call(
        functools.partial(_dispatch_kernel, n_e=n_e, n_blocks=n_blocks, groups_a=groups_a),
        grid_spec=grid_spec,
        out_shape=jax.ShapeDtypeStruct((n_blocks * BM * SUBLANES, LANES), F32),
        compiler_params=pltpu.CompilerParams(dimension_semantics=("arbitrary",), vmem_limit_bytes=VMEM_LIMIT),
        name="dispatch",
    )(*meta, lastblk, nused, lpos.reshape(nt, 1, tm * TOP_K), hn_a, hn_b)


def _experts_kernel(blk_e_ref, nused_ref, next_e_ref, xs_ref, bgu_ref, bd_ref, wgu_hbm_ref, wd_hbm_ref, ys_ref,
                    wgu_ref, wd_ref, wgu_bf_ref, wd_bf_ref, wsem):
    i = pl.program_id(0)
    nused = nused_ref[0]
    d, d_ff2 = wgu_ref.shape
    grp = 2 * LANES

    def weight_copies(e):
        gu_cols = d_ff2 // W_PARTS
        dn_rows = (d_ff2 // 2) // W_PARTS
        cps = []
        for p in range(W_PARTS):
            cps.append(pltpu.make_async_copy(wgu_hbm_ref.at[e, :, pl.ds(p * gu_cols, gu_cols)],
                                             wgu_ref.at[:, pl.ds(p * gu_cols, gu_cols)], wsem.at[p]))
            cps.append(pltpu.make_async_copy(wd_hbm_ref.at[e, pl.ds(p * dn_rows, dn_rows), :],
                                             wd_ref.at[pl.ds(p * dn_rows, dn_rows), :], wsem.at[W_PARTS + p]))
        return cps

    @pl.when(i < nused)
    def _():
        e = blk_e_ref[i]
        e_prev = blk_e_ref[jnp.maximum(i - 1, 0)]

        @pl.when(i == 0)
        def _():
            for cp in weight_copies(e):
                cp.start()

        @pl.when((i == 0) | (e != e_prev))
        def _():
            for cp in weight_copies(e):
                cp.wait()
            pr = _iota((grp, grp), 0)
            pc = _iota((grp, grp), 1)
            perm = (pr == jnp.where(pc < LANES, 2 * pc, 2 * (pc - LANES) + 1)).astype(BF16)
            for g in range(d_ff2 // grp):
                w = wgu_ref[:, g * grp:(g + 1) * grp].astype(BF16)
                wgu_bf_ref[:, g * grp:(g + 1) * grp] = _dot(w, perm).astype(BF16)
            wd_bf_ref[...] = wd_ref[...].astype(BF16)

            @pl.when(next_e_ref[e] >= 0)
            def _():
                for cp in weight_copies(next_e_ref[e]):
                    cp.start()

        x = jnp.concatenate([_load_token_tiles(xs_ref, BM, s).astype(BF16) for s in range(SUBLANES)], axis=1)
        hgu = _dot(x, wgu_bf_ref[...]) + bgu_ref[0]
        acts = []
        for g in range(d_ff2 // grp):
            gate = jnp.minimum(hgu[:, g * grp:g * grp + LANES], SWIGLU_LIMIT)
            up = jnp.clip(hgu[:, g * grp + LANES:(g + 1) * grp], -SWIGLU_LIMIT, SWIGLU_LIMIT)
            glu = gate * (1.0 / (1.0 + jnp.exp(-SWIGLU_ALPHA * gate)))
            acts.append(((up + 1.0) * glu).astype(BF16))
        act = jnp.concatenate(acts, axis=1)
        _store_token_tiles(ys_ref, _dot(act, wd_bf_ref[...]) + bd_ref[0])

    @pl.when(i >= nused)
    def _():
        ys_ref[...] = jnp.zeros_like(ys_ref)


def _experts(xs, blk_e, nused, next_e, w_gate_up, b_gu_perm, w_down, b_down):
    n_e, d, d_ff2 = w_gate_up.shape
    n_blocks = xs.shape[0] // (BM * SUBLANES)
    expert = lambda i, be, nu, ne: be[jnp.minimum(i, jnp.maximum(nu[0] - 1, 0))]
    rows_spec = pl.BlockSpec((BM * SUBLANES, LANES), lambda i, *_: (i, 0))
    grid_spec = pltpu.PrefetchScalarGridSpec(
        num_scalar_prefetch=3,
        grid=(n_blocks,),
        in_specs=[pl.BlockSpec((BM * SUBLANES, LANES), lambda i, be, nu, ne: (jnp.minimum(i, jnp.maximum(nu[0] - 1, 0)), 0)),
                  pl.BlockSpec((1, 1, d_ff2), lambda i, *s: (expert(i, *s), 0, 0)),
                  pl.BlockSpec((1, 1, d), lambda i, *s: (expert(i, *s), 0, 0)),
                  pl.BlockSpec(memory_space=pl.ANY), pl.BlockSpec(memory_space=pl.ANY)],
        out_specs=rows_spec,
        scratch_shapes=[pltpu.VMEM((d, d_ff2), F32), pltpu.VMEM((d_ff2 // 2, d), F32),
                        pltpu.VMEM((d, d_ff2), BF16), pltpu.VMEM((d_ff2 // 2, d), BF16),
                        pltpu.SemaphoreType.DMA((2 * W_PARTS,))],
    )
    return pl.pallas_call(
        _experts_kernel,
        grid_spec=grid_spec,
        out_shape=jax.ShapeDtypeStruct(xs.shape, F32),
        compiler_params=pltpu.CompilerParams(dimension_semantics=("arbitrary",), vmem_limit_bytes=VMEM_LIMIT),
        name="experts",
    )(blk_e, nused, next_e, xs, b_gu_perm, b_down, w_gate_up, w_down)


def _combine_kernel(off_ref, cnt_ref, lst_ref, lpos_ref, gates_ref, h_ref, g_ref, ys_ref, out_ref,
                    stage_ref, acc_ref, sem, *, n_e, g0):
    i = pl.program_id(0)
    ng = pl.num_programs(0)
    tm, d = h_ref.shape
    slot = (g0 + i) % 2
    meta = (off_ref, cnt_ref, lst_ref)

    def chunks(g, s, wait):
        _group_chunks(meta, g, n_e, tm,
                      lambda lrow, grow, size: pltpu.make_async_copy(
                          _rows(ys_ref, grow, size), _rows(stage_ref, s * (tm * TOP_K) + lrow, size),
                          sem.at[s]), wait)

    @pl.when(i == 0)
    def _():
        chunks(g0, slot, False)

    @pl.when(i + 1 < ng)
    def _():
        chunks(g0 + i + 1, 1 - slot, False)

    chunks(g0 + i, slot, True)

    def mix(t, c):
        acc = None
        for k in range(TOP_K):
            pos = pl.multiple_of(lpos_ref[0, 0, t * TOP_K + k], SUBLANES)
            term = stage_ref[pl.ds(pos, SUBLANES), :] * gates_ref[0, 0, t * TOP_K + k]
            acc = term if acc is None else acc + term
        acc_ref[pl.ds(pl.multiple_of(t * SUBLANES, SUBLANES), SUBLANES), :] = acc
        return c

    lax.fori_loop(0, tm, mix, 0, unroll=8)

    h = h_ref[...]
    parts = []
    sq = jnp.zeros((tm, LANES), F32)
    for s in range(d // LANES):
        y = h[:, s * LANES:(s + 1) * LANES] + _load_token_tiles(acc_ref, tm, s)
        sq = sq + y * y
        parts.append(y)
    rinv = lax.rsqrt(jnp.sum(sq, axis=-1, keepdims=True) / d + EPS)
    out_ref[...] = jnp.concatenate(parts, axis=1) * rinv * g_ref[...]


def _combine(ys, h, lpos, gates, meta, g_final, tm, n_e, g0):
    n, d = h.shape
    grid_spec = pltpu.PrefetchScalarGridSpec(
        num_scalar_prefetch=3,
        grid=(n // tm,),
        in_specs=[pl.BlockSpec((1, 1, tm * TOP_K), lambda i, *_: (g0 + i, 0, 0), memory_space=pltpu.SMEM),
                  pl.BlockSpec((1, 1, tm * TOP_K), lambda i, *_: (g0 + i, 0, 0), memory_space=pltpu.SMEM),
                  pl.BlockSpec((tm, d), lambda i, *_: (i, 0)),
                  pl.BlockSpec((1, d), lambda i, *_: (0, 0)),
                  pl.BlockSpec(memory_space=pl.ANY)],
        out_specs=pl.BlockSpec((tm, d), lambda i, *_: (i, 0)),
        scratch_shapes=[pltpu.VMEM((2 * tm * TOP_K * SUBLANES, LANES), F32), pltpu.VMEM((tm * SUBLANES, LANES), F32),
                        pltpu.SemaphoreType.DMA((2,))],
    )
    ngroups = lpos.shape[0] // tm
    return pl.pallas_call(
        functools.partial(_combine_kernel, n_e=n_e, g0=g0),
        grid_spec=grid_spec,
        out_shape=jax.ShapeDtypeStruct((n, d), F32),
        compiler_params=pltpu.CompilerParams(dimension_semantics=("arbitrary",), vmem_limit_bytes=VMEM_LIMIT),
        name="combine",
    )(*meta, lpos.reshape(ngroups, 1, tm * TOP_K), gates.reshape(ngroups, 1, tm * TOP_K), h, g_final, ys)


def _largest_tile(cands, *sizes):
    for c in cands:
        if all(s % c == 0 for s in sizes):
            return c
    raise ValueError(f"no tile in {cands} divides {sizes}")


def kernel(x_prompt, x_sample, mem_prompt, cache_win_k, cache_win_v, state_conv, cache_mem_k, cache_mem_v, g_attn_norm, w_in, conv_w, attn_sinks, g_mem_norm, w_mem_kv, g_mix_out, w_out, g_ffn_norm, w_router, b_router, w_gate_up, b_gate_up, w_down, b_down, g_final):
    depth = w_in.shape[0]
    assert depth == 1, "single-layer step"
    b, s, d = x_prompt.shape
    nb, t_dec, _ = x_sample.shape
    n_e = w_router.shape[2]
    d_ff2 = w_gate_up.shape[3]
    cw = conv_w.shape[2]
    win = cache_win_k.shape[2]
    m_tok = cache_mem_k.shape[2]
    assert s % TQ == 0 and nb % BB == 0 and win == WINDOW and t_dec <= SUBLANES and d_ff2 % (2 * LANES) == 0
    assert d == SUBLANES * LANES, "token-tile layout: one token is one (8, 128) f32 tile"

    row = lambda a: a.reshape(1, -1)
    w_in_bf = w_in[0].astype(BF16)
    w_out_bf = w_out[0].astype(BF16)
    w_r_bf = w_router[0].astype(BF16)
    sinks = attn_sinks[0].astype(F32)
    shared = (row(g_attn_norm[0]), w_in_bf, conv_w[0], row(g_mix_out[0]), w_out_bf, row(g_ffn_norm[0]),
              w_r_bf, row(b_router[0]))

    mk_p, mv_p, mkt, mvb = _memkv(mem_prompt, row(g_mem_norm[0]), w_mem_kv[0].astype(BF16))
    h_p, hn_p, tope_p, gates_p, lastk, lastv, convst = _mixer_p(x_prompt, sinks, *shared, mkt, mvb)

    zeros = lambda r: jnp.zeros((nb, r, cw), F32)
    st = state_conv[0]
    pm1 = jnp.concatenate([st[:, 1:2], zeros(t_dec - 1)], axis=1).reshape(nb * t_dec, cw)
    pm2 = jnp.concatenate([st, zeros(t_dec - 2)], axis=1).reshape(nb * t_dec, cw)
    h_s, hn_s, tope_s, gates_s, nwk, nwv, u_s = _mixer_s(
        x_sample.reshape(nb * t_dec, d), t_dec, pm1, pm2,
        cache_win_k[0].reshape(nb, win, KV_WIDTH), cache_win_v[0].reshape(nb, win, KV_WIDTH),
        cache_mem_k[0].reshape(nb, m_tok, MEM_WIDTH), cache_mem_v[0].reshape(nb, m_tok, MEM_WIDTH),
        sinks, *shared)

    n_p, n_s = b * s, nb * t_dec
    n = n_p + n_s
    tm = _largest_tile((512, 256, 128, 64, 32, 16, 8), n_p, n_s)
    tope = jnp.concatenate([tope_p, tope_s], axis=0)
    gates = jnp.concatenate([gates_p, gates_s], axis=0)

    lpos, cnt_f = _lpos(tope, tm, n_e)
    cnt = cnt_f[:, 0, :].astype(jnp.int32)
    counts = jnp.sum(cnt, axis=0)
    padded = (counts + BM - 1) // BM * BM
    pad_ends = jnp.cumsum(padded)
    pad_starts = pad_ends - padded
    nk = n * TOP_K
    n_blocks = -(-nk // BM) + n_e
    nused = (pad_ends[-1:] // BM).astype(jnp.int32)
    blk_start = jnp.arange(n_blocks, dtype=jnp.int32) * BM
    blk_e = jnp.minimum(jnp.sum((pad_ends[None, :] <= blk_start[:, None]).astype(jnp.int32), axis=1), n_e - 1)
    lastblk = jnp.where(padded > 0, pad_ends // BM - 1, -1).astype(jnp.int32)
    off = pad_starts[None, :] + jnp.cumsum(cnt, axis=0) - cnt
    lstart = jnp.cumsum(cnt, axis=1) - cnt
    meta = (off.reshape(-1).astype(jnp.int32), cnt.reshape(-1), lstart.reshape(-1).astype(jnp.int32))

    xs = _dispatch(hn_p, hn_s, lpos, meta, lastblk, nused, tm, n_e, n_blocks)

    grp = 2 * LANES
    b_gu = b_gate_up[0].reshape(n_e, d_ff2 // grp, LANES, 2).transpose(0, 1, 3, 2).reshape(n_e, 1, d_ff2)
    owner = jnp.where(padded > 0, jnp.arange(n_e, dtype=jnp.int32), n_e)
    following = jnp.concatenate([lax.cummin(owner, reverse=True)[1:], jnp.full((1,), n_e, jnp.int32)])
    next_e = jnp.where(following < n_e, following, -1).astype(jnp.int32)
    ys = _experts(xs, blk_e, nused, next_e, w_gate_up[0], b_gu, w_down[0], b_down[0].reshape(n_e, 1, d))

    g_fin = row(g_final)
    y_p = _combine(ys, h_p, lpos, gates, meta, g_fin, tm, n_e, 0)
    y_s = _combine(ys, h_s, lpos, gates, meta, g_fin, tm, n_e, n_p // tm)

    kv5 = lambda a, bsz, r, hds: a.reshape(1, bsz, r, hds, HEAD_DIM)
    return (y_p.reshape(b, s, d), y_s.reshape(nb, t_dec, d),
            kv5(lastk, b, WINDOW, N_KV_HEADS), kv5(lastv, b, WINDOW, N_KV_HEADS),
            convst.reshape(1, b, 2, cw),
            kv5(mk_p, b, m_tok, N_MEM_HEADS), kv5(mv_p, b, m_tok, N_MEM_HEADS),
            kv5(nwk, nb, win, N_KV_HEADS), kv5(nwv, nb, win, N_KV_HEADS),
            u_s.reshape(nb, t_dec, cw)[:, t_dec - 2:].reshape(1, nb, 2, cw))
```

```python
import functools

import jax
import jax.numpy as jnp
from jax import lax
from jax.experimental import pallas as pl
from jax.experimental.pallas import tpu as pltpu

F32 = jnp.float32
BF16 = jnp.bfloat16

HEAD_DIM = 64
N_Q_HEADS = 8
N_KV_HEADS = 2
WINDOW = 128
ATTN_WIDTH = N_Q_HEADS * HEAD_DIM
KV_WIDTH = N_KV_HEADS * HEAD_DIM
N_MEM_HEADS = 4
MEM_WIDTH = N_MEM_HEADS * HEAD_DIM
TOP_K = 4
SWIGLU_LIMIT = 7.0
SWIGLU_ALPHA = 1.702
EPS = 1e-5
ATTN_SCALE = HEAD_DIM ** -0.5
ALIBI_SLOPES = tuple(2.0 ** (-8.0 * (h + 1) / N_Q_HEADS) for h in range(N_Q_HEADS))

LANES = 128
SUBLANES = 8
VMEM_LIMIT = 56 * 1024 * 1024

TQ = 512
BB = 16
BM = 512
REP = 8
W_PARTS = 4


def _rms(x):
    return x * lax.rsqrt(jnp.mean(x * x, axis=-1, keepdims=True) + EPS)


def _dot(a, b):
    return jnp.dot(a, b, preferred_element_type=F32)


def _dot_nt(a, b):
    return lax.dot_general(a, b, (((1,), (1,)), ((), ())), preferred_element_type=F32)


def _iota(shape, axis):
    return lax.broadcasted_iota(jnp.int32, shape, axis)


def _store_token_tiles(ref, x, base=0):
    t = x.shape[0]
    for s in range(x.shape[1] // LANES):
        ref[pl.ds(base + s, t, stride=SUBLANES), :] = x[:, s * LANES:(s + 1) * LANES]


def _load_token_tiles(ref, t, s, base=0):
    return ref[pl.ds(base + s, t, stride=SUBLANES), :]


def _memkv_kernel(mem_ref, g_ref, w_ref, mk_ref, mv_ref, mkt_ref, mvb_ref):
    xn = (_rms(mem_ref[0]) * g_ref[...]).astype(BF16)
    kv = _dot(xn, w_ref[...])
    mk = kv[:, :MEM_WIDTH]
    mv = kv[:, MEM_WIDTH:]
    mk_ref[0] = mk
    mv_ref[0] = mv
    mkt_ref[0] = mk.T.astype(BF16)
    mvb_ref[0] = mv.astype(BF16)


def _memkv(mem, g, w_bf):
    b, m, d = mem.shape
    out_f = jax.ShapeDtypeStruct((b, m, MEM_WIDTH), F32)
    out_b = jax.ShapeDtypeStruct((b, m, MEM_WIDTH), BF16)
    out_t = jax.ShapeDtypeStruct((b, MEM_WIDTH, m), BF16)
    blk = lambda r, c: pl.BlockSpec((1, r, c), lambda i: (i, 0, 0))
    return pl.pallas_call(
        _memkv_kernel,
        grid=(b,),
        in_specs=[blk(m, d), pl.BlockSpec((1, d), lambda i: (0, 0)),
                  pl.BlockSpec((d, 2 * MEM_WIDTH), lambda i: (0, 0))],
        out_specs=[blk(m, MEM_WIDTH), blk(m, MEM_WIDTH), blk(MEM_WIDTH, m), blk(m, MEM_WIDTH)],
        out_shape=[out_f, out_f, out_t, out_b],
        name="memkv",
    )(mem, g, w_bf)


def _router_topk(hn, w_r_ref, b_r_ref, tope_ref, gates_ref):
    n_e = w_r_ref.shape[1]
    logits = _dot(hn.astype(BF16), w_r_ref[...]) + b_r_ref[...]
    rows = logits.shape[0]
    col = _iota((rows, n_e), 1).astype(F32)
    vals, idxs = [], []
    cur = logits
    for _ in range(TOP_K):
        m = jnp.max(cur, axis=-1, keepdims=True)
        idx = jnp.min(jnp.where(cur == m, col, float(n_e)), axis=-1, keepdims=True)
        vals.append(m)
        idxs.append(idx)
        cur = jnp.where(col == idx, -jnp.inf, cur)
    exps = [jnp.exp(v - vals[0]) for v in vals]
    tot = exps[0] + exps[1] + exps[2] + exps[3]
    col4 = _iota((rows, TOP_K), 1)
    te = jnp.zeros((rows, TOP_K), F32)
    ga = jnp.zeros((rows, TOP_K), F32)
    for k in range(TOP_K):
        te = jnp.where(col4 == k, idxs[k], te)
        ga = jnp.where(col4 == k, exps[k] / tot, ga)
    tope_ref[...] = te.astype(jnp.int32)
    gates_ref[...] = ga


def _mix_out(x, attn, conv_out, cross, g_mix_ref, w_out_ref, g_ffn_ref, w_r_ref, b_r_ref,
             h_ref, hn_ref, tope_ref, gates_ref):
    mix = jnp.concatenate([_rms(attn), _rms(conv_out), _rms(cross)], axis=-1) * g_mix_ref[...]
    h = x + _dot(mix.astype(BF16), w_out_ref[...])
    hn = _rms(h) * g_ffn_ref[...]
    h_ref[...] = h
    _store_token_tiles(hn_ref, hn)
    _router_topk(hn, w_r_ref, b_r_ref, tope_ref, gates_ref)


def _swa_block(q_blk, kk, vv, prev_lim, sinks_ref):
    blk = WINDOW
    lane = _iota((2 * blk, KV_WIDTH), 1)
    lo = lane < HEAD_DIM
    kk_r = pltpu.roll(kk, HEAD_DIM, axis=1)
    vv_r = pltpu.roll(vv, HEAD_DIM, axis=1)
    kdup = [jnp.where(lo, kk, kk_r).astype(BF16), jnp.where(lo, kk_r, kk).astype(BF16)]
    vlo = [jnp.where(lo, vv, 0.0).astype(BF16), jnp.where(lo, vv_r, 0.0).astype(BF16)]
    vhi = [jnp.where(lo, 0.0, vv_r).astype(BF16), jnp.where(lo, 0.0, vv).astype(BF16)]
    qi = _iota((blk, 2 * blk), 0)
    kj = _iota((blk, 2 * blk), 1)
    dist = blk + qi - kj
    mask = (dist >= 0) & (dist < WINDOW) & (kj >= prev_lim)
    distf = dist.astype(F32)
    qlo = _iota((blk, 2 * HEAD_DIM), 1) < HEAD_DIM
    outs = []
    for p in range(N_Q_HEADS // 2):
        kh = (2 * p) // (N_Q_HEADS // N_KV_HEADS)
        qp = q_blk[:, p * 2 * HEAD_DIM:(p + 1) * 2 * HEAD_DIM]
        acc = None
        for e in range(2):
            h = 2 * p + e
            qm = jnp.where(qlo if e == 0 else jnp.logical_not(qlo), qp, 0.0).astype(BF16)
            s = _dot_nt(qm, kdup[kh]) - ALIBI_SLOPES[h] * distf
            s = jnp.where(mask, s, -jnp.inf)
            sink = sinks_ref[h]
            m = jnp.maximum(jnp.max(s, axis=-1, keepdims=True), sink)
            pe = jnp.exp(s - m)
            denom = jnp.sum(pe, axis=-1, keepdims=True) + jnp.exp(sink - m)
            o = _dot(pe.astype(BF16), (vlo if e == 0 else vhi)[kh]) / denom
            acc = o if acc is None else acc + o
        outs.append(acc)
    return jnp.concatenate(outs, axis=1)


def _mem_attend_shared(mq, mkt, mvb):
    t = mq.shape[0]
    m_tok = mvb.shape[0]
    qhead = _iota((t, MEM_WIDTH), 1) // HEAD_DIM
    vhead = _iota((m_tok, MEM_WIDTH), 1) // HEAD_DIM
    cross = None
    for h in range(N_MEM_HEADS):
        qm = jnp.where(qhead == h, mq, 0.0).astype(BF16)
        s = _dot(qm, mkt)
        m = jnp.max(s, axis=-1, keepdims=True)
        pe = jnp.exp(s - m)
        denom = jnp.sum(pe, axis=-1, keepdims=True)
        vm = jnp.where(vhead == h, mvb, jnp.zeros_like(mvb))
        o = _dot(pe.astype(BF16), vm) / denom
        cross = o if cross is None else cross + o
    return cross


def _mixer_p_kernel(sinks_ref, x_ref, g_attn_ref, w_in_ref, conv_w_ref, g_mix_ref, w_out_ref, g_ffn_ref,
                    w_r_ref, b_r_ref, mkt_ref, mvb_ref,
                    h_ref, hn_ref, tope_ref, gates_ref, lastk_ref, lastv_ref, convst_ref,
                    ck_ref, cv_ref, cu_ref):
    j = pl.program_id(1)
    nj = pl.num_programs(1)

    @pl.when(j == 0)
    def _():
        ck_ref[...] = jnp.zeros_like(ck_ref)
        cv_ref[...] = jnp.zeros_like(cv_ref)
        cu_ref[...] = jnp.zeros_like(cu_ref)

    x = x_ref[0]
    xn = (_rms(x) * g_attn_ref[...]).astype(BF16)
    z = _dot(xn, w_in_ref[...])
    c0 = ATTN_WIDTH
    c1 = c0 + KV_WIDTH
    c2 = c1 + KV_WIDTH
    cw = conv_w_ref.shape[1]
    c3, c4, c5 = c2 + cw, c2 + 2 * cw, c2 + 3 * cw
    q = z[:, :c0] * ATTN_SCALE
    k = z[:, c0:c1]
    v = z[:, c1:c2]
    cb = z[:, c2:c3]
    cc = z[:, c3:c4]
    cvv = z[:, c4:c5]
    mq = z[:, c5:] * ATTN_SCALE

    blk = WINDOW
    attn_blocks = []
    for i in range(TQ // blk):
        if i == 0:
            pk, pv = ck_ref[...], cv_ref[...]
            prev_lim = jnp.where(j > 0, 0, blk)
        else:
            pk, pv = k[(i - 1) * blk:i * blk], v[(i - 1) * blk:i * blk]
            prev_lim = 0
        kk = jnp.concatenate([pk, k[i * blk:(i + 1) * blk]], axis=0)
        vv = jnp.concatenate([pv, v[i * blk:(i + 1) * blk]], axis=0)
        attn_blocks.append(_swa_block(q[i * blk:(i + 1) * blk], kk, vv, prev_lim, sinks_ref))
    attn = jnp.concatenate(attn_blocks, axis=0)
    ck_ref[...] = k[TQ - blk:]
    cv_ref[...] = v[TQ - blk:]

    u = cc * cvv
    row = _iota(u.shape, 0)
    u1 = jnp.where(row == 0, cu_ref[SUBLANES - 1:SUBLANES, :], pltpu.roll(u, 1, axis=0))
    u2 = jnp.where(row == 0, cu_ref[SUBLANES - 2:SUBLANES - 1, :],
                   jnp.where(row == 1, cu_ref[SUBLANES - 1:SUBLANES, :], pltpu.roll(u, 2, axis=0)))
    conv_out = cb * (conv_w_ref[0:1, :] * u2 + conv_w_ref[1:2, :] * u1 + conv_w_ref[2:3, :] * u)
    cu_ref[...] = u[TQ - SUBLANES:]

    cross = _mem_attend_shared(mq, mkt_ref[0], mvb_ref[0])

    @pl.when(j == nj - 1)
    def _():
        lastk_ref[0] = k[TQ - blk:]
        lastv_ref[0] = v[TQ - blk:]
        convst_ref[0] = u[TQ - 2:]

    _mix_out(x, attn, conv_out, cross, g_mix_ref, w_out_ref, g_ffn_ref, w_r_ref, b_r_ref,
             h_ref, hn_ref, tope_ref, gates_ref)


def _mixer_p(x, sinks, g_attn, w_in, conv_w, g_mix, w_out, g_ffn, w_r, b_r, mkt, mvb):
    b, s, d = x.shape
    nj = s // TQ
    n = b * s
    cw = conv_w.shape[1]
    full = lambda a: pl.BlockSpec(a.shape, lambda bi, ji, *_: (0,) * a.ndim)
    tok = lambda w: pl.BlockSpec((TQ, w), lambda bi, ji, *_: (bi * nj + ji, 0))
    per_b = lambda r, c: pl.BlockSpec((1, r, c), lambda bi, ji, *_: (bi, 0, 0))
    grid_spec = pltpu.PrefetchScalarGridSpec(
        num_scalar_prefetch=1,
        grid=(b, nj),
        in_specs=[pl.BlockSpec((1, TQ, d), lambda bi, ji, *_: (bi, ji, 0)),
                  full(g_attn), full(w_in), full(conv_w), full(g_mix), full(w_out), full(g_ffn),
                  full(w_r), full(b_r), per_b(MEM_WIDTH, mkt.shape[2]), per_b(mvb.shape[1], MEM_WIDTH)],
        out_specs=[tok(d), pl.BlockSpec((TQ * SUBLANES, LANES), lambda bi, ji, *_: (bi * nj + ji, 0)),
                   tok(TOP_K), tok(TOP_K),
                   per_b(WINDOW, KV_WIDTH), per_b(WINDOW, KV_WIDTH), per_b(2, cw)],
        scratch_shapes=[pltpu.VMEM((WINDOW, KV_WIDTH), F32), pltpu.VMEM((WINDOW, KV_WIDTH), F32),
                        pltpu.VMEM((SUBLANES, cw), F32)],
    )
    return pl.pallas_call(
        _mixer_p_kernel,
        grid_spec=grid_spec,
        out_shape=[jax.ShapeDtypeStruct((n, d), F32), jax.ShapeDtypeStruct((n * SUBLANES, LANES), F32),
                   jax.ShapeDtypeStruct((n, TOP_K), jnp.int32), jax.ShapeDtypeStruct((n, TOP_K), F32),
                   jax.ShapeDtypeStruct((b, WINDOW, KV_WIDTH), F32),
                   jax.ShapeDtypeStruct((b, WINDOW, KV_WIDTH), F32),
                   jax.ShapeDtypeStruct((b, 2, cw), F32)],
        compiler_params=pltpu.CompilerParams(dimension_semantics=("arbitrary", "arbitrary"),
                                             vmem_limit_bytes=VMEM_LIMIT),
        name="mixer_p",
    )(sinks, x, g_attn, w_in, conv_w, g_mix, w_out, g_ffn, w_r, b_r, mkt, mvb)


def _per_head_column(values, hrow):
    col = jnp.zeros(hrow.shape, F32)
    for h in range(N_Q_HEADS):
        col = jnp.where(hrow == h, values[h], col)
    return col


def _mixer_s_kernel(sinks_ref, x_ref, pm1_ref, pm2_ref, wk_ref, wv_ref, mk_ref, mv_ref,
                    g_attn_ref, w_in_ref, conv_w_ref, g_mix_ref, w_out_ref, g_ffn_ref, w_r_ref, b_r_ref,
                    h_ref, hn_ref, tope_ref, gates_ref, nwk_ref, nwv_ref, u_ref, *, t_dec):
    r_tok = BB * t_dec
    r_exp = r_tok * REP
    qrows = t_dec * REP
    x = x_ref[...]
    xn = (_rms(x) * g_attn_ref[...]).astype(BF16)
    z = _dot(xn, w_in_ref[...])
    c0 = ATTN_WIDTH
    c1 = c0 + KV_WIDTH
    c2 = c1 + KV_WIDTH
    cw = conv_w_ref.shape[1]
    c3, c4, c5 = c2 + cw, c2 + 2 * cw, c2 + 3 * cw
    q = z[:, :c0] * ATTN_SCALE
    k_new = z[:, c0:c1]
    v_new = z[:, c1:c2]
    cb = z[:, c2:c3]
    cc = z[:, c3:c4]
    cvv = z[:, c4:c5]
    mq = z[:, c5:] * ATTN_SCALE
    win = wk_ref.shape[1]

    xi = _iota((KV_WIDTH, ATTN_WIDTH), 0)
    xl = _iota((KV_WIDTH, ATTN_WIDTH), 1)
    q_per_kv = N_Q_HEADS // N_KV_HEADS
    expand = (xi == (xl // (q_per_kv * HEAD_DIM)) * HEAD_DIM + xl % HEAD_DIM).astype(BF16)
    rr = _iota((r_exp, r_tok), 0)
    rc = _iota((r_exp, r_tok), 1)
    rep = (rr // REP == rc).astype(BF16)

    hrow = _iota((r_exp, 1), 0) % REP
    trow = (_iota((r_exp, 1), 0) // REP) % t_dec
    slope_col = _per_head_column(ALIBI_SLOPES, hrow)
    sink_col = _per_head_column([sinks_ref[h] for h in range(N_Q_HEADS)], hrow)

    qexp = jnp.where(hrow == _iota((r_exp, ATTN_WIDTH), 1) // HEAD_DIM, _dot(rep, q.astype(BF16)), 0.0)
    kexp = _dot(wk_ref[...].reshape(BB * win, KV_WIDTH).astype(BF16), expand).astype(BF16)
    vexp = _dot(wv_ref[...].reshape(BB * win, KV_WIDTH).astype(BF16), expand).astype(BF16)
    s = jnp.einsum("bqc,bkc->bqk", qexp.astype(BF16).reshape(BB, qrows, ATTN_WIDTH),
                   kexp.reshape(BB, win, ATTN_WIDTH), preferred_element_type=F32).reshape(r_exp, win)
    scol = _iota((r_exp, win), 1)
    s = s - slope_col * (win + trow - scol).astype(F32)
    s = jnp.where(scol > trow, s, -jnp.inf)
    knew_exp = _dot(k_new.astype(BF16), expand).astype(BF16)
    vnew_exp = _dot(v_new.astype(BF16), expand).astype(BF16)
    s_new, v_rep = [], []
    for jn in range(t_dec):
        rep_j = (rc == (rr // qrows) * t_dec + jn).astype(BF16)
        k_rep = _dot(rep_j, knew_exp)
        v_rep.append(_dot(rep_j, vnew_exp))
        sj = jnp.sum(qexp * k_rep, axis=-1, keepdims=True) - slope_col * (trow - jn).astype(F32)
        s_new.append(jnp.where(trow >= jn, sj, -jnp.inf))
    m = jnp.maximum(jnp.max(s, axis=-1, keepdims=True), sink_col)
    for sj in s_new:
        m = jnp.maximum(m, sj)
    pe = jnp.exp(s - m)
    denom = jnp.sum(pe, axis=-1, keepdims=True) + jnp.exp(sink_col - m)
    o = jnp.einsum("bqk,bkc->bqc", pe.astype(BF16).reshape(BB, qrows, win),
                   vexp.reshape(BB, win, ATTN_WIDTH), preferred_element_type=F32).reshape(r_exp, ATTN_WIDTH)
    for jn in range(t_dec):
        pj = jnp.exp(s_new[jn] - m)
        denom = denom + pj
        o = o + pj * v_rep[jn]
    o = jnp.where(hrow == _iota((r_exp, ATTN_WIDTH), 1) // HEAD_DIM, o / denom, 0.0)
    attn = jnp.sum(o.reshape(r_tok, REP, ATTN_WIDTH), axis=1)

    m_tok = mk_ref.shape[1]
    mhead = _iota((r_exp, MEM_WIDTH), 1) // HEAD_DIM
    mqexp = jnp.where(hrow == mhead, _dot(rep, mq.astype(BF16)), 0.0).astype(BF16)
    sm = jnp.einsum("bqc,bmc->bqm", mqexp.reshape(BB, qrows, MEM_WIDTH), mk_ref[...].astype(BF16),
                    preferred_element_type=F32).reshape(r_exp, m_tok)
    mm = jnp.max(sm, axis=-1, keepdims=True)
    pm = jnp.exp(sm - mm)
    dm = jnp.sum(pm, axis=-1, keepdims=True)
    om = jnp.einsum("bqm,bmc->bqc", pm.astype(BF16).reshape(BB, qrows, m_tok), mv_ref[...].astype(BF16),
                    preferred_element_type=F32).reshape(r_exp, MEM_WIDTH)
    om = jnp.where(hrow == mhead, om / dm, 0.0)
    cross = jnp.sum(om.reshape(r_tok, REP, MEM_WIDTH), axis=1)

    u = cc * cvv
    tt = _iota(u.shape, 0) % t_dec
    u1 = jnp.where(tt >= 1, pltpu.roll(u, 1, axis=0), pm1_ref[...])
    u2 = jnp.where(tt >= 2, pltpu.roll(u, 2, axis=0), pm2_ref[...])
    conv_out = cb * (conv_w_ref[0:1, :] * u2 + conv_w_ref[1:2, :] * u1 + conv_w_ref[2:3, :] * u)
    u_ref[...] = u

    nwk_ref[:, 0:win - t_dec, :] = wk_ref[:, t_dec:win, :]
    nwv_ref[:, 0:win - t_dec, :] = wv_ref[:, t_dec:win, :]
    for b in range(BB):
        nwk_ref[b, win - t_dec:win, :] = k_new[b * t_dec:(b + 1) * t_dec, :]
        nwv_ref[b, win - t_dec:win, :] = v_new[b * t_dec:(b + 1) * t_dec, :]

    _mix_out(x, attn, conv_out, cross, g_mix_ref, w_out_ref, g_ffn_ref, w_r_ref, b_r_ref,
             h_ref, hn_ref, tope_ref, gates_ref)


def _mixer_s(x2, t_dec, pm1, pm2, wk, wv, mk, mv, sinks, g_attn, w_in, conv_w, g_mix, w_out, g_ffn, w_r, b_r):
    n, d = x2.shape
    nb = wk.shape[0]
    win = wk.shape[1]
    m_tok = mk.shape[1]
    cw = conv_w.shape[1]
    r_tok = BB * t_dec
    full = lambda a: pl.BlockSpec(a.shape, lambda i, *_: (0,) * a.ndim)
    tok = lambda w: pl.BlockSpec((r_tok, w), lambda i, *_: (i, 0))
    per_b = lambda r, c: pl.BlockSpec((BB, r, c), lambda i, *_: (i, 0, 0))
    grid_spec = pltpu.PrefetchScalarGridSpec(
        num_scalar_prefetch=1,
        grid=(nb // BB,),
        in_specs=[tok(d), tok(cw), tok(cw), per_b(win, KV_WIDTH), per_b(win, KV_WIDTH),
                  per_b(m_tok, MEM_WIDTH), per_b(m_tok, MEM_WIDTH),
                  full(g_attn), full(w_in), full(conv_w), full(g_mix), full(w_out), full(g_ffn),
                  full(w_r), full(b_r)],
        out_specs=[tok(d), pl.BlockSpec((r_tok * SUBLANES, LANES), lambda i, *_: (i, 0)),
                   tok(TOP_K), tok(TOP_K), per_b(win, KV_WIDTH), per_b(win, KV_WIDTH), tok(cw)],
    )
    return pl.pallas_call(
        functools.partial(_mixer_s_kernel, t_dec=t_dec),
        grid_spec=grid_spec,
        out_shape=[jax.ShapeDtypeStruct((n, d), F32), jax.ShapeDtypeStruct((n * SUBLANES, LANES), F32),
                   jax.ShapeDtypeStruct((n, TOP_K), jnp.int32), jax.ShapeDtypeStruct((n, TOP_K), F32),
                   jax.ShapeDtypeStruct((nb, win, KV_WIDTH), F32), jax.ShapeDtypeStruct((nb, win, KV_WIDTH), F32),
                   jax.ShapeDtypeStruct((n, cw), F32)],
        compiler_params=pltpu.CompilerParams(dimension_semantics=("arbitrary",), vmem_limit_bytes=VMEM_LIMIT),
        name="mixer_s",
    )(sinks, x2, pm1, pm2, wk, wv, mk, mv, g_attn, w_in, conv_w, g_mix, w_out, g_ffn, w_r, b_r)


def _lpos_kernel(tope_ref, lpos_ref, counts_ref, tri_ref, *, n_e):
    i = pl.program_id(0)
    tm = tope_ref.shape[0]

    @pl.when(i == 0)
    def _():
        tri_ref[...] = (_iota((tm, tm), 0) > _iota((tm, tm), 1)).astype(BF16)

    te = tope_ref[...]
    col = _iota((tm, n_e), 1)
    hits = [te[:, k:k + 1] == col for k in range(TOP_K)]
    onehot = jnp.zeros((tm, n_e), F32)
    lower = jnp.zeros((tm, n_e), F32)
    for k in range(TOP_K):
        onehot = onehot + hits[k].astype(F32)
        lower = lower + (te[:, k:k + 1] < col).astype(F32)
    ahead = _dot(tri_ref[...], onehot.astype(BF16)) + jnp.sum(lower, axis=0, keepdims=True)
    col4 = _iota((tm, TOP_K), 1)
    pos = jnp.zeros((tm, TOP_K), F32)
    for k in range(TOP_K):
        pos = jnp.where(col4 == k, jnp.sum(jnp.where(hits[k], ahead, 0.0), axis=-1, keepdims=True), pos)
    half = (i % 2) * (tm * TOP_K)
    lpos_ref[...] = (pos.astype(jnp.int32) + half) * SUBLANES
    counts_ref[0] = jnp.sum(onehot, axis=0, keepdims=True)


def _lpos(tope, tm, n_e):
    n = tope.shape[0]
    return pl.pallas_call(
        functools.partial(_lpos_kernel, n_e=n_e),
        grid=(n // tm,),
        in_specs=[pl.BlockSpec((tm, TOP_K), lambda i: (i, 0))],
        out_specs=[pl.BlockSpec((tm, TOP_K), lambda i: (i, 0)), pl.BlockSpec((1, 1, n_e), lambda i: (i, 0, 0))],
        out_shape=[jax.ShapeDtypeStruct((n, TOP_K), jnp.int32), jax.ShapeDtypeStruct((n // tm, 1, n_e), F32)],
        scratch_shapes=[pltpu.VMEM((tm, tm), BF16)],
        compiler_params=pltpu.CompilerParams(dimension_semantics=("arbitrary",)),
        name="lpos",
    )(tope)


def _rows(ref, first_row, n_rows):
    return ref.at[pl.ds(pl.multiple_of(first_row * SUBLANES, SUBLANES), n_rows * SUBLANES)]


def _group_chunks(meta, g, n_e, max_rows, make_copy, wait):
    off_ref, cnt_ref, lst_ref = meta
    if wait:
        make_copy(0, 0, max_rows * TOP_K).wait()
        return

    def per_expert(e, c):
        off = off_ref[g * n_e + e]
        lst = lst_ref[g * n_e + e]
        _run_copies(cnt_ref[g * n_e + e], max_rows, lambda done, size: make_copy(lst + done, off + done, size),
                    wait=False)
        return c

    lax.fori_loop(0, n_e, per_expert, 0)


def _run_copies(n_rows, max_rows, make_copy, wait):
    done = 0
    for size in [1 << b for b in range(max_rows.bit_length() - 1, -1, -1)]:
        bit = n_rows & size

        @pl.when(bit != 0)
        def _():
            cp = make_copy(done, size)
            cp.wait() if wait else cp.start()
        done = done + bit


def _dispatch_kernel(off_ref, cnt_ref, lst_ref, zrow_ref, zcnt_ref, nused_ref, lpos_ref, hn_a_ref, hn_b_ref, xs_ref,
                     stage_ref, zbuf_ref, sem, zsem, *, n_e, n_blocks, groups_a):
    j = pl.program_id(0)
    nj = pl.num_programs(0)
    tm = hn_a_ref.shape[0] // SUBLANES
    slot = j % 2
    meta = (off_ref, cnt_ref, lst_ref)

    def chunks(g, s, wait):
        _group_chunks(meta, g, n_e, tm,
                      lambda lrow, grow, size: pltpu.make_async_copy(
                          _rows(stage_ref, s * (tm * TOP_K) + lrow, size), _rows(xs_ref, grow, size),
                          sem.at[s]), wait)

    def zero_fill(wait):
        def zero_run(first_row, n_rows):
            _run_copies(n_rows, BM, lambda done, size: pltpu.make_async_copy(
                _rows(zbuf_ref, 0, size), _rows(xs_ref, first_row + done, size), zsem), wait)

        def expert_pad(e, c):
            zero_run(zrow_ref[e], zcnt_ref[e])
            return c

        def tail_block(b, c):
            zero_run(b * BM, jnp.int32(BM))
            return c

        lax.fori_loop(0, n_e, expert_pad, 0)
        lax.fori_loop(nused_ref[0], n_blocks, tail_block, 0)

    @pl.when(j == 0)
    def _():
        zbuf_ref[...] = jnp.zeros_like(zbuf_ref)
        zero_fill(False)

    @pl.when(j >= 2)
    def _():
        chunks(j - 2, slot, True)

    def place_from(hn_ref):
        def place(t, c):
            tile = hn_ref[pl.ds(pl.multiple_of(t * SUBLANES, SUBLANES), SUBLANES), :]
            for k in range(TOP_K):
                pos = pl.multiple_of(lpos_ref[0, 0, t * TOP_K + k], SUBLANES)
                stage_ref[pl.ds(pos, SUBLANES), :] = tile
            return c
        lax.fori_loop(0, tm, place, 0, unroll=8)

    @pl.when(j < groups_a)
    def _():
        place_from(hn_a_ref)

    @pl.when(j >= groups_a)
    def _():
        place_from(hn_b_ref)

    chunks(j, slot, False)

    @pl.when(j == nj - 1)
    def _():
        @pl.when(j >= 1)
        def _():
            chunks(j - 1, 1 - slot, True)
        chunks(j, slot, True)
        zero_fill(True)


def _dispatch(hn_a, hn_b, lpos, meta, zrow, zcnt, nused, tm, n_e, n_blocks):
    nt = lpos.shape[0] // tm
    groups_a = hn_a.shape[0] // (tm * SUBLANES)
    assert groups_a >= 1 and groups_a + hn_b.shape[0] // (tm * SUBLANES) == nt
    grid_spec = pltpu.PrefetchScalarGridSpec(
        num_scalar_prefetch=6,
        grid=(nt,),
        in_specs=[pl.BlockSpec((1, 1, tm * TOP_K), lambda j, *_: (j, 0, 0), memory_space=pltpu.SMEM),
                  pl.BlockSpec((tm * SUBLANES, LANES), lambda j, *_: (jnp.minimum(j, groups_a - 1), 0)),
                  pl.BlockSpec((tm * SUBLANES, LANES), lambda j, *_: (jnp.maximum(j - groups_a, 0), 0))],
        out_specs=pl.BlockSpec(memory_space=pl.ANY),
        scratch_shapes=[pltpu.VMEM((2 * tm * TOP_K * SUBLANES, LANES), F32), pltpu.VMEM((BM * SUBLANES, LANES), F32),
                        pltpu.SemaphoreType.DMA((2,)), pltpu.SemaphoreType.DMA(())],
    )
    return pl.pallas_call(
        functools.partial(_dispatch_kernel, n_e=n_e, n_blocks=n_blocks, groups_a=groups_a),
        grid_spec=grid_spec,
        out_shape=jax.ShapeDtypeStruct((n_blocks * BM * SUBLANES, LANES), F32),
        compiler_params=pltpu.CompilerParams(dimension_semantics=("arbitrary",), vmem_limit_bytes=VMEM_LIMIT),
        name="dispatch",
    )(*meta, zrow, zcnt, nused, lpos.reshape(nt, 1, tm * TOP_K), hn_a, hn_b)


def _experts_kernel(blk_e_ref, nused_ref, next_e_ref, xs_ref, bgu_ref, bd_ref, wgu_hbm_ref, wd_hbm_ref, ys_ref,
                    wgu_ref, wd_ref, wgu_bf_ref, wd_bf_ref, wsem):
    i = pl.program_id(0)
    nused = nused_ref[0]
    d, d_ff2 = wgu_ref.shape
    grp = 2 * LANES

    def weight_copies(e):
        gu_cols = d_ff2 // W_PARTS
        dn_rows = (d_ff2 // 2) // W_PARTS
        cps = []
        for p in range(W_PARTS):
            cps.append(pltpu.make_async_copy(wgu_hbm_ref.at[e, :, pl.ds(p * gu_cols, gu_cols)],
                                             wgu_ref.at[:, pl.ds(p * gu_cols, gu_cols)], wsem.at[p]))
            cps.append(pltpu.make_async_copy(wd_hbm_ref.at[e, pl.ds(p * dn_rows, dn_rows), :],
                                             wd_ref.at[pl.ds(p * dn_rows, dn_rows), :], wsem.at[W_PARTS + p]))
        return cps

    @pl.when(i < nused)
    def _():
        e = blk_e_ref[i]
        e_prev = blk_e_ref[jnp.maximum(i - 1, 0)]

        @pl.when(i == 0)
        def _():
            for cp in weight_copies(e):
                cp.start()

        @pl.when((i == 0) | (e != e_prev))
        def _():
            for cp in weight_copies(e):
                cp.wait()
            pr = _iota((grp, grp), 0)
            pc = _iota((grp, grp), 1)
            perm = (pr == jnp.where(pc < LANES, 2 * pc, 2 * (pc - LANES) + 1)).astype(BF16)
            for g in range(d_ff2 // grp):
                w = wgu_ref[:, g * grp:(g + 1) * grp].astype(BF16)
                wgu_bf_ref[:, g * grp:(g + 1) * grp] = _dot(w, perm).astype(BF16)
            wd_bf_ref[...] = wd_ref[...].astype(BF16)

            @pl.when(next_e_ref[e] >= 0)
            def _():
                for cp in weight_copies(next_e_ref[e]):
                    cp.start()

        x = jnp.concatenate([_load_token_tiles(xs_ref, BM, s).astype(BF16) for s in range(SUBLANES)], axis=1)
        hgu = _dot(x, wgu_bf_ref[...]) + bgu_ref[0]
        acts = []
        for g in range(d_ff2 // grp):
            gate = jnp.minimum(hgu[:, g * grp:g * grp + LANES], SWIGLU_LIMIT)
            up = jnp.clip(hgu[:, g * grp + LANES:(g + 1) * grp], -SWIGLU_LIMIT, SWIGLU_LIMIT)
            glu = gate * (1.0 / (1.0 + jnp.exp(-SWIGLU_ALPHA * gate)))
            acts.append(((up + 1.0) * glu).astype(BF16))
        act = jnp.concatenate(acts, axis=1)
        _store_token_tiles(ys_ref, _dot(act, wd_bf_ref[...]) + bd_ref[0])

    @pl.when(i >= nused)
    def _():
        ys_ref[...] = jnp.zeros_like(ys_ref)


def _experts(xs, blk_e, nused, next_e, w_gate_up, b_gu_perm, w_down, b_down):
    n_e, d, d_ff2 = w_gate_up.shape
    n_blocks = xs.shape[0] // (BM * SUBLANES)
    expert = lambda i, be, nu, ne: be[jnp.minimum(i, jnp.maximum(nu[0] - 1, 0))]
    rows_spec = pl.BlockSpec((BM * SUBLANES, LANES), lambda i, *_: (i, 0))
    grid_spec = pltpu.PrefetchScalarGridSpec(
        num_scalar_prefetch=3,
        grid=(n_blocks,),
        in_specs=[pl.BlockSpec((BM * SUBLANES, LANES), lambda i, be, nu, ne: (jnp.minimum(i, jnp.maximum(nu[0] - 1, 0)), 0)),
                  pl.BlockSpec((1, 1, d_ff2), lambda i, *s: (expert(i, *s), 0, 0)),
                  pl.BlockSpec((1, 1, d), lambda i, *s: (expert(i, *s), 0, 0)),
                  pl.BlockSpec(memory_space=pl.ANY), pl.BlockSpec(memory_space=pl.ANY)],
        out_specs=rows_spec,
        scratch_shapes=[pltpu.VMEM((d, d_ff2), F32), pltpu.VMEM((d_ff2 // 2, d), F32),
                        pltpu.VMEM((d, d_ff2), BF16), pltpu.VMEM((d_ff2 // 2, d), BF16),
                        pltpu.SemaphoreType.DMA((2 * W_PARTS,))],
    )
    return pl.pallas_call(
        _experts_kernel,
        grid_spec=grid_spec,
        out_shape=jax.ShapeDtypeStruct(xs.shape, F32),
        compiler_params=pltpu.CompilerParams(dimension_semantics=("arbitrary",), vmem_limit_bytes=VMEM_LIMIT),
        name="experts",
    )(blk_e, nused, next_e, xs, b_gu_perm, b_down, w_gate_up, w_down)


def _combine_kernel(off_ref, cnt_ref, lst_ref, lpos_ref, gates_ref, h_ref, g_ref, ys_ref, out_ref,
                    stage_ref, acc_ref, sem, *, n_e, g0):
    i = pl.program_id(0)
    ng = pl.num_programs(0)
    tm, d = h_ref.shape
    slot = (g0 + i) % 2
    meta = (off_ref, cnt_ref, lst_ref)

    def chunks(g, s, wait):
        _group_chunks(meta, g, n_e, tm,
                      lambda lrow, grow, size: pltpu.make_async_copy(
                          _rows(ys_ref, grow, size), _rows(stage_ref, s * (tm * TOP_K) + lrow, size),
                          sem.at[s]), wait)

    @pl.when(i == 0)
    def _():
        chunks(g0, slot, False)

    @pl.when(i + 1 < ng)
    def _():
        chunks(g0 + i + 1, 1 - slot, False)

    chunks(g0 + i, slot, True)

    def mix(t, c):
        acc = None
        for k in range(TOP_K):
            pos = pl.multiple_of(lpos_ref[0, 0, t * TOP_K + k], SUBLANES)
            term = stage_ref[pl.ds(pos, SUBLANES), :] * gates_ref[0, 0, t * TOP_K + k]
            acc = term if acc is None else acc + term
        acc_ref[pl.ds(pl.multiple_of(t * SUBLANES, SUBLANES), SUBLANES), :] = acc
        return c

    lax.fori_loop(0, tm, mix, 0, unroll=8)

    h = h_ref[...]
    parts = []
    sq = jnp.zeros((tm, LANES), F32)
    for s in range(d // LANES):
        y = h[:, s * LANES:(s + 1) * LANES] + _load_token_tiles(acc_ref, tm, s)
        sq = sq + y * y
        parts.append(y)
    rinv = lax.rsqrt(jnp.sum(sq, axis=-1, keepdims=True) / d + EPS)
    out_ref[...] = jnp.concatenate(parts, axis=1) * rinv * g_ref[...]


def _combine(ys, h, lpos, gates, meta, g_final, tm, n_e, g0):
    n, d = h.shape
    grid_spec = pltpu.PrefetchScalarGridSpec(
        num_scalar_prefetch=3,
        grid=(n // tm,),
        in_specs=[pl.BlockSpec((1, 1, tm * TOP_K), lambda i, *_: (g0 + i, 0, 0), memory_space=pltpu.SMEM),
                  pl.BlockSpec((1, 1, tm * TOP_K), lambda i, *_: (g0 + i, 0, 0), memory_space=pltpu.SMEM),
                  pl.BlockSpec((tm, d), lambda i, *_: (i, 0)),
                  pl.BlockSpec((1, d), lambda i, *_: (0, 0)),
                  pl.BlockSpec(memory_space=pl.ANY)],
        out_specs=pl.BlockSpec((tm, d), lambda i, *_: (i, 0)),
        scratch_shapes=[pltpu.VMEM((2 * tm * TOP_K * SUBLANES, LANES), F32), pltpu.VMEM((tm * SUBLANES, LANES), F32),
                        pltpu.SemaphoreType.DMA((2,))],
    )
    ngroups = lpos.shape[0] // tm
    return pl.pallas_call(
        functools.partial(_combine_kernel, n_e=n_e, g0=g0),
        grid_spec=grid_spec,
        out_shape=jax.ShapeDtypeStruct((n, d), F32),
        compiler_params=pltpu.CompilerParams(dimension_semantics=("arbitrary",), vmem_limit_bytes=VMEM_LIMIT),
        name="combine",
    )(*meta, lpos.reshape(ngroups, 1, tm * TOP_K), gates.reshape(ngroups, 1, tm * TOP_K), h, g_final, ys)


def _largest_tile(cands, *sizes):
    for c in cands:
        if all(s % c == 0 for s in sizes):
            return c
    raise ValueError(f"no tile in {cands} divides {sizes}")


def kernel(x_prompt, x_sample, mem_prompt, cache_win_k, cache_win_v, state_conv, cache_mem_k, cache_mem_v, g_attn_norm, w_in, conv_w, attn_sinks, g_mem_norm, w_mem_kv, g_mix_out, w_out, g_ffn_norm, w_router, b_router, w_gate_up, b_gate_up, w_down, b_down, g_final):
    depth = w_in.shape[0]
    assert depth == 1, "single-layer step"
    b, s, d = x_prompt.shape
    nb, t_dec, _ = x_sample.shape
    n_e = w_router.shape[2]
    d_ff2 = w_gate_up.shape[3]
    cw = conv_w.shape[2]
    win = cache_win_k.shape[2]
    m_tok = cache_mem_k.shape[2]
    assert s % TQ == 0 and nb % BB == 0 and win == WINDOW and t_dec <= SUBLANES and d_ff2 % (2 * LANES) == 0
    assert d == SUBLANES * LANES, "token-tile layout: one token is one (8, 128) f32 tile"

    row = lambda a: a.reshape(1, -1)
    w_in_bf = w_in[0].astype(BF16)
    w_out_bf = w_out[0].astype(BF16)
    w_r_bf = w_router[0].astype(BF16)
    sinks = attn_sinks[0].astype(F32)
    shared = (row(g_attn_norm[0]), w_in_bf, conv_w[0], row(g_mix_out[0]), w_out_bf, row(g_ffn_norm[0]),
              w_r_bf, row(b_router[0]))

    mk_p, mv_p, mkt, mvb = _memkv(mem_prompt, row(g_mem_norm[0]), w_mem_kv[0].astype(BF16))
    h_p, hn_p, tope_p, gates_p, lastk, lastv, convst = _mixer_p(x_prompt, sinks, *shared, mkt, mvb)

    zeros = lambda r: jnp.zeros((nb, r, cw), F32)
    st = state_conv[0]
    pm1 = jnp.concatenate([st[:, 1:2], zeros(t_dec - 1)], axis=1).reshape(nb * t_dec, cw)
    pm2 = jnp.concatenate([st, zeros(t_dec - 2)], axis=1).reshape(nb * t_dec, cw)
    h_s, hn_s, tope_s, gates_s, nwk, nwv, u_s = _mixer_s(
        x_sample.reshape(nb * t_dec, d), t_dec, pm1, pm2,
        cache_win_k[0].reshape(nb, win, KV_WIDTH), cache_win_v[0].reshape(nb, win, KV_WIDTH),
        cache_mem_k[0].reshape(nb, m_tok, MEM_WIDTH), cache_mem_v[0].reshape(nb, m_tok, MEM_WIDTH),
        sinks, *shared)

    n_p, n_s = b * s, nb * t_dec
    n = n_p + n_s
    tm = _largest_tile((512, 256, 128, 64, 32, 16, 8), n_p, n_s)
    tope = jnp.concatenate([tope_p, tope_s], axis=0)
    gates = jnp.concatenate([gates_p, gates_s], axis=0)

    lpos, cnt_f = _lpos(tope, tm, n_e)
    cnt = cnt_f[:, 0, :].astype(jnp.int32)
    counts = jnp.sum(cnt, axis=0)
    padded = (counts + BM - 1) // BM * BM
    pad_ends = jnp.cumsum(padded)
    pad_starts = pad_ends - padded
    nk = n * TOP_K
    n_blocks = -(-nk // BM) + n_e
    nused = (pad_ends[-1:] // BM).astype(jnp.int32)
    blk_start = jnp.arange(n_blocks, dtype=jnp.int32) * BM
    blk_e = jnp.minimum(jnp.sum((pad_ends[None, :] <= blk_start[:, None]).astype(jnp.int32), axis=1), n_e - 1)
    zrow = (pad_starts + counts).astype(jnp.int32)
    zcnt = (padded - counts).astype(jnp.int32)
    off = pad_starts[None, :] + jnp.cumsum(cnt, axis=0) - cnt
    lstart = jnp.cumsum(cnt, axis=1) - cnt
    meta = (off.reshape(-1).astype(jnp.int32), cnt.reshape(-1), lstart.reshape(-1).astype(jnp.int32))

    xs = _dispatch(hn_p, hn_s, lpos, meta, zrow, zcnt, nused, tm, n_e, n_blocks)

    grp = 2 * LANES
    b_gu = b_gate_up[0].reshape(n_e, d_ff2 // grp, LANES, 2).transpose(0, 1, 3, 2).reshape(n_e, 1, d_ff2)
    owner = jnp.where(padded > 0, jnp.arange(n_e, dtype=jnp.int32), n_e)
    following = jnp.concatenate([lax.cummin(owner, reverse=True)[1:], jnp.full((1,), n_e, jnp.int32)])
    next_e = jnp.where(following < n_e, following, -1).astype(jnp.int32)
    ys = _experts(xs, blk_e, nused, next_e, w_gate_up[0], b_gu, w_down[0], b_down[0].reshape(n_e, 1, d))

    g_fin = row(g_final)
    y_p = _combine(ys, h_p, lpos, gates, meta, g_fin, tm, n_e, 0)
    y_s = _combine(ys, h_s, lpos, gates, meta, g_fin, tm, n_e, n_p // tm)

    kv5 = lambda a, bsz, r, hds: a.reshape(1, bsz, r, hds, HEAD_DIM)
    return (y_p.reshape(b, s, d), y_s.reshape(nb, t_dec, d),
            kv5(lastk, b, WINDOW, N_KV_HEADS), kv5(lastv, b, WINDOW, N_KV_HEADS),
            convst.reshape(1, b, 2, cw),
            kv5(mk_p, b, m_tok, N_MEM_HEADS), kv5(mv_p, b, m_tok, N_MEM_HEADS),
            kv5(nwk, nb, win, N_KV_HEADS), kv5(nwv, nb, win, N_KV_HEADS),
            u_s.reshape(nb, t_dec, cw)[:, t_dec - 2:].reshape(1, nb, 2, cw))
```

```python
import functools

import jax
import jax.numpy as jnp
from jax import lax
from jax.experimental import pallas as pl
from jax.experimental.pallas import tpu as pltpu

F32 = jnp.float32
BF16 = jnp.bfloat16

HEAD_DIM = 64
N_Q_HEADS = 8
N_KV_HEADS = 2
WINDOW = 128
ATTN_WIDTH = N_Q_HEADS * HEAD_DIM
KV_WIDTH = N_KV_HEADS * HEAD_DIM
N_MEM_HEADS = 4
MEM_WIDTH = N_MEM_HEADS * HEAD_DIM
TOP_K = 4
SWIGLU_LIMIT = 7.0
SWIGLU_ALPHA = 1.702
EPS = 1e-5
ATTN_SCALE = HEAD_DIM ** -0.5
ALIBI_SLOPES = tuple(2.0 ** (-8.0 * (h + 1) / N_Q_HEADS) for h in range(N_Q_HEADS))

LANES = 128
SUBLANES = 8
VMEM_LIMIT = 56 * 1024 * 1024

TQ = 512
BB = 16
BM = 512
REP = 8
W_PARTS = 4
MIX_CHUNK = 256
IN_CHUNK = 256


def _rms(x):
    return x * lax.rsqrt(jnp.mean(x * x, axis=-1, keepdims=True) + EPS)


def _dot(a, b):
    return jnp.dot(a, b, preferred_element_type=F32)


def _dot_nt(a, b):
    return lax.dot_general(a, b, (((1,), (1,)), ((), ())), preferred_element_type=F32)


def _iota(shape, axis):
    return lax.broadcasted_iota(jnp.int32, shape, axis)


def _store_token_tiles(ref, x, base=0):
    t = x.shape[0]
    for s in range(x.shape[1] // LANES):
        ref[pl.ds(base + s, t, stride=SUBLANES), :] = x[:, s * LANES:(s + 1) * LANES]


def _load_token_tiles(ref, t, s, base=0):
    return ref[pl.ds(base + s, t, stride=SUBLANES), :]


def _memkv_kernel(mem_ref, g_ref, w_ref, mk_ref, mv_ref, mkt_ref, mvb_ref):
    xn = (_rms(mem_ref[0]) * g_ref[...]).astype(BF16)
    kv = _dot(xn, w_ref[...])
    mk = kv[:, :MEM_WIDTH]
    mv = kv[:, MEM_WIDTH:]
    mk_ref[0] = mk
    mv_ref[0] = mv
    mkt_ref[0] = mk.T.astype(BF16)
    mvb_ref[0] = mv.astype(BF16)


def _memkv(mem, g, w_bf):
    b, m, d = mem.shape
    out_f = jax.ShapeDtypeStruct((b, m, MEM_WIDTH), F32)
    out_b = jax.ShapeDtypeStruct((b, m, MEM_WIDTH), BF16)
    out_t = jax.ShapeDtypeStruct((b, MEM_WIDTH, m), BF16)
    blk = lambda r, c: pl.BlockSpec((1, r, c), lambda i: (i, 0, 0))
    return pl.pallas_call(
        _memkv_kernel,
        grid=(b,),
        in_specs=[blk(m, d), pl.BlockSpec((1, d), lambda i: (0, 0)),
                  pl.BlockSpec((d, 2 * MEM_WIDTH), lambda i: (0, 0))],
        out_specs=[blk(m, MEM_WIDTH), blk(m, MEM_WIDTH), blk(MEM_WIDTH, m), blk(m, MEM_WIDTH)],
        out_shape=[out_f, out_f, out_t, out_b],
        name="memkv",
    )(mem, g, w_bf)


def _router_topk(hn, w_r_ref, b_r_ref, tope_ref, gates_ref, rows_at):
    n_e = w_r_ref.shape[1]
    logits = _dot(hn.astype(BF16), w_r_ref[...]) + b_r_ref[...]
    rows = logits.shape[0]
    col = _iota((rows, n_e), 1).astype(F32)
    vals, idxs = [], []
    cur = logits
    for _ in range(TOP_K):
        m = jnp.max(cur, axis=-1, keepdims=True)
        idx = jnp.min(jnp.where(cur == m, col, float(n_e)), axis=-1, keepdims=True)
        vals.append(m)
        idxs.append(idx)
        cur = jnp.where(col == idx, -jnp.inf, cur)
    exps = [jnp.exp(v - vals[0]) for v in vals]
    tot = exps[0] + exps[1] + exps[2] + exps[3]
    col4 = _iota((rows, TOP_K), 1)
    te = jnp.zeros((rows, TOP_K), F32)
    ga = jnp.zeros((rows, TOP_K), F32)
    for k in range(TOP_K):
        te = jnp.where(col4 == k, idxs[k], te)
        ga = jnp.where(col4 == k, exps[k] / tot, ga)
    tope_ref[rows_at, :] = te.astype(jnp.int32)
    gates_ref[rows_at, :] = ga


def _mix_out(x, attn, conv_out, cross, g_mix_ref, w_out_ref, g_ffn_ref, w_r_ref, b_r_ref,
             h_ref, hn_ref, tope_ref, gates_ref):
    rows = x.shape[0]
    chunk = MIX_CHUNK if rows % MIX_CHUNK == 0 else rows
    for r0 in range(0, rows, chunk):
        at = slice(r0, r0 + chunk)
        mix = jnp.concatenate([_rms(attn[at]), _rms(conv_out[at]), _rms(cross[at])], axis=-1) * g_mix_ref[...]
        h = x[at] + _dot(mix.astype(BF16), w_out_ref[...])
        hn = _rms(h) * g_ffn_ref[...]
        h_ref[at, :] = h
        _store_token_tiles(hn_ref, hn, base=r0 * SUBLANES)
        _router_topk(hn, w_r_ref, b_r_ref, tope_ref, gates_ref, at)


def _swa_bias(prev_lim):
    blk = WINDOW
    qi = _iota((blk, 2 * blk), 0)
    kj = _iota((blk, 2 * blk), 1)
    dist = blk + qi - kj
    mask = (dist >= 0) & (dist < WINDOW) & (kj >= prev_lim)
    distf = dist.astype(F32)
    return [jnp.where(mask, -ALIBI_SLOPES[h] * distf, -jnp.inf) for h in range(N_Q_HEADS)]


def _swa_block(q_blk, kk, vv, bias, sinks_ref):
    blk = WINDOW
    lane = _iota((2 * blk, KV_WIDTH), 1)
    lo = lane < HEAD_DIM
    kk_r = pltpu.roll(kk, HEAD_DIM, axis=1)
    vv_r = pltpu.roll(vv, HEAD_DIM, axis=1)
    kdup = [jnp.where(lo, kk, kk_r).astype(BF16), jnp.where(lo, kk_r, kk).astype(BF16)]
    vlo = [jnp.where(lo, vv, 0.0).astype(BF16), jnp.where(lo, vv_r, 0.0).astype(BF16)]
    vhi = [jnp.where(lo, 0.0, vv_r).astype(BF16), jnp.where(lo, 0.0, vv).astype(BF16)]
    qlo = _iota((blk, 2 * HEAD_DIM), 1) < HEAD_DIM
    outs = []
    for p in range(N_Q_HEADS // 2):
        kh = (2 * p) // (N_Q_HEADS // N_KV_HEADS)
        qp = q_blk[:, p * 2 * HEAD_DIM:(p + 1) * 2 * HEAD_DIM]
        acc = None
        for e in range(2):
            h = 2 * p + e
            qm = jnp.where(qlo if e == 0 else jnp.logical_not(qlo), qp, 0.0).astype(BF16)
            s = _dot_nt(qm, kdup[kh]) + bias[h]
            sink = sinks_ref[h]
            m = jnp.maximum(jnp.max(s, axis=-1, keepdims=True), sink)
            pe = jnp.exp(s - m)
            denom = jnp.sum(pe, axis=-1, keepdims=True) + jnp.exp(sink - m)
            o = _dot(pe.astype(BF16), (vlo if e == 0 else vhi)[kh]) / denom
            acc = o if acc is None else acc + o
        outs.append(acc)
    return jnp.concatenate(outs, axis=1)


def _mem_attend_shared(mq, mkt, mvb):
    t = mq.shape[0]
    m_tok = mvb.shape[0]
    qhead = _iota((t, MEM_WIDTH), 1) // HEAD_DIM
    vhead = _iota((m_tok, MEM_WIDTH), 1) // HEAD_DIM
    cross = None
    for h in range(N_MEM_HEADS):
        qm = jnp.where(qhead == h, mq, 0.0).astype(BF16)
        s = _dot(qm, mkt)
        m = jnp.max(s, axis=-1, keepdims=True)
        pe = jnp.exp(s - m)
        denom = jnp.sum(pe, axis=-1, keepdims=True)
        vm = jnp.where(vhead == h, mvb, jnp.zeros_like(mvb))
        o = _dot(pe.astype(BF16), vm) / denom
        cross = o if cross is None else cross + o
    return cross


def _mixer_p_kernel(sinks_ref, x_ref, g_attn_ref, w_in_ref, conv_w_ref, g_mix_ref, w_out_ref, g_ffn_ref,
                    w_r_ref, b_r_ref, mkt_ref, mvb_ref,
                    h_ref, hn_ref, tope_ref, gates_ref, lastk_ref, lastv_ref, convst_ref,
                    ck_ref, cv_ref, cu_ref):
    j = pl.program_id(1)
    nj = pl.num_programs(1)

    @pl.when(j == 0)
    def _():
        ck_ref[...] = jnp.zeros_like(ck_ref)
        cv_ref[...] = jnp.zeros_like(cv_ref)
        cu_ref[...] = jnp.zeros_like(cu_ref)

    x = x_ref[0]
    z = jnp.concatenate([_dot((_rms(x[r0:r0 + IN_CHUNK]) * g_attn_ref[...]).astype(BF16), w_in_ref[...])
                         for r0 in range(0, TQ, IN_CHUNK)], axis=0)
    c0 = ATTN_WIDTH
    c1 = c0 + KV_WIDTH
    c2 = c1 + KV_WIDTH
    cw = conv_w_ref.shape[1]
    c3, c4, c5 = c2 + cw, c2 + 2 * cw, c2 + 3 * cw
    q = z[:, :c0] * ATTN_SCALE
    k = z[:, c0:c1]
    v = z[:, c1:c2]
    cb = z[:, c2:c3]
    cc = z[:, c3:c4]
    cvv = z[:, c4:c5]
    mq = z[:, c5:] * ATTN_SCALE

    blk = WINDOW
    attn_blocks = []
    bias_inner = _swa_bias(0)
    for i in range(TQ // blk):
        if i == 0:
            pk, pv = ck_ref[...], cv_ref[...]
            bias = _swa_bias(jnp.where(j > 0, 0, blk))
        else:
            pk, pv = k[(i - 1) * blk:i * blk], v[(i - 1) * blk:i * blk]
            bias = bias_inner
        kk = jnp.concatenate([pk, k[i * blk:(i + 1) * blk]], axis=0)
        vv = jnp.concatenate([pv, v[i * blk:(i + 1) * blk]], axis=0)
        attn_blocks.append(_swa_block(q[i * blk:(i + 1) * blk], kk, vv, bias, sinks_ref))
    attn = jnp.concatenate(attn_blocks, axis=0)
    ck_ref[...] = k[TQ - blk:]
    cv_ref[...] = v[TQ - blk:]

    u = cc * cvv
    row = _iota(u.shape, 0)
    u1 = jnp.where(row == 0, cu_ref[SUBLANES - 1:SUBLANES, :], pltpu.roll(u, 1, axis=0))
    u2 = jnp.where(row == 0, cu_ref[SUBLANES - 2:SUBLANES - 1, :],
                   jnp.where(row == 1, cu_ref[SUBLANES - 1:SUBLANES, :], pltpu.roll(u, 2, axis=0)))
    conv_out = cb * (conv_w_ref[0:1, :] * u2 + conv_w_ref[1:2, :] * u1 + conv_w_ref[2:3, :] * u)
    cu_ref[...] = u[TQ - SUBLANES:]

    cross = _mem_attend_shared(mq, mkt_ref[0], mvb_ref[0])

    @pl.when(j == nj - 1)
    def _():
        lastk_ref[0] = k[TQ - blk:]
        lastv_ref[0] = v[TQ - blk:]
        convst_ref[0] = u[TQ - 2:]

    _mix_out(x, attn, conv_out, cross, g_mix_ref, w_out_ref, g_ffn_ref, w_r_ref, b_r_ref,
             h_ref, hn_ref, tope_ref, gates_ref)


def _mixer_p(x, sinks, g_attn, w_in, conv_w, g_mix, w_out, g_ffn, w_r, b_r, mkt, mvb):
    b, s, d = x.shape
    nj = s // TQ
    n = b * s
    cw = conv_w.shape[1]
    full = lambda a: pl.BlockSpec(a.shape, lambda bi, ji, *_: (0,) * a.ndim)
    tok = lambda w: pl.BlockSpec((TQ, w), lambda bi, ji, *_: (bi * nj + ji, 0))
    per_b = lambda r, c: pl.BlockSpec((1, r, c), lambda bi, ji, *_: (bi, 0, 0))
    grid_spec = pltpu.PrefetchScalarGridSpec(
        num_scalar_prefetch=1,
        grid=(b, nj),
        in_specs=[pl.BlockSpec((1, TQ, d), lambda bi, ji, *_: (bi, ji, 0)),
                  full(g_attn), full(w_in), full(conv_w), full(g_mix), full(w_out), full(g_ffn),
                  full(w_r), full(b_r), per_b(MEM_WIDTH, mkt.shape[2]), per_b(mvb.shape[1], MEM_WIDTH)],
        out_specs=[tok(d), pl.BlockSpec((TQ * SUBLANES, LANES), lambda bi, ji, *_: (bi * nj + ji, 0)),
                   tok(TOP_K), tok(TOP_K),
                   per_b(WINDOW, KV_WIDTH), per_b(WINDOW, KV_WIDTH), per_b(2, cw)],
        scratch_shapes=[pltpu.VMEM((WINDOW, KV_WIDTH), F32), pltpu.VMEM((WINDOW, KV_WIDTH), F32),
                        pltpu.VMEM((SUBLANES, cw), F32)],
    )
    return pl.pallas_call(
        _mixer_p_kernel,
        grid_spec=grid_spec,
        out_shape=[jax.ShapeDtypeStruct((n, d), F32), jax.ShapeDtypeStruct((n * SUBLANES, LANES), F32),
                   jax.ShapeDtypeStruct((n, TOP_K), jnp.int32), jax.ShapeDtypeStruct((n, TOP_K), F32),
                   jax.ShapeDtypeStruct((b, WINDOW, KV_WIDTH), F32),
                   jax.ShapeDtypeStruct((b, WINDOW, KV_WIDTH), F32),
                   jax.ShapeDtypeStruct((b, 2, cw), F32)],
        compiler_params=pltpu.CompilerParams(dimension_semantics=("arbitrary", "arbitrary"),
                                             vmem_limit_bytes=VMEM_LIMIT),
        name="mixer_p",
    )(sinks, x, g_attn, w_in, conv_w, g_mix, w_out, g_ffn, w_r, b_r, mkt, mvb)


def _per_head_column(values, hrow):
    col = jnp.zeros(hrow.shape, F32)
    for h in range(N_Q_HEADS):
        col = jnp.where(hrow == h, values[h], col)
    return col


def _mixer_s_kernel(sinks_ref, x_ref, pm1_ref, pm2_ref, wk_ref, wv_ref, mk_ref, mv_ref,
                    g_attn_ref, w_in_ref, conv_w_ref, g_mix_ref, w_out_ref, g_ffn_ref, w_r_ref, b_r_ref,
                    h_ref, hn_ref, tope_ref, gates_ref, nwk_ref, nwv_ref, u_ref, *, t_dec):
    r_tok = BB * t_dec
    r_exp = r_tok * REP
    qrows = t_dec * REP
    x = x_ref[...]
    xn = (_rms(x) * g_attn_ref[...]).astype(BF16)
    z = _dot(xn, w_in_ref[...])
    c0 = ATTN_WIDTH
    c1 = c0 + KV_WIDTH
    c2 = c1 + KV_WIDTH
    cw = conv_w_ref.shape[1]
    c3, c4, c5 = c2 + cw, c2 + 2 * cw, c2 + 3 * cw
    q = z[:, :c0] * ATTN_SCALE
    k_new = z[:, c0:c1]
    v_new = z[:, c1:c2]
    cb = z[:, c2:c3]
    cc = z[:, c3:c4]
    cvv = z[:, c4:c5]
    mq = z[:, c5:] * ATTN_SCALE
    win = wk_ref.shape[1]

    xi = _iota((KV_WIDTH, ATTN_WIDTH), 0)
    xl = _iota((KV_WIDTH, ATTN_WIDTH), 1)
    q_per_kv = N_Q_HEADS // N_KV_HEADS
    expand = (xi == (xl // (q_per_kv * HEAD_DIM)) * HEAD_DIM + xl % HEAD_DIM).astype(BF16)
    rr = _iota((r_exp, r_tok), 0)
    rc = _iota((r_exp, r_tok), 1)
    rep = (rr // REP == rc).astype(BF16)

    hrow = _iota((r_exp, 1), 0) % REP
    trow = (_iota((r_exp, 1), 0) // REP) % t_dec
    slope_col = _per_head_column(ALIBI_SLOPES, hrow)
    sink_col = _per_head_column([sinks_ref[h] for h in range(N_Q_HEADS)], hrow)

    qexp = jnp.where(hrow == _iota((r_exp, ATTN_WIDTH), 1) // HEAD_DIM, _dot(rep, q.astype(BF16)), 0.0)
    kexp = _dot(wk_ref[...].reshape(BB * win, KV_WIDTH).astype(BF16), expand).astype(BF16)
    vexp = _dot(wv_ref[...].reshape(BB * win, KV_WIDTH).astype(BF16), expand).astype(BF16)
    s = jnp.einsum("bqc,bkc->bqk", qexp.astype(BF16).reshape(BB, qrows, ATTN_WIDTH),
                   kexp.reshape(BB, win, ATTN_WIDTH), preferred_element_type=F32).reshape(r_exp, win)
    scol = _iota((r_exp, win), 1)
    s = s - slope_col * (win + trow - scol).astype(F32)
    s = jnp.where(scol > trow, s, -jnp.inf)
    knew_exp = _dot(k_new.astype(BF16), expand).astype(BF16)
    vnew_exp = _dot(v_new.astype(BF16), expand).astype(BF16)
    s_new, v_rep = [], []
    for jn in range(t_dec):
        rep_j = (rc == (rr // qrows) * t_dec + jn).astype(BF16)
        k_rep = _dot(rep_j, knew_exp)
        v_rep.append(_dot(rep_j, vnew_exp))
        sj = jnp.sum(qexp * k_rep, axis=-1, keepdims=True) - slope_col * (trow - jn).astype(F32)
        s_new.append(jnp.where(trow >= jn, sj, -jnp.inf))
    m = jnp.maximum(jnp.max(s, axis=-1, keepdims=True), sink_col)
    for sj in s_new:
        m = jnp.maximum(m, sj)
    pe = jnp.exp(s - m)
    denom = jnp.sum(pe, axis=-1, keepdims=True) + jnp.exp(sink_col - m)
    o = jnp.einsum("bqk,bkc->bqc", pe.astype(BF16).reshape(BB, qrows, win),
                   vexp.reshape(BB, win, ATTN_WIDTH), preferred_element_type=F32).reshape(r_exp, ATTN_WIDTH)
    for jn in range(t_dec):
        pj = jnp.exp(s_new[jn] - m)
        denom = denom + pj
        o = o + pj * v_rep[jn]
    o = jnp.where(hrow == _iota((r_exp, ATTN_WIDTH), 1) // HEAD_DIM, o / denom, 0.0)
    attn = jnp.sum(o.reshape(r_tok, REP, ATTN_WIDTH), axis=1)

    m_tok = mk_ref.shape[1]
    mhead = _iota((r_exp, MEM_WIDTH), 1) // HEAD_DIM
    mqexp = jnp.where(hrow == mhead, _dot(rep, mq.astype(BF16)), 0.0).astype(BF16)
    sm = jnp.einsum("bqc,bmc->bqm", mqexp.reshape(BB, qrows, MEM_WIDTH), mk_ref[...].astype(BF16),
                    preferred_element_type=F32).reshape(r_exp, m_tok)
    mm = jnp.max(sm, axis=-1, keepdims=True)
    pm = jnp.exp(sm - mm)
    dm = jnp.sum(pm, axis=-1, keepdims=True)
    om = jnp.einsum("bqm,bmc->bqc", pm.astype(BF16).reshape(BB, qrows, m_tok), mv_ref[...].astype(BF16),
                    preferred_element_type=F32).reshape(r_exp, MEM_WIDTH)
    om = jnp.where(hrow == mhead, om / dm, 0.0)
    cross = jnp.sum(om.reshape(r_tok, REP, MEM_WIDTH), axis=1)

    u = cc * cvv
    tt = _iota(u.shape, 0) % t_dec
    u1 = jnp.where(tt >= 1, pltpu.roll(u, 1, axis=0), pm1_ref[...])
    u2 = jnp.where(tt >= 2, pltpu.roll(u, 2, axis=0), pm2_ref[...])
    conv_out = cb * (conv_w_ref[0:1, :] * u2 + conv_w_ref[1:2, :] * u1 + conv_w_ref[2:3, :] * u)
    u_ref[...] = u

    nwk_ref[:, 0:win - t_dec, :] = wk_ref[:, t_dec:win, :]
    nwv_ref[:, 0:win - t_dec, :] = wv_ref[:, t_dec:win, :]
    for b in range(BB):
        nwk_ref[b, win - t_dec:win, :] = k_new[b * t_dec:(b + 1) * t_dec, :]
        nwv_ref[b, win - t_dec:win, :] = v_new[b * t_dec:(b + 1) * t_dec, :]

    _mix_out(x, attn, conv_out, cross, g_mix_ref, w_out_ref, g_ffn_ref, w_r_ref, b_r_ref,
             h_ref, hn_ref, tope_ref, gates_ref)


def _mixer_s(x2, t_dec, pm1, pm2, wk, wv, mk, mv, sinks, g_attn, w_in, conv_w, g_mix, w_out, g_ffn, w_r, b_r):
    n, d = x2.shape
    nb = wk.shape[0]
    win = wk.shape[1]
    m_tok = mk.shape[1]
    cw = conv_w.shape[1]
    r_tok = BB * t_dec
    full = lambda a: pl.BlockSpec(a.shape, lambda i, *_: (0,) * a.ndim)
    tok = lambda w: pl.BlockSpec((r_tok, w), lambda i, *_: (i, 0))
    per_b = lambda r, c: pl.BlockSpec((BB, r, c), lambda i, *_: (i, 0, 0))
    grid_spec = pltpu.PrefetchScalarGridSpec(
        num_scalar_prefetch=1,
        grid=(nb // BB,),
        in_specs=[tok(d), tok(cw), tok(cw), per_b(win, KV_WIDTH), per_b(win, KV_WIDTH),
                  per_b(m_tok, MEM_WIDTH), per_b(m_tok, MEM_WIDTH),
                  full(g_attn), full(w_in), full(conv_w), full(g_mix), full(w_out), full(g_ffn),
                  full(w_r), full(b_r)],
        out_specs=[tok(d), pl.BlockSpec((r_tok * SUBLANES, LANES), lambda i, *_: (i, 0)),
                   tok(TOP_K), tok(TOP_K), per_b(win, KV_WIDTH), per_b(win, KV_WIDTH), tok(cw)],
    )
    return pl.pallas_call(
        functools.partial(_mixer_s_kernel, t_dec=t_dec),
        grid_spec=grid_spec,
        out_shape=[jax.ShapeDtypeStruct((n, d), F32), jax.ShapeDtypeStruct((n * SUBLANES, LANES), F32),
                   jax.ShapeDtypeStruct((n, TOP_K), jnp.int32), jax.ShapeDtypeStruct((n, TOP_K), F32),
                   jax.ShapeDtypeStruct((nb, win, KV_WIDTH), F32), jax.ShapeDtypeStruct((nb, win, KV_WIDTH), F32),
                   jax.ShapeDtypeStruct((n, cw), F32)],
        compiler_params=pltpu.CompilerParams(dimension_semantics=("arbitrary",), vmem_limit_bytes=VMEM_LIMIT),
        name="mixer_s",
    )(sinks, x2, pm1, pm2, wk, wv, mk, mv, g_attn, w_in, conv_w, g_mix, w_out, g_ffn, w_r, b_r)


def _lpos_kernel(tope_ref, lpos_ref, counts_ref, tri_ref, *, n_e):
    i = pl.program_id(0)
    tm = tope_ref.shape[0]

    @pl.when(i == 0)
    def _():
        tri_ref[...] = (_iota((tm, tm), 0) > _iota((tm, tm), 1)).astype(BF16)

    te = tope_ref[...]
    col = _iota((tm, n_e), 1)
    hits = [te[:, k:k + 1] == col for k in range(TOP_K)]
    onehot = jnp.zeros((tm, n_e), F32)
    lower = jnp.zeros((tm, n_e), F32)
    for k in range(TOP_K):
        onehot = onehot + hits[k].astype(F32)
        lower = lower + (te[:, k:k + 1] < col).astype(F32)
    ahead = _dot(tri_ref[...], onehot.astype(BF16)) + jnp.sum(lower, axis=0, keepdims=True)
    col4 = _iota((tm, TOP_K), 1)
    pos = jnp.zeros((tm, TOP_K), F32)
    for k in range(TOP_K):
        pos = jnp.where(col4 == k, jnp.sum(jnp.where(hits[k], ahead, 0.0), axis=-1, keepdims=True), pos)
    half = (i % 2) * (tm * TOP_K)
    lpos_ref[...] = (pos.astype(jnp.int32) + half) * SUBLANES
    counts_ref[0] = jnp.sum(onehot, axis=0, keepdims=True)


def _lpos(tope, tm, n_e):
    n = tope.shape[0]
    return pl.pallas_call(
        functools.partial(_lpos_kernel, n_e=n_e),
        grid=(n // tm,),
        in_specs=[pl.BlockSpec((tm, TOP_K), lambda i: (i, 0))],
        out_specs=[pl.BlockSpec((tm, TOP_K), lambda i: (i, 0)), pl.BlockSpec((1, 1, n_e), lambda i: (i, 0, 0))],
        out_shape=[jax.ShapeDtypeStruct((n, TOP_K), jnp.int32), jax.ShapeDtypeStruct((n // tm, 1, n_e), F32)],
        scratch_shapes=[pltpu.VMEM((tm, tm), BF16)],
        compiler_params=pltpu.CompilerParams(dimension_semantics=("arbitrary",)),
        name="lpos",
    )(tope)


def _rows(ref, first_row, n_rows):
    return ref.at[pl.ds(pl.multiple_of(first_row * SUBLANES, SUBLANES), n_rows * SUBLANES)]


def _group_chunks(meta, g, n_e, max_rows, make_copy, wait):
    off_ref, cnt_ref, lst_ref = meta
    if wait:
        make_copy(0, 0, max_rows * TOP_K).wait()
        return

    def per_expert(e, c):
        off = off_ref[g * n_e + e]
        lst = lst_ref[g * n_e + e]
        _run_copies(cnt_ref[g * n_e + e], max_rows, lambda done, size: make_copy(lst + done, off + done, size),
                    wait=False)
        return c

    lax.fori_loop(0, n_e, per_expert, 0)


def _run_copies(n_rows, max_rows, make_copy, wait):
    done = 0
    for size in [1 << b for b in range(max_rows.bit_length() - 1, -1, -1)]:
        bit = n_rows & size

        @pl.when(bit != 0)
        def _():
            cp = make_copy(done, size)
            cp.wait() if wait else cp.start()
        done = done + bit


def _dispatch_kernel(off_ref, cnt_ref, lst_ref, zrow_ref, zcnt_ref, nused_ref, lpos_ref, hn_a_ref, hn_b_ref, xs_ref,
                     stage_ref, zbuf_ref, sem, zsem, *, n_e, n_blocks, groups_a):
    j = pl.program_id(0)
    nj = pl.num_programs(0)
    tm = hn_a_ref.shape[0] // SUBLANES
    slot = j % 2
    meta = (off_ref, cnt_ref, lst_ref)

    def chunks(g, s, wait):
        _group_chunks(meta, g, n_e, tm,
                      lambda lrow, grow, size: pltpu.make_async_copy(
                          _rows(stage_ref, s * (tm * TOP_K) + lrow, size), _rows(xs_ref, grow, size),
                          sem.at[s]), wait)

    def zero_fill(wait):
        def zero_run(first_row, n_rows):
            _run_copies(n_rows, BM, lambda done, size: pltpu.make_async_copy(
                _rows(zbuf_ref, 0, size), _rows(xs_ref, first_row + done, size), zsem), wait)

        def expert_pad(e, c):
            zero_run(zrow_ref[e], zcnt_ref[e])
            return c

        def tail_block(b, c):
            zero_run(b * BM, jnp.int32(BM))
            return c

        lax.fori_loop(0, n_e, expert_pad, 0)
        lax.fori_loop(nused_ref[0], n_blocks, tail_block, 0)

    @pl.when(j == 0)
    def _():
        zbuf_ref[...] = jnp.zeros_like(zbuf_ref)
        zero_fill(False)

    @pl.when(j >= 2)
    def _():
        chunks(j - 2, slot, True)

    def place_from(hn_ref):
        def place(t, c):
            tile = hn_ref[pl.ds(pl.multiple_of(t * SUBLANES, SUBLANES), SUBLANES), :]
            for k in range(TOP_K):
                pos = pl.multiple_of(lpos_ref[0, 0, t * TOP_K + k], SUBLANES)
                stage_ref[pl.ds(pos, SUBLANES), :] = tile
            return c
        lax.fori_loop(0, tm, place, 0, unroll=8)

    @pl.when(j < groups_a)
    def _():
        place_from(hn_a_ref)

    @pl.when(j >= groups_a)
    def _():
        place_from(hn_b_ref)

    chunks(j, slot, False)

    @pl.when(j == nj - 1)
    def _():
        @pl.when(j >= 1)
        def _():
            chunks(j - 1, 1 - slot, True)
        chunks(j, slot, True)
        zero_fill(True)


def _dispatch(hn_a, hn_b, lpos, meta, zrow, zcnt, nused, tm, n_e, n_blocks):
    nt = lpos.shape[0] // tm
    groups_a = hn_a.shape[0] // (tm * SUBLANES)
    assert groups_a >= 1 and groups_a + hn_b.shape[0] // (tm * SUBLANES) == nt
    grid_spec = pltpu.PrefetchScalarGridSpec(
        num_scalar_prefetch=6,
        grid=(nt,),
        in_specs=[pl.BlockSpec((1, 1, tm * TOP_K), lambda j, *_: (j, 0, 0), memory_space=pltpu.SMEM),
                  pl.BlockSpec((tm * SUBLANES, LANES), lambda j, *_: (jnp.minimum(j, groups_a - 1), 0)),
                  pl.BlockSpec((tm * SUBLANES, LANES), lambda j, *_: (jnp.maximum(j - groups_a, 0), 0))],
        out_specs=pl.BlockSpec(memory_space=pl.ANY),
        scratch_shapes=[pltpu.VMEM((2 * tm * TOP_K * SUBLANES, LANES), F32), pltpu.VMEM((BM * SUBLANES, LANES), F32),
                        pltpu.SemaphoreType.DMA((2,)), pltpu.SemaphoreType.DMA(())],
    )
    return pl.pallas_call(
        functools.partial(_dispatch_kernel, n_e=n_e, n_blocks=n_blocks, groups_a=groups_a),
        grid_spec=grid_spec,
        out_shape=jax.ShapeDtypeStruct((n_blocks * BM * SUBLANES, LANES), F32),
        compiler_params=pltpu.CompilerParams(dimension_semantics=("arbitrary",), vmem_limit_bytes=VMEM_LIMIT),
        name="dispatch",
    )(*meta, zrow, zcnt, nused, lpos.reshape(nt, 1, tm * TOP_K), hn_a, hn_b)


def _experts_kernel(blk_e_ref, nused_ref, next_e_ref, xs_ref, bgu_ref, bd_ref, wgu_hbm_ref, wd_hbm_ref, ys_ref,
                    wgu_ref, wd_ref, wgu_bf_ref, wd_bf_ref, wsem):
    i = pl.program_id(0)
    nused = nused_ref[0]
    d, d_ff2 = wgu_ref.shape
    grp = 2 * LANES

    def weight_copies(e):
        gu_cols = d_ff2 // W_PARTS
        dn_rows = (d_ff2 // 2) // W_PARTS
        cps = []
        for p in range(W_PARTS):
            cps.append(pltpu.make_async_copy(wgu_hbm_ref.at[e, :, pl.ds(p * gu_cols, gu_cols)],
                                             wgu_ref.at[:, pl.ds(p * gu_cols, gu_cols)], wsem.at[p]))
            cps.append(pltpu.make_async_copy(wd_hbm_ref.at[e, pl.ds(p * dn_rows, dn_rows), :],
                                             wd_ref.at[pl.ds(p * dn_rows, dn_rows), :], wsem.at[W_PARTS + p]))
        return cps

    @pl.when(i < nused)
    def _():
        e = blk_e_ref[i]
        e_prev = blk_e_ref[jnp.maximum(i - 1, 0)]

        @pl.when(i == 0)
        def _():
            for cp in weight_copies(e):
                cp.start()

        @pl.when((i == 0) | (e != e_prev))
        def _():
            for cp in weight_copies(e):
                cp.wait()
            pr = _iota((grp, grp), 0)
            pc = _iota((grp, grp), 1)
            perm = (pr == jnp.where(pc < LANES, 2 * pc, 2 * (pc - LANES) + 1)).astype(BF16)
            for g in range(d_ff2 // grp):
                w = wgu_ref[:, g * grp:(g + 1) * grp].astype(BF16)
                wgu_bf_ref[:, g * grp:(g + 1) * grp] = _dot(w, perm).astype(BF16)
            wd_bf_ref[...] = wd_ref[...].astype(BF16)

            @pl.when(next_e_ref[e] >= 0)
            def _():
                for cp in weight_copies(next_e_ref[e]):
                    cp.start()

        x = jnp.concatenate([_load_token_tiles(xs_ref, BM, s).astype(BF16) for s in range(SUBLANES)], axis=1)
        hgu = _dot(x, wgu_bf_ref[...]) + bgu_ref[0]
        acts = []
        for g in range(d_ff2 // grp):
            gate = jnp.minimum(hgu[:, g * grp:g * grp + LANES], SWIGLU_LIMIT)
            up = jnp.clip(hgu[:, g * grp + LANES:(g + 1) * grp], -SWIGLU_LIMIT, SWIGLU_LIMIT)
            glu = gate * (1.0 / (1.0 + jnp.exp(-SWIGLU_ALPHA * gate)))
            acts.append(((up + 1.0) * glu).astype(BF16))
        act = jnp.concatenate(acts, axis=1)
        _store_token_tiles(ys_ref, _dot(act, wd_bf_ref[...]) + bd_ref[0])

    @pl.when(i >= nused)
    def _():
        ys_ref[...] = jnp.zeros_like(ys_ref)


def _experts(xs, blk_e, nused, next_e, w_gate_up, b_gu_perm, w_down, b_down):
    n_e, d, d_ff2 = w_gate_up.shape
    n_blocks = xs.shape[0] // (BM * SUBLANES)
    expert = lambda i, be, nu, ne: be[jnp.minimum(i, jnp.maximum(nu[0] - 1, 0))]
    rows_spec = pl.BlockSpec((BM * SUBLANES, LANES), lambda i, *_: (i, 0))
    grid_spec = pltpu.PrefetchScalarGridSpec(
        num_scalar_prefetch=3,
        grid=(n_blocks,),
        in_specs=[pl.BlockSpec((BM * SUBLANES, LANES), lambda i, be, nu, ne: (jnp.minimum(i, jnp.maximum(nu[0] - 1, 0)), 0)),
                  pl.BlockSpec((1, 1, d_ff2), lambda i, *s: (expert(i, *s), 0, 0)),
                  pl.BlockSpec((1, 1, d), lambda i, *s: (expert(i, *s), 0, 0)),
                  pl.BlockSpec(memory_space=pl.ANY), pl.BlockSpec(memory_space=pl.ANY)],
        out_specs=rows_spec,
        scratch_shapes=[pltpu.VMEM((d, d_ff2), F32), pltpu.VMEM((d_ff2 // 2, d), F32),
                        pltpu.VMEM((d, d_ff2), BF16), pltpu.VMEM((d_ff2 // 2, d), BF16),
                        pltpu.SemaphoreType.DMA((2 * W_PARTS,))],
    )
    return pl.pallas_call(
        _experts_kernel,
        grid_spec=grid_spec,
        out_shape=jax.ShapeDtypeStruct(xs.shape, F32),
        compiler_params=pltpu.CompilerParams(dimension_semantics=("arbitrary",), vmem_limit_bytes=VMEM_LIMIT),
        name="experts",
    )(blk_e, nused, next_e, xs, b_gu_perm, b_down, w_gate_up, w_down)


def _combine_kernel(off_ref, cnt_ref, lst_ref, lpos_ref, gates_ref, h_ref, g_ref, ys_ref, out_ref,
                    stage_ref, acc_ref, sem, *, n_e, g0):
    i = pl.program_id(0)
    ng = pl.num_programs(0)
    tm, d = h_ref.shape
    slot = (g0 + i) % 2
    meta = (off_ref, cnt_ref, lst_ref)

    def chunks(g, s, wait):
        _group_chunks(meta, g, n_e, tm,
                      lambda lrow, grow, size: pltpu.make_async_copy(
                          _rows(ys_ref, grow, size), _rows(stage_ref, s * (tm * TOP_K) + lrow, size),
                          sem.at[s]), wait)

    @pl.when(i == 0)
    def _():
        chunks(g0, slot, False)

    @pl.when(i + 1 < ng)
    def _():
        chunks(g0 + i + 1, 1 - slot, False)

    chunks(g0 + i, slot, True)

    def mix(t, c):
        acc = None
        for k in range(TOP_K):
            pos = pl.multiple_of(lpos_ref[0, 0, t * TOP_K + k], SUBLANES)
            term = stage_ref[pl.ds(pos, SUBLANES), :] * gates_ref[0, 0, t * TOP_K + k]
            acc = term if acc is None else acc + term
        acc_ref[pl.ds(pl.multiple_of(t * SUBLANES, SUBLANES), SUBLANES), :] = acc
        return c

    lax.fori_loop(0, tm, mix, 0, unroll=8)

    h = h_ref[...]
    parts = []
    sq = jnp.zeros((tm, LANES), F32)
    for s in range(d // LANES):
        y = h[:, s * LANES:(s + 1) * LANES] + _load_token_tiles(acc_ref, tm, s)
        sq = sq + y * y
        parts.append(y)
    rinv = lax.rsqrt(jnp.sum(sq, axis=-1, keepdims=True) / d + EPS)
    out_ref[...] = jnp.concatenate(parts, axis=1) * rinv * g_ref[...]


def _combine(ys, h, lpos, gates, meta, g_final, tm, n_e, g0):
    n, d = h.shape
    grid_spec = pltpu.PrefetchScalarGridSpec(
        num_scalar_prefetch=3,
        grid=(n // tm,),
        in_specs=[pl.BlockSpec((1, 1, tm * TOP_K), lambda i, *_: (g0 + i, 0, 0), memory_space=pltpu.SMEM),
                  pl.BlockSpec((1, 1, tm * TOP_K), lambda i, *_: (g0 + i, 0, 0), memory_space=pltpu.SMEM),
                  pl.BlockSpec((tm, d), lambda i, *_: (i, 0)),
                  pl.BlockSpec((1, d), lambda i, *_: (0, 0)),
                  pl.BlockSpec(memory_space=pl.ANY)],
        out_specs=pl.BlockSpec((tm, d), lambda i, *_: (i, 0)),
        scratch_shapes=[pltpu.VMEM((2 * tm * TOP_K * SUBLANES, LANES), F32), pltpu.VMEM((tm * SUBLANES, LANES), F32),
                        pltpu.SemaphoreType.DMA((2,))],
    )
    ngroups = lpos.shape[0] // tm
    return pl.pallas_call(
        functools.partial(_combine_kernel, n_e=n_e, g0=g0),
        grid_spec=grid_spec,
        out_shape=jax.ShapeDtypeStruct((n, d), F32),
        compiler_params=pltpu.CompilerParams(dimension_semantics=("arbitrary",), vmem_limit_bytes=VMEM_LIMIT),
        name="combine",
    )(*meta, lpos.reshape(ngroups, 1, tm * TOP_K), gates.reshape(ngroups, 1, tm * TOP_K), h, g_final, ys)


def _largest_tile(cands, *sizes):
    for c in cands:
        if all(s % c == 0 for s in sizes):
            return c
    raise ValueError(f"no tile in {cands} divides {sizes}")


def kernel(x_prompt, x_sample, mem_prompt, cache_win_k, cache_win_v, state_conv, cache_mem_k, cache_mem_v, g_attn_norm, w_in, conv_w, attn_sinks, g_mem_norm, w_mem_kv, g_mix_out, w_out, g_ffn_norm, w_router, b_router, w_gate_up, b_gate_up, w_down, b_down, g_final):
    depth = w_in.shape[0]
    assert depth == 1, "single-layer step"
    b, s, d = x_prompt.shape
    nb, t_dec, _ = x_sample.shape
    n_e = w_router.shape[2]
    d_ff2 = w_gate_up.shape[3]
    cw = conv_w.shape[2]
    win = cache_win_k.shape[2]
    m_tok = cache_mem_k.shape[2]
    assert s % TQ == 0 and nb % BB == 0 and win == WINDOW and t_dec <= SUBLANES and d_ff2 % (2 * LANES) == 0
    assert d == SUBLANES * LANES, "token-tile layout: one token is one (8, 128) f32 tile"

    row = lambda a: a.reshape(1, -1)
    w_in_bf = w_in[0].astype(BF16)
    w_out_bf = w_out[0].astype(BF16)
    w_r_bf = w_router[0].astype(BF16)
    sinks = attn_sinks[0].astype(F32)
    shared = (row(g_attn_norm[0]), w_in_bf, conv_w[0], row(g_mix_out[0]), w_out_bf, row(g_ffn_norm[0]),
              w_r_bf, row(b_router[0]))

    mk_p, mv_p, mkt, mvb = _memkv(mem_prompt, row(g_mem_norm[0]), w_mem_kv[0].astype(BF16))
    h_p, hn_p, tope_p, gates_p, lastk, lastv, convst = _mixer_p(x_prompt, sinks, *shared, mkt, mvb)

    zeros = lambda r: jnp.zeros((nb, r, cw), F32)
    st = state_conv[0]
    pm1 = jnp.concatenate([st[:, 1:2], zeros(t_dec - 1)], axis=1).reshape(nb * t_dec, cw)
    pm2 = jnp.concatenate([st, zeros(t_dec - 2)], axis=1).reshape(nb * t_dec, cw)
    h_s, hn_s, tope_s, gates_s, nwk, nwv, u_s = _mixer_s(
        x_sample.reshape(nb * t_dec, d), t_dec, pm1, pm2,
        cache_win_k[0].reshape(nb, win, KV_WIDTH), cache_win_v[0].reshape(nb, win, KV_WIDTH),
        cache_mem_k[0].reshape(nb, m_tok, MEM_WIDTH), cache_mem_v[0].reshape(nb, m_tok, MEM_WIDTH),
        sinks, *shared)

    n_p, n_s = b * s, nb * t_dec
    n = n_p + n_s
    tm = _largest_tile((512, 256, 128, 64, 32, 16, 8), n_p, n_s)
    tope = jnp.concatenate([tope_p, tope_s], axis=0)
    gates = jnp.concatenate([gates_p, gates_s], axis=0)

    lpos, cnt_f = _lpos(tope, tm, n_e)
    cnt = cnt_f[:, 0, :].astype(jnp.int32)
    counts = jnp.sum(cnt, axis=0)
    padded = (counts + BM - 1) // BM * BM
    pad_ends = jnp.cumsum(padded)
    pad_starts = pad_ends - padded
    nk = n * TOP_K
    n_blocks = -(-nk // BM) + n_e
    nused = (pad_ends[-1:] // BM).astype(jnp.int32)
    blk_start = jnp.arange(n_blocks, dtype=jnp.int32) * BM
    blk_e = jnp.minimum(jnp.sum((pad_ends[None, :] <= blk_start[:, None]).astype(jnp.int32), axis=1), n_e - 1)
    zrow = (pad_starts + counts).astype(jnp.int32)
    zcnt = (padded - counts).astype(jnp.int32)
    off = pad_starts[None, :] + jnp.cumsum(cnt, axis=0) - cnt
    lstart = jnp.cumsum(cnt, axis=1) - cnt
    meta = (off.reshape(-1).astype(jnp.int32), cnt.reshape(-1), lstart.reshape(-1).astype(jnp.int32))

    xs = _dispatch(hn_p, hn_s, lpos, meta, zrow, zcnt, nused, tm, n_e, n_blocks)

    grp = 2 * LANES
    b_gu = b_gate_up[0].reshape(n_e, d_ff2 // grp, LANES, 2).transpose(0, 1, 3, 2).reshape(n_e, 1, d_ff2)
    owner = jnp.where(padded > 0, jnp.arange(n_e, dtype=jnp.int32), n_e)
    following = jnp.concatenate([lax.cummin(owner, reverse=True)[1:], jnp.full((1,), n_e, jnp.int32)])
    next_e = jnp.where(following < n_e, following, -1).astype(jnp.int32)
    ys = _experts(xs, blk_e, nused, next_e, w_gate_up[0], b_gu, w_down[0], b_down[0].reshape(n_e, 1, d))

    g_fin = row(g_final)
    y_p = _combine(ys, h_p, lpos, gates, meta, g_fin, tm, n_e, 0)
    y_s = _combine(ys, h_s, lpos, gates, meta, g_fin, tm, n_e, n_p // tm)

    kv5 = lambda a, bsz, r, hds: a.reshape(1, bsz, r, hds, HEAD_DIM)
    return (y_p.reshape(b, s, d), y_s.reshape(nb, t_dec, d),
            kv5(lastk, b, WINDOW, N_KV_HEADS), kv5(lastv, b, WINDOW, N_KV_HEADS),
            convst.reshape(1, b, 2, cw),
            kv5(mk_p, b, m_tok, N_MEM_HEADS), kv5(mv_p, b, m_tok, N_MEM_HEADS),
            kv5(nwk, nb, win, N_KV_HEADS), kv5(nwv, nb, win, N_KV_HEADS),
            u_s.reshape(nb, t_dec, cw)[:, t_dec - 2:].reshape(1, nb, 2, cw))
```

```python
import functools

import jax
import jax.numpy as jnp
from jax import lax
from jax.experimental import pallas as pl
from jax.experimental.pallas import tpu as pltpu

F32 = jnp.float32
BF16 = jnp.bfloat16

HEAD_DIM = 64
N_Q_HEADS = 8
N_KV_HEADS = 2
WINDOW = 128
ATTN_WIDTH = N_Q_HEADS * HEAD_DIM
KV_WIDTH = N_KV_HEADS * HEAD_DIM
N_MEM_HEADS = 4
MEM_WIDTH = N_MEM_HEADS * HEAD_DIM
TOP_K = 4
SWIGLU_LIMIT = 7.0
SWIGLU_ALPHA = 1.702
EPS = 1e-5
ATTN_SCALE = HEAD_DIM ** -0.5
ALIBI_SLOPES = tuple(2.0 ** (-8.0 * (h + 1) / N_Q_HEADS) for h in range(N_Q_HEADS))

LANES = 128
SUBLANES = 8
VMEM_LIMIT = 56 * 1024 * 1024

TQ = 512
BB = 16
BM = 512
REP = 8
W_PARTS = 4
MIX_CHUNK = 256
IN_CHUNK = 256
TAIL_CHUNK = 128


def _rms(x):
    return x * lax.rsqrt(jnp.mean(x * x, axis=-1, keepdims=True) + EPS)


def _dot(a, b):
    return jnp.dot(a, b, preferred_element_type=F32)


def _dot_nt(a, b):
    return lax.dot_general(a, b, (((1,), (1,)), ((), ())), preferred_element_type=F32)


def _iota(shape, axis):
    return lax.broadcasted_iota(jnp.int32, shape, axis)


def _store_token_tiles(ref, x, base=0):
    t = x.shape[0]
    for s in range(x.shape[1] // LANES):
        ref[pl.ds(base + s, t, stride=SUBLANES), :] = x[:, s * LANES:(s + 1) * LANES]


def _load_token_tiles(ref, t, s, base=0):
    return ref[pl.ds(base + s, t, stride=SUBLANES), :]


def _memkv_kernel(mem_ref, g_ref, w_ref, mk_ref, mv_ref, mkt_ref, mvb_ref):
    xn = (_rms(mem_ref[0]) * g_ref[...]).astype(BF16)
    kv = _dot(xn, w_ref[...])
    mk = kv[:, :MEM_WIDTH]
    mv = kv[:, MEM_WIDTH:]
    mk_ref[0] = mk
    mv_ref[0] = mv
    mkt_ref[0] = mk.T.astype(BF16)
    mvb_ref[0] = mv.astype(BF16)


def _memkv(mem, g, w_bf):
    b, m, d = mem.shape
    out_f = jax.ShapeDtypeStruct((b, m, MEM_WIDTH), F32)
    out_b = jax.ShapeDtypeStruct((b, m, MEM_WIDTH), BF16)
    out_t = jax.ShapeDtypeStruct((b, MEM_WIDTH, m), BF16)
    blk = lambda r, c: pl.BlockSpec((1, r, c), lambda i: (i, 0, 0))
    return pl.pallas_call(
        _memkv_kernel,
        grid=(b,),
        in_specs=[blk(m, d), pl.BlockSpec((1, d), lambda i: (0, 0)),
                  pl.BlockSpec((d, 2 * MEM_WIDTH), lambda i: (0, 0))],
        out_specs=[blk(m, MEM_WIDTH), blk(m, MEM_WIDTH), blk(MEM_WIDTH, m), blk(m, MEM_WIDTH)],
        out_shape=[out_f, out_f, out_t, out_b],
        name="memkv",
    )(mem, g, w_bf)


def _router_topk(hn, w_r_ref, b_r_ref, tope_ref, gates_ref, rows_at):
    n_e = w_r_ref.shape[1]
    logits = _dot(hn.astype(BF16), w_r_ref[...]) + b_r_ref[...]
    rows = logits.shape[0]
    col = _iota((rows, n_e), 1).astype(F32)
    vals, idxs = [], []
    cur = logits
    for _ in range(TOP_K):
        m = jnp.max(cur, axis=-1, keepdims=True)
        idx = jnp.min(jnp.where(cur == m, col, float(n_e)), axis=-1, keepdims=True)
        vals.append(m)
        idxs.append(idx)
        cur = jnp.where(col == idx, -jnp.inf, cur)
    exps = [jnp.exp(v - vals[0]) for v in vals]
    tot = exps[0] + exps[1] + exps[2] + exps[3]
    col4 = _iota((rows, TOP_K), 1)
    te = jnp.zeros((rows, TOP_K), F32)
    ga = jnp.zeros((rows, TOP_K), F32)
    for k in range(TOP_K):
        te = jnp.where(col4 == k, idxs[k], te)
        ga = jnp.where(col4 == k, exps[k] / tot, ga)
    tope_ref[rows_at, :] = te.astype(jnp.int32)
    gates_ref[rows_at, :] = ga


def _mix_out(x, attn, conv_out, cross, g_mix_ref, w_out_ref, g_ffn_ref, w_r_ref, b_r_ref,
             h_ref, hn_ref, tope_ref, gates_ref):
    rows = x.shape[0]
    chunk = MIX_CHUNK if rows % MIX_CHUNK == 0 else rows
    for r0 in range(0, rows, chunk):
        at = slice(r0, r0 + chunk)
        mix = jnp.concatenate([_rms(attn[at]), _rms(conv_out[at]), _rms(cross[at])], axis=-1) * g_mix_ref[...]
        h = x[at] + _dot(mix.astype(BF16), w_out_ref[...])
        hn = _rms(h) * g_ffn_ref[...]
        h_ref[at, :] = h
        _store_token_tiles(hn_ref, hn, base=r0 * SUBLANES)
        _router_topk(hn, w_r_ref, b_r_ref, tope_ref, gates_ref, at)


def _swa_bias(prev_lim):
    blk = WINDOW
    qi = _iota((blk, 2 * blk), 0)
    kj = _iota((blk, 2 * blk), 1)
    dist = blk + qi - kj
    mask = (dist >= 0) & (dist < WINDOW) & (kj >= prev_lim)
    distf = dist.astype(F32)
    return [jnp.where(mask, -ALIBI_SLOPES[h] * distf, -jnp.inf) for h in range(N_Q_HEADS)]


def _swa_block(q_blk, kk, vv, bias, sinks_ref):
    blk = WINDOW
    lane = _iota((2 * blk, KV_WIDTH), 1)
    lo = lane < HEAD_DIM
    kk_r = pltpu.roll(kk, HEAD_DIM, axis=1)
    vv_r = pltpu.roll(vv, HEAD_DIM, axis=1)
    kdup = [jnp.where(lo, kk, kk_r).astype(BF16), jnp.where(lo, kk_r, kk).astype(BF16)]
    vlo = [jnp.where(lo, vv, 0.0).astype(BF16), jnp.where(lo, vv_r, 0.0).astype(BF16)]
    vhi = [jnp.where(lo, 0.0, vv_r).astype(BF16), jnp.where(lo, 0.0, vv).astype(BF16)]
    qlo = _iota((blk, 2 * HEAD_DIM), 1) < HEAD_DIM
    outs = []
    for p in range(N_Q_HEADS // 2):
        kh = (2 * p) // (N_Q_HEADS // N_KV_HEADS)
        qp = q_blk[:, p * 2 * HEAD_DIM:(p + 1) * 2 * HEAD_DIM]
        acc = None
        for e in range(2):
            h = 2 * p + e
            qm = jnp.where(qlo if e == 0 else jnp.logical_not(qlo), qp, 0.0).astype(BF16)
            s = _dot_nt(qm, kdup[kh]) + bias[h]
            sink = sinks_ref[h]
            m = jnp.maximum(jnp.max(s, axis=-1, keepdims=True), sink)
            pe = jnp.exp(s - m)
            denom = jnp.sum(pe, axis=-1, keepdims=True) + jnp.exp(sink - m)
            o = _dot(pe.astype(BF16), (vlo if e == 0 else vhi)[kh]) / denom
            acc = o if acc is None else acc + o
        outs.append(acc)
    return jnp.concatenate(outs, axis=1)


def _mem_attend_shared(mq, mkt, mvb):
    t = mq.shape[0]
    m_tok = mvb.shape[0]
    qhead = _iota((t, MEM_WIDTH), 1) // HEAD_DIM
    vhead = _iota((m_tok, MEM_WIDTH), 1) // HEAD_DIM
    cross = None
    for h in range(N_MEM_HEADS):
        qm = jnp.where(qhead == h, mq, 0.0).astype(BF16)
        s = _dot(qm, mkt)
        m = jnp.max(s, axis=-1, keepdims=True)
        pe = jnp.exp(s - m)
        denom = jnp.sum(pe, axis=-1, keepdims=True)
        vm = jnp.where(vhead == h, mvb, jnp.zeros_like(mvb))
        o = _dot(pe.astype(BF16), vm) / denom
        cross = o if cross is None else cross + o
    return cross


def _mixer_p_kernel(sinks_ref, x_ref, g_attn_ref, w_in_ref, conv_w_ref, g_mix_ref, w_out_ref, g_ffn_ref,
                    w_r_ref, b_r_ref, mkt_ref, mvb_ref,
                    h_ref, hn_ref, tope_ref, gates_ref, lastk_ref, lastv_ref, convst_ref,
                    ck_ref, cv_ref, cu_ref):
    j = pl.program_id(1)
    nj = pl.num_programs(1)

    @pl.when(j == 0)
    def _():
        ck_ref[...] = jnp.zeros_like(ck_ref)
        cv_ref[...] = jnp.zeros_like(cv_ref)
        cu_ref[...] = jnp.zeros_like(cu_ref)

    x = x_ref[0]
    z = jnp.concatenate([_dot((_rms(x[r0:r0 + IN_CHUNK]) * g_attn_ref[...]).astype(BF16), w_in_ref[...])
                         for r0 in range(0, TQ, IN_CHUNK)], axis=0)
    c0 = ATTN_WIDTH
    c1 = c0 + KV_WIDTH
    c2 = c1 + KV_WIDTH
    cw = conv_w_ref.shape[1]
    c3, c4, c5 = c2 + cw, c2 + 2 * cw, c2 + 3 * cw
    q = z[:, :c0] * ATTN_SCALE
    k = z[:, c0:c1]
    v = z[:, c1:c2]
    cb = z[:, c2:c3]
    cc = z[:, c3:c4]
    cvv = z[:, c4:c5]
    mq = z[:, c5:] * ATTN_SCALE

    blk = WINDOW
    attn_blocks = []
    bias_inner = _swa_bias(0)
    for i in range(TQ // blk):
        if i == 0:
            pk, pv = ck_ref[...], cv_ref[...]
            bias = _swa_bias(jnp.where(j > 0, 0, blk))
        else:
            pk, pv = k[(i - 1) * blk:i * blk], v[(i - 1) * blk:i * blk]
            bias = bias_inner
        kk = jnp.concatenate([pk, k[i * blk:(i + 1) * blk]], axis=0)
        vv = jnp.concatenate([pv, v[i * blk:(i + 1) * blk]], axis=0)
        attn_blocks.append(_swa_block(q[i * blk:(i + 1) * blk], kk, vv, bias, sinks_ref))
    attn = jnp.concatenate(attn_blocks, axis=0)
    ck_ref[...] = k[TQ - blk:]
    cv_ref[...] = v[TQ - blk:]

    u = cc * cvv
    row = _iota(u.shape, 0)
    u1 = jnp.where(row == 0, cu_ref[SUBLANES - 1:SUBLANES, :], pltpu.roll(u, 1, axis=0))
    u2 = jnp.where(row == 0, cu_ref[SUBLANES - 2:SUBLANES - 1, :],
                   jnp.where(row == 1, cu_ref[SUBLANES - 1:SUBLANES, :], pltpu.roll(u, 2, axis=0)))
    conv_out = cb * (conv_w_ref[0:1, :] * u2 + conv_w_ref[1:2, :] * u1 + conv_w_ref[2:3, :] * u)
    cu_ref[...] = u[TQ - SUBLANES:]

    cross = _mem_attend_shared(mq, mkt_ref[0], mvb_ref[0])

    @pl.when(j == nj - 1)
    def _():
        lastk_ref[0] = k[TQ - blk:]
        lastv_ref[0] = v[TQ - blk:]
        convst_ref[0] = u[TQ - 2:]

    _mix_out(x, attn, conv_out, cross, g_mix_ref, w_out_ref, g_ffn_ref, w_r_ref, b_r_ref,
             h_ref, hn_ref, tope_ref, gates_ref)


def _mixer_p(x, sinks, g_attn, w_in, conv_w, g_mix, w_out, g_ffn, w_r, b_r, mkt, mvb):
    b, s, d = x.shape
    nj = s // TQ
    n = b * s
    cw = conv_w.shape[1]
    full = lambda a: pl.BlockSpec(a.shape, lambda bi, ji, *_: (0,) * a.ndim)
    tok = lambda w: pl.BlockSpec((TQ, w), lambda bi, ji, *_: (bi * nj + ji, 0))
    per_b = lambda r, c: pl.BlockSpec((1, r, c), lambda bi, ji, *_: (bi, 0, 0))
    grid_spec = pltpu.PrefetchScalarGridSpec(
        num_scalar_prefetch=1,
        grid=(b, nj),
        in_specs=[pl.BlockSpec((1, TQ, d), lambda bi, ji, *_: (bi, ji, 0)),
                  full(g_attn), full(w_in), full(conv_w), full(g_mix), full(w_out), full(g_ffn),
                  full(w_r), full(b_r), per_b(MEM_WIDTH, mkt.shape[2]), per_b(mvb.shape[1], MEM_WIDTH)],
        out_specs=[tok(d), pl.BlockSpec((TQ * SUBLANES, LANES), lambda bi, ji, *_: (bi * nj + ji, 0)),
                   tok(TOP_K), tok(TOP_K),
                   per_b(WINDOW, KV_WIDTH), per_b(WINDOW, KV_WIDTH), per_b(2, cw)],
        scratch_shapes=[pltpu.VMEM((WINDOW, KV_WIDTH), F32), pltpu.VMEM((WINDOW, KV_WIDTH), F32),
                        pltpu.VMEM((SUBLANES, cw), F32)],
    )
    return pl.pallas_call(
        _mixer_p_kernel,
        grid_spec=grid_spec,
        out_shape=[jax.ShapeDtypeStruct((n, d), F32), jax.ShapeDtypeStruct((n * SUBLANES, LANES), F32),
                   jax.ShapeDtypeStruct((n, TOP_K), jnp.int32), jax.ShapeDtypeStruct((n, TOP_K), F32),
                   jax.ShapeDtypeStruct((b, WINDOW, KV_WIDTH), F32),
                   jax.ShapeDtypeStruct((b, WINDOW, KV_WIDTH), F32),
                   jax.ShapeDtypeStruct((b, 2, cw), F32)],
        compiler_params=pltpu.CompilerParams(dimension_semantics=("arbitrary", "arbitrary"),
                                             vmem_limit_bytes=VMEM_LIMIT),
        name="mixer_p",
    )(sinks, x, g_attn, w_in, conv_w, g_mix, w_out, g_ffn, w_r, b_r, mkt, mvb)


def _per_head_column(values, hrow):
    col = jnp.zeros(hrow.shape, F32)
    for h in range(N_Q_HEADS):
        col = jnp.where(hrow == h, values[h], col)
    return col


def _mixer_s_kernel(sinks_ref, x_ref, pm1_ref, pm2_ref, wk_ref, wv_ref, mk_ref, mv_ref,
                    g_attn_ref, w_in_ref, conv_w_ref, g_mix_ref, w_out_ref, g_ffn_ref, w_r_ref, b_r_ref,
                    h_ref, hn_ref, tope_ref, gates_ref, nwk_ref, nwv_ref, u_ref, *, t_dec):
    r_tok = BB * t_dec
    r_exp = r_tok * REP
    qrows = t_dec * REP
    x = x_ref[...]
    xn = (_rms(x) * g_attn_ref[...]).astype(BF16)
    z = _dot(xn, w_in_ref[...])
    c0 = ATTN_WIDTH
    c1 = c0 + KV_WIDTH
    c2 = c1 + KV_WIDTH
    cw = conv_w_ref.shape[1]
    c3, c4, c5 = c2 + cw, c2 + 2 * cw, c2 + 3 * cw
    q = z[:, :c0] * ATTN_SCALE
    k_new = z[:, c0:c1]
    v_new = z[:, c1:c2]
    cb = z[:, c2:c3]
    cc = z[:, c3:c4]
    cvv = z[:, c4:c5]
    mq = z[:, c5:] * ATTN_SCALE
    win = wk_ref.shape[1]

    xi = _iota((KV_WIDTH, ATTN_WIDTH), 0)
    xl = _iota((KV_WIDTH, ATTN_WIDTH), 1)
    q_per_kv = N_Q_HEADS // N_KV_HEADS
    expand = (xi == (xl // (q_per_kv * HEAD_DIM)) * HEAD_DIM + xl % HEAD_DIM).astype(BF16)
    rr = _iota((r_exp, r_tok), 0)
    rc = _iota((r_exp, r_tok), 1)
    rep = (rr // REP == rc).astype(BF16)

    hrow = _iota((r_exp, 1), 0) % REP
    trow = (_iota((r_exp, 1), 0) // REP) % t_dec
    slope_col = _per_head_column(ALIBI_SLOPES, hrow)
    sink_col = _per_head_column([sinks_ref[h] for h in range(N_Q_HEADS)], hrow)

    qexp = jnp.where(hrow == _iota((r_exp, ATTN_WIDTH), 1) // HEAD_DIM, _dot(rep, q.astype(BF16)), 0.0)
    kexp = _dot(wk_ref[...].reshape(BB * win, KV_WIDTH).astype(BF16), expand).astype(BF16)
    vexp = _dot(wv_ref[...].reshape(BB * win, KV_WIDTH).astype(BF16), expand).astype(BF16)
    s = jnp.einsum("bqc,bkc->bqk", qexp.astype(BF16).reshape(BB, qrows, ATTN_WIDTH),
                   kexp.reshape(BB, win, ATTN_WIDTH), preferred_element_type=F32).reshape(r_exp, win)
    scol = _iota((r_exp, win), 1)
    s = s - slope_col * (win + trow - scol).astype(F32)
    s = jnp.where(scol > trow, s, -jnp.inf)
    knew_exp = _dot(k_new.astype(BF16), expand).astype(BF16)
    vnew_exp = _dot(v_new.astype(BF16), expand).astype(BF16)
    s_new, v_rep = [], []
    for jn in range(t_dec):
        rep_j = (rc == (rr // qrows) * t_dec + jn).astype(BF16)
        k_rep = _dot(rep_j, knew_exp)
        v_rep.append(_dot(rep_j, vnew_exp))
        sj = jnp.sum(qexp * k_rep, axis=-1, keepdims=True) - slope_col * (trow - jn).astype(F32)
        s_new.append(jnp.where(trow >= jn, sj, -jnp.inf))
    m = jnp.maximum(jnp.max(s, axis=-1, keepdims=True), sink_col)
    for sj in s_new:
        m = jnp.maximum(m, sj)
    pe = jnp.exp(s - m)
    denom = jnp.sum(pe, axis=-1, keepdims=True) + jnp.exp(sink_col - m)
    o = jnp.einsum("bqk,bkc->bqc", pe.astype(BF16).reshape(BB, qrows, win),
                   vexp.reshape(BB, win, ATTN_WIDTH), preferred_element_type=F32).reshape(r_exp, ATTN_WIDTH)
    for jn in range(t_dec):
        pj = jnp.exp(s_new[jn] - m)
        denom = denom + pj
        o = o + pj * v_rep[jn]
    o = jnp.where(hrow == _iota((r_exp, ATTN_WIDTH), 1) // HEAD_DIM, o / denom, 0.0)
    attn = jnp.sum(o.reshape(r_tok, REP, ATTN_WIDTH), axis=1)

    m_tok = mk_ref.shape[1]
    mhead = _iota((r_exp, MEM_WIDTH), 1) // HEAD_DIM
    mqexp = jnp.where(hrow == mhead, _dot(rep, mq.astype(BF16)), 0.0).astype(BF16)
    sm = jnp.einsum("bqc,bmc->bqm", mqexp.reshape(BB, qrows, MEM_WIDTH), mk_ref[...].astype(BF16),
                    preferred_element_type=F32).reshape(r_exp, m_tok)
    mm = jnp.max(sm, axis=-1, keepdims=True)
    pm = jnp.exp(sm - mm)
    dm = jnp.sum(pm, axis=-1, keepdims=True)
    om = jnp.einsum("bqm,bmc->bqc", pm.astype(BF16).reshape(BB, qrows, m_tok), mv_ref[...].astype(BF16),
                    preferred_element_type=F32).reshape(r_exp, MEM_WIDTH)
    om = jnp.where(hrow == mhead, om / dm, 0.0)
    cross = jnp.sum(om.reshape(r_tok, REP, MEM_WIDTH), axis=1)

    u = cc * cvv
    tt = _iota(u.shape, 0) % t_dec
    u1 = jnp.where(tt >= 1, pltpu.roll(u, 1, axis=0), pm1_ref[...])
    u2 = jnp.where(tt >= 2, pltpu.roll(u, 2, axis=0), pm2_ref[...])
    conv_out = cb * (conv_w_ref[0:1, :] * u2 + conv_w_ref[1:2, :] * u1 + conv_w_ref[2:3, :] * u)
    u_ref[...] = u

    nwk_ref[:, 0:win - t_dec, :] = wk_ref[:, t_dec:win, :]
    nwv_ref[:, 0:win - t_dec, :] = wv_ref[:, t_dec:win, :]
    for b in range(BB):
        nwk_ref[b, win - t_dec:win, :] = k_new[b * t_dec:(b + 1) * t_dec, :]
        nwv_ref[b, win - t_dec:win, :] = v_new[b * t_dec:(b + 1) * t_dec, :]

    _mix_out(x, attn, conv_out, cross, g_mix_ref, w_out_ref, g_ffn_ref, w_r_ref, b_r_ref,
             h_ref, hn_ref, tope_ref, gates_ref)


def _mixer_s(x2, t_dec, pm1, pm2, wk, wv, mk, mv, sinks, g_attn, w_in, conv_w, g_mix, w_out, g_ffn, w_r, b_r):
    n, d = x2.shape
    nb = wk.shape[0]
    win = wk.shape[1]
    m_tok = mk.shape[1]
    cw = conv_w.shape[1]
    r_tok = BB * t_dec
    full = lambda a: pl.BlockSpec(a.shape, lambda i, *_: (0,) * a.ndim)
    tok = lambda w: pl.BlockSpec((r_tok, w), lambda i, *_: (i, 0))
    per_b = lambda r, c: pl.BlockSpec((BB, r, c), lambda i, *_: (i, 0, 0))
    grid_spec = pltpu.PrefetchScalarGridSpec(
        num_scalar_prefetch=1,
        grid=(nb // BB,),
        in_specs=[tok(d), tok(cw), tok(cw), per_b(win, KV_WIDTH), per_b(win, KV_WIDTH),
                  per_b(m_tok, MEM_WIDTH), per_b(m_tok, MEM_WIDTH),
                  full(g_attn), full(w_in), full(conv_w), full(g_mix), full(w_out), full(g_ffn),
                  full(w_r), full(b_r)],
        out_specs=[tok(d), pl.BlockSpec((r_tok * SUBLANES, LANES), lambda i, *_: (i, 0)),
                   tok(TOP_K), tok(TOP_K), per_b(win, KV_WIDTH), per_b(win, KV_WIDTH), tok(cw)],
    )
    return pl.pallas_call(
        functools.partial(_mixer_s_kernel, t_dec=t_dec),
        grid_spec=grid_spec,
        out_shape=[jax.ShapeDtypeStruct((n, d), F32), jax.ShapeDtypeStruct((n * SUBLANES, LANES), F32),
                   jax.ShapeDtypeStruct((n, TOP_K), jnp.int32), jax.ShapeDtypeStruct((n, TOP_K), F32),
                   jax.ShapeDtypeStruct((nb, win, KV_WIDTH), F32), jax.ShapeDtypeStruct((nb, win, KV_WIDTH), F32),
                   jax.ShapeDtypeStruct((n, cw), F32)],
        compiler_params=pltpu.CompilerParams(dimension_semantics=("arbitrary",), vmem_limit_bytes=VMEM_LIMIT),
        name="mixer_s",
    )(sinks, x2, pm1, pm2, wk, wv, mk, mv, g_attn, w_in, conv_w, g_mix, w_out, g_ffn, w_r, b_r)


def _lpos_kernel(tope_ref, lpos_ref, counts_ref, tri_ref, *, n_e):
    i = pl.program_id(0)
    tm = tope_ref.shape[0]

    @pl.when(i == 0)
    def _():
        tri_ref[...] = (_iota((tm, tm), 0) > _iota((tm, tm), 1)).astype(BF16)

    te = tope_ref[...]
    col = _iota((tm, n_e), 1)
    hits = [te[:, k:k + 1] == col for k in range(TOP_K)]
    onehot = jnp.zeros((tm, n_e), F32)
    lower = jnp.zeros((tm, n_e), F32)
    for k in range(TOP_K):
        onehot = onehot + hits[k].astype(F32)
        lower = lower + (te[:, k:k + 1] < col).astype(F32)
    ahead = _dot(tri_ref[...], onehot.astype(BF16)) + jnp.sum(lower, axis=0, keepdims=True)
    col4 = _iota((tm, TOP_K), 1)
    pos = jnp.zeros((tm, TOP_K), F32)
    for k in range(TOP_K):
        pos = jnp.where(col4 == k, jnp.sum(jnp.where(hits[k], ahead, 0.0), axis=-1, keepdims=True), pos)
    half = (i % 2) * (tm * TOP_K)
    lpos_ref[...] = (pos.astype(jnp.int32) + half) * SUBLANES
    counts_ref[0] = jnp.sum(onehot, axis=0, keepdims=True)


def _lpos(tope, tm, n_e):
    n = tope.shape[0]
    return pl.pallas_call(
        functools.partial(_lpos_kernel, n_e=n_e),
        grid=(n // tm,),
        in_specs=[pl.BlockSpec((tm, TOP_K), lambda i: (i, 0))],
        out_specs=[pl.BlockSpec((tm, TOP_K), lambda i: (i, 0)), pl.BlockSpec((1, 1, n_e), lambda i: (i, 0, 0))],
        out_shape=[jax.ShapeDtypeStruct((n, TOP_K), jnp.int32), jax.ShapeDtypeStruct((n // tm, 1, n_e), F32)],
        scratch_shapes=[pltpu.VMEM((tm, tm), BF16)],
        compiler_params=pltpu.CompilerParams(dimension_semantics=("arbitrary",)),
        name="lpos",
    )(tope)


def _rows(ref, first_row, n_rows):
    return ref.at[pl.ds(pl.multiple_of(first_row * SUBLANES, SUBLANES), n_rows * SUBLANES)]


def _group_chunks(meta, g, n_e, max_rows, make_copy, wait):
    off_ref, cnt_ref, lst_ref = meta
    if wait:
        make_copy(0, 0, max_rows * TOP_K).wait()
        return

    def per_expert(e, c):
        off = off_ref[g * n_e + e]
        lst = lst_ref[g * n_e + e]
        _run_copies(cnt_ref[g * n_e + e], max_rows, lambda done, size: make_copy(lst + done, off + done, size),
                    wait=False)
        return c

    lax.fori_loop(0, n_e, per_expert, 0)


def _run_copies(n_rows, max_rows, make_copy, wait):
    done = 0
    for size in [1 << b for b in range(max_rows.bit_length() - 1, -1, -1)]:
        bit = n_rows & size

        @pl.when(bit != 0)
        def _():
            cp = make_copy(done, size)
            cp.wait() if wait else cp.start()
        done = done + bit


def _dispatch_kernel(off_ref, cnt_ref, lst_ref, zrow_ref, zcnt_ref, nused_ref, lpos_ref, hn_a_ref, hn_b_ref, xs_ref,
                     stage_ref, zbuf_ref, sem, zsem, *, n_e, n_blocks, groups_a):
    j = pl.program_id(0)
    nj = pl.num_programs(0)
    tm = hn_a_ref.shape[0] // SUBLANES
    slot = j % 2
    meta = (off_ref, cnt_ref, lst_ref)

    def chunks(g, s, wait):
        _group_chunks(meta, g, n_e, tm,
                      lambda lrow, grow, size: pltpu.make_async_copy(
                          _rows(stage_ref, s * (tm * TOP_K) + lrow, size), _rows(xs_ref, grow, size),
                          sem.at[s]), wait)

    def zero_fill(wait):
        def zero_run(first_row, n_rows):
            _run_copies(n_rows, BM, lambda done, size: pltpu.make_async_copy(
                _rows(zbuf_ref, 0, size), _rows(xs_ref, first_row + done, size), zsem), wait)

        def expert_pad(e, c):
            zero_run(zrow_ref[e], zcnt_ref[e])
            return c

        def tail_block(b, c):
            zero_run(b * BM, jnp.int32(BM))
            return c

        lax.fori_loop(0, n_e, expert_pad, 0)
        lax.fori_loop(nused_ref[0], n_blocks, tail_block, 0)

    @pl.when(j == 0)
    def _():
        zbuf_ref[...] = jnp.zeros_like(zbuf_ref)
        zero_fill(False)

    @pl.when(j >= 2)
    def _():
        chunks(j - 2, slot, True)

    def place_from(hn_ref):
        def place(t, c):
            tile = hn_ref[pl.ds(pl.multiple_of(t * SUBLANES, SUBLANES), SUBLANES), :]
            for k in range(TOP_K):
                pos = pl.multiple_of(lpos_ref[0, 0, t * TOP_K + k], SUBLANES)
                stage_ref[pl.ds(pos, SUBLANES), :] = tile
            return c
        lax.fori_loop(0, tm, place, 0, unroll=8)

    @pl.when(j < groups_a)
    def _():
        place_from(hn_a_ref)

    @pl.when(j >= groups_a)
    def _():
        place_from(hn_b_ref)

    chunks(j, slot, False)

    @pl.when(j == nj - 1)
    def _():
        @pl.when(j >= 1)
        def _():
            chunks(j - 1, 1 - slot, True)
        chunks(j, slot, True)
        zero_fill(True)


def _dispatch(hn_a, hn_b, lpos, meta, zrow, zcnt, nused, tm, n_e, n_blocks):
    nt = lpos.shape[0] // tm
    groups_a = hn_a.shape[0] // (tm * SUBLANES)
    assert groups_a >= 1 and groups_a + hn_b.shape[0] // (tm * SUBLANES) == nt
    grid_spec = pltpu.PrefetchScalarGridSpec(
        num_scalar_prefetch=6,
        grid=(nt,),
        in_specs=[pl.BlockSpec((1, 1, tm * TOP_K), lambda j, *_: (j, 0, 0), memory_space=pltpu.SMEM),
                  pl.BlockSpec((tm * SUBLANES, LANES), lambda j, *_: (jnp.minimum(j, groups_a - 1), 0)),
                  pl.BlockSpec((tm * SUBLANES, LANES), lambda j, *_: (jnp.maximum(j - groups_a, 0), 0))],
        out_specs=pl.BlockSpec(memory_space=pl.ANY),
        scratch_shapes=[pltpu.VMEM((2 * tm * TOP_K * SUBLANES, LANES), F32), pltpu.VMEM((BM * SUBLANES, LANES), F32),
                        pltpu.SemaphoreType.DMA((2,)), pltpu.SemaphoreType.DMA(())],
    )
    return pl.pallas_call(
        functools.partial(_dispatch_kernel, n_e=n_e, n_blocks=n_blocks, groups_a=groups_a),
        grid_spec=grid_spec,
        out_shape=jax.ShapeDtypeStruct((n_blocks * BM * SUBLANES, LANES), F32),
        compiler_params=pltpu.CompilerParams(dimension_semantics=("arbitrary",), vmem_limit_bytes=VMEM_LIMIT),
        name="dispatch",
    )(*meta, zrow, zcnt, nused, lpos.reshape(nt, 1, tm * TOP_K), hn_a, hn_b)


def _experts_kernel(blk_e_ref, nused_ref, next_e_ref, blk_rows_ref, xs_ref, bgu_ref, bd_ref, wgu_hbm_ref, wd_hbm_ref, ys_ref,
                    wgu_ref, wd_ref, wgu_bf_ref, wd_bf_ref, wsem):
    i = pl.program_id(0)
    nused = nused_ref[0]
    d, d_ff2 = wgu_ref.shape
    grp = 2 * LANES

    def weight_copies(e):
        gu_cols = d_ff2 // W_PARTS
        dn_rows = (d_ff2 // 2) // W_PARTS
        cps = []
        for p in range(W_PARTS):
            cps.append(pltpu.make_async_copy(wgu_hbm_ref.at[e, :, pl.ds(p * gu_cols, gu_cols)],
                                             wgu_ref.at[:, pl.ds(p * gu_cols, gu_cols)], wsem.at[p]))
            cps.append(pltpu.make_async_copy(wd_hbm_ref.at[e, pl.ds(p * dn_rows, dn_rows), :],
                                             wd_ref.at[pl.ds(p * dn_rows, dn_rows), :], wsem.at[W_PARTS + p]))
        return cps

    @pl.when(i < nused)
    def _():
        e = blk_e_ref[i]
        e_prev = blk_e_ref[jnp.maximum(i - 1, 0)]

        @pl.when(i == 0)
        def _():
            for cp in weight_copies(e):
                cp.start()

        @pl.when((i == 0) | (e != e_prev))
        def _():
            for cp in weight_copies(e):
                cp.wait()
            pr = _iota((grp, grp), 0)
            pc = _iota((grp, grp), 1)
            perm = (pr == jnp.where(pc < LANES, 2 * pc, 2 * (pc - LANES) + 1)).astype(BF16)
            for g in range(d_ff2 // grp):
                w = wgu_ref[:, g * grp:(g + 1) * grp].astype(BF16)
                wgu_bf_ref[:, g * grp:(g + 1) * grp] = _dot(w, perm).astype(BF16)
            wd_bf_ref[...] = wd_ref[...].astype(BF16)

            @pl.when(next_e_ref[e] >= 0)
            def _():
                for cp in weight_copies(next_e_ref[e]):
                    cp.start()

        def ffn(r0, n_rows):
            x = jnp.concatenate([_load_token_tiles(xs_ref, n_rows, s, base=r0 * SUBLANES).astype(BF16)
                                 for s in range(SUBLANES)], axis=1)
            hgu = _dot(x, wgu_bf_ref[...]) + bgu_ref[0]
            acts = []
            for g in range(d_ff2 // grp):
                gate = jnp.minimum(hgu[:, g * grp:g * grp + LANES], SWIGLU_LIMIT)
                up = jnp.clip(hgu[:, g * grp + LANES:(g + 1) * grp], -SWIGLU_LIMIT, SWIGLU_LIMIT)
                glu = gate * (1.0 / (1.0 + jnp.exp(-SWIGLU_ALPHA * gate)))
                acts.append(((up + 1.0) * glu).astype(BF16))
            act = jnp.concatenate(acts, axis=1)
            _store_token_tiles(ys_ref, _dot(act, wd_bf_ref[...]) + bd_ref[0], base=r0 * SUBLANES)

        rows = blk_rows_ref[i]

        @pl.when(rows == BM)
        def _():
            ffn(0, BM)

        @pl.when(rows < BM)
        def _():
            for r0 in range(0, BM, TAIL_CHUNK):
                @pl.when(r0 < rows)
                def _():
                    ffn(r0, TAIL_CHUNK)

                @pl.when(r0 >= rows)
                def _():
                    ys_ref[r0 * SUBLANES:(r0 + TAIL_CHUNK) * SUBLANES, :] = jnp.zeros(
                        (TAIL_CHUNK * SUBLANES, LANES), F32)

    @pl.when(i >= nused)
    def _():
        ys_ref[...] = jnp.zeros_like(ys_ref)


def _experts(xs, blk_e, nused, next_e, blk_rows, w_gate_up, b_gu_perm, w_down, b_down):
    n_e, d, d_ff2 = w_gate_up.shape
    n_blocks = xs.shape[0] // (BM * SUBLANES)
    expert = lambda i, be, nu, *_: be[jnp.minimum(i, jnp.maximum(nu[0] - 1, 0))]
    rows_spec = pl.BlockSpec((BM * SUBLANES, LANES), lambda i, *_: (i, 0))
    grid_spec = pltpu.PrefetchScalarGridSpec(
        num_scalar_prefetch=4,
        grid=(n_blocks,),
        in_specs=[pl.BlockSpec((BM * SUBLANES, LANES), lambda i, be, nu, *_: (jnp.minimum(i, jnp.maximum(nu[0] - 1, 0)), 0)),
                  pl.BlockSpec((1, 1, d_ff2), lambda i, *s: (expert(i, *s), 0, 0)),
                  pl.BlockSpec((1, 1, d), lambda i, *s: (expert(i, *s), 0, 0)),
                  pl.BlockSpec(memory_space=pl.ANY), pl.BlockSpec(memory_space=pl.ANY)],
        out_specs=rows_spec,
        scratch_shapes=[pltpu.VMEM((d, d_ff2), F32), pltpu.VMEM((d_ff2 // 2, d), F32),
                        pltpu.VMEM((d, d_ff2), BF16), pltpu.VMEM((d_ff2 // 2, d), BF16),
                        pltpu.SemaphoreType.DMA((2 * W_PARTS,))],
    )
    return pl.pallas_call(
        _experts_kernel,
        grid_spec=grid_spec,
        out_shape=jax.ShapeDtypeStruct(xs.shape, F32),
        compiler_params=pltpu.CompilerParams(dimension_semantics=("arbitrary",), vmem_limit_bytes=VMEM_LIMIT),
        name="experts",
    )(blk_e, nused, next_e, blk_rows, xs, b_gu_perm, b_down, w_gate_up, w_down)


def _combine_kernel(off_ref, cnt_ref, lst_ref, lpos_ref, gates_ref, h_ref, g_ref, ys_ref, out_ref,
                    stage_ref, acc_ref, sem, *, n_e, g0):
    i = pl.program_id(0)
    ng = pl.num_programs(0)
    tm, d = h_ref.shape
    slot = (g0 + i) % 2
    meta = (off_ref, cnt_ref, lst_ref)

    def chunks(g, s, wait):
        _group_chunks(meta, g, n_e, tm,
                      lambda lrow, grow, size: pltpu.make_async_copy(
                          _rows(ys_ref, grow, size), _rows(stage_ref, s * (tm * TOP_K) + lrow, size),
                          sem.at[s]), wait)

    @pl.when(i == 0)
    def _():
        chunks(g0, slot, False)

    @pl.when(i + 1 < ng)
    def _():
        chunks(g0 + i + 1, 1 - slot, False)

    chunks(g0 + i, slot, True)

    def mix(t, c):
        acc = None
        for k in range(TOP_K):
            pos = pl.multiple_of(lpos_ref[0, 0, t * TOP_K + k], SUBLANES)
            term = stage_ref[pl.ds(pos, SUBLANES), :] * gates_ref[0, 0, t * TOP_K + k]
            acc = term if acc is None else acc + term
        acc_ref[pl.ds(pl.multiple_of(t * SUBLANES, SUBLANES), SUBLANES), :] = acc
        return c

    lax.fori_loop(0, tm, mix, 0, unroll=8)

    h = h_ref[...]
    parts = []
    sq = jnp.zeros((tm, LANES), F32)
    for s in range(d // LANES):
        y = h[:, s * LANES:(s + 1) * LANES] + _load_token_tiles(acc_ref, tm, s)
        sq = sq + y * y
        parts.append(y)
    rinv = lax.rsqrt(jnp.sum(sq, axis=-1, keepdims=True) / d + EPS)
    out_ref[...] = jnp.concatenate(parts, axis=1) * rinv * g_ref[...]


def _combine(ys, h, lpos, gates, meta, g_final, tm, n_e, g0):
    n, d = h.shape
    grid_spec = pltpu.PrefetchScalarGridSpec(
        num_scalar_prefetch=3,
        grid=(n // tm,),
        in_specs=[pl.BlockSpec((1, 1, tm * TOP_K), lambda i, *_: (g0 + i, 0, 0), memory_space=pltpu.SMEM),
                  pl.BlockSpec((1, 1, tm * TOP_K), lambda i, *_: (g0 + i, 0, 0), memory_space=pltpu.SMEM),
                  pl.BlockSpec((tm, d), lambda i, *_: (i, 0)),
                  pl.BlockSpec((1, d), lambda i, *_: (0, 0)),
                  pl.BlockSpec(memory_space=pl.ANY)],
        out_specs=pl.BlockSpec((tm, d), lambda i, *_: (i, 0)),
        scratch_shapes=[pltpu.VMEM((2 * tm * TOP_K * SUBLANES, LANES), F32), pltpu.VMEM((tm * SUBLANES, LANES), F32),
                        pltpu.SemaphoreType.DMA((2,))],
    )
    ngroups = lpos.shape[0] // tm
    return pl.pallas_call(
        functools.partial(_combine_kernel, n_e=n_e, g0=g0),
        grid_spec=grid_spec,
        out_shape=jax.ShapeDtypeStruct((n, d), F32),
        compiler_params=pltpu.CompilerParams(dimension_semantics=("arbitrary",), vmem_limit_bytes=VMEM_LIMIT),
        name="combine",
    )(*meta, lpos.reshape(ngroups, 1, tm * TOP_K), gates.reshape(ngroups, 1, tm * TOP_K), h, g_final, ys)


def _largest_tile(cands, *sizes):
    for c in cands:
        if all(s % c == 0 for s in sizes):
            return c
    raise ValueError(f"no tile in {cands} divides {sizes}")


def kernel(x_prompt, x_sample, mem_prompt, cache_win_k, cache_win_v, state_conv, cache_mem_k, cache_mem_v, g_attn_norm, w_in, conv_w, attn_sinks, g_mem_norm, w_mem_kv, g_mix_out, w_out, g_ffn_norm, w_router, b_router, w_gate_up, b_gate_up, w_down, b_down, g_final):
    depth = w_in.shape[0]
    assert depth == 1, "single-layer step"
    b, s, d = x_prompt.shape
    nb, t_dec, _ = x_sample.shape
    n_e = w_router.shape[2]
    d_ff2 = w_gate_up.shape[3]
    cw = conv_w.shape[2]
    win = cache_win_k.shape[2]
    m_tok = cache_mem_k.shape[2]
    assert s % TQ == 0 and nb % BB == 0 and win == WINDOW and t_dec <= SUBLANES and d_ff2 % (2 * LANES) == 0
    assert d == SUBLANES * LANES, "token-tile layout: one token is one (8, 128) f32 tile"

    row = lambda a: a.reshape(1, -1)
    w_in_bf = w_in[0].astype(BF16)
    w_out_bf = w_out[0].astype(BF16)
    w_r_bf = w_router[0].astype(BF16)
    sinks = attn_sinks[0].astype(F32)
    shared = (row(g_attn_norm[0]), w_in_bf, conv_w[0], row(g_mix_out[0]), w_out_bf, row(g_ffn_norm[0]),
              w_r_bf, row(b_router[0]))

    mk_p, mv_p, mkt, mvb = _memkv(mem_prompt, row(g_mem_norm[0]), w_mem_kv[0].astype(BF16))
    h_p, hn_p, tope_p, gates_p, lastk, lastv, convst = _mixer_p(x_prompt, sinks, *shared, mkt, mvb)

    zeros = lambda r: jnp.zeros((nb, r, cw), F32)
    st = state_conv[0]
    pm1 = jnp.concatenate([st[:, 1:2], zeros(t_dec - 1)], axis=1).reshape(nb * t_dec, cw)
    pm2 = jnp.concatenate([st, zeros(t_dec - 2)], axis=1).reshape(nb * t_dec, cw)
    h_s, hn_s, tope_s, gates_s, nwk, nwv, u_s = _mixer_s(
        x_sample.reshape(nb * t_dec, d), t_dec, pm1, pm2,
        cache_win_k[0].reshape(nb, win, KV_WIDTH), cache_win_v[0].reshape(nb, win, KV_WIDTH),
        cache_mem_k[0].reshape(nb, m_tok, MEM_WIDTH), cache_mem_v[0].reshape(nb, m_tok, MEM_WIDTH),
        sinks, *shared)

    n_p, n_s = b * s, nb * t_dec
    n = n_p + n_s
    tm = _largest_tile((512, 256, 128, 64, 32, 16, 8), n_p, n_s)
    tope = jnp.concatenate([tope_p, tope_s], axis=0)
    gates = jnp.concatenate([gates_p, gates_s], axis=0)

    lpos, cnt_f = _lpos(tope, tm, n_e)
    cnt = cnt_f[:, 0, :].astype(jnp.int32)
    counts = jnp.sum(cnt, axis=0)
    padded = (counts + BM - 1) // BM * BM
    pad_ends = jnp.cumsum(padded)
    pad_starts = pad_ends - padded
    nk = n * TOP_K
    n_blocks = -(-nk // BM) + n_e
    nused = (pad_ends[-1:] // BM).astype(jnp.int32)
    blk_start = jnp.arange(n_blocks, dtype=jnp.int32) * BM
    blk_e = jnp.minimum(jnp.sum((pad_ends[None, :] <= blk_start[:, None]).astype(jnp.int32), axis=1), n_e - 1)
    zrow = (pad_starts + counts).astype(jnp.int32)
    zcnt = (padded - counts).astype(jnp.int32)
    off = pad_starts[None, :] + jnp.cumsum(cnt, axis=0) - cnt
    lstart = jnp.cumsum(cnt, axis=1) - cnt
    meta = (off.reshape(-1).astype(jnp.int32), cnt.reshape(-1), lstart.reshape(-1).astype(jnp.int32))

    xs = _dispatch(hn_p, hn_s, lpos, meta, zrow, zcnt, nused, tm, n_e, n_blocks)

    grp = 2 * LANES
    b_gu = b_gate_up[0].reshape(n_e, d_ff2 // grp, LANES, 2).transpose(0, 1, 3, 2).reshape(n_e, 1, d_ff2)
    owner = jnp.where(padded > 0, jnp.arange(n_e, dtype=jnp.int32), n_e)
    following = jnp.concatenate([lax.cummin(owner, reverse=True)[1:], jnp.full((1,), n_e, jnp.int32)])
    next_e = jnp.where(following < n_e, following, -1).astype(jnp.int32)
    blk_rows = jnp.clip((pad_starts + counts)[blk_e] - blk_start, 0, BM).astype(jnp.int32)
    ys = _experts(xs, blk_e, nused, next_e, blk_rows, w_gate_up[0], b_gu, w_down[0], b_down[0].reshape(n_e, 1, d))

    g_fin = row(g_final)
    y_p = _combine(ys, h_p, lpos, gates, meta, g_fin, tm, n_e, 0)
    y_s = _combine(ys, h_s, lpos, gates, meta, g_fin, tm, n_e, n_p // tm)

    kv5 = lambda a, bsz, r, hds: a.reshape(1, bsz, r, hds, HEAD_DIM)
    return (y_p.reshape(b, s, d), y_s.reshape(nb, t_dec, d),
            kv5(lastk, b, WINDOW, N_KV_HEADS), kv5(lastv, b, WINDOW, N_KV_HEADS),
            convst.reshape(1, b, 2, cw),
            kv5(mk_p, b, m_tok, N_MEM_HEADS), kv5(mv_p, b, m_tok, N_MEM_HEADS),
            kv5(nwk, nb, win, N_KV_HEADS), kv5(nwv, nb, win, N_KV_HEADS),
            u_s.reshape(nb, t_dec, cw)[:, t_dec - 2:].reshape(1, nb, 2, cw))
```

```python
import functools

import jax
import jax.numpy as jnp
from jax import lax
from jax.experimental import pallas as pl
from jax.experimental.pallas import tpu as pltpu

F32 = jnp.float32
BF16 = jnp.bfloat16

HEAD_DIM = 64
N_Q_HEADS = 8
N_KV_HEADS = 2
WINDOW = 128
ATTN_WIDTH = N_Q_HEADS * HEAD_DIM
KV_WIDTH = N_KV_HEADS * HEAD_DIM
N_MEM_HEADS = 4
MEM_WIDTH = N_MEM_HEADS * HEAD_DIM
TOP_K = 4
SWIGLU_LIMIT = 7.0
SWIGLU_ALPHA = 1.702
EPS = 1e-5
ATTN_SCALE = HEAD_DIM ** -0.5
ALIBI_SLOPES = tuple(2.0 ** (-8.0 * (h + 1) / N_Q_HEADS) for h in range(N_Q_HEADS))

LANES = 128
SUBLANES = 8
VMEM_LIMIT = 56 * 1024 * 1024

TQ = 512
BB = 16
BM = 512
REP = 8
W_PARTS = 4
MIX_CHUNK = 256
IN_CHUNK = 256
TAIL_CHUNK = 128


def _rms(x):
    return x * lax.rsqrt(jnp.mean(x * x, axis=-1, keepdims=True) + EPS)


def _dot(a, b):
    return jnp.dot(a, b, preferred_element_type=F32)


def _dot_nt(a, b):
    return lax.dot_general(a, b, (((1,), (1,)), ((), ())), preferred_element_type=F32)


def _iota(shape, axis):
    return lax.broadcasted_iota(jnp.int32, shape, axis)


def _store_token_tiles(ref, x, base=0):
    t = x.shape[0]
    for s in range(x.shape[1] // LANES):
        ref[pl.ds(base + s, t, stride=SUBLANES), :] = x[:, s * LANES:(s + 1) * LANES]


def _load_token_tiles(ref, t, s, base=0):
    return ref[pl.ds(base + s, t, stride=SUBLANES), :]


def _memkv_kernel(mem_ref, g_ref, w_ref, mk_ref, mv_ref, mkt_ref, mvb_ref):
    xn = (_rms(mem_ref[0]) * g_ref[...]).astype(BF16)
    kv = _dot(xn, w_ref[...])
    mk = kv[:, :MEM_WIDTH]
    mv = kv[:, MEM_WIDTH:]
    mk_ref[0] = mk
    mv_ref[0] = mv
    mkt_ref[0] = mk.T.astype(BF16)
    mvb_ref[0] = mv.astype(BF16)


def _memkv(mem, g, w_bf):
    b, m, d = mem.shape
    out_f = jax.ShapeDtypeStruct((b, m, MEM_WIDTH), F32)
    out_b = jax.ShapeDtypeStruct((b, m, MEM_WIDTH), BF16)
    out_t = jax.ShapeDtypeStruct((b, MEM_WIDTH, m), BF16)
    blk = lambda r, c: pl.BlockSpec((1, r, c), lambda i: (i, 0, 0))
    return pl.pallas_call(
        _memkv_kernel,
        grid=(b,),
        in_specs=[blk(m, d), pl.BlockSpec((1, d), lambda i: (0, 0)),
                  pl.BlockSpec((d, 2 * MEM_WIDTH), lambda i: (0, 0))],
        out_specs=[blk(m, MEM_WIDTH), blk(m, MEM_WIDTH), blk(MEM_WIDTH, m), blk(m, MEM_WIDTH)],
        out_shape=[out_f, out_f, out_t, out_b],
        name="memkv",
    )(mem, g, w_bf)


def _router_topk(hn, w_r_ref, b_r_ref, tope_ref, gates_ref, rows_at):
    n_e = w_r_ref.shape[1]
    logits = _dot(hn.astype(BF16), w_r_ref[...]) + b_r_ref[...]
    rows = logits.shape[0]
    col = _iota((rows, n_e), 1).astype(F32)
    vals, idxs = [], []
    cur = logits
    for _ in range(TOP_K):
        m = jnp.max(cur, axis=-1, keepdims=True)
        idx = jnp.min(jnp.where(cur == m, col, float(n_e)), axis=-1, keepdims=True)
        vals.append(m)
        idxs.append(idx)
        cur = jnp.where(col == idx, -jnp.inf, cur)
    exps = [jnp.exp(v - vals[0]) for v in vals]
    tot = exps[0] + exps[1] + exps[2] + exps[3]
    col4 = _iota((rows, TOP_K), 1)
    te = jnp.zeros((rows, TOP_K), F32)
    ga = jnp.zeros((rows, TOP_K), F32)
    for k in range(TOP_K):
        te = jnp.where(col4 == k, idxs[k], te)
        ga = jnp.where(col4 == k, exps[k] / tot, ga)
    tope_ref[rows_at, :] = te.astype(jnp.int32)
    gates_ref[rows_at, :] = ga


def _mix_out(x, attn, conv_out, cross, g_mix_ref, w_out_ref, g_ffn_ref, w_r_ref, b_r_ref,
             h_ref, hn_ref, tope_ref, gates_ref):
    rows = x.shape[0]
    chunk = MIX_CHUNK if rows % MIX_CHUNK == 0 else rows
    for r0 in range(0, rows, chunk):
        at = slice(r0, r0 + chunk)
        mix = jnp.concatenate([_rms(attn[at]), _rms(conv_out[at]), _rms(cross[at])], axis=-1) * g_mix_ref[...]
        h = x[at] + _dot(mix.astype(BF16), w_out_ref[...])
        hn = _rms(h) * g_ffn_ref[...]
        h_ref[at, :] = h
        _store_token_tiles(hn_ref, hn, base=r0 * SUBLANES)
        _router_topk(hn, w_r_ref, b_r_ref, tope_ref, gates_ref, at)


def _swa_bias(prev_lim):
    blk = WINDOW
    qi = _iota((blk, 2 * blk), 0)
    kj = _iota((blk, 2 * blk), 1)
    dist = blk + qi - kj
    mask = (dist >= 0) & (dist < WINDOW) & (kj >= prev_lim)
    distf = dist.astype(F32)
    return [jnp.where(mask, -ALIBI_SLOPES[h] * distf, -jnp.inf) for h in range(N_Q_HEADS)]


def _swa_block(q_blk, kk, vv, bias, sinks_ref):
    blk = WINDOW
    lane = _iota((2 * blk, KV_WIDTH), 1)
    lo = lane < HEAD_DIM
    kk_r = pltpu.roll(kk, HEAD_DIM, axis=1)
    vv_r = pltpu.roll(vv, HEAD_DIM, axis=1)
    kdup = [jnp.where(lo, kk, kk_r).astype(BF16), jnp.where(lo, kk_r, kk).astype(BF16)]
    vlo = [jnp.where(lo, vv, 0.0).astype(BF16), jnp.where(lo, vv_r, 0.0).astype(BF16)]
    vhi = [jnp.where(lo, 0.0, vv_r).astype(BF16), jnp.where(lo, 0.0, vv).astype(BF16)]
    qlo = _iota((blk, 2 * HEAD_DIM), 1) < HEAD_DIM
    outs = []
    for p in range(N_Q_HEADS // 2):
        kh = (2 * p) // (N_Q_HEADS // N_KV_HEADS)
        qp = q_blk[:, p * 2 * HEAD_DIM:(p + 1) * 2 * HEAD_DIM]
        acc = None
        for e in range(2):
            h = 2 * p + e
            qm = jnp.where(qlo if e == 0 else jnp.logical_not(qlo), qp, 0.0).astype(BF16)
            s = _dot_nt(qm, kdup[kh]) + bias[h]
            sink = sinks_ref[h]
            m = jnp.maximum(jnp.max(s, axis=-1, keepdims=True), sink)
            pe = jnp.exp(s - m)
            denom = jnp.sum(pe, axis=-1, keepdims=True) + jnp.exp(sink - m)
            o = _dot(pe.astype(BF16), (vlo if e == 0 else vhi)[kh]) / denom
            acc = o if acc is None else acc + o
        outs.append(acc)
    return jnp.concatenate(outs, axis=1)


def _mem_attend_shared(mq, mkt, mvb):
    t = mq.shape[0]
    m_tok = mvb.shape[0]
    qhead = _iota((t, MEM_WIDTH), 1) // HEAD_DIM
    vhead = _iota((m_tok, MEM_WIDTH), 1) // HEAD_DIM
    cross = None
    for h in range(N_MEM_HEADS):
        qm = jnp.where(qhead == h, mq, 0.0).astype(BF16)
        s = _dot(qm, mkt)
        m = jnp.max(s, axis=-1, keepdims=True)
        pe = jnp.exp(s - m)
        denom = jnp.sum(pe, axis=-1, keepdims=True)
        vm = jnp.where(vhead == h, mvb, jnp.zeros_like(mvb))
        o = _dot(pe.astype(BF16), vm) / denom
        cross = o if cross is None else cross + o
    return cross


def _mixer_p_kernel(sinks_ref, x_ref, g_attn_ref, w_in_ref, conv_w_ref, g_mix_ref, w_out_ref, g_ffn_ref,
                    w_r_ref, b_r_ref, mkt_ref, mvb_ref,
                    h_ref, hn_ref, tope_ref, gates_ref, lastk_ref, lastv_ref, convst_ref,
                    ck_ref, cv_ref, cu_ref):
    j = pl.program_id(1)
    nj = pl.num_programs(1)

    @pl.when(j == 0)
    def _():
        ck_ref[...] = jnp.zeros_like(ck_ref)
        cv_ref[...] = jnp.zeros_like(cv_ref)
        cu_ref[...] = jnp.zeros_like(cu_ref)

    x = x_ref[0]
    z = jnp.concatenate([_dot((_rms(x[r0:r0 + IN_CHUNK]) * g_attn_ref[...]).astype(BF16), w_in_ref[...])
                         for r0 in range(0, TQ, IN_CHUNK)], axis=0)
    c0 = ATTN_WIDTH
    c1 = c0 + KV_WIDTH
    c2 = c1 + KV_WIDTH
    cw = conv_w_ref.shape[1]
    c3, c4, c5 = c2 + cw, c2 + 2 * cw, c2 + 3 * cw
    q = z[:, :c0] * ATTN_SCALE
    k = z[:, c0:c1]
    v = z[:, c1:c2]
    cb = z[:, c2:c3]
    cc = z[:, c3:c4]
    cvv = z[:, c4:c5]
    mq = z[:, c5:] * ATTN_SCALE

    blk = WINDOW
    attn_blocks = []
    bias_inner = _swa_bias(0)
    for i in range(TQ // blk):
        if i == 0:
            pk, pv = ck_ref[...], cv_ref[...]
            bias = _swa_bias(jnp.where(j > 0, 0, blk))
        else:
            pk, pv = k[(i - 1) * blk:i * blk], v[(i - 1) * blk:i * blk]
            bias = bias_inner
        kk = jnp.concatenate([pk, k[i * blk:(i + 1) * blk]], axis=0)
        vv = jnp.concatenate([pv, v[i * blk:(i + 1) * blk]], axis=0)
        attn_blocks.append(_swa_block(q[i * blk:(i + 1) * blk], kk, vv, bias, sinks_ref))
    attn = jnp.concatenate(attn_blocks, axis=0)
    ck_ref[...] = k[TQ - blk:]
    cv_ref[...] = v[TQ - blk:]

    u = cc * cvv
    row = _iota(u.shape, 0)
    u1 = jnp.where(row == 0, cu_ref[SUBLANES - 1:SUBLANES, :], pltpu.roll(u, 1, axis=0))
    u2 = jnp.where(row == 0, cu_ref[SUBLANES - 2:SUBLANES - 1, :],
                   jnp.where(row == 1, cu_ref[SUBLANES - 1:SUBLANES, :], pltpu.roll(u, 2, axis=0)))
    conv_out = cb * (conv_w_ref[0:1, :] * u2 + conv_w_ref[1:2, :] * u1 + conv_w_ref[2:3, :] * u)
    cu_ref[...] = u[TQ - SUBLANES:]

    cross = _mem_attend_shared(mq, mkt_ref[0], mvb_ref[0])

    @pl.when(j == nj - 1)
    def _():
        lastk_ref[0] = k[TQ - blk:]
        lastv_ref[0] = v[TQ - blk:]
        convst_ref[0] = u[TQ - 2:]

    _mix_out(x, attn, conv_out, cross, g_mix_ref, w_out_ref, g_ffn_ref, w_r_ref, b_r_ref,
             h_ref, hn_ref, tope_ref, gates_ref)


def _mixer_p(x, sinks, g_attn, w_in, conv_w, g_mix, w_out, g_ffn, w_r, b_r, mkt, mvb):
    b, s, d = x.shape
    nj = s // TQ
    n = b * s
    cw = conv_w.shape[1]
    full = lambda a: pl.BlockSpec(a.shape, lambda bi, ji, *_: (0,) * a.ndim)
    tok = lambda w: pl.BlockSpec((TQ, w), lambda bi, ji, *_: (bi * nj + ji, 0))
    per_b = lambda r, c: pl.BlockSpec((1, r, c), lambda bi, ji, *_: (bi, 0, 0))
    grid_spec = pltpu.PrefetchScalarGridSpec(
        num_scalar_prefetch=1,
        grid=(b, nj),
        in_specs=[pl.BlockSpec((1, TQ, d), lambda bi, ji, *_: (bi, ji, 0)),
                  full(g_attn), full(w_in), full(conv_w), full(g_mix), full(w_out), full(g_ffn),
                  full(w_r), full(b_r), per_b(MEM_WIDTH, mkt.shape[2]), per_b(mvb.shape[1], MEM_WIDTH)],
        out_specs=[tok(d), pl.BlockSpec((TQ * SUBLANES, LANES), lambda bi, ji, *_: (bi * nj + ji, 0)),
                   tok(TOP_K), tok(TOP_K),
                   per_b(WINDOW, KV_WIDTH), per_b(WINDOW, KV_WIDTH), per_b(2, cw)],
        scratch_shapes=[pltpu.VMEM((WINDOW, KV_WIDTH), F32), pltpu.VMEM((WINDOW, KV_WIDTH), F32),
                        pltpu.VMEM((SUBLANES, cw), F32)],
    )
    return pl.pallas_call(
        _mixer_p_kernel,
        grid_spec=grid_spec,
        out_shape=[jax.ShapeDtypeStruct((n, d), F32), jax.ShapeDtypeStruct((n * SUBLANES, LANES), F32),
                   jax.ShapeDtypeStruct((n, TOP_K), jnp.int32), jax.ShapeDtypeStruct((n, TOP_K), F32),
                   jax.ShapeDtypeStruct((b, WINDOW, KV_WIDTH), F32),
                   jax.ShapeDtypeStruct((b, WINDOW, KV_WIDTH), F32),
                   jax.ShapeDtypeStruct((b, 2, cw), F32)],
        compiler_params=pltpu.CompilerParams(dimension_semantics=("arbitrary", "arbitrary"),
                                             vmem_limit_bytes=VMEM_LIMIT),
        name="mixer_p",
    )(sinks, x, g_attn, w_in, conv_w, g_mix, w_out, g_ffn, w_r, b_r, mkt, mvb)


def _per_head_column(values, hrow):
    col = jnp.zeros(hrow.shape, F32)
    for h in range(N_Q_HEADS):
        col = jnp.where(hrow == h, values[h], col)
    return col


def _mixer_s_kernel(sinks_ref, x_ref, pm1_ref, pm2_ref, wk_ref, wv_ref, mk_ref, mv_ref,
                    g_attn_ref, w_in_ref, conv_w_ref, g_mix_ref, w_out_ref, g_ffn_ref, w_r_ref, b_r_ref,
                    h_ref, hn_ref, tope_ref, gates_ref, nwk_ref, nwv_ref, u_ref, *, t_dec):
    r_tok = BB * t_dec
    r_exp = r_tok * REP
    qrows = t_dec * REP
    x = x_ref[...]
    xn = (_rms(x) * g_attn_ref[...]).astype(BF16)
    z = _dot(xn, w_in_ref[...])
    c0 = ATTN_WIDTH
    c1 = c0 + KV_WIDTH
    c2 = c1 + KV_WIDTH
    cw = conv_w_ref.shape[1]
    c3, c4, c5 = c2 + cw, c2 + 2 * cw, c2 + 3 * cw
    q = z[:, :c0] * ATTN_SCALE
    k_new = z[:, c0:c1]
    v_new = z[:, c1:c2]
    cb = z[:, c2:c3]
    cc = z[:, c3:c4]
    cvv = z[:, c4:c5]
    mq = z[:, c5:] * ATTN_SCALE
    win = wk_ref.shape[1]

    xi = _iota((KV_WIDTH, ATTN_WIDTH), 0)
    xl = _iota((KV_WIDTH, ATTN_WIDTH), 1)
    q_per_kv = N_Q_HEADS // N_KV_HEADS
    expand = (xi == (xl // (q_per_kv * HEAD_DIM)) * HEAD_DIM + xl % HEAD_DIM).astype(BF16)
    rr = _iota((r_exp, r_tok), 0)
    rc = _iota((r_exp, r_tok), 1)
    rep = (rr // REP == rc).astype(BF16)

    hrow = _iota((r_exp, 1), 0) % REP
    trow = (_iota((r_exp, 1), 0) // REP) % t_dec
    slope_col = _per_head_column(ALIBI_SLOPES, hrow)
    sink_col = _per_head_column([sinks_ref[h] for h in range(N_Q_HEADS)], hrow)

    qexp = jnp.where(hrow == _iota((r_exp, ATTN_WIDTH), 1) // HEAD_DIM, _dot(rep, q.astype(BF16)), 0.0)
    kexp = _dot(wk_ref[...].reshape(BB * win, KV_WIDTH).astype(BF16), expand).astype(BF16)
    vexp = _dot(wv_ref[...].reshape(BB * win, KV_WIDTH).astype(BF16), expand).astype(BF16)
    s = jnp.einsum("bqc,bkc->bqk", qexp.astype(BF16).reshape(BB, qrows, ATTN_WIDTH),
                   kexp.reshape(BB, win, ATTN_WIDTH), preferred_element_type=F32).reshape(r_exp, win)
    scol = _iota((r_exp, win), 1)
    s = s - slope_col * (win + trow - scol).astype(F32)
    s = jnp.where(scol > trow, s, -jnp.inf)
    knew_exp = _dot(k_new.astype(BF16), expand).astype(BF16)
    vnew_exp = _dot(v_new.astype(BF16), expand).astype(BF16)
    s_new, v_rep = [], []
    for jn in range(t_dec):
        rep_j = (rc == (rr // qrows) * t_dec + jn).astype(BF16)
        k_rep = _dot(rep_j, knew_exp)
        v_rep.append(_dot(rep_j, vnew_exp))
        sj = jnp.sum(qexp * k_rep, axis=-1, keepdims=True) - slope_col * (trow - jn).astype(F32)
        s_new.append(jnp.where(trow >= jn, sj, -jnp.inf))
    m = jnp.maximum(jnp.max(s, axis=-1, keepdims=True), sink_col)
    for sj in s_new:
        m = jnp.maximum(m, sj)
    pe = jnp.exp(s - m)
    denom = jnp.sum(pe, axis=-1, keepdims=True) + jnp.exp(sink_col - m)
    o = jnp.einsum("bqk,bkc->bqc", pe.astype(BF16).reshape(BB, qrows, win),
                   vexp.reshape(BB, win, ATTN_WIDTH), preferred_element_type=F32).reshape(r_exp, ATTN_WIDTH)
    for jn in range(t_dec):
        pj = jnp.exp(s_new[jn] - m)
        denom = denom + pj
        o = o + pj * v_rep[jn]
    o = jnp.where(hrow == _iota((r_exp, ATTN_WIDTH), 1) // HEAD_DIM, o / denom, 0.0)
    attn = jnp.sum(o.reshape(r_tok, REP, ATTN_WIDTH), axis=1)

    m_tok = mk_ref.shape[1]
    mhead = _iota((r_exp, MEM_WIDTH), 1) // HEAD_DIM
    mqexp = jnp.where(hrow == mhead, _dot(rep, mq.astype(BF16)), 0.0).astype(BF16)
    sm = jnp.einsum("bqc,bmc->bqm", mqexp.reshape(BB, qrows, MEM_WIDTH), mk_ref[...].astype(BF16),
                    preferred_element_type=F32).reshape(r_exp, m_tok)
    mm = jnp.max(sm, axis=-1, keepdims=True)
    pm = jnp.exp(sm - mm)
    dm = jnp.sum(pm, axis=-1, keepdims=True)
    om = jnp.einsum("bqm,bmc->bqc", pm.astype(BF16).reshape(BB, qrows, m_tok), mv_ref[...].astype(BF16),
                    preferred_element_type=F32).reshape(r_exp, MEM_WIDTH)
    om = jnp.where(hrow == mhead, om / dm, 0.0)
    cross = jnp.sum(om.reshape(r_tok, REP, MEM_WIDTH), axis=1)

    u = cc * cvv
    tt = _iota(u.shape, 0) % t_dec
    u1 = jnp.where(tt >= 1, pltpu.roll(u, 1, axis=0), pm1_ref[...])
    u2 = jnp.where(tt >= 2, pltpu.roll(u, 2, axis=0), pm2_ref[...])
    conv_out = cb * (conv_w_ref[0:1, :] * u2 + conv_w_ref[1:2, :] * u1 + conv_w_ref[2:3, :] * u)
    u_ref[...] = u

    nwk_ref[:, 0:win - t_dec, :] = wk_ref[:, t_dec:win, :]
    nwv_ref[:, 0:win - t_dec, :] = wv_ref[:, t_dec:win, :]
    for b in range(BB):
        nwk_ref[b, win - t_dec:win, :] = k_new[b * t_dec:(b + 1) * t_dec, :]
        nwv_ref[b, win - t_dec:win, :] = v_new[b * t_dec:(b + 1) * t_dec, :]

    _mix_out(x, attn, conv_out, cross, g_mix_ref, w_out_ref, g_ffn_ref, w_r_ref, b_r_ref,
             h_ref, hn_ref, tope_ref, gates_ref)


def _mixer_s(x2, t_dec, pm1, pm2, wk, wv, mk, mv, sinks, g_attn, w_in, conv_w, g_mix, w_out, g_ffn, w_r, b_r):
    n, d = x2.shape
    nb = wk.shape[0]
    win = wk.shape[1]
    m_tok = mk.shape[1]
    cw = conv_w.shape[1]
    r_tok = BB * t_dec
    full = lambda a: pl.BlockSpec(a.shape, lambda i, *_: (0,) * a.ndim)
    tok = lambda w: pl.BlockSpec((r_tok, w), lambda i, *_: (i, 0))
    per_b = lambda r, c: pl.BlockSpec((BB, r, c), lambda i, *_: (i, 0, 0))
    grid_spec = pltpu.PrefetchScalarGridSpec(
        num_scalar_prefetch=1,
        grid=(nb // BB,),
        in_specs=[tok(d), tok(cw), tok(cw), per_b(win, KV_WIDTH), per_b(win, KV_WIDTH),
                  per_b(m_tok, MEM_WIDTH), per_b(m_tok, MEM_WIDTH),
                  full(g_attn), full(w_in), full(conv_w), full(g_mix), full(w_out), full(g_ffn),
                  full(w_r), full(b_r)],
        out_specs=[tok(d), pl.BlockSpec((r_tok * SUBLANES, LANES), lambda i, *_: (i, 0)),
                   tok(TOP_K), tok(TOP_K), per_b(win, KV_WIDTH), per_b(win, KV_WIDTH), tok(cw)],
    )
    return pl.pallas_call(
        functools.partial(_mixer_s_kernel, t_dec=t_dec),
        grid_spec=grid_spec,
        out_shape=[jax.ShapeDtypeStruct((n, d), F32), jax.ShapeDtypeStruct((n * SUBLANES, LANES), F32),
                   jax.ShapeDtypeStruct((n, TOP_K), jnp.int32), jax.ShapeDtypeStruct((n, TOP_K), F32),
                   jax.ShapeDtypeStruct((nb, win, KV_WIDTH), F32), jax.ShapeDtypeStruct((nb, win, KV_WIDTH), F32),
                   jax.ShapeDtypeStruct((n, cw), F32)],
        compiler_params=pltpu.CompilerParams(dimension_semantics=("arbitrary",), vmem_limit_bytes=VMEM_LIMIT),
        name="mixer_s",
    )(sinks, x2, pm1, pm2, wk, wv, mk, mv, g_attn, w_in, conv_w, g_mix, w_out, g_ffn, w_r, b_r)


def _lpos_kernel(tope_ref, lpos_ref, counts_ref, tri_ref, *, n_e):
    i = pl.program_id(0)
    tm = tope_ref.shape[0]

    @pl.when(i == 0)
    def _():
        tri_ref[...] = (_iota((tm, tm), 0) > _iota((tm, tm), 1)).astype(BF16)

    te = tope_ref[...]
    col = _iota((tm, n_e), 1)
    hits = [te[:, k:k + 1] == col for k in range(TOP_K)]
    onehot = jnp.zeros((tm, n_e), F32)
    lower = jnp.zeros((tm, n_e), F32)
    for k in range(TOP_K):
        onehot = onehot + hits[k].astype(F32)
        lower = lower + (te[:, k:k + 1] < col).astype(F32)
    ahead = _dot(tri_ref[...], onehot.astype(BF16)) + jnp.sum(lower, axis=0, keepdims=True)
    col4 = _iota((tm, TOP_K), 1)
    pos = jnp.zeros((tm, TOP_K), F32)
    for k in range(TOP_K):
        pos = jnp.where(col4 == k, jnp.sum(jnp.where(hits[k], ahead, 0.0), axis=-1, keepdims=True), pos)
    half = (i % 2) * (tm * TOP_K)
    lpos_ref[...] = (pos.astype(jnp.int32) + half) * SUBLANES
    counts_ref[0] = jnp.sum(onehot, axis=0, keepdims=True)


def _lpos(tope, tm, n_e):
    n = tope.shape[0]
    return pl.pallas_call(
        functools.partial(_lpos_kernel, n_e=n_e),
        grid=(n // tm,),
        in_specs=[pl.BlockSpec((tm, TOP_K), lambda i: (i, 0))],
        out_specs=[pl.BlockSpec((tm, TOP_K), lambda i: (i, 0)), pl.BlockSpec((1, 1, n_e), lambda i: (i, 0, 0))],
        out_shape=[jax.ShapeDtypeStruct((n, TOP_K), jnp.int32), jax.ShapeDtypeStruct((n // tm, 1, n_e), F32)],
        scratch_shapes=[pltpu.VMEM((tm, tm), BF16)],
        compiler_params=pltpu.CompilerParams(dimension_semantics=("arbitrary",)),
        name="lpos",
    )(tope)


def _rows(ref, first_row, n_rows):
    return ref.at[pl.ds(pl.multiple_of(first_row * SUBLANES, SUBLANES), n_rows * SUBLANES)]


def _group_chunks(meta, g, n_e, max_rows, make_copy, wait):
    off_ref, cnt_ref, lst_ref = meta
    if wait:
        make_copy(0, 0, max_rows * TOP_K).wait()
        return

    def per_expert(e, c):
        off = off_ref[g * n_e + e]
        lst = lst_ref[g * n_e + e]
        _run_copies(cnt_ref[g * n_e + e], max_rows, lambda done, size: make_copy(lst + done, off + done, size),
                    wait=False)
        return c

    lax.fori_loop(0, n_e, per_expert, 0)


def _run_copies(n_rows, max_rows, make_copy, wait):
    done = 0
    for size in [1 << b for b in range(max_rows.bit_length() - 1, -1, -1)]:
        bit = n_rows & size

        @pl.when(bit != 0)
        def _():
            cp = make_copy(done, size)
            cp.wait() if wait else cp.start()
        done = done + bit


def _dispatch_kernel(off_ref, cnt_ref, lst_ref, zrow_ref, zcnt_ref, nused_ref, lpos_ref, hn_a_ref, hn_b_ref, xs_ref,
                     stage_ref, zbuf_ref, sem, zsem, *, n_e, n_blocks, groups_a):
    j = pl.program_id(0)
    nj = pl.num_programs(0)
    tm = hn_a_ref.shape[0] // SUBLANES
    slot = j % 2
    meta = (off_ref, cnt_ref, lst_ref)

    def chunks(g, s, wait):
        _group_chunks(meta, g, n_e, tm,
                      lambda lrow, grow, size: pltpu.make_async_copy(
                          _rows(stage_ref, s * (tm * TOP_K) + lrow, size), _rows(xs_ref, grow, size),
                          sem.at[s]), wait)

    def zero_fill(wait):
        def zero_run(first_row, n_rows):
            _run_copies(n_rows, BM, lambda done, size: pltpu.make_async_copy(
                _rows(zbuf_ref, 0, size), _rows(xs_ref, first_row + done, size), zsem), wait)

        def expert_pad(e, c):
            zero_run(zrow_ref[e], zcnt_ref[e])
            return c

        def tail_block(b, c):
            zero_run(b * BM, jnp.int32(BM))
            return c

        lax.fori_loop(0, n_e, expert_pad, 0)
        lax.fori_loop(nused_ref[0], n_blocks, tail_block, 0)

    @pl.when(j == 0)
    def _():
        zbuf_ref[...] = jnp.zeros_like(zbuf_ref)
        zero_fill(False)

    @pl.when(j >= 2)
    def _():
        chunks(j - 2, slot, True)

    def place_from(hn_ref):
        def place(t, c):
            tile = hn_ref[pl.ds(pl.multiple_of(t * SUBLANES, SUBLANES), SUBLANES), :]
            for k in range(TOP_K):
                pos = pl.multiple_of(lpos_ref[0, 0, t * TOP_K + k], SUBLANES)
                stage_ref[pl.ds(pos, SUBLANES), :] = tile
            return c
        lax.fori_loop(0, tm, place, 0, unroll=8)

    @pl.when(j < groups_a)
    def _():
        place_from(hn_a_ref)

    @pl.when(j >= groups_a)
    def _():
        place_from(hn_b_ref)

    chunks(j, slot, False)

    @pl.when(j == nj - 1)
    def _():
        @pl.when(j >= 1)
        def _():
            chunks(j - 1, 1 - slot, True)
        chunks(j, slot, True)
        zero_fill(True)


def _dispatch(hn_a, hn_b, lpos, meta, zrow, zcnt, nused, tm, n_e, n_blocks):
    nt = lpos.shape[0] // tm
    groups_a = hn_a.shape[0] // (tm * SUBLANES)
    assert groups_a >= 1 and groups_a + hn_b.shape[0] // (tm * SUBLANES) == nt
    grid_spec = pltpu.PrefetchScalarGridSpec(
        num_scalar_prefetch=6,
        grid=(nt,),
        in_specs=[pl.BlockSpec((1, 1, tm * TOP_K), lambda j, *_: (j, 0, 0), memory_space=pltpu.SMEM),
                  pl.BlockSpec((tm * SUBLANES, LANES), lambda j, *_: (jnp.minimum(j, groups_a - 1), 0)),
                  pl.BlockSpec((tm * SUBLANES, LANES), lambda j, *_: (jnp.maximum(j - groups_a, 0), 0))],
        out_specs=pl.BlockSpec(memory_space=pl.ANY),
        scratch_shapes=[pltpu.VMEM((2 * tm * TOP_K * SUBLANES, LANES), F32), pltpu.VMEM((BM * SUBLANES, LANES), F32),
                        pltpu.SemaphoreType.DMA((2,)), pltpu.SemaphoreType.DMA(())],
    )
    return pl.pallas_call(
        functools.partial(_dispatch_kernel, n_e=n_e, n_blocks=n_blocks, groups_a=groups_a),
        grid_spec=grid_spec,
        out_shape=jax.ShapeDtypeStruct((n_blocks * BM * SUBLANES, LANES), F32),
        compiler_params=pltpu.CompilerParams(dimension_semantics=("arbitrary",), vmem_limit_bytes=VMEM_LIMIT),
        name="dispatch",
    )(*meta, zrow, zcnt, nused, lpos.reshape(nt, 1, tm * TOP_K), hn_a, hn_b)


def _experts_kernel(blk_e_ref, nused_ref, next_e_ref, blk_rows_ref, xs_ref, bgu_ref, bd_ref, wgu_hbm_ref, wd_hbm_ref, ys_ref,
                    wgu_ref, wd_ref, wgu_bf_ref, wd_bf_ref, wsem):
    i = pl.program_id(0)
    nused = nused_ref[0]
    d, d_ff2 = wgu_ref.shape
    grp = 2 * LANES

    def weight_copies(e):
        gu_cols = d_ff2 // W_PARTS
        dn_rows = (d_ff2 // 2) // W_PARTS
        cps = []
        for p in range(W_PARTS):
            cps.append(pltpu.make_async_copy(wgu_hbm_ref.at[e, :, pl.ds(p * gu_cols, gu_cols)],
                                             wgu_ref.at[:, pl.ds(p * gu_cols, gu_cols)], wsem.at[p]))
            cps.append(pltpu.make_async_copy(wd_hbm_ref.at[e, pl.ds(p * dn_rows, dn_rows), :],
                                             wd_ref.at[pl.ds(p * dn_rows, dn_rows), :], wsem.at[W_PARTS + p]))
        return cps

    @pl.when(i < nused)
    def _():
        e = blk_e_ref[i]
        e_prev = blk_e_ref[jnp.maximum(i - 1, 0)]

        @pl.when(i == 0)
        def _():
            for cp in weight_copies(e):
                cp.start()

        @pl.when((i == 0) | (e != e_prev))
        def _():
            for cp in weight_copies(e):
                cp.wait()
            pr = _iota((grp, grp), 0)
            pc = _iota((grp, grp), 1)
            perm = (pr == jnp.where(pc < LANES, 2 * pc, 2 * (pc - LANES) + 1)).astype(BF16)
            for g in range(d_ff2 // grp):
                w = wgu_ref[:, g * grp:(g + 1) * grp].astype(BF16)
                wgu_bf_ref[:, g * grp:(g + 1) * grp] = _dot(w, perm).astype(BF16)
            wd_bf_ref[...] = wd_ref[...].astype(BF16)

            @pl.when(next_e_ref[e] >= 0)
            def _():
                for cp in weight_copies(next_e_ref[e]):
                    cp.start()

        def ffn(r0, n_rows):
            x = jnp.concatenate([_load_token_tiles(xs_ref, n_rows, s, base=r0 * SUBLANES).astype(BF16)
                                 for s in range(SUBLANES)], axis=1)
            hgu = _dot(x, wgu_bf_ref[...]) + bgu_ref[0]
            acts = []
            for g in range(d_ff2 // grp):
                gate = jnp.minimum(hgu[:, g * grp:g * grp + LANES], SWIGLU_LIMIT)
                up = jnp.clip(hgu[:, g * grp + LANES:(g + 1) * grp], -SWIGLU_LIMIT, SWIGLU_LIMIT)
                glu = gate * (1.0 / (1.0 + jnp.exp(-SWIGLU_ALPHA * gate)))
                acts.append(((up + 1.0) * glu).astype(BF16))
            act = jnp.concatenate(acts, axis=1)
            _store_token_tiles(ys_ref, _dot(act, wd_bf_ref[...]) + bd_ref[0], base=r0 * SUBLANES)

        rows = blk_rows_ref[i]

        @pl.when(rows == BM)
        def _():
            ffn(0, BM)

        @pl.when(rows < BM)
        def _():
            for r0 in range(0, BM, TAIL_CHUNK):
                @pl.when(r0 < rows)
                def _():
                    ffn(r0, TAIL_CHUNK)

                @pl.when(r0 >= rows)
                def _():
                    ys_ref[r0 * SUBLANES:(r0 + TAIL_CHUNK) * SUBLANES, :] = jnp.zeros(
                        (TAIL_CHUNK * SUBLANES, LANES), F32)

    @pl.when(i >= nused)
    def _():
        ys_ref[...] = jnp.zeros_like(ys_ref)


def _experts(xs, blk_e, nused, next_e, blk_rows, w_gate_up, b_gu_perm, w_down, b_down):
    n_e, d, d_ff2 = w_gate_up.shape
    n_blocks = xs.shape[0] // (BM * SUBLANES)
    expert = lambda i, be, nu, *_: be[jnp.minimum(i, jnp.maximum(nu[0] - 1, 0))]
    rows_spec = pl.BlockSpec((BM * SUBLANES, LANES), lambda i, *_: (i, 0))
    grid_spec = pltpu.PrefetchScalarGridSpec(
        num_scalar_prefetch=4,
        grid=(n_blocks,),
        in_specs=[pl.BlockSpec((BM * SUBLANES, LANES), lambda i, be, nu, *_: (jnp.minimum(i, jnp.maximum(nu[0] - 1, 0)), 0)),
                  pl.BlockSpec((1, 1, d_ff2), lambda i, *s: (expert(i, *s), 0, 0)),
                  pl.BlockSpec((1, 1, d), lambda i, *s: (expert(i, *s), 0, 0)),
                  pl.BlockSpec(memory_space=pl.ANY), pl.BlockSpec(memory_space=pl.ANY)],
        out_specs=rows_spec,
        scratch_shapes=[pltpu.VMEM((d, d_ff2), F32), pltpu.VMEM((d_ff2 // 2, d), F32),
                        pltpu.VMEM((d, d_ff2), BF16), pltpu.VMEM((d_ff2 // 2, d), BF16),
                        pltpu.SemaphoreType.DMA((2 * W_PARTS,))],
    )
    return pl.pallas_call(
        _experts_kernel,
        grid_spec=grid_spec,
        out_shape=jax.ShapeDtypeStruct(xs.shape, F32),
        compiler_params=pltpu.CompilerParams(dimension_semantics=("arbitrary",), vmem_limit_bytes=VMEM_LIMIT),
        name="experts",
    )(blk_e, nused, next_e, blk_rows, xs, b_gu_perm, b_down, w_gate_up, w_down)


def _combine_kernel(off_ref, cnt_ref, lst_ref, lpos_ref, gates_ref, h_ref, g_ref, ys_ref, out_ref,
                    stage_ref, acc_ref, sem, *, n_e, g0):
    i = pl.program_id(0)
    ng = pl.num_programs(0)
    tm, d = h_ref.shape
    slot = (g0 + i) % 2
    meta = (off_ref, cnt_ref, lst_ref)

    def chunks(g, s, wait):
        _group_chunks(meta, g, n_e, tm,
                      lambda lrow, grow, size: pltpu.make_async_copy(
                          _rows(ys_ref, grow, size), _rows(stage_ref, s * (tm * TOP_K) + lrow, size),
                          sem.at[s]), wait)

    @pl.when(i == 0)
    def _():
        chunks(g0, slot, False)

    @pl.when(i + 1 < ng)
    def _():
        chunks(g0 + i + 1, 1 - slot, False)

    chunks(g0 + i, slot, True)

    def mix(t, c):
        acc = None
        for k in range(TOP_K):
            pos = pl.multiple_of(lpos_ref[0, 0, t * TOP_K + k], SUBLANES)
            term = stage_ref[pl.ds(pos, SUBLANES), :] * gates_ref[0, 0, t * TOP_K + k]
            acc = term if acc is None else acc + term
        acc_ref[pl.ds(pl.multiple_of(t * SUBLANES, SUBLANES), SUBLANES), :] = acc
        return c

    lax.fori_loop(0, tm, mix, 0, unroll=8)

    h = h_ref[...]
    parts = []
    sq = jnp.zeros((tm, LANES), F32)
    for s in range(d // LANES):
        y = h[:, s * LANES:(s + 1) * LANES] + _load_token_tiles(acc_ref, tm, s)
        sq = sq + y * y
        parts.append(y)
    rinv = lax.rsqrt(jnp.sum(sq, axis=-1, keepdims=True) / d + EPS)
    out_ref[...] = jnp.concatenate(parts, axis=1) * rinv * g_ref[...]


def _combine(ys, h, lpos, gates, meta, g_final, tm, n_e, g0):
    n, d = h.shape
    grid_spec = pltpu.PrefetchScalarGridSpec(
        num_scalar_prefetch=3,
        grid=(n // tm,),
        in_specs=[pl.BlockSpec((1, 1, tm * TOP_K), lambda i, *_: (g0 + i, 0, 0), memory_space=pltpu.SMEM),
                  pl.BlockSpec((1, 1, tm * TOP_K), lambda i, *_: (g0 + i, 0, 0), memory_space=pltpu.SMEM),
                  pl.BlockSpec((tm, d), lambda i, *_: (i, 0)),
                  pl.BlockSpec((1, d), lambda i, *_: (0, 0)),
                  pl.BlockSpec(memory_space=pl.ANY)],
        out_specs=pl.BlockSpec((tm, d), lambda i, *_: (i, 0)),
        scratch_shapes=[pltpu.VMEM((2 * tm * TOP_K * SUBLANES, LANES), F32), pltpu.VMEM((tm * SUBLANES, LANES), F32),
                        pltpu.SemaphoreType.DMA((2,))],
    )
    ngroups = lpos.shape[0] // tm
    return pl.pallas_call(
        functools.partial(_combine_kernel, n_e=n_e, g0=g0),
        grid_spec=grid_spec,
        out_shape=jax.ShapeDtypeStruct((n, d), F32),
        compiler_params=pltpu.CompilerParams(dimension_semantics=("arbitrary",), vmem_limit_bytes=VMEM_LIMIT),
        name="combine",
    )(*meta, lpos.reshape(ngroups, 1, tm * TOP_K), gates.reshape(ngroups, 1, tm * TOP_K), h, g_final, ys)


def _largest_tile(cands, *sizes):
    for c in cands:
        if all(s % c == 0 for s in sizes):
            return c
    raise ValueError(f"no tile in {cands} divides {sizes}")


def kernel(x_prompt, x_sample, mem_prompt, cache_win_k, cache_win_v, state_conv, cache_mem_k, cache_mem_v, g_attn_norm, w_in, conv_w, attn_sinks, g_mem_norm, w_mem_kv, g_mix_out, w_out, g_ffn_norm, w_router, b_router, w_gate_up, b_gate_up, w_down, b_down, g_final):
    depth = w_in.shape[0]
    assert depth == 1, "single-layer step"
    b, s, d = x_prompt.shape
    nb, t_dec, _ = x_sample.shape
    n_e = w_router.shape[2]
    d_ff2 = w_gate_up.shape[3]
    cw = conv_w.shape[2]
    win = cache_win_k.shape[2]
    m_tok = cache_mem_k.shape[2]
    assert s % TQ == 0 and nb % BB == 0 and win == WINDOW and t_dec <= SUBLANES and d_ff2 % (2 * LANES) == 0
    assert d == SUBLANES * LANES, "token-tile layout: one token is one (8, 128) f32 tile"

    row = lambda a: a.reshape(1, -1)
    w_in_bf = w_in[0].astype(BF16)
    w_out_bf = w_out[0].astype(BF16)
    w_r_bf = w_router[0].astype(BF16)
    sinks = attn_sinks[0].astype(F32)
    shared = (row(g_attn_norm[0]), w_in_bf, conv_w[0], row(g_mix_out[0]), w_out_bf, row(g_ffn_norm[0]),
              w_r_bf, row(b_router[0]))

    mk_p, mv_p, mkt, mvb = _memkv(mem_prompt, row(g_mem_norm[0]), w_mem_kv[0].astype(BF16))
    h_p, hn_p, tope_p, gates_p, lastk, lastv, convst = _mixer_p(x_prompt, sinks, *shared, mkt, mvb)

    zeros = lambda r: jnp.zeros((nb, r, cw), F32)
    st = state_conv[0]
    pm1 = jnp.concatenate([st[:, 1:2], zeros(t_dec - 1)], axis=1).reshape(nb * t_dec, cw)
    pm2 = jnp.concatenate([st, zeros(t_dec - 2)], axis=1).reshape(nb * t_dec, cw)
    h_s, hn_s, tope_s, gates_s, nwk, nwv, u_s = _mixer_s(
        x_sample.reshape(nb * t_dec, d), t_dec, pm1, pm2,
        cache_win_k[0].reshape(nb, win, KV_WIDTH), cache_win_v[0].reshape(nb, win, KV_WIDTH),
        cache_mem_k[0].reshape(nb, m_tok, MEM_WIDTH), cache_mem_v[0].reshape(nb, m_tok, MEM_WIDTH),
        sinks, *shared)

    n_p, n_s = b * s, nb * t_dec
    n = n_p + n_s
    tm = _largest_tile((512, 256, 128, 64, 32, 16, 8), n_p, n_s)
    tope = jnp.concatenate([tope_p, tope_s], axis=0)
    gates = jnp.concatenate([gates_p, gates_s], axis=0)

    lpos, cnt_f = _lpos(tope, tm, n_e)
    cnt = cnt_f[:, 0, :].astype(jnp.int32)
    counts = jnp.sum(cnt, axis=0)
    padded = (counts + BM - 1) // BM * BM
    pad_ends = jnp.cumsum(padded)
    pad_starts = pad_ends - padded
    nk = n * TOP_K
    n_blocks = -(-nk // BM) + n_e
    nused = (pad_ends[-1:] // BM).astype(jnp.int32)
    blk_start = jnp.arange(n_blocks, dtype=jnp.int32) * BM
    blk_e = jnp.minimum(jnp.sum((pad_ends[None, :] <= blk_start[:, None]).astype(jnp.int32), axis=1), n_e - 1)
    zrow = (pad_starts + counts).astype(jnp.int32)
    zcnt = (padded - counts).astype(jnp.int32)
    off = pad_starts[None, :] + jnp.cumsum(cnt, axis=0) - cnt
    lstart = jnp.cumsum(cnt, axis=1) - cnt
    meta = (off.reshape(-1).astype(jnp.int32), cnt.reshape(-1), lstart.reshape(-1).astype(jnp.int32))

    xs = _dispatch(hn_p, hn_s, lpos, meta, zrow, zcnt, nused, tm, n_e, n_blocks)

    grp = 2 * LANES
    b_gu = b_gate_up[0].reshape(n_e, d_ff2 // grp, LANES, 2).transpose(0, 1, 3, 2).reshape(n_e, 1, d_ff2)
    owner = jnp.where(padded > 0, jnp.arange(n_e, dtype=jnp.int32), n_e)
    following = jnp.concatenate([lax.cummin(owner, reverse=True)[1:], jnp.full((1,), n_e, jnp.int32)])
    next_e = jnp.where(following < n_e, following, -1).astype(jnp.int32)
    own = blk_e[:, None] == jnp.arange(n_e, dtype=jnp.int32)[None, :]
    blk_rows = jnp.clip(jnp.sum(jnp.where(own, (pad_starts + counts)[None, :], 0), axis=1) - blk_start, 0, BM)
    blk_rows = blk_rows.astype(jnp.int32)
    ys = _experts(xs, blk_e, nused, next_e, blk_rows, w_gate_up[0], b_gu, w_down[0], b_down[0].reshape(n_e, 1, d))

    g_fin = row(g_final)
    y_p = _combine(ys, h_p, lpos, gates, meta, g_fin, tm, n_e, 0)
    y_s = _combine(ys, h_s, lpos, gates, meta, g_fin, tm, n_e, n_p // tm)

    kv5 = lambda a, bsz, r, hds: a.reshape(1, bsz, r, hds, HEAD_DIM)
    return (y_p.reshape(b, s, d), y_s.reshape(nb, t_dec, d),
            kv5(lastk, b, WINDOW, N_KV_HEADS), kv5(lastv, b, WINDOW, N_KV_HEADS),
            convst.reshape(1, b, 2, cw),
            kv5(mk_p, b, m_tok, N_MEM_HEADS), kv5(mv_p, b, m_tok, N_MEM_HEADS),
            kv5(nwk, nb, win, N_KV_HEADS), kv5(nwv, nb, win, N_KV_HEADS),
            u_s.reshape(nb, t_dec, cw)[:, t_dec - 2:].reshape(1, nb, 2, cw))
```

```python
import functools

import jax
import jax.numpy as jnp
from jax import lax
from jax.experimental import pallas as pl
from jax.experimental.pallas import tpu as pltpu

F32 = jnp.float32
BF16 = jnp.bfloat16

HEAD_DIM = 64
N_Q_HEADS = 8
N_KV_HEADS = 2
WINDOW = 128
ATTN_WIDTH = N_Q_HEADS * HEAD_DIM
KV_WIDTH = N_KV_HEADS * HEAD_DIM
N_MEM_HEADS = 4
MEM_WIDTH = N_MEM_HEADS * HEAD_DIM
TOP_K = 4
SWIGLU_LIMIT = 7.0
SWIGLU_ALPHA = 1.702
EPS = 1e-5
ATTN_SCALE = HEAD_DIM ** -0.5
ALIBI_SLOPES = tuple(2.0 ** (-8.0 * (h + 1) / N_Q_HEADS) for h in range(N_Q_HEADS))

LANES = 128
SUBLANES = 8
VMEM_LIMIT = 56 * 1024 * 1024

TQ = 1024
BB = 16
BM = 512
REP = 8
W_PARTS = 4
MIX_CHUNK = 256
IN_CHUNK = 256
TAIL_CHUNK = 128


def _rms(x):
    return x * lax.rsqrt(jnp.mean(x * x, axis=-1, keepdims=True) + EPS)


def _dot(a, b):
    return jnp.dot(a, b, preferred_element_type=F32)


def _dot_nt(a, b):
    return lax.dot_general(a, b, (((1,), (1,)), ((), ())), preferred_element_type=F32)


def _iota(shape, axis):
    return lax.broadcasted_iota(jnp.int32, shape, axis)


def _store_token_tiles(ref, x, base=0):
    t = x.shape[0]
    for s in range(x.shape[1] // LANES):
        ref[pl.ds(base + s, t, stride=SUBLANES), :] = x[:, s * LANES:(s + 1) * LANES]


def _load_token_tiles(ref, t, s, base=0):
    return ref[pl.ds(base + s, t, stride=SUBLANES), :]


def _memkv_kernel(mem_ref, g_ref, w_ref, mk_ref, mv_ref, mkt_ref, mvb_ref):
    xn = (_rms(mem_ref[0]) * g_ref[...]).astype(BF16)
    kv = _dot(xn, w_ref[...])
    mk = kv[:, :MEM_WIDTH]
    mv = kv[:, MEM_WIDTH:]
    mk_ref[0] = mk
    mv_ref[0] = mv
    mkt_ref[0] = mk.T.astype(BF16)
    mvb_ref[0] = mv.astype(BF16)


def _memkv(mem, g, w_bf):
    b, m, d = mem.shape
    out_f = jax.ShapeDtypeStruct((b, m, MEM_WIDTH), F32)
    out_b = jax.ShapeDtypeStruct((b, m, MEM_WIDTH), BF16)
    out_t = jax.ShapeDtypeStruct((b, MEM_WIDTH, m), BF16)
    blk = lambda r, c: pl.BlockSpec((1, r, c), lambda i: (i, 0, 0))
    return pl.pallas_call(
        _memkv_kernel,
        grid=(b,),
        in_specs=[blk(m, d), pl.BlockSpec((1, d), lambda i: (0, 0)),
                  pl.BlockSpec((d, 2 * MEM_WIDTH), lambda i: (0, 0))],
        out_specs=[blk(m, MEM_WIDTH), blk(m, MEM_WIDTH), blk(MEM_WIDTH, m), blk(m, MEM_WIDTH)],
        out_shape=[out_f, out_f, out_t, out_b],
        name="memkv",
    )(mem, g, w_bf)


def _router_topk(hn, w_r_ref, b_r_ref, tope_ref, gates_ref, rows_at):
    n_e = w_r_ref.shape[1]
    logits = _dot(hn.astype(BF16), w_r_ref[...]) + b_r_ref[...]
    rows = logits.shape[0]
    col = _iota((rows, n_e), 1).astype(F32)
    vals, idxs = [], []
    cur = logits
    for _ in range(TOP_K):
        m = jnp.max(cur, axis=-1, keepdims=True)
        idx = jnp.min(jnp.where(cur == m, col, float(n_e)), axis=-1, keepdims=True)
        vals.append(m)
        idxs.append(idx)
        cur = jnp.where(col == idx, -jnp.inf, cur)
    exps = [jnp.exp(v - vals[0]) for v in vals]
    tot = exps[0] + exps[1] + exps[2] + exps[3]
    col4 = _iota((rows, TOP_K), 1)
    te = jnp.zeros((rows, TOP_K), F32)
    ga = jnp.zeros((rows, TOP_K), F32)
    for k in range(TOP_K):
        te = jnp.where(col4 == k, idxs[k], te)
        ga = jnp.where(col4 == k, exps[k] / tot, ga)
    tope_ref[rows_at, :] = te.astype(jnp.int32)
    gates_ref[rows_at, :] = ga


def _mix_out(x, attn, conv_out, cross, g_mix_ref, w_out_ref, g_ffn_ref, w_r_ref, b_r_ref,
             h_ref, hn_ref, tope_ref, gates_ref):
    rows = x.shape[0]
    chunk = MIX_CHUNK if rows % MIX_CHUNK == 0 else rows
    for r0 in range(0, rows, chunk):
        at = slice(r0, r0 + chunk)
        mix = jnp.concatenate([_rms(attn[at]), _rms(conv_out[at]), _rms(cross[at])], axis=-1) * g_mix_ref[...]
        h = x[at] + _dot(mix.astype(BF16), w_out_ref[...])
        hn = _rms(h) * g_ffn_ref[...]
        h_ref[at, :] = h
        _store_token_tiles(hn_ref, hn, base=r0 * SUBLANES)
        _router_topk(hn, w_r_ref, b_r_ref, tope_ref, gates_ref, at)


def _swa_bias(prev_lim):
    blk = WINDOW
    qi = _iota((blk, 2 * blk), 0)
    kj = _iota((blk, 2 * blk), 1)
    dist = blk + qi - kj
    mask = (dist >= 0) & (dist < WINDOW) & (kj >= prev_lim)
    distf = dist.astype(F32)
    return [jnp.where(mask, -ALIBI_SLOPES[h] * distf, -jnp.inf) for h in range(N_Q_HEADS)]


def _swa_block(q_blk, kk, vv, bias, sinks_ref):
    blk = WINDOW
    lane = _iota((2 * blk, KV_WIDTH), 1)
    lo = lane < HEAD_DIM
    kk_r = pltpu.roll(kk, HEAD_DIM, axis=1)
    vv_r = pltpu.roll(vv, HEAD_DIM, axis=1)
    kdup = [jnp.where(lo, kk, kk_r).astype(BF16), jnp.where(lo, kk_r, kk).astype(BF16)]
    vlo = [jnp.where(lo, vv, 0.0).astype(BF16), jnp.where(lo, vv_r, 0.0).astype(BF16)]
    vhi = [jnp.where(lo, 0.0, vv_r).astype(BF16), jnp.where(lo, 0.0, vv).astype(BF16)]
    qlo = _iota((blk, 2 * HEAD_DIM), 1) < HEAD_DIM
    outs = []
    for p in range(N_Q_HEADS // 2):
        kh = (2 * p) // (N_Q_HEADS // N_KV_HEADS)
        qp = q_blk[:, p * 2 * HEAD_DIM:(p + 1) * 2 * HEAD_DIM]
        acc = None
        for e in range(2):
            h = 2 * p + e
            qm = jnp.where(qlo if e == 0 else jnp.logical_not(qlo), qp, 0.0).astype(BF16)
            s = _dot_nt(qm, kdup[kh]) + bias[h]
            sink = sinks_ref[h]
            m = jnp.maximum(jnp.max(s, axis=-1, keepdims=True), sink)
            pe = jnp.exp(s - m)
            denom = jnp.sum(pe, axis=-1, keepdims=True) + jnp.exp(sink - m)
            o = _dot(pe.astype(BF16), (vlo if e == 0 else vhi)[kh]) / denom
            acc = o if acc is None else acc + o
        outs.append(acc)
    return jnp.concatenate(outs, axis=1)


def _mem_attend_shared(mq, mkt, mvb):
    t = mq.shape[0]
    m_tok = mvb.shape[0]
    qhead = _iota((t, MEM_WIDTH), 1) // HEAD_DIM
    vhead = _iota((m_tok, MEM_WIDTH), 1) // HEAD_DIM
    cross = None
    for h in range(N_MEM_HEADS):
        qm = jnp.where(qhead == h, mq, 0.0).astype(BF16)
        s = _dot(qm, mkt)
        m = jnp.max(s, axis=-1, keepdims=True)
        pe = jnp.exp(s - m)
        denom = jnp.sum(pe, axis=-1, keepdims=True)
        vm = jnp.where(vhead == h, mvb, jnp.zeros_like(mvb))
        o = _dot(pe.astype(BF16), vm) / denom
        cross = o if cross is None else cross + o
    return cross


def _mixer_p_kernel(sinks_ref, x_ref, g_attn_ref, w_in_ref, conv_w_ref, g_mix_ref, w_out_ref, g_ffn_ref,
                    w_r_ref, b_r_ref, mkt_ref, mvb_ref,
                    h_ref, hn_ref, tope_ref, gates_ref, lastk_ref, lastv_ref, convst_ref,
                    ck_ref, cv_ref, cu_ref):
    j = pl.program_id(1)
    nj = pl.num_programs(1)

    @pl.when(j == 0)
    def _():
        ck_ref[...] = jnp.zeros_like(ck_ref)
        cv_ref[...] = jnp.zeros_like(cv_ref)
        cu_ref[...] = jnp.zeros_like(cu_ref)

    x = x_ref[0]
    z = jnp.concatenate([_dot((_rms(x[r0:r0 + IN_CHUNK]) * g_attn_ref[...]).astype(BF16), w_in_ref[...])
                         for r0 in range(0, TQ, IN_CHUNK)], axis=0)
    c0 = ATTN_WIDTH
    c1 = c0 + KV_WIDTH
    c2 = c1 + KV_WIDTH
    cw = conv_w_ref.shape[1]
    c3, c4, c5 = c2 + cw, c2 + 2 * cw, c2 + 3 * cw
    q = z[:, :c0] * ATTN_SCALE
    k = z[:, c0:c1]
    v = z[:, c1:c2]
    cb = z[:, c2:c3]
    cc = z[:, c3:c4]
    cvv = z[:, c4:c5]
    mq = z[:, c5:] * ATTN_SCALE

    blk = WINDOW
    attn_blocks = []
    bias_inner = _swa_bias(0)
    for i in range(TQ // blk):
        if i == 0:
            pk, pv = ck_ref[...], cv_ref[...]
            bias = _swa_bias(jnp.where(j > 0, 0, blk))
        else:
            pk, pv = k[(i - 1) * blk:i * blk], v[(i - 1) * blk:i * blk]
            bias = bias_inner
        kk = jnp.concatenate([pk, k[i * blk:(i + 1) * blk]], axis=0)
        vv = jnp.concatenate([pv, v[i * blk:(i + 1) * blk]], axis=0)
        attn_blocks.append(_swa_block(q[i * blk:(i + 1) * blk], kk, vv, bias, sinks_ref))
    attn = jnp.concatenate(attn_blocks, axis=0)
    ck_ref[...] = k[TQ - blk:]
    cv_ref[...] = v[TQ - blk:]

    u = cc * cvv
    row = _iota(u.shape, 0)
    u1 = jnp.where(row == 0, cu_ref[SUBLANES - 1:SUBLANES, :], pltpu.roll(u, 1, axis=0))
    u2 = jnp.where(row == 0, cu_ref[SUBLANES - 2:SUBLANES - 1, :],
                   jnp.where(row == 1, cu_ref[SUBLANES - 1:SUBLANES, :], pltpu.roll(u, 2, axis=0)))
    conv_out = cb * (conv_w_ref[0:1, :] * u2 + conv_w_ref[1:2, :] * u1 + conv_w_ref[2:3, :] * u)
    cu_ref[...] = u[TQ - SUBLANES:]

    cross = _mem_attend_shared(mq, mkt_ref[0], mvb_ref[0])

    @pl.when(j == nj - 1)
    def _():
        lastk_ref[0] = k[TQ - blk:]
        lastv_ref[0] = v[TQ - blk:]
        convst_ref[0] = u[TQ - 2:]

    _mix_out(x, attn, conv_out, cross, g_mix_ref, w_out_ref, g_ffn_ref, w_r_ref, b_r_ref,
             h_ref, hn_ref, tope_ref, gates_ref)


def _mixer_p(x, sinks, g_attn, w_in, conv_w, g_mix, w_out, g_ffn, w_r, b_r, mkt, mvb):
    b, s, d = x.shape
    nj = s // TQ
    n = b * s
    cw = conv_w.shape[1]
    full = lambda a: pl.BlockSpec(a.shape, lambda bi, ji, *_: (0,) * a.ndim)
    tok = lambda w: pl.BlockSpec((TQ, w), lambda bi, ji, *_: (bi * nj + ji, 0))
    per_b = lambda r, c: pl.BlockSpec((1, r, c), lambda bi, ji, *_: (bi, 0, 0))
    grid_spec = pltpu.PrefetchScalarGridSpec(
        num_scalar_prefetch=1,
        grid=(b, nj),
        in_specs=[pl.BlockSpec((1, TQ, d), lambda bi, ji, *_: (bi, ji, 0)),
                  full(g_attn), full(w_in), full(conv_w), full(g_mix), full(w_out), full(g_ffn),
                  full(w_r), full(b_r), per_b(MEM_WIDTH, mkt.shape[2]), per_b(mvb.shape[1], MEM_WIDTH)],
        out_specs=[tok(d), pl.BlockSpec((TQ * SUBLANES, LANES), lambda bi, ji, *_: (bi * nj + ji, 0)),
                   tok(TOP_K), tok(TOP_K),
                   per_b(WINDOW, KV_WIDTH), per_b(WINDOW, KV_WIDTH), per_b(2, cw)],
        scratch_shapes=[pltpu.VMEM((WINDOW, KV_WIDTH), F32), pltpu.VMEM((WINDOW, KV_WIDTH), F32),
                        pltpu.VMEM((SUBLANES, cw), F32)],
    )
    return pl.pallas_call(
        _mixer_p_kernel,
        grid_spec=grid_spec,
        out_shape=[jax.ShapeDtypeStruct((n, d), F32), jax.ShapeDtypeStruct((n * SUBLANES, LANES), F32),
                   jax.ShapeDtypeStruct((n, TOP_K), jnp.int32), jax.ShapeDtypeStruct((n, TOP_K), F32),
                   jax.ShapeDtypeStruct((b, WINDOW, KV_WIDTH), F32),
                   jax.ShapeDtypeStruct((b, WINDOW, KV_WIDTH), F32),
                   jax.ShapeDtypeStruct((b, 2, cw), F32)],
        compiler_params=pltpu.CompilerParams(dimension_semantics=("arbitrary", "arbitrary"),
                                             vmem_limit_bytes=VMEM_LIMIT),
        name="mixer_p",
    )(sinks, x, g_attn, w_in, conv_w, g_mix, w_out, g_ffn, w_r, b_r, mkt, mvb)


def _per_head_column(values, hrow):
    col = jnp.zeros(hrow.shape, F32)
    for h in range(N_Q_HEADS):
        col = jnp.where(hrow == h, values[h], col)
    return col


def _mixer_s_kernel(sinks_ref, x_ref, pm1_ref, pm2_ref, wk_ref, wv_ref, mk_ref, mv_ref,
                    g_attn_ref, w_in_ref, conv_w_ref, g_mix_ref, w_out_ref, g_ffn_ref, w_r_ref, b_r_ref,
                    h_ref, hn_ref, tope_ref, gates_ref, nwk_ref, nwv_ref, u_ref, *, t_dec):
    r_tok = BB * t_dec
    r_exp = r_tok * REP
    qrows = t_dec * REP
    x = x_ref[...]
    xn = (_rms(x) * g_attn_ref[...]).astype(BF16)
    z = _dot(xn, w_in_ref[...])
    c0 = ATTN_WIDTH
    c1 = c0 + KV_WIDTH
    c2 = c1 + KV_WIDTH
    cw = conv_w_ref.shape[1]
    c3, c4, c5 = c2 + cw, c2 + 2 * cw, c2 + 3 * cw
    q = z[:, :c0] * ATTN_SCALE
    k_new = z[:, c0:c1]
    v_new = z[:, c1:c2]
    cb = z[:, c2:c3]
    cc = z[:, c3:c4]
    cvv = z[:, c4:c5]
    mq = z[:, c5:] * ATTN_SCALE
    win = wk_ref.shape[1]

    xi = _iota((KV_WIDTH, ATTN_WIDTH), 0)
    xl = _iota((KV_WIDTH, ATTN_WIDTH), 1)
    q_per_kv = N_Q_HEADS // N_KV_HEADS
    expand = (xi == (xl // (q_per_kv * HEAD_DIM)) * HEAD_DIM + xl % HEAD_DIM).astype(BF16)
    rr = _iota((r_exp, r_tok), 0)
    rc = _iota((r_exp, r_tok), 1)
    rep = (rr // REP == rc).astype(BF16)

    hrow = _iota((r_exp, 1), 0) % REP
    trow = (_iota((r_exp, 1), 0) // REP) % t_dec
    slope_col = _per_head_column(ALIBI_SLOPES, hrow)
    sink_col = _per_head_column([sinks_ref[h] for h in range(N_Q_HEADS)], hrow)

    qexp = jnp.where(hrow == _iota((r_exp, ATTN_WIDTH), 1) // HEAD_DIM, _dot(rep, q.astype(BF16)), 0.0)
    kexp = _dot(wk_ref[...].reshape(BB * win, KV_WIDTH).astype(BF16), expand).astype(BF16)
    vexp = _dot(wv_ref[...].reshape(BB * win, KV_WIDTH).astype(BF16), expand).astype(BF16)
    s = jnp.einsum("bqc,bkc->bqk", qexp.astype(BF16).reshape(BB, qrows, ATTN_WIDTH),
                   kexp.reshape(BB, win, ATTN_WIDTH), preferred_element_type=F32).reshape(r_exp, win)
    scol = _iota((r_exp, win), 1)
    s = s - slope_col * (win + trow - scol).astype(F32)
    s = jnp.where(scol > trow, s, -jnp.inf)
    knew_exp = _dot(k_new.astype(BF16), expand).astype(BF16)
    vnew_exp = _dot(v_new.astype(BF16), expand).astype(BF16)
    s_new, v_rep = [], []
    for jn in range(t_dec):
        rep_j = (rc == (rr // qrows) * t_dec + jn).astype(BF16)
        k_rep = _dot(rep_j, knew_exp)
        v_rep.append(_dot(rep_j, vnew_exp))
        sj = jnp.sum(qexp * k_rep, axis=-1, keepdims=True) - slope_col * (trow - jn).astype(F32)
        s_new.append(jnp.where(trow >= jn, sj, -jnp.inf))
    m = jnp.maximum(jnp.max(s, axis=-1, keepdims=True), sink_col)
    for sj in s_new:
        m = jnp.maximum(m, sj)
    pe = jnp.exp(s - m)
    denom = jnp.sum(pe, axis=-1, keepdims=True) + jnp.exp(sink_col - m)
    o = jnp.einsum("bqk,bkc->bqc", pe.astype(BF16).reshape(BB, qrows, win),
                   vexp.reshape(BB, win, ATTN_WIDTH), preferred_element_type=F32).reshape(r_exp, ATTN_WIDTH)
    for jn in range(t_dec):
        pj = jnp.exp(s_new[jn] - m)
        denom = denom + pj
        o = o + pj * v_rep[jn]
    o = jnp.where(hrow == _iota((r_exp, ATTN_WIDTH), 1) // HEAD_DIM, o / denom, 0.0)
    attn = jnp.sum(o.reshape(r_tok, REP, ATTN_WIDTH), axis=1)

    m_tok = mk_ref.shape[1]
    mhead = _iota((r_exp, MEM_WIDTH), 1) // HEAD_DIM
    mqexp = jnp.where(hrow == mhead, _dot(rep, mq.astype(BF16)), 0.0).astype(BF16)
    sm = jnp.einsum("bqc,bmc->bqm", mqexp.reshape(BB, qrows, MEM_WIDTH), mk_ref[...].astype(BF16),
                    preferred_element_type=F32).reshape(r_exp, m_tok)
    mm = jnp.max(sm, axis=-1, keepdims=True)
    pm = jnp.exp(sm - mm)
    dm = jnp.sum(pm, axis=-1, keepdims=True)
    om = jnp.einsum("bqm,bmc->bqc", pm.astype(BF16).reshape(BB, qrows, m_tok), mv_ref[...].astype(BF16),
                    preferred_element_type=F32).reshape(r_exp, MEM_WIDTH)
    om = jnp.where(hrow == mhead, om / dm, 0.0)
    cross = jnp.sum(om.reshape(r_tok, REP, MEM_WIDTH), axis=1)

    u = cc * cvv
    tt = _iota(u.shape, 0) % t_dec
    u1 = jnp.where(tt >= 1, pltpu.roll(u, 1, axis=0), pm1_ref[...])
    u2 = jnp.where(tt >= 2, pltpu.roll(u, 2, axis=0), pm2_ref[...])
    conv_out = cb * (conv_w_ref[0:1, :] * u2 + conv_w_ref[1:2, :] * u1 + conv_w_ref[2:3, :] * u)
    u_ref[...] = u

    nwk_ref[:, 0:win - t_dec, :] = wk_ref[:, t_dec:win, :]
    nwv_ref[:, 0:win - t_dec, :] = wv_ref[:, t_dec:win, :]
    for b in range(BB):
        nwk_ref[b, win - t_dec:win, :] = k_new[b * t_dec:(b + 1) * t_dec, :]
        nwv_ref[b, win - t_dec:win, :] = v_new[b * t_dec:(b + 1) * t_dec, :]

    _mix_out(x, attn, conv_out, cross, g_mix_ref, w_out_ref, g_ffn_ref, w_r_ref, b_r_ref,
             h_ref, hn_ref, tope_ref, gates_ref)


def _mixer_s(x2, t_dec, pm1, pm2, wk, wv, mk, mv, sinks, g_attn, w_in, conv_w, g_mix, w_out, g_ffn, w_r, b_r):
    n, d = x2.shape
    nb = wk.shape[0]
    win = wk.shape[1]
    m_tok = mk.shape[1]
    cw = conv_w.shape[1]
    r_tok = BB * t_dec
    full = lambda a: pl.BlockSpec(a.shape, lambda i, *_: (0,) * a.ndim)
    tok = lambda w: pl.BlockSpec((r_tok, w), lambda i, *_: (i, 0))
    per_b = lambda r, c: pl.BlockSpec((BB, r, c), lambda i, *_: (i, 0, 0))
    grid_spec = pltpu.PrefetchScalarGridSpec(
        num_scalar_prefetch=1,
        grid=(nb // BB,),
        in_specs=[tok(d), tok(cw), tok(cw), per_b(win, KV_WIDTH), per_b(win, KV_WIDTH),
                  per_b(m_tok, MEM_WIDTH), per_b(m_tok, MEM_WIDTH),
                  full(g_attn), full(w_in), full(conv_w), full(g_mix), full(w_out), full(g_ffn),
                  full(w_r), full(b_r)],
        out_specs=[tok(d), pl.BlockSpec((r_tok * SUBLANES, LANES), lambda i, *_: (i, 0)),
                   tok(TOP_K), tok(TOP_K), per_b(win, KV_WIDTH), per_b(win, KV_WIDTH), tok(cw)],
    )
    return pl.pallas_call(
        functools.partial(_mixer_s_kernel, t_dec=t_dec),
        grid_spec=grid_spec,
        out_shape=[jax.ShapeDtypeStruct((n, d), F32), jax.ShapeDtypeStruct((n * SUBLANES, LANES), F32),
                   jax.ShapeDtypeStruct((n, TOP_K), jnp.int32), jax.ShapeDtypeStruct((n, TOP_K), F32),
                   jax.ShapeDtypeStruct((nb, win, KV_WIDTH), F32), jax.ShapeDtypeStruct((nb, win, KV_WIDTH), F32),
                   jax.ShapeDtypeStruct((n, cw), F32)],
        compiler_params=pltpu.CompilerParams(dimension_semantics=("arbitrary",), vmem_limit_bytes=VMEM_LIMIT),
        name="mixer_s",
    )(sinks, x2, pm1, pm2, wk, wv, mk, mv, g_attn, w_in, conv_w, g_mix, w_out, g_ffn, w_r, b_r)


def _lpos_kernel(tope_ref, lpos_ref, counts_ref, tri_ref, *, n_e):
    i = pl.program_id(0)
    tm = tope_ref.shape[0]

    @pl.when(i == 0)
    def _():
        tri_ref[...] = (_iota((tm, tm), 0) > _iota((tm, tm), 1)).astype(BF16)

    te = tope_ref[...]
    col = _iota((tm, n_e), 1)
    hits = [te[:, k:k + 1] == col for k in range(TOP_K)]
    onehot = jnp.zeros((tm, n_e), F32)
    lower = jnp.zeros((tm, n_e), F32)
    for k in range(TOP_K):
        onehot = onehot + hits[k].astype(F32)
        lower = lower + (te[:, k:k + 1] < col).astype(F32)
    ahead = _dot(tri_ref[...], onehot.astype(BF16)) + jnp.sum(lower, axis=0, keepdims=True)
    col4 = _iota((tm, TOP_K), 1)
    pos = jnp.zeros((tm, TOP_K), F32)
    for k in range(TOP_K):
        pos = jnp.where(col4 == k, jnp.sum(jnp.where(hits[k], ahead, 0.0), axis=-1, keepdims=True), pos)
    half = (i % 2) * (tm * TOP_K)
    lpos_ref[...] = (pos.astype(jnp.int32) + half) * SUBLANES
    counts_ref[0] = jnp.sum(onehot, axis=0, keepdims=True)


def _lpos(tope, tm, n_e):
    n = tope.shape[0]
    return pl.pallas_call(
        functools.partial(_lpos_kernel, n_e=n_e),
        grid=(n // tm,),
        in_specs=[pl.BlockSpec((tm, TOP_K), lambda i: (i, 0))],
        out_specs=[pl.BlockSpec((tm, TOP_K), lambda i: (i, 0)), pl.BlockSpec((1, 1, n_e), lambda i: (i, 0, 0))],
        out_shape=[jax.ShapeDtypeStruct((n, TOP_K), jnp.int32), jax.ShapeDtypeStruct((n // tm, 1, n_e), F32)],
        scratch_shapes=[pltpu.VMEM((tm, tm), BF16)],
        compiler_params=pltpu.CompilerParams(dimension_semantics=("arbitrary",)),
        name="lpos",
    )(tope)


def _rows(ref, first_row, n_rows):
    return ref.at[pl.ds(pl.multiple_of(first_row * SUBLANES, SUBLANES), n_rows * SUBLANES)]


def _group_chunks(meta, g, n_e, max_rows, make_copy, wait):
    off_ref, cnt_ref, lst_ref = meta
    if wait:
        make_copy(0, 0, max_rows * TOP_K).wait()
        return

    def per_expert(e, c):
        off = off_ref[g * n_e + e]
        lst = lst_ref[g * n_e + e]
        _run_copies(cnt_ref[g * n_e + e], max_rows, lambda done, size: make_copy(lst + done, off + done, size),
                    wait=False)
        return c

    lax.fori_loop(0, n_e, per_expert, 0)


def _run_copies(n_rows, max_rows, make_copy, wait):
    done = 0
    for size in [1 << b for b in range(max_rows.bit_length() - 1, -1, -1)]:
        bit = n_rows & size

        @pl.when(bit != 0)
        def _():
            cp = make_copy(done, size)
            cp.wait() if wait else cp.start()
        done = done + bit


def _dispatch_kernel(off_ref, cnt_ref, lst_ref, zrow_ref, zcnt_ref, nused_ref, lpos_ref, hn_a_ref, hn_b_ref, xs_ref,
                     stage_ref, zbuf_ref, sem, zsem, *, n_e, n_blocks, groups_a):
    j = pl.program_id(0)
    nj = pl.num_programs(0)
    tm = hn_a_ref.shape[0] // SUBLANES
    slot = j % 2
    meta = (off_ref, cnt_ref, lst_ref)

    def chunks(g, s, wait):
        _group_chunks(meta, g, n_e, tm,
                      lambda lrow, grow, size: pltpu.make_async_copy(
                          _rows(stage_ref, s * (tm * TOP_K) + lrow, size), _rows(xs_ref, grow, size),
                          sem.at[s]), wait)

    def zero_fill(wait):
        def zero_run(first_row, n_rows):
            _run_copies(n_rows, BM, lambda done, size: pltpu.make_async_copy(
                _rows(zbuf_ref, 0, size), _rows(xs_ref, first_row + done, size), zsem), wait)

        def expert_pad(e, c):
            zero_run(zrow_ref[e], zcnt_ref[e])
            return c

        def tail_block(b, c):
            zero_run(b * BM, jnp.int32(BM))
            return c

        lax.fori_loop(0, n_e, expert_pad, 0)
        lax.fori_loop(nused_ref[0], n_blocks, tail_block, 0)

    @pl.when(j == 0)
    def _():
        zbuf_ref[...] = jnp.zeros_like(zbuf_ref)
        zero_fill(False)

    @pl.when(j >= 2)
    def _():
        chunks(j - 2, slot, True)

    def place_from(hn_ref):
        def place(t, c):
            tile = hn_ref[pl.ds(pl.multiple_of(t * SUBLANES, SUBLANES), SUBLANES), :]
            for k in range(TOP_K):
                pos = pl.multiple_of(lpos_ref[0, 0, t * TOP_K + k], SUBLANES)
                stage_ref[pl.ds(pos, SUBLANES), :] = tile
            return c
        lax.fori_loop(0, tm, place, 0, unroll=8)

    @pl.when(j < groups_a)
    def _():
        place_from(hn_a_ref)

    @pl.when(j >= groups_a)
    def _():
        place_from(hn_b_ref)

    chunks(j, slot, False)

    @pl.when(j == nj - 1)
    def _():
        @pl.when(j >= 1)
        def _():
            chunks(j - 1, 1 - slot, True)
        chunks(j, slot, True)
        zero_fill(True)


def _dispatch(hn_a, hn_b, lpos, meta, zrow, zcnt, nused, tm, n_e, n_blocks):
    nt = lpos.shape[0] // tm
    groups_a = hn_a.shape[0] // (tm * SUBLANES)
    assert groups_a >= 1 and groups_a + hn_b.shape[0] // (tm * SUBLANES) == nt
    grid_spec = pltpu.PrefetchScalarGridSpec(
        num_scalar_prefetch=6,
        grid=(nt,),
        in_specs=[pl.BlockSpec((1, 1, tm * TOP_K), lambda j, *_: (j, 0, 0), memory_space=pltpu.SMEM),
                  pl.BlockSpec((tm * SUBLANES, LANES), lambda j, *_: (jnp.minimum(j, groups_a - 1), 0)),
                  pl.BlockSpec((tm * SUBLANES, LANES), lambda j, *_: (jnp.maximum(j - groups_a, 0), 0))],
        out_specs=pl.BlockSpec(memory_space=pl.ANY),
        scratch_shapes=[pltpu.VMEM((2 * tm * TOP_K * SUBLANES, LANES), F32), pltpu.VMEM((BM * SUBLANES, LANES), F32),
                        pltpu.SemaphoreType.DMA((2,)), pltpu.SemaphoreType.DMA(())],
    )
    return pl.pallas_call(
        functools.partial(_dispatch_kernel, n_e=n_e, n_blocks=n_blocks, groups_a=groups_a),
        grid_spec=grid_spec,
        out_shape=jax.ShapeDtypeStruct((n_blocks * BM * SUBLANES, LANES), F32),
        compiler_params=pltpu.CompilerParams(dimension_semantics=("arbitrary",), vmem_limit_bytes=VMEM_LIMIT),
        name="dispatch",
    )(*meta, zrow, zcnt, nused, lpos.reshape(nt, 1, tm * TOP_K), hn_a, hn_b)


def _experts_kernel(blk_e_ref, nused_ref, next_e_ref, blk_rows_ref, xs_ref, bgu_ref, bd_ref, wgu_hbm_ref, wd_hbm_ref, ys_ref,
                    wgu_ref, wd_ref, wgu_bf_ref, wd_bf_ref, wsem):
    i = pl.program_id(0)
    nused = nused_ref[0]
    d, d_ff2 = wgu_ref.shape
    grp = 2 * LANES

    def weight_copies(e):
        gu_cols = d_ff2 // W_PARTS
        dn_rows = (d_ff2 // 2) // W_PARTS
        cps = []
        for p in range(W_PARTS):
            cps.append(pltpu.make_async_copy(wgu_hbm_ref.at[e, :, pl.ds(p * gu_cols, gu_cols)],
                                             wgu_ref.at[:, pl.ds(p * gu_cols, gu_cols)], wsem.at[p]))
            cps.append(pltpu.make_async_copy(wd_hbm_ref.at[e, pl.ds(p * dn_rows, dn_rows), :],
                                             wd_ref.at[pl.ds(p * dn_rows, dn_rows), :], wsem.at[W_PARTS + p]))
        return cps

    @pl.when(i < nused)
    def _():
        e = blk_e_ref[i]
        e_prev = blk_e_ref[jnp.maximum(i - 1, 0)]

        @pl.when(i == 0)
        def _():
            for cp in weight_copies(e):
                cp.start()

        @pl.when((i == 0) | (e != e_prev))
        def _():
            for cp in weight_copies(e):
                cp.wait()
            pr = _iota((grp, grp), 0)
            pc = _iota((grp, grp), 1)
            perm = (pr == jnp.where(pc < LANES, 2 * pc, 2 * (pc - LANES) + 1)).astype(BF16)
            for g in range(d_ff2 // grp):
                w = wgu_ref[:, g * grp:(g + 1) * grp].astype(BF16)
                wgu_bf_ref[:, g * grp:(g + 1) * grp] = _dot(w, perm).astype(BF16)
            wd_bf_ref[...] = wd_ref[...].astype(BF16)

            @pl.when(next_e_ref[e] >= 0)
            def _():
                for cp in weight_copies(next_e_ref[e]):
                    cp.start()

        def ffn(r0, n_rows):
            x = jnp.concatenate([_load_token_tiles(xs_ref, n_rows, s, base=r0 * SUBLANES).astype(BF16)
                                 for s in range(SUBLANES)], axis=1)
            hgu = _dot(x, wgu_bf_ref[...]) + bgu_ref[0]
            acts = []
            for g in range(d_ff2 // grp):
                gate = jnp.minimum(hgu[:, g * grp:g * grp + LANES], SWIGLU_LIMIT)
                up = jnp.clip(hgu[:, g * grp + LANES:(g + 1) * grp], -SWIGLU_LIMIT, SWIGLU_LIMIT)
                glu = gate * (1.0 / (1.0 + jnp.exp(-SWIGLU_ALPHA * gate)))
                acts.append(((up + 1.0) * glu).astype(BF16))
            act = jnp.concatenate(acts, axis=1)
            _store_token_tiles(ys_ref, _dot(act, wd_bf_ref[...]) + bd_ref[0], base=r0 * SUBLANES)

        rows = blk_rows_ref[i]

        @pl.when(rows == BM)
        def _():
            ffn(0, BM)

        @pl.when(rows < BM)
        def _():
            for r0 in range(0, BM, TAIL_CHUNK):
                @pl.when(r0 < rows)
                def _():
                    ffn(r0, TAIL_CHUNK)

                @pl.when(r0 >= rows)
                def _():
                    ys_ref[r0 * SUBLANES:(r0 + TAIL_CHUNK) * SUBLANES, :] = jnp.zeros(
                        (TAIL_CHUNK * SUBLANES, LANES), F32)

    @pl.when(i >= nused)
    def _():
        ys_ref[...] = jnp.zeros_like(ys_ref)


def _experts(xs, blk_e, nused, next_e, blk_rows, w_gate_up, b_gu_perm, w_down, b_down):
    n_e, d, d_ff2 = w_gate_up.shape
    n_blocks = xs.shape[0] // (BM * SUBLANES)
    expert = lambda i, be, nu, *_: be[jnp.minimum(i, jnp.maximum(nu[0] - 1, 0))]
    rows_spec = pl.BlockSpec((BM * SUBLANES, LANES), lambda i, *_: (i, 0))
    grid_spec = pltpu.PrefetchScalarGridSpec(
        num_scalar_prefetch=4,
        grid=(n_blocks,),
        in_specs=[pl.BlockSpec((BM * SUBLANES, LANES), lambda i, be, nu, *_: (jnp.minimum(i, jnp.maximum(nu[0] - 1, 0)), 0)),
                  pl.BlockSpec((1, 1, d_ff2), lambda i, *s: (expert(i, *s), 0, 0)),
                  pl.BlockSpec((1, 1, d), lambda i, *s: (expert(i, *s), 0, 0)),
                  pl.BlockSpec(memory_space=pl.ANY), pl.BlockSpec(memory_space=pl.ANY)],
        out_specs=rows_spec,
        scratch_shapes=[pltpu.VMEM((d, d_ff2), F32), pltpu.VMEM((d_ff2 // 2, d), F32),
                        pltpu.VMEM((d, d_ff2), BF16), pltpu.VMEM((d_ff2 // 2, d), BF16),
                        pltpu.SemaphoreType.DMA((2 * W_PARTS,))],
    )
    return pl.pallas_call(
        _experts_kernel,
        grid_spec=grid_spec,
        out_shape=jax.ShapeDtypeStruct(xs.shape, F32),
        compiler_params=pltpu.CompilerParams(dimension_semantics=("arbitrary",), vmem_limit_bytes=VMEM_LIMIT),
        name="experts",
    )(blk_e, nused, next_e, blk_rows, xs, b_gu_perm, b_down, w_gate_up, w_down)


def _combine_kernel(off_ref, cnt_ref, lst_ref, lpos_ref, gates_ref, h_ref, g_ref, ys_ref, out_ref,
                    stage_ref, acc_ref, sem, *, n_e, g0):
    i = pl.program_id(0)
    ng = pl.num_programs(0)
    tm, d = h_ref.shape
    slot = (g0 + i) % 2
    meta = (off_ref, cnt_ref, lst_ref)

    def chunks(g, s, wait):
        _group_chunks(meta, g, n_e, tm,
                      lambda lrow, grow, size: pltpu.make_async_copy(
                          _rows(ys_ref, grow, size), _rows(stage_ref, s * (tm * TOP_K) + lrow, size),
                          sem.at[s]), wait)

    @pl.when(i == 0)
    def _():
        chunks(g0, slot, False)

    @pl.when(i + 1 < ng)
    def _():
        chunks(g0 + i + 1, 1 - slot, False)

    chunks(g0 + i, slot, True)

    def mix(t, c):
        acc = None
        for k in range(TOP_K):
            pos = pl.multiple_of(lpos_ref[0, 0, t * TOP_K + k], SUBLANES)
            term = stage_ref[pl.ds(pos, SUBLANES), :] * gates_ref[0, 0, t * TOP_K + k]
            acc = term if acc is None else acc + term
        acc_ref[pl.ds(pl.multiple_of(t * SUBLANES, SUBLANES), SUBLANES), :] = acc
        return c

    lax.fori_loop(0, tm, mix, 0, unroll=8)

    h = h_ref[...]
    parts = []
    sq = jnp.zeros((tm, LANES), F32)
    for s in range(d // LANES):
        y = h[:, s * LANES:(s + 1) * LANES] + _load_token_tiles(acc_ref, tm, s)
        sq = sq + y * y
        parts.append(y)
    rinv = lax.rsqrt(jnp.sum(sq, axis=-1, keepdims=True) / d + EPS)
    out_ref[...] = jnp.concatenate(parts, axis=1) * rinv * g_ref[...]


def _combine(ys, h, lpos, gates, meta, g_final, tm, n_e, g0):
    n, d = h.shape
    grid_spec = pltpu.PrefetchScalarGridSpec(
        num_scalar_prefetch=3,
        grid=(n // tm,),
        in_specs=[pl.BlockSpec((1, 1, tm * TOP_K), lambda i, *_: (g0 + i, 0, 0), memory_space=pltpu.SMEM),
                  pl.BlockSpec((1, 1, tm * TOP_K), lambda i, *_: (g0 + i, 0, 0), memory_space=pltpu.SMEM),
                  pl.BlockSpec((tm, d), lambda i, *_: (i, 0)),
                  pl.BlockSpec((1, d), lambda i, *_: (0, 0)),
                  pl.BlockSpec(memory_space=pl.ANY)],
        out_specs=pl.BlockSpec((tm, d), lambda i, *_: (i, 0)),
        scratch_shapes=[pltpu.VMEM((2 * tm * TOP_K * SUBLANES, LANES), F32), pltpu.VMEM((tm * SUBLANES, LANES), F32),
                        pltpu.SemaphoreType.DMA((2,))],
    )
    ngroups = lpos.shape[0] // tm
    return pl.pallas_call(
        functools.partial(_combine_kernel, n_e=n_e, g0=g0),
        grid_spec=grid_spec,
        out_shape=jax.ShapeDtypeStruct((n, d), F32),
        compiler_params=pltpu.CompilerParams(dimension_semantics=("arbitrary",), vmem_limit_bytes=VMEM_LIMIT),
        name="combine",
    )(*meta, lpos.reshape(ngroups, 1, tm * TOP_K), gates.reshape(ngroups, 1, tm * TOP_K), h, g_final, ys)


def _largest_tile(cands, *sizes):
    for c in cands:
        if all(s % c == 0 for s in sizes):
            return c
    raise ValueError(f"no tile in {cands} divides {sizes}")


def kernel(x_prompt, x_sample, mem_prompt, cache_win_k, cache_win_v, state_conv, cache_mem_k, cache_mem_v, g_attn_norm, w_in, conv_w, attn_sinks, g_mem_norm, w_mem_kv, g_mix_out, w_out, g_ffn_norm, w_router, b_router, w_gate_up, b_gate_up, w_down, b_down, g_final):
    depth = w_in.shape[0]
    assert depth == 1, "single-layer step"
    b, s, d = x_prompt.shape
    nb, t_dec, _ = x_sample.shape
    n_e = w_router.shape[2]
    d_ff2 = w_gate_up.shape[3]
    cw = conv_w.shape[2]
    win = cache_win_k.shape[2]
    m_tok = cache_mem_k.shape[2]
    assert s % TQ == 0 and nb % BB == 0 and win == WINDOW and t_dec <= SUBLANES and d_ff2 % (2 * LANES) == 0
    assert d == SUBLANES * LANES, "token-tile layout: one token is one (8, 128) f32 tile"

    row = lambda a: a.reshape(1, -1)
    w_in_bf = w_in[0].astype(BF16)
    w_out_bf = w_out[0].astype(BF16)
    w_r_bf = w_router[0].astype(BF16)
    sinks = attn_sinks[0].astype(F32)
    shared = (row(g_attn_norm[0]), w_in_bf, conv_w[0], row(g_mix_out[0]), w_out_bf, row(g_ffn_norm[0]),
              w_r_bf, row(b_router[0]))

    mk_p, mv_p, mkt, mvb = _memkv(mem_prompt, row(g_mem_norm[0]), w_mem_kv[0].astype(BF16))
    h_p, hn_p, tope_p, gates_p, lastk, lastv, convst = _mixer_p(x_prompt, sinks, *shared, mkt, mvb)

    zeros = lambda r: jnp.zeros((nb, r, cw), F32)
    st = state_conv[0]
    pm1 = jnp.concatenate([st[:, 1:2], zeros(t_dec - 1)], axis=1).reshape(nb * t_dec, cw)
    pm2 = jnp.concatenate([st, zeros(t_dec - 2)], axis=1).reshape(nb * t_dec, cw)
    h_s, hn_s, tope_s, gates_s, nwk, nwv, u_s = _mixer_s(
        x_sample.reshape(nb * t_dec, d), t_dec, pm1, pm2,
        cache_win_k[0].reshape(nb, win, KV_WIDTH), cache_win_v[0].reshape(nb, win, KV_WIDTH),
        cache_mem_k[0].reshape(nb, m_tok, MEM_WIDTH), cache_mem_v[0].reshape(nb, m_tok, MEM_WIDTH),
        sinks, *shared)

    n_p, n_s = b * s, nb * t_dec
    n = n_p + n_s
    tm = _largest_tile((512, 256, 128, 64, 32, 16, 8), n_p, n_s)
    tope = jnp.concatenate([tope_p, tope_s], axis=0)
    gates = jnp.concatenate([gates_p, gates_s], axis=0)

    lpos, cnt_f = _lpos(tope, tm, n_e)
    cnt = cnt_f[:, 0, :].astype(jnp.int32)
    counts = jnp.sum(cnt, axis=0)
    padded = (counts + BM - 1) // BM * BM
    pad_ends = jnp.cumsum(padded)
    pad_starts = pad_ends - padded
    nk = n * TOP_K
    n_blocks = -(-nk // BM) + n_e
    nused = (pad_ends[-1:] // BM).astype(jnp.int32)
    blk_start = jnp.arange(n_blocks, dtype=jnp.int32) * BM
    blk_e = jnp.minimum(jnp.sum((pad_ends[None, :] <= blk_start[:, None]).astype(jnp.int32), axis=1), n_e - 1)
    zrow = (pad_starts + counts).astype(jnp.int32)
    zcnt = (padded - counts).astype(jnp.int32)
    off = pad_starts[None, :] + jnp.cumsum(cnt, axis=0) - cnt
    lstart = jnp.cumsum(cnt, axis=1) - cnt
    meta = (off.reshape(-1).astype(jnp.int32), cnt.reshape(-1), lstart.reshape(-1).astype(jnp.int32))

    xs = _dispatch(hn_p, hn_s, lpos, meta, zrow, zcnt, nused, tm, n_e, n_blocks)

    grp = 2 * LANES
    b_gu = b_gate_up[0].reshape(n_e, d_ff2 // grp, LANES, 2).transpose(0, 1, 3, 2).reshape(n_e, 1, d_ff2)
    owner = jnp.where(padded > 0, jnp.arange(n_e, dtype=jnp.int32), n_e)
    following = jnp.concatenate([lax.cummin(owner, reverse=True)[1:], jnp.full((1,), n_e, jnp.int32)])
    next_e = jnp.where(following < n_e, following, -1).astype(jnp.int32)
    own = blk_e[:, None] == jnp.arange(n_e, dtype=jnp.int32)[None, :]
    blk_rows = jnp.clip(jnp.sum(jnp.where(own, (pad_starts + counts)[None, :], 0), axis=1) - blk_start, 0, BM)
    blk_rows = blk_rows.astype(jnp.int32)
    ys = _experts(xs, blk_e, nused, next_e, blk_rows, w_gate_up[0], b_gu, w_down[0], b_down[0].reshape(n_e, 1, d))

    g_fin = row(g_final)
    y_p = _combine(ys, h_p, lpos, gates, meta, g_fin, tm, n_e, 0)
    y_s = _combine(ys, h_s, lpos, gates, meta, g_fin, tm, n_e, n_p // tm)

    kv5 = lambda a, bsz, r, hds: a.reshape(1, bsz, r, hds, HEAD_DIM)
    return (y_p.reshape(b, s, d), y_s.reshape(nb, t_dec, d),
            kv5(lastk, b, WINDOW, N_KV_HEADS), kv5(lastv, b, WINDOW, N_KV_HEADS),
            convst.reshape(1, b, 2, cw),
            kv5(mk_p, b, m_tok, N_MEM_HEADS), kv5(mv_p, b, m_tok, N_MEM_HEADS),
            kv5(nwk, nb, win, N_KV_HEADS), kv5(nwv, nb, win, N_KV_HEADS),
            u_s.reshape(nb, t_dec, cw)[:, t_dec - 2:].reshape(1, nb, 2, cw))
```

```python
import functools

import jax
import jax.numpy as jnp
from jax import lax
from jax.experimental import pallas as pl
from jax.experimental.pallas import tpu as pltpu

F32 = jnp.float32
BF16 = jnp.bfloat16

HEAD_DIM = 64
N_Q_HEADS = 8
N_KV_HEADS = 2
WINDOW = 128
ATTN_WIDTH = N_Q_HEADS * HEAD_DIM
KV_WIDTH = N_KV_HEADS * HEAD_DIM
N_MEM_HEADS = 4
MEM_WIDTH = N_MEM_HEADS * HEAD_DIM
TOP_K = 4
SWIGLU_LIMIT = 7.0
SWIGLU_ALPHA = 1.702
EPS = 1e-5
ATTN_SCALE = HEAD_DIM ** -0.5
ALIBI_SLOPES = tuple(2.0 ** (-8.0 * (h + 1) / N_Q_HEADS) for h in range(N_Q_HEADS))

LANES = 128
SUBLANES = 8
VMEM_LIMIT = 56 * 1024 * 1024

TQ = 1024
BB = 16
BM = 512
REP = 8
W_PARTS = 4
MIX_CHUNK = 512
IN_CHUNK = 256
TAIL_CHUNK = 128
TOKEN_UNROLL = 16


def _rms(x):
    return x * lax.rsqrt(jnp.mean(x * x, axis=-1, keepdims=True) + EPS)


def _dot(a, b):
    return jnp.dot(a, b, preferred_element_type=F32)


def _dot_nt(a, b):
    return lax.dot_general(a, b, (((1,), (1,)), ((), ())), preferred_element_type=F32)


def _iota(shape, axis):
    return lax.broadcasted_iota(jnp.int32, shape, axis)


def _store_token_tiles(ref, x, base=0):
    t = x.shape[0]
    for s in range(x.shape[1] // LANES):
        ref[pl.ds(base + s, t, stride=SUBLANES), :] = x[:, s * LANES:(s + 1) * LANES]


def _load_token_tiles(ref, t, s, base=0):
    return ref[pl.ds(base + s, t, stride=SUBLANES), :]


def _memkv_kernel(mem_ref, g_ref, w_ref, mk_ref, mv_ref, mkt_ref, mvb_ref):
    xn = (_rms(mem_ref[0]) * g_ref[...]).astype(BF16)
    kv = _dot(xn, w_ref[...])
    mk = kv[:, :MEM_WIDTH]
    mv = kv[:, MEM_WIDTH:]
    mk_ref[0] = mk
    mv_ref[0] = mv
    mkt_ref[0] = mk.T.astype(BF16)
    mvb_ref[0] = mv.astype(BF16)


def _memkv(mem, g, w_bf):
    b, m, d = mem.shape
    out_f = jax.ShapeDtypeStruct((b, m, MEM_WIDTH), F32)
    out_b = jax.ShapeDtypeStruct((b, m, MEM_WIDTH), BF16)
    out_t = jax.ShapeDtypeStruct((b, MEM_WIDTH, m), BF16)
    blk = lambda r, c: pl.BlockSpec((1, r, c), lambda i: (i, 0, 0))
    return pl.pallas_call(
        _memkv_kernel,
        grid=(b,),
        in_specs=[blk(m, d), pl.BlockSpec((1, d), lambda i: (0, 0)),
                  pl.BlockSpec((d, 2 * MEM_WIDTH), lambda i: (0, 0))],
        out_specs=[blk(m, MEM_WIDTH), blk(m, MEM_WIDTH), blk(MEM_WIDTH, m), blk(m, MEM_WIDTH)],
        out_shape=[out_f, out_f, out_t, out_b],
        name="memkv",
    )(mem, g, w_bf)


def _router_topk(hn, w_r_ref, b_r_ref, tope_ref, gates_ref, rows_at):
    n_e = w_r_ref.shape[1]
    logits = _dot(hn.astype(BF16), w_r_ref[...]) + b_r_ref[...]
    rows = logits.shape[0]
    col = _iota((rows, n_e), 1).astype(F32)
    vals, idxs = [], []
    cur = logits
    for _ in range(TOP_K):
        m = jnp.max(cur, axis=-1, keepdims=True)
        idx = jnp.min(jnp.where(cur == m, col, float(n_e)), axis=-1, keepdims=True)
        vals.append(m)
        idxs.append(idx)
        cur = jnp.where(col == idx, -jnp.inf, cur)
    exps = [jnp.exp(v - vals[0]) for v in vals]
    tot = exps[0] + exps[1] + exps[2] + exps[3]
    col4 = _iota((rows, TOP_K), 1)
    te = jnp.zeros((rows, TOP_K), F32)
    ga = jnp.zeros((rows, TOP_K), F32)
    for k in range(TOP_K):
        te = jnp.where(col4 == k, idxs[k], te)
        ga = jnp.where(col4 == k, exps[k] / tot, ga)
    tope_ref[rows_at, :] = te.astype(jnp.int32)
    gates_ref[rows_at, :] = ga


def _mix_out(x, attn, conv_out, cross, g_mix_ref, w_out_ref, g_ffn_ref, w_r_ref, b_r_ref,
             h_ref, hn_ref, tope_ref, gates_ref):
    rows = x.shape[0]
    chunk = MIX_CHUNK if rows % MIX_CHUNK == 0 else rows
    for r0 in range(0, rows, chunk):
        at = slice(r0, r0 + chunk)
        mix = jnp.concatenate([_rms(attn[at]), _rms(conv_out[at]), _rms(cross[at])], axis=-1) * g_mix_ref[...]
        h = x[at] + _dot(mix.astype(BF16), w_out_ref[...])
        hn = _rms(h) * g_ffn_ref[...]
        h_ref[at, :] = h
        _store_token_tiles(hn_ref, hn, base=r0 * SUBLANES)
        _router_topk(hn, w_r_ref, b_r_ref, tope_ref, gates_ref, at)


def _swa_bias(prev_lim):
    blk = WINDOW
    qi = _iota((blk, 2 * blk), 0)
    kj = _iota((blk, 2 * blk), 1)
    dist = blk + qi - kj
    mask = (dist >= 0) & (dist < WINDOW) & (kj >= prev_lim)
    distf = dist.astype(F32)
    return [jnp.where(mask, -ALIBI_SLOPES[h] * distf, -jnp.inf) for h in range(N_Q_HEADS)]


def _swa_block(q_blk, kk, vv, bias, sinks_ref):
    blk = WINDOW
    lane = _iota((2 * blk, KV_WIDTH), 1)
    lo = lane < HEAD_DIM
    kk_r = pltpu.roll(kk, HEAD_DIM, axis=1)
    vv_r = pltpu.roll(vv, HEAD_DIM, axis=1)
    kdup = [jnp.where(lo, kk, kk_r).astype(BF16), jnp.where(lo, kk_r, kk).astype(BF16)]
    vlo = [jnp.where(lo, vv, 0.0).astype(BF16), jnp.where(lo, vv_r, 0.0).astype(BF16)]
    vhi = [jnp.where(lo, 0.0, vv_r).astype(BF16), jnp.where(lo, 0.0, vv).astype(BF16)]
    qlo = _iota((blk, 2 * HEAD_DIM), 1) < HEAD_DIM
    outs = []
    for p in range(N_Q_HEADS // 2):
        kh = (2 * p) // (N_Q_HEADS // N_KV_HEADS)
        qp = q_blk[:, p * 2 * HEAD_DIM:(p + 1) * 2 * HEAD_DIM]
        acc = None
        for e in range(2):
            h = 2 * p + e
            qm = jnp.where(qlo if e == 0 else jnp.logical_not(qlo), qp, 0.0).astype(BF16)
            s = _dot_nt(qm, kdup[kh]) + bias[h]
            sink = sinks_ref[h]
            m = jnp.maximum(jnp.max(s, axis=-1, keepdims=True), sink)
            pe = jnp.exp(s - m)
            denom = jnp.sum(pe, axis=-1, keepdims=True) + jnp.exp(sink - m)
            o = _dot(pe.astype(BF16), (vlo if e == 0 else vhi)[kh]) / denom
            acc = o if acc is None else acc + o
        outs.append(acc)
    return jnp.concatenate(outs, axis=1)


def _mem_attend_shared(mq, mkt, mvb):
    t = mq.shape[0]
    m_tok = mvb.shape[0]
    qhead = _iota((t, MEM_WIDTH), 1) // HEAD_DIM
    vhead = _iota((m_tok, MEM_WIDTH), 1) // HEAD_DIM
    cross = None
    for h in range(N_MEM_HEADS):
        qm = jnp.where(qhead == h, mq, 0.0).astype(BF16)
        s = _dot(qm, mkt)
        m = jnp.max(s, axis=-1, keepdims=True)
        pe = jnp.exp(s - m)
        denom = jnp.sum(pe, axis=-1, keepdims=True)
        vm = jnp.where(vhead == h, mvb, jnp.zeros_like(mvb))
        o = _dot(pe.astype(BF16), vm) / denom
        cross = o if cross is None else cross + o
    return cross


def _mixer_p_kernel(sinks_ref, x_ref, g_attn_ref, w_in_ref, conv_w_ref, g_mix_ref, w_out_ref, g_ffn_ref,
                    w_r_ref, b_r_ref, mkt_ref, mvb_ref,
                    h_ref, hn_ref, tope_ref, gates_ref, lastk_ref, lastv_ref, convst_ref,
                    ck_ref, cv_ref, cu_ref):
    j = pl.program_id(1)
    nj = pl.num_programs(1)

    @pl.when(j == 0)
    def _():
        ck_ref[...] = jnp.zeros_like(ck_ref)
        cv_ref[...] = jnp.zeros_like(cv_ref)
        cu_ref[...] = jnp.zeros_like(cu_ref)

    x = x_ref[0]
    z = jnp.concatenate([_dot((_rms(x[r0:r0 + IN_CHUNK]) * g_attn_ref[...]).astype(BF16), w_in_ref[...])
                         for r0 in range(0, TQ, IN_CHUNK)], axis=0)
    c0 = ATTN_WIDTH
    c1 = c0 + KV_WIDTH
    c2 = c1 + KV_WIDTH
    cw = conv_w_ref.shape[1]
    c3, c4, c5 = c2 + cw, c2 + 2 * cw, c2 + 3 * cw
    q = z[:, :c0] * ATTN_SCALE
    k = z[:, c0:c1]
    v = z[:, c1:c2]
    cb = z[:, c2:c3]
    cc = z[:, c3:c4]
    cvv = z[:, c4:c5]
    mq = z[:, c5:] * ATTN_SCALE

    blk = WINDOW
    attn_blocks = []
    bias_inner = _swa_bias(0)
    for i in range(TQ // blk):
        if i == 0:
            pk, pv = ck_ref[...], cv_ref[...]
            bias = _swa_bias(jnp.where(j > 0, 0, blk))
        else:
            pk, pv = k[(i - 1) * blk:i * blk], v[(i - 1) * blk:i * blk]
            bias = bias_inner
        kk = jnp.concatenate([pk, k[i * blk:(i + 1) * blk]], axis=0)
        vv = jnp.concatenate([pv, v[i * blk:(i + 1) * blk]], axis=0)
        attn_blocks.append(_swa_block(q[i * blk:(i + 1) * blk], kk, vv, bias, sinks_ref))
    attn = jnp.concatenate(attn_blocks, axis=0)
    ck_ref[...] = k[TQ - blk:]
    cv_ref[...] = v[TQ - blk:]

    u = cc * cvv
    row = _iota(u.shape, 0)
    u1 = jnp.where(row == 0, cu_ref[SUBLANES - 1:SUBLANES, :], pltpu.roll(u, 1, axis=0))
    u2 = jnp.where(row == 0, cu_ref[SUBLANES - 2:SUBLANES - 1, :],
                   jnp.where(row == 1, cu_ref[SUBLANES - 1:SUBLANES, :], pltpu.roll(u, 2, axis=0)))
    conv_out = cb * (conv_w_ref[0:1, :] * u2 + conv_w_ref[1:2, :] * u1 + conv_w_ref[2:3, :] * u)
    cu_ref[...] = u[TQ - SUBLANES:]

    cross = _mem_attend_shared(mq, mkt_ref[0], mvb_ref[0])

    @pl.when(j == nj - 1)
    def _():
        lastk_ref[0] = k[TQ - blk:]
        lastv_ref[0] = v[TQ - blk:]
        convst_ref[0] = u[TQ - 2:]

    _mix_out(x, attn, conv_out, cross, g_mix_ref, w_out_ref, g_ffn_ref, w_r_ref, b_r_ref,
             h_ref, hn_ref, tope_ref, gates_ref)


def _mixer_p(x, sinks, g_attn, w_in, conv_w, g_mix, w_out, g_ffn, w_r, b_r, mkt, mvb):
    b, s, d = x.shape
    nj = s // TQ
    n = b * s
    cw = conv_w.shape[1]
    full = lambda a: pl.BlockSpec(a.shape, lambda bi, ji, *_: (0,) * a.ndim)
    tok = lambda w: pl.BlockSpec((TQ, w), lambda bi, ji, *_: (bi * nj + ji, 0))
    per_b = lambda r, c: pl.BlockSpec((1, r, c), lambda bi, ji, *_: (bi, 0, 0))
    grid_spec = pltpu.PrefetchScalarGridSpec(
        num_scalar_prefetch=1,
        grid=(b, nj),
        in_specs=[pl.BlockSpec((1, TQ, d), lambda bi, ji, *_: (bi, ji, 0)),
                  full(g_attn), full(w_in), full(conv_w), full(g_mix), full(w_out), full(g_ffn),
                  full(w_r), full(b_r), per_b(MEM_WIDTH, mkt.shape[2]), per_b(mvb.shape[1], MEM_WIDTH)],
        out_specs=[tok(d), pl.BlockSpec((TQ * SUBLANES, LANES), lambda bi, ji, *_: (bi * nj + ji, 0)),
                   tok(TOP_K), tok(TOP_K),
                   per_b(WINDOW, KV_WIDTH), per_b(WINDOW, KV_WIDTH), per_b(2, cw)],
        scratch_shapes=[pltpu.VMEM((WINDOW, KV_WIDTH), F32), pltpu.VMEM((WINDOW, KV_WIDTH), F32),
                        pltpu.VMEM((SUBLANES, cw), F32)],
    )
    return pl.pallas_call(
        _mixer_p_kernel,
        grid_spec=grid_spec,
        out_shape=[jax.ShapeDtypeStruct((n, d), F32), jax.ShapeDtypeStruct((n * SUBLANES, LANES), F32),
                   jax.ShapeDtypeStruct((n, TOP_K), jnp.int32), jax.ShapeDtypeStruct((n, TOP_K), F32),
                   jax.ShapeDtypeStruct((b, WINDOW, KV_WIDTH), F32),
                   jax.ShapeDtypeStruct((b, WINDOW, KV_WIDTH), F32),
                   jax.ShapeDtypeStruct((b, 2, cw), F32)],
        compiler_params=pltpu.CompilerParams(dimension_semantics=("arbitrary", "arbitrary"),
                                             vmem_limit_bytes=VMEM_LIMIT),
        name="mixer_p",
    )(sinks, x, g_attn, w_in, conv_w, g_mix, w_out, g_ffn, w_r, b_r, mkt, mvb)


def _per_head_column(values, hrow):
    col = jnp.zeros(hrow.shape, F32)
    for h in range(N_Q_HEADS):
        col = jnp.where(hrow == h, values[h], col)
    return col


def _mixer_s_kernel(sinks_ref, x_ref, pm1_ref, pm2_ref, wk_ref, wv_ref, mk_ref, mv_ref,
                    g_attn_ref, w_in_ref, conv_w_ref, g_mix_ref, w_out_ref, g_ffn_ref, w_r_ref, b_r_ref,
                    h_ref, hn_ref, tope_ref, gates_ref, nwk_ref, nwv_ref, u_ref, *, t_dec):
    r_tok = BB * t_dec
    r_exp = r_tok * REP
    qrows = t_dec * REP
    x = x_ref[...]
    xn = (_rms(x) * g_attn_ref[...]).astype(BF16)
    z = _dot(xn, w_in_ref[...])
    c0 = ATTN_WIDTH
    c1 = c0 + KV_WIDTH
    c2 = c1 + KV_WIDTH
    cw = conv_w_ref.shape[1]
    c3, c4, c5 = c2 + cw, c2 + 2 * cw, c2 + 3 * cw
    q = z[:, :c0] * ATTN_SCALE
    k_new = z[:, c0:c1]
    v_new = z[:, c1:c2]
    cb = z[:, c2:c3]
    cc = z[:, c3:c4]
    cvv = z[:, c4:c5]
    mq = z[:, c5:] * ATTN_SCALE
    win = wk_ref.shape[1]

    xi = _iota((KV_WIDTH, ATTN_WIDTH), 0)
    xl = _iota((KV_WIDTH, ATTN_WIDTH), 1)
    q_per_kv = N_Q_HEADS // N_KV_HEADS
    expand = (xi == (xl // (q_per_kv * HEAD_DIM)) * HEAD_DIM + xl % HEAD_DIM).astype(BF16)
    rr = _iota((r_exp, r_tok), 0)
    rc = _iota((r_exp, r_tok), 1)
    rep = (rr // REP == rc).astype(BF16)

    hrow = _iota((r_exp, 1), 0) % REP
    trow = (_iota((r_exp, 1), 0) // REP) % t_dec
    slope_col = _per_head_column(ALIBI_SLOPES, hrow)
    sink_col = _per_head_column([sinks_ref[h] for h in range(N_Q_HEADS)], hrow)

    qexp = jnp.where(hrow == _iota((r_exp, ATTN_WIDTH), 1) // HEAD_DIM, _dot(rep, q.astype(BF16)), 0.0)
    kexp = _dot(wk_ref[...].reshape(BB * win, KV_WIDTH).astype(BF16), expand).astype(BF16)
    vexp = _dot(wv_ref[...].reshape(BB * win, KV_WIDTH).astype(BF16), expand).astype(BF16)
    s = jnp.einsum("bqc,bkc->bqk", qexp.astype(BF16).reshape(BB, qrows, ATTN_WIDTH),
                   kexp.reshape(BB, win, ATTN_WIDTH), preferred_element_type=F32).reshape(r_exp, win)
    scol = _iota((r_exp, win), 1)
    s = s - slope_col * (win + trow - scol).astype(F32)
    s = jnp.where(scol > trow, s, -jnp.inf)
    knew_exp = _dot(k_new.astype(BF16), expand).astype(BF16)
    vnew_exp = _dot(v_new.astype(BF16), expand).astype(BF16)
    s_new, v_rep = [], []
    for jn in range(t_dec):
        rep_j = (rc == (rr // qrows) * t_dec + jn).astype(BF16)
        k_rep = _dot(rep_j, knew_exp)
        v_rep.append(_dot(rep_j, vnew_exp))
        sj = jnp.sum(qexp * k_rep, axis=-1, keepdims=True) - slope_col * (trow - jn).astype(F32)
        s_new.append(jnp.where(trow >= jn, sj, -jnp.inf))
    m = jnp.maximum(jnp.max(s, axis=-1, keepdims=True), sink_col)
    for sj in s_new:
        m = jnp.maximum(m, sj)
    pe = jnp.exp(s - m)
    denom = jnp.sum(pe, axis=-1, keepdims=True) + jnp.exp(sink_col - m)
    o = jnp.einsum("bqk,bkc->bqc", pe.astype(BF16).reshape(BB, qrows, win),
                   vexp.reshape(BB, win, ATTN_WIDTH), preferred_element_type=F32).reshape(r_exp, ATTN_WIDTH)
    for jn in range(t_dec):
        pj = jnp.exp(s_new[jn] - m)
        denom = denom + pj
        o = o + pj * v_rep[jn]
    o = jnp.where(hrow == _iota((r_exp, ATTN_WIDTH), 1) // HEAD_DIM, o / denom, 0.0)
    attn = jnp.sum(o.reshape(r_tok, REP, ATTN_WIDTH), axis=1)

    m_tok = mk_ref.shape[1]
    mhead = _iota((r_exp, MEM_WIDTH), 1) // HEAD_DIM
    mqexp = jnp.where(hrow == mhead, _dot(rep, mq.astype(BF16)), 0.0).astype(BF16)
    sm = jnp.einsum("bqc,bmc->bqm", mqexp.reshape(BB, qrows, MEM_WIDTH), mk_ref[...].astype(BF16),
                    preferred_element_type=F32).reshape(r_exp, m_tok)
    mm = jnp.max(sm, axis=-1, keepdims=True)
    pm = jnp.exp(sm - mm)
    dm = jnp.sum(pm, axis=-1, keepdims=True)
    om = jnp.einsum("bqm,bmc->bqc", pm.astype(BF16).reshape(BB, qrows, m_tok), mv_ref[...].astype(BF16),
                    preferred_element_type=F32).reshape(r_exp, MEM_WIDTH)
    om = jnp.where(hrow == mhead, om / dm, 0.0)
    cross = jnp.sum(om.reshape(r_tok, REP, MEM_WIDTH), axis=1)

    u = cc * cvv
    tt = _iota(u.shape, 0) % t_dec
    u1 = jnp.where(tt >= 1, pltpu.roll(u, 1, axis=0), pm1_ref[...])
    u2 = jnp.where(tt >= 2, pltpu.roll(u, 2, axis=0), pm2_ref[...])
    conv_out = cb * (conv_w_ref[0:1, :] * u2 + conv_w_ref[1:2, :] * u1 + conv_w_ref[2:3, :] * u)
    u_ref[...] = u

    nwk_ref[:, 0:win - t_dec, :] = wk_ref[:, t_dec:win, :]
    nwv_ref[:, 0:win - t_dec, :] = wv_ref[:, t_dec:win, :]
    for b in range(BB):
        nwk_ref[b, win - t_dec:win, :] = k_new[b * t_dec:(b + 1) * t_dec, :]
        nwv_ref[b, win - t_dec:win, :] = v_new[b * t_dec:(b + 1) * t_dec, :]

    _mix_out(x, attn, conv_out, cross, g_mix_ref, w_out_ref, g_ffn_ref, w_r_ref, b_r_ref,
             h_ref, hn_ref, tope_ref, gates_ref)


def _mixer_s(x2, t_dec, pm1, pm2, wk, wv, mk, mv, sinks, g_attn, w_in, conv_w, g_mix, w_out, g_ffn, w_r, b_r):
    n, d = x2.shape
    nb = wk.shape[0]
    win = wk.shape[1]
    m_tok = mk.shape[1]
    cw = conv_w.shape[1]
    r_tok = BB * t_dec
    full = lambda a: pl.BlockSpec(a.shape, lambda i, *_: (0,) * a.ndim)
    tok = lambda w: pl.BlockSpec((r_tok, w), lambda i, *_: (i, 0))
    per_b = lambda r, c: pl.BlockSpec((BB, r, c), lambda i, *_: (i, 0, 0))
    grid_spec = pltpu.PrefetchScalarGridSpec(
        num_scalar_prefetch=1,
        grid=(nb // BB,),
        in_specs=[tok(d), tok(cw), tok(cw), per_b(win, KV_WIDTH), per_b(win, KV_WIDTH),
                  per_b(m_tok, MEM_WIDTH), per_b(m_tok, MEM_WIDTH),
                  full(g_attn), full(w_in), full(conv_w), full(g_mix), full(w_out), full(g_ffn),
                  full(w_r), full(b_r)],
        out_specs=[tok(d), pl.BlockSpec((r_tok * SUBLANES, LANES), lambda i, *_: (i, 0)),
                   tok(TOP_K), tok(TOP_K), per_b(win, KV_WIDTH), per_b(win, KV_WIDTH), tok(cw)],
    )
    return pl.pallas_call(
        functools.partial(_mixer_s_kernel, t_dec=t_dec),
        grid_spec=grid_spec,
        out_shape=[jax.ShapeDtypeStruct((n, d), F32), jax.ShapeDtypeStruct((n * SUBLANES, LANES), F32),
                   jax.ShapeDtypeStruct((n, TOP_K), jnp.int32), jax.ShapeDtypeStruct((n, TOP_K), F32),
                   jax.ShapeDtypeStruct((nb, win, KV_WIDTH), F32), jax.ShapeDtypeStruct((nb, win, KV_WIDTH), F32),
                   jax.ShapeDtypeStruct((n, cw), F32)],
        compiler_params=pltpu.CompilerParams(dimension_semantics=("arbitrary",), vmem_limit_bytes=VMEM_LIMIT),
        name="mixer_s",
    )(sinks, x2, pm1, pm2, wk, wv, mk, mv, g_attn, w_in, conv_w, g_mix, w_out, g_ffn, w_r, b_r)


def _lpos_kernel(tope_ref, lpos_ref, counts_ref, tri_ref, *, n_e):
    i = pl.program_id(0)
    tm = tope_ref.shape[0]

    @pl.when(i == 0)
    def _():
        tri_ref[...] = (_iota((tm, tm), 0) > _iota((tm, tm), 1)).astype(BF16)

    te = tope_ref[...]
    col = _iota((tm, n_e), 1)
    hits = [te[:, k:k + 1] == col for k in range(TOP_K)]
    onehot = jnp.zeros((tm, n_e), F32)
    lower = jnp.zeros((tm, n_e), F32)
    for k in range(TOP_K):
        onehot = onehot + hits[k].astype(F32)
        lower = lower + (te[:, k:k + 1] < col).astype(F32)
    ahead = _dot(tri_ref[...], onehot.astype(BF16)) + jnp.sum(lower, axis=0, keepdims=True)
    col4 = _iota((tm, TOP_K), 1)
    pos = jnp.zeros((tm, TOP_K), F32)
    for k in range(TOP_K):
        pos = jnp.where(col4 == k, jnp.sum(jnp.where(hits[k], ahead, 0.0), axis=-1, keepdims=True), pos)
    half = (i % 2) * (tm * TOP_K)
    lpos_ref[...] = (pos.astype(jnp.int32) + half) * SUBLANES
    counts_ref[0] = jnp.sum(onehot, axis=0, keepdims=True)


def _lpos(tope, tm, n_e):
    n = tope.shape[0]
    return pl.pallas_call(
        functools.partial(_lpos_kernel, n_e=n_e),
        grid=(n // tm,),
        in_specs=[pl.BlockSpec((tm, TOP_K), lambda i: (i, 0))],
        out_specs=[pl.BlockSpec((tm, TOP_K), lambda i: (i, 0)), pl.BlockSpec((1, 1, n_e), lambda i: (i, 0, 0))],
        out_shape=[jax.ShapeDtypeStruct((n, TOP_K), jnp.int32), jax.ShapeDtypeStruct((n // tm, 1, n_e), F32)],
        scratch_shapes=[pltpu.VMEM((tm, tm), BF16)],
        compiler_params=pltpu.CompilerParams(dimension_semantics=("arbitrary",)),
        name="lpos",
    )(tope)


def _rows(ref, first, n_rows):
    return ref.at[pl.ds(pl.multiple_of(first, SUBLANES), n_rows * SUBLANES)]


def _group_chunks(meta, g, n_e, max_rows, make_copy, wait):
    off_ref, cnt_ref, lst_ref = meta
    if wait:
        make_copy(0, 0, max_rows * TOP_K).wait()
        return

    def per_expert(e, c):
        off = off_ref[g * n_e + e]
        lst = lst_ref[g * n_e + e]
        _run_copies(cnt_ref[g * n_e + e], max_rows, lambda done, size: make_copy(lst + done, off + done, size),
                    wait=False)
        return c

    lax.fori_loop(0, n_e, per_expert, 0)


def _run_copies(n_rows, max_rows, make_copy, wait):
    done = 0
    for size in [1 << b for b in range(max_rows.bit_length() - 1, -1, -1)]:
        bit = n_rows & (size * SUBLANES)

        @pl.when(bit != 0)
        def _():
            cp = make_copy(done, size)
            cp.wait() if wait else cp.start()
        done = done + bit


def _dispatch_kernel(off_ref, cnt_ref, lst_ref, zrow_ref, zcnt_ref, nused_ref, lpos_ref, hn_a_ref, hn_b_ref, xs_ref,
                     stage_ref, zbuf_ref, sem, zsem, *, n_e, n_blocks, groups_a):
    j = pl.program_id(0)
    nj = pl.num_programs(0)
    tm = hn_a_ref.shape[0] // SUBLANES
    slot = j % 2
    meta = (off_ref, cnt_ref, lst_ref)

    def chunks(g, s, wait):
        _group_chunks(meta, g, n_e, tm,
                      lambda lrow, grow, size: pltpu.make_async_copy(
                          _rows(stage_ref, s * (tm * TOP_K * SUBLANES) + lrow, size), _rows(xs_ref, grow, size),
                          sem.at[s]), wait)

    def zero_fill(wait):
        def zero_run(first_row, n_rows):
            _run_copies(n_rows, BM, lambda done, size: pltpu.make_async_copy(
                _rows(zbuf_ref, 0, size), _rows(xs_ref, first_row + done, size), zsem), wait)

        def expert_pad(e, c):
            zero_run(zrow_ref[e], zcnt_ref[e])
            return c

        def tail_block(b, c):
            zero_run(b * (BM * SUBLANES), jnp.int32(BM * SUBLANES))
            return c

        lax.fori_loop(0, n_e, expert_pad, 0)
        lax.fori_loop(nused_ref[0], n_blocks, tail_block, 0)

    @pl.when(j == 0)
    def _():
        zbuf_ref[...] = jnp.zeros_like(zbuf_ref)
        zero_fill(False)

    @pl.when(j >= 2)
    def _():
        chunks(j - 2, slot, True)

    def place_from(hn_ref):
        def place(t, c):
            tile = hn_ref[pl.ds(pl.multiple_of(t * SUBLANES, SUBLANES), SUBLANES), :]
            for k in range(TOP_K):
                pos = pl.multiple_of(lpos_ref[0, 0, t * TOP_K + k], SUBLANES)
                stage_ref[pl.ds(pos, SUBLANES), :] = tile
            return c
        lax.fori_loop(0, tm, place, 0, unroll=TOKEN_UNROLL)

    @pl.when(j < groups_a)
    def _():
        place_from(hn_a_ref)

    @pl.when(j >= groups_a)
    def _():
        place_from(hn_b_ref)

    chunks(j, slot, False)

    @pl.when(j == nj - 1)
    def _():
        @pl.when(j >= 1)
        def _():
            chunks(j - 1, 1 - slot, True)
        chunks(j, slot, True)
        zero_fill(True)


def _dispatch(hn_a, hn_b, lpos, meta, zrow, zcnt, nused, tm, n_e, n_blocks):
    nt = lpos.shape[0] // tm
    groups_a = hn_a.shape[0] // (tm * SUBLANES)
    assert groups_a >= 1 and groups_a + hn_b.shape[0] // (tm * SUBLANES) == nt
    grid_spec = pltpu.PrefetchScalarGridSpec(
        num_scalar_prefetch=6,
        grid=(nt,),
        in_specs=[pl.BlockSpec((1, 1, tm * TOP_K), lambda j, *_: (j, 0, 0), memory_space=pltpu.SMEM),
                  pl.BlockSpec((tm * SUBLANES, LANES), lambda j, *_: (jnp.minimum(j, groups_a - 1), 0)),
                  pl.BlockSpec((tm * SUBLANES, LANES), lambda j, *_: (jnp.maximum(j - groups_a, 0), 0))],
        out_specs=pl.BlockSpec(memory_space=pl.ANY),
        scratch_shapes=[pltpu.VMEM((2 * tm * TOP_K * SUBLANES, LANES), F32), pltpu.VMEM((BM * SUBLANES, LANES), F32),
                        pltpu.SemaphoreType.DMA((2,)), pltpu.SemaphoreType.DMA(())],
    )
    return pl.pallas_call(
        functools.partial(_dispatch_kernel, n_e=n_e, n_blocks=n_blocks, groups_a=groups_a),
        grid_spec=grid_spec,
        out_shape=jax.ShapeDtypeStruct((n_blocks * BM * SUBLANES, LANES), F32),
        compiler_params=pltpu.CompilerParams(dimension_semantics=("arbitrary",), vmem_limit_bytes=VMEM_LIMIT),
        name="dispatch",
    )(*meta, zrow, zcnt, nused, lpos.reshape(nt, 1, tm * TOP_K), hn_a, hn_b)


def _experts_kernel(blk_e_ref, nused_ref, next_e_ref, blk_rows_ref, xs_ref, bgu_ref, bd_ref, wgu_hbm_ref, wd_hbm_ref, ys_ref,
                    wgu_ref, wd_ref, wgu_bf_ref, wd_bf_ref, wsem):
    i = pl.program_id(0)
    nused = nused_ref[0]
    d, d_ff2 = wgu_ref.shape
    grp = 2 * LANES

    def weight_copies(e):
        gu_cols = d_ff2 // W_PARTS
        dn_rows = (d_ff2 // 2) // W_PARTS
        cps = []
        for p in range(W_PARTS):
            cps.append(pltpu.make_async_copy(wgu_hbm_ref.at[e, :, pl.ds(p * gu_cols, gu_cols)],
                                             wgu_ref.at[:, pl.ds(p * gu_cols, gu_cols)], wsem.at[p]))
            cps.append(pltpu.make_async_copy(wd_hbm_ref.at[e, pl.ds(p * dn_rows, dn_rows), :],
                                             wd_ref.at[pl.ds(p * dn_rows, dn_rows), :], wsem.at[W_PARTS + p]))
        return cps

    @pl.when(i < nused)
    def _():
        e = blk_e_ref[i]
        e_prev = blk_e_ref[jnp.maximum(i - 1, 0)]

        @pl.when(i == 0)
        def _():
            for cp in weight_copies(e):
                cp.start()

        @pl.when((i == 0) | (e != e_prev))
        def _():
            for cp in weight_copies(e):
                cp.wait()
            pr = _iota((grp, grp), 0)
            pc = _iota((grp, grp), 1)
            perm = (pr == jnp.where(pc < LANES, 2 * pc, 2 * (pc - LANES) + 1)).astype(BF16)
            for g in range(d_ff2 // grp):
                w = wgu_ref[:, g * grp:(g + 1) * grp].astype(BF16)
                wgu_bf_ref[:, g * grp:(g + 1) * grp] = _dot(w, perm).astype(BF16)
            wd_bf_ref[...] = wd_ref[...].astype(BF16)

            @pl.when(next_e_ref[e] >= 0)
            def _():
                for cp in weight_copies(next_e_ref[e]):
                    cp.start()

        def ffn(r0, n_rows):
            x = jnp.concatenate([_load_token_tiles(xs_ref, n_rows, s, base=r0 * SUBLANES).astype(BF16)
                                 for s in range(SUBLANES)], axis=1)
            hgu = _dot(x, wgu_bf_ref[...]) + bgu_ref[0]
            acts = []
            for g in range(d_ff2 // grp):
                gate = jnp.minimum(hgu[:, g * grp:g * grp + LANES], SWIGLU_LIMIT)
                up = jnp.clip(hgu[:, g * grp + LANES:(g + 1) * grp], -SWIGLU_LIMIT, SWIGLU_LIMIT)
                glu = gate * (1.0 / (1.0 + jnp.exp(-SWIGLU_ALPHA * gate)))
                acts.append(((up + 1.0) * glu).astype(BF16))
            act = jnp.concatenate(acts, axis=1)
            _store_token_tiles(ys_ref, _dot(act, wd_bf_ref[...]) + bd_ref[0], base=r0 * SUBLANES)

        rows = blk_rows_ref[i]

        @pl.when(rows == BM)
        def _():
            ffn(0, BM)

        @pl.when(rows < BM)
        def _():
            for r0 in range(0, BM, TAIL_CHUNK):
                @pl.when(r0 < rows)
                def _():
                    ffn(r0, TAIL_CHUNK)

                @pl.when(r0 >= rows)
                def _():
                    ys_ref[r0 * SUBLANES:(r0 + TAIL_CHUNK) * SUBLANES, :] = jnp.zeros(
                        (TAIL_CHUNK * SUBLANES, LANES), F32)

    @pl.when(i >= nused)
    def _():
        ys_ref[...] = jnp.zeros_like(ys_ref)


def _experts(xs, blk_e, nused, next_e, blk_rows, w_gate_up, b_gu_perm, w_down, b_down):
    n_e, d, d_ff2 = w_gate_up.shape
    n_blocks = xs.shape[0] // (BM * SUBLANES)
    expert = lambda i, be, nu, *_: be[jnp.minimum(i, jnp.maximum(nu[0] - 1, 0))]
    rows_spec = pl.BlockSpec((BM * SUBLANES, LANES), lambda i, *_: (i, 0))
    grid_spec = pltpu.PrefetchScalarGridSpec(
        num_scalar_prefetch=4,
        grid=(n_blocks,),
        in_specs=[pl.BlockSpec((BM * SUBLANES, LANES), lambda i, be, nu, *_: (jnp.minimum(i, jnp.maximum(nu[0] - 1, 0)), 0)),
                  pl.BlockSpec((1, 1, d_ff2), lambda i, *s: (expert(i, *s), 0, 0)),
                  pl.BlockSpec((1, 1, d), lambda i, *s: (expert(i, *s), 0, 0)),
                  pl.BlockSpec(memory_space=pl.ANY), pl.BlockSpec(memory_space=pl.ANY)],
        out_specs=rows_spec,
        scratch_shapes=[pltpu.VMEM((d, d_ff2), F32), pltpu.VMEM((d_ff2 // 2, d), F32),
                        pltpu.VMEM((d, d_ff2), BF16), pltpu.VMEM((d_ff2 // 2, d), BF16),
                        pltpu.SemaphoreType.DMA((2 * W_PARTS,))],
    )
    return pl.pallas_call(
        _experts_kernel,
        grid_spec=grid_spec,
        out_shape=jax.ShapeDtypeStruct(xs.shape, F32),
        compiler_params=pltpu.CompilerParams(dimension_semantics=("arbitrary",), vmem_limit_bytes=VMEM_LIMIT),
        name="experts",
    )(blk_e, nused, next_e, blk_rows, xs, b_gu_perm, b_down, w_gate_up, w_down)


def _combine_kernel(off_ref, cnt_ref, lst_ref, lpos_ref, gates_ref, h_ref, g_ref, ys_ref, out_ref,
                    stage_ref, acc_ref, sem, *, n_e, g0):
    i = pl.program_id(0)
    ng = pl.num_programs(0)
    tm, d = h_ref.shape
    slot = (g0 + i) % 2
    meta = (off_ref, cnt_ref, lst_ref)

    def chunks(g, s, wait):
        _group_chunks(meta, g, n_e, tm,
                      lambda lrow, grow, size: pltpu.make_async_copy(
                          _rows(ys_ref, grow, size), _rows(stage_ref, s * (tm * TOP_K * SUBLANES) + lrow, size),
                          sem.at[s]), wait)

    @pl.when(i == 0)
    def _():
        chunks(g0, slot, False)

    @pl.when(i + 1 < ng)
    def _():
        chunks(g0 + i + 1, 1 - slot, False)

    chunks(g0 + i, slot, True)

    def mix(t, c):
        acc = None
        for k in range(TOP_K):
            pos = pl.multiple_of(lpos_ref[0, 0, t * TOP_K + k], SUBLANES)
            term = stage_ref[pl.ds(pos, SUBLANES), :] * gates_ref[0, 0, t * TOP_K + k]
            acc = term if acc is None else acc + term
        acc_ref[pl.ds(pl.multiple_of(t * SUBLANES, SUBLANES), SUBLANES), :] = acc
        return c

    lax.fori_loop(0, tm, mix, 0, unroll=TOKEN_UNROLL)

    h = h_ref[...]
    parts = []
    sq = jnp.zeros((tm, LANES), F32)
    for s in range(d // LANES):
        y = h[:, s * LANES:(s + 1) * LANES] + _load_token_tiles(acc_ref, tm, s)
        sq = sq + y * y
        parts.append(y)
    rinv = lax.rsqrt(jnp.sum(sq, axis=-1, keepdims=True) / d + EPS)
    out_ref[...] = jnp.concatenate(parts, axis=1) * rinv * g_ref[...]


def _combine(ys, h, lpos, gates, meta, g_final, tm, n_e, g0):
    n, d = h.shape
    grid_spec = pltpu.PrefetchScalarGridSpec(
        num_scalar_prefetch=3,
        grid=(n // tm,),
        in_specs=[pl.BlockSpec((1, 1, tm * TOP_K), lambda i, *_: (g0 + i, 0, 0), memory_space=pltpu.SMEM),
                  pl.BlockSpec((1, 1, tm * TOP_K), lambda i, *_: (g0 + i, 0, 0), memory_space=pltpu.SMEM),
                  pl.BlockSpec((tm, d), lambda i, *_: (i, 0)),
                  pl.BlockSpec((1, d), lambda i, *_: (0, 0)),
                  pl.BlockSpec(memory_space=pl.ANY)],
        out_specs=pl.BlockSpec((tm, d), lambda i, *_: (i, 0)),
        scratch_shapes=[pltpu.VMEM((2 * tm * TOP_K * SUBLANES, LANES), F32), pltpu.VMEM((tm * SUBLANES, LANES), F32),
                        pltpu.SemaphoreType.DMA((2,))],
    )
    ngroups = lpos.shape[0] // tm
    return pl.pallas_call(
        functools.partial(_combine_kernel, n_e=n_e, g0=g0),
        grid_spec=grid_spec,
        out_shape=jax.ShapeDtypeStruct((n, d), F32),
        compiler_params=pltpu.CompilerParams(dimension_semantics=("arbitrary",), vmem_limit_bytes=VMEM_LIMIT),
        name="combine",
    )(*meta, lpos.reshape(ngroups, 1, tm * TOP_K), gates.reshape(ngroups, 1, tm * TOP_K), h, g_final, ys)


def _largest_tile(cands, *sizes):
    for c in cands:
        if all(s % c == 0 for s in sizes):
            return c
    raise ValueError(f"no tile in {cands} divides {sizes}")


def kernel(x_prompt, x_sample, mem_prompt, cache_win_k, cache_win_v, state_conv, cache_mem_k, cache_mem_v, g_attn_norm, w_in, conv_w, attn_sinks, g_mem_norm, w_mem_kv, g_mix_out, w_out, g_ffn_norm, w_router, b_router, w_gate_up, b_gate_up, w_down, b_down, g_final):
    depth = w_in.shape[0]
    assert depth == 1, "single-layer step"
    b, s, d = x_prompt.shape
    nb, t_dec, _ = x_sample.shape
    n_e = w_router.shape[2]
    d_ff2 = w_gate_up.shape[3]
    cw = conv_w.shape[2]
    win = cache_win_k.shape[2]
    m_tok = cache_mem_k.shape[2]
    assert s % TQ == 0 and nb % BB == 0 and win == WINDOW and t_dec <= SUBLANES and d_ff2 % (2 * LANES) == 0
    assert d == SUBLANES * LANES, "token-tile layout: one token is one (8, 128) f32 tile"

    row = lambda a: a.reshape(1, -1)
    w_in_bf = w_in[0].astype(BF16)
    w_out_bf = w_out[0].astype(BF16)
    w_r_bf = w_router[0].astype(BF16)
    sinks = attn_sinks[0].astype(F32)
    shared = (row(g_attn_norm[0]), w_in_bf, conv_w[0], row(g_mix_out[0]), w_out_bf, row(g_ffn_norm[0]),
              w_r_bf, row(b_router[0]))

    mk_p, mv_p, mkt, mvb = _memkv(mem_prompt, row(g_mem_norm[0]), w_mem_kv[0].astype(BF16))
    h_p, hn_p, tope_p, gates_p, lastk, lastv, convst = _mixer_p(x_prompt, sinks, *shared, mkt, mvb)

    zeros = lambda r: jnp.zeros((nb, r, cw), F32)
    st = state_conv[0]
    pm1 = jnp.concatenate([st[:, 1:2], zeros(t_dec - 1)], axis=1).reshape(nb * t_dec, cw)
    pm2 = jnp.concatenate([st, zeros(t_dec - 2)], axis=1).reshape(nb * t_dec, cw)
    h_s, hn_s, tope_s, gates_s, nwk, nwv, u_s = _mixer_s(
        x_sample.reshape(nb * t_dec, d), t_dec, pm1, pm2,
        cache_win_k[0].reshape(nb, win, KV_WIDTH), cache_win_v[0].reshape(nb, win, KV_WIDTH),
        cache_mem_k[0].reshape(nb, m_tok, MEM_WIDTH), cache_mem_v[0].reshape(nb, m_tok, MEM_WIDTH),
        sinks, *shared)

    n_p, n_s = b * s, nb * t_dec
    n = n_p + n_s
    tm = _largest_tile((512, 256, 128, 64, 32, 16, 8), n_p, n_s)
    tope = jnp.concatenate([tope_p, tope_s], axis=0)
    gates = jnp.concatenate([gates_p, gates_s], axis=0)

    lpos, cnt_f = _lpos(tope, tm, n_e)
    cnt = cnt_f[:, 0, :].astype(jnp.int32)
    counts = jnp.sum(cnt, axis=0)
    padded = (counts + BM - 1) // BM * BM
    pad_ends = jnp.cumsum(padded)
    pad_starts = pad_ends - padded
    nk = n * TOP_K
    n_blocks = -(-nk // BM) + n_e
    nused = (pad_ends[-1:] // BM).astype(jnp.int32)
    blk_start = jnp.arange(n_blocks, dtype=jnp.int32) * BM
    blk_e = jnp.minimum(jnp.sum((pad_ends[None, :] <= blk_start[:, None]).astype(jnp.int32), axis=1), n_e - 1)
    zrow = ((pad_starts + counts) * SUBLANES).astype(jnp.int32)
    zcnt = ((padded - counts) * SUBLANES).astype(jnp.int32)
    off = pad_starts[None, :] + jnp.cumsum(cnt, axis=0) - cnt
    lstart = jnp.cumsum(cnt, axis=1) - cnt
    meta = tuple((a.reshape(-1) * SUBLANES).astype(jnp.int32) for a in (off, cnt, lstart))

    xs = _dispatch(hn_p, hn_s, lpos, meta, zrow, zcnt, nused, tm, n_e, n_blocks)

    grp = 2 * LANES
    b_gu = b_gate_up[0].reshape(n_e, d_ff2 // grp, LANES, 2).transpose(0, 1, 3, 2).reshape(n_e, 1, d_ff2)
    owner = jnp.where(padded > 0, jnp.arange(n_e, dtype=jnp.int32), n_e)
    following = jnp.concatenate([lax.cummin(owner, reverse=True)[1:], jnp.full((1,), n_e, jnp.int32)])
    next_e = jnp.where(following < n_e, following, -1).astype(jnp.int32)
    own = blk_e[:, None] == jnp.arange(n_e, dtype=jnp.int32)[None, :]
    blk_rows = jnp.clip(jnp.sum(jnp.where(own, (pad_starts + counts)[None, :], 0), axis=1) - blk_start, 0, BM)
    blk_rows = blk_rows.astype(jnp.int32)
    ys = _experts(xs, blk_e, nused, next_e, blk_rows, w_gate_up[0], b_gu, w_down[0], b_down[0].reshape(n_e, 1, d))

    g_fin = row(g_final)
    y_p = _combine(ys, h_p, lpos, gates, meta, g_fin, tm, n_e, 0)
    y_s = _combine(ys, h_s, lpos, gates, meta, g_fin, tm, n_e, n_p // tm)

    kv5 = lambda a, bsz, r, hds: a.reshape(1, bsz, r, hds, HEAD_DIM)
    return (y_p.reshape(b, s, d), y_s.reshape(nb, t_dec, d),
            kv5(lastk, b, WINDOW, N_KV_HEADS), kv5(lastv, b, WINDOW, N_KV_HEADS),
            convst.reshape(1, b, 2, cw),
            kv5(mk_p, b, m_tok, N_MEM_HEADS), kv5(mv_p, b, m_tok, N_MEM_HEADS),
            kv5(nwk, nb, win, N_KV_HEADS), kv5(nwv, nb, win, N_KV_HEADS),
            u_s.reshape(nb, t_dec, cw)[:, t_dec - 2:].reshape(1, nb, 2, cw))
```

```python
import functools

import jax
import jax.numpy as jnp
from jax import lax
from jax.experimental import pallas as pl
from jax.experimental.pallas import tpu as pltpu

F32 = jnp.float32
BF16 = jnp.bfloat16

HEAD_DIM = 64
N_Q_HEADS = 8
N_KV_HEADS = 2
WINDOW = 128
ATTN_WIDTH = N_Q_HEADS * HEAD_DIM
KV_WIDTH = N_KV_HEADS * HEAD_DIM
N_MEM_HEADS = 4
MEM_WIDTH = N_MEM_HEADS * HEAD_DIM
TOP_K = 4
SWIGLU_LIMIT = 7.0
SWIGLU_ALPHA = 1.702
EPS = 1e-5
ATTN_SCALE = HEAD_DIM ** -0.5
ALIBI_SLOPES = tuple(2.0 ** (-8.0 * (h + 1) / N_Q_HEADS) for h in range(N_Q_HEADS))

LANES = 128
SUBLANES = 8
VMEM_LIMIT = 56 * 1024 * 1024

TQ = 1024
BB = 16
BM = 1024
REP = 8
W_PARTS = 4
MIX_CHUNK = 512
IN_CHUNK = 256
TAIL_CHUNK = 128
TOKEN_UNROLL = 16


def _rms(x):
    return x * lax.rsqrt(jnp.mean(x * x, axis=-1, keepdims=True) + EPS)


def _dot(a, b):
    return jnp.dot(a, b, preferred_element_type=F32)


def _dot_nt(a, b):
    return lax.dot_general(a, b, (((1,), (1,)), ((), ())), preferred_element_type=F32)


def _iota(shape, axis):
    return lax.broadcasted_iota(jnp.int32, shape, axis)


def _store_token_tiles(ref, x, base=0):
    t = x.shape[0]
    for s in range(x.shape[1] // LANES):
        ref[pl.ds(base + s, t, stride=SUBLANES), :] = x[:, s * LANES:(s + 1) * LANES]


def _load_token_tiles(ref, t, s, base=0):
    return ref[pl.ds(base + s, t, stride=SUBLANES), :]


def _memkv_kernel(mem_ref, g_ref, w_ref, mk_ref, mv_ref, mkt_ref, mvb_ref):
    xn = (_rms(mem_ref[0]) * g_ref[...]).astype(BF16)
    kv = _dot(xn, w_ref[...])
    mk = kv[:, :MEM_WIDTH]
    mv = kv[:, MEM_WIDTH:]
    mk_ref[0] = mk
    mv_ref[0] = mv
    mkt_ref[0] = mk.T.astype(BF16)
    mvb_ref[0] = mv.astype(BF16)


def _memkv(mem, g, w_bf):
    b, m, d = mem.shape
    out_f = jax.ShapeDtypeStruct((b, m, MEM_WIDTH), F32)
    out_b = jax.ShapeDtypeStruct((b, m, MEM_WIDTH), BF16)
    out_t = jax.ShapeDtypeStruct((b, MEM_WIDTH, m), BF16)
    blk = lambda r, c: pl.BlockSpec((1, r, c), lambda i: (i, 0, 0))
    return pl.pallas_call(
        _memkv_kernel,
        grid=(b,),
        in_specs=[blk(m, d), pl.BlockSpec((1, d), lambda i: (0, 0)),
                  pl.BlockSpec((d, 2 * MEM_WIDTH), lambda i: (0, 0))],
        out_specs=[blk(m, MEM_WIDTH), blk(m, MEM_WIDTH), blk(MEM_WIDTH, m), blk(m, MEM_WIDTH)],
        out_shape=[out_f, out_f, out_t, out_b],
        name="memkv",
    )(mem, g, w_bf)


def _router_topk(hn, w_r_ref, b_r_ref, tope_ref, gates_ref, rows_at):
    n_e = w_r_ref.shape[1]
    logits = _dot(hn.astype(BF16), w_r_ref[...]) + b_r_ref[...]
    rows = logits.shape[0]
    col = _iota((rows, n_e), 1).astype(F32)
    vals, idxs = [], []
    cur = logits
    for _ in range(TOP_K):
        m = jnp.max(cur, axis=-1, keepdims=True)
        idx = jnp.min(jnp.where(cur == m, col, float(n_e)), axis=-1, keepdims=True)
        vals.append(m)
        idxs.append(idx)
        cur = jnp.where(col == idx, -jnp.inf, cur)
    exps = [jnp.exp(v - vals[0]) for v in vals]
    tot = exps[0] + exps[1] + exps[2] + exps[3]
    col4 = _iota((rows, TOP_K), 1)
    te = jnp.zeros((rows, TOP_K), F32)
    ga = jnp.zeros((rows, TOP_K), F32)
    for k in range(TOP_K):
        te = jnp.where(col4 == k, idxs[k], te)
        ga = jnp.where(col4 == k, exps[k] / tot, ga)
    tope_ref[rows_at, :] = te.astype(jnp.int32)
    gates_ref[rows_at, :] = ga


def _mix_out(x, attn, conv_out, cross, g_mix_ref, w_out_ref, g_ffn_ref, w_r_ref, b_r_ref,
             h_ref, hn_ref, tope_ref, gates_ref):
    rows = x.shape[0]
    chunk = MIX_CHUNK if rows % MIX_CHUNK == 0 else rows
    for r0 in range(0, rows, chunk):
        at = slice(r0, r0 + chunk)
        mix = jnp.concatenate([_rms(attn[at]), _rms(conv_out[at]), _rms(cross[at])], axis=-1) * g_mix_ref[...]
        h = x[at] + _dot(mix.astype(BF16), w_out_ref[...])
        hn = _rms(h) * g_ffn_ref[...]
        h_ref[at, :] = h
        _store_token_tiles(hn_ref, hn, base=r0 * SUBLANES)
        _router_topk(hn, w_r_ref, b_r_ref, tope_ref, gates_ref, at)


def _swa_bias(prev_lim):
    blk = WINDOW
    qi = _iota((blk, 2 * blk), 0)
    kj = _iota((blk, 2 * blk), 1)
    dist = blk + qi - kj
    mask = (dist >= 0) & (dist < WINDOW) & (kj >= prev_lim)
    distf = dist.astype(F32)
    return [jnp.where(mask, -ALIBI_SLOPES[h] * distf, -jnp.inf) for h in range(N_Q_HEADS)]


def _swa_block(q_blk, kk, vv, bias, sinks_ref):
    blk = WINDOW
    lane = _iota((2 * blk, KV_WIDTH), 1)
    lo = lane < HEAD_DIM
    kk_r = pltpu.roll(kk, HEAD_DIM, axis=1)
    vv_r = pltpu.roll(vv, HEAD_DIM, axis=1)
    kdup = [jnp.where(lo, kk, kk_r).astype(BF16), jnp.where(lo, kk_r, kk).astype(BF16)]
    vlo = [jnp.where(lo, vv, 0.0).astype(BF16), jnp.where(lo, vv_r, 0.0).astype(BF16)]
    vhi = [jnp.where(lo, 0.0, vv_r).astype(BF16), jnp.where(lo, 0.0, vv).astype(BF16)]
    qlo = _iota((blk, 2 * HEAD_DIM), 1) < HEAD_DIM
    outs = []
    for p in range(N_Q_HEADS // 2):
        kh = (2 * p) // (N_Q_HEADS // N_KV_HEADS)
        qp = q_blk[:, p * 2 * HEAD_DIM:(p + 1) * 2 * HEAD_DIM]
        acc = None
        for e in range(2):
            h = 2 * p + e
            qm = jnp.where(qlo if e == 0 else jnp.logical_not(qlo), qp, 0.0).astype(BF16)
            s = _dot_nt(qm, kdup[kh]) + bias[h]
            sink = sinks_ref[h]
            m = jnp.maximum(jnp.max(s, axis=-1, keepdims=True), sink)
            pe = jnp.exp(s - m)
            denom = jnp.sum(pe, axis=-1, keepdims=True) + jnp.exp(sink - m)
            o = _dot(pe.astype(BF16), (vlo if e == 0 else vhi)[kh]) / denom
            acc = o if acc is None else acc + o
        outs.append(acc)
    return jnp.concatenate(outs, axis=1)


def _mem_attend_shared(mq, mkt, mvb):
    t = mq.shape[0]
    m_tok = mvb.shape[0]
    qhead = _iota((t, MEM_WIDTH), 1) // HEAD_DIM
    vhead = _iota((m_tok, MEM_WIDTH), 1) // HEAD_DIM
    cross = None
    for h in range(N_MEM_HEADS):
        qm = jnp.where(qhead == h, mq, 0.0).astype(BF16)
        s = _dot(qm, mkt)
        m = jnp.max(s, axis=-1, keepdims=True)
        pe = jnp.exp(s - m)
        denom = jnp.sum(pe, axis=-1, keepdims=True)
        vm = jnp.where(vhead == h, mvb, jnp.zeros_like(mvb))
        o = _dot(pe.astype(BF16), vm) / denom
        cross = o if cross is None else cross + o
    return cross


def _mixer_p_kernel(sinks_ref, x_ref, g_attn_ref, w_in_ref, conv_w_ref, g_mix_ref, w_out_ref, g_ffn_ref,
                    w_r_ref, b_r_ref, mkt_ref, mvb_ref,
                    h_ref, hn_ref, tope_ref, gates_ref, lastk_ref, lastv_ref, convst_ref,
                    ck_ref, cv_ref, cu_ref):
    j = pl.program_id(1)
    nj = pl.num_programs(1)

    @pl.when(j == 0)
    def _():
        ck_ref[...] = jnp.zeros_like(ck_ref)
        cv_ref[...] = jnp.zeros_like(cv_ref)
        cu_ref[...] = jnp.zeros_like(cu_ref)

    x = x_ref[0]
    z = jnp.concatenate([_dot((_rms(x[r0:r0 + IN_CHUNK]) * g_attn_ref[...]).astype(BF16), w_in_ref[...])
                         for r0 in range(0, TQ, IN_CHUNK)], axis=0)
    c0 = ATTN_WIDTH
    c1 = c0 + KV_WIDTH
    c2 = c1 + KV_WIDTH
    cw = conv_w_ref.shape[1]
    c3, c4, c5 = c2 + cw, c2 + 2 * cw, c2 + 3 * cw
    q = z[:, :c0] * ATTN_SCALE
    k = z[:, c0:c1]
    v = z[:, c1:c2]
    cb = z[:, c2:c3]
    cc = z[:, c3:c4]
    cvv = z[:, c4:c5]
    mq = z[:, c5:] * ATTN_SCALE

    blk = WINDOW
    attn_blocks = []
    bias_inner = _swa_bias(0)
    for i in range(TQ // blk):
        if i == 0:
            pk, pv = ck_ref[...], cv_ref[...]
            bias = _swa_bias(jnp.where(j > 0, 0, blk))
        else:
            pk, pv = k[(i - 1) * blk:i * blk], v[(i - 1) * blk:i * blk]
            bias = bias_inner
        kk = jnp.concatenate([pk, k[i * blk:(i + 1) * blk]], axis=0)
        vv = jnp.concatenate([pv, v[i * blk:(i + 1) * blk]], axis=0)
        attn_blocks.append(_swa_block(q[i * blk:(i + 1) * blk], kk, vv, bias, sinks_ref))
    attn = jnp.concatenate(attn_blocks, axis=0)
    ck_ref[...] = k[TQ - blk:]
    cv_ref[...] = v[TQ - blk:]

    u = cc * cvv
    row = _iota(u.shape, 0)
    u1 = jnp.where(row == 0, cu_ref[SUBLANES - 1:SUBLANES, :], pltpu.roll(u, 1, axis=0))
    u2 = jnp.where(row == 0, cu_ref[SUBLANES - 2:SUBLANES - 1, :],
                   jnp.where(row == 1, cu_ref[SUBLANES - 1:SUBLANES, :], pltpu.roll(u, 2, axis=0)))
    conv_out = cb * (conv_w_ref[0:1, :] * u2 + conv_w_ref[1:2, :] * u1 + conv_w_ref[2:3, :] * u)
    cu_ref[...] = u[TQ - SUBLANES:]

    cross = _mem_attend_shared(mq, mkt_ref[0], mvb_ref[0])

    @pl.when(j == nj - 1)
    def _():
        lastk_ref[0] = k[TQ - blk:]
        lastv_ref[0] = v[TQ - blk:]
        convst_ref[0] = u[TQ - 2:]

    _mix_out(x, attn, conv_out, cross, g_mix_ref, w_out_ref, g_ffn_ref, w_r_ref, b_r_ref,
             h_ref, hn_ref, tope_ref, gates_ref)


def _mixer_p(x, sinks, g_attn, w_in, conv_w, g_mix, w_out, g_ffn, w_r, b_r, mkt, mvb):
    b, s, d = x.shape
    nj = s // TQ
    n = b * s
    cw = conv_w.shape[1]
    full = lambda a: pl.BlockSpec(a.shape, lambda bi, ji, *_: (0,) * a.ndim)
    tok = lambda w: pl.BlockSpec((TQ, w), lambda bi, ji, *_: (bi * nj + ji, 0))
    per_b = lambda r, c: pl.BlockSpec((1, r, c), lambda bi, ji, *_: (bi, 0, 0))
    grid_spec = pltpu.PrefetchScalarGridSpec(
        num_scalar_prefetch=1,
        grid=(b, nj),
        in_specs=[pl.BlockSpec((1, TQ, d), lambda bi, ji, *_: (bi, ji, 0)),
                  full(g_attn), full(w_in), full(conv_w), full(g_mix), full(w_out), full(g_ffn),
                  full(w_r), full(b_r), per_b(MEM_WIDTH, mkt.shape[2]), per_b(mvb.shape[1], MEM_WIDTH)],
        out_specs=[tok(d), pl.BlockSpec((TQ * SUBLANES, LANES), lambda bi, ji, *_: (bi * nj + ji, 0)),
                   tok(TOP_K), tok(TOP_K),
                   per_b(WINDOW, KV_WIDTH), per_b(WINDOW, KV_WIDTH), per_b(2, cw)],
        scratch_shapes=[pltpu.VMEM((WINDOW, KV_WIDTH), F32), pltpu.VMEM((WINDOW, KV_WIDTH), F32),
                        pltpu.VMEM((SUBLANES, cw), F32)],
    )
    return pl.pallas_call(
        _mixer_p_kernel,
        grid_spec=grid_spec,
        out_shape=[jax.ShapeDtypeStruct((n, d), F32), jax.ShapeDtypeStruct((n * SUBLANES, LANES), F32),
                   jax.ShapeDtypeStruct((n, TOP_K), jnp.int32), jax.ShapeDtypeStruct((n, TOP_K), F32),
                   jax.ShapeDtypeStruct((b, WINDOW, KV_WIDTH), F32),
                   jax.ShapeDtypeStruct((b, WINDOW, KV_WIDTH), F32),
                   jax.ShapeDtypeStruct((b, 2, cw), F32)],
        compiler_params=pltpu.CompilerParams(dimension_semantics=("arbitrary", "arbitrary"),
                                             vmem_limit_bytes=VMEM_LIMIT),
        name="mixer_p",
    )(sinks, x, g_attn, w_in, conv_w, g_mix, w_out, g_ffn, w_r, b_r, mkt, mvb)


def _per_head_column(values, hrow):
    col = jnp.zeros(hrow.shape, F32)
    for h in range(N_Q_HEADS):
        col = jnp.where(hrow == h, values[h], col)
    return col


def _mixer_s_kernel(sinks_ref, x_ref, pm1_ref, pm2_ref, wk_ref, wv_ref, mk_ref, mv_ref,
                    g_attn_ref, w_in_ref, conv_w_ref, g_mix_ref, w_out_ref, g_ffn_ref, w_r_ref, b_r_ref,
                    h_ref, hn_ref, tope_ref, gates_ref, nwk_ref, nwv_ref, u_ref, *, t_dec):
    r_tok = BB * t_dec
    r_exp = r_tok * REP
    qrows = t_dec * REP
    x = x_ref[...]
    xn = (_rms(x) * g_attn_ref[...]).astype(BF16)
    z = _dot(xn, w_in_ref[...])
    c0 = ATTN_WIDTH
    c1 = c0 + KV_WIDTH
    c2 = c1 + KV_WIDTH
    cw = conv_w_ref.shape[1]
    c3, c4, c5 = c2 + cw, c2 + 2 * cw, c2 + 3 * cw
    q = z[:, :c0] * ATTN_SCALE
    k_new = z[:, c0:c1]
    v_new = z[:, c1:c2]
    cb = z[:, c2:c3]
    cc = z[:, c3:c4]
    cvv = z[:, c4:c5]
    mq = z[:, c5:] * ATTN_SCALE
    win = wk_ref.shape[1]

    xi = _iota((KV_WIDTH, ATTN_WIDTH), 0)
    xl = _iota((KV_WIDTH, ATTN_WIDTH), 1)
    q_per_kv = N_Q_HEADS // N_KV_HEADS
    expand = (xi == (xl // (q_per_kv * HEAD_DIM)) * HEAD_DIM + xl % HEAD_DIM).astype(BF16)
    rr = _iota((r_exp, r_tok), 0)
    rc = _iota((r_exp, r_tok), 1)
    rep = (rr // REP == rc).astype(BF16)

    hrow = _iota((r_exp, 1), 0) % REP
    trow = (_iota((r_exp, 1), 0) // REP) % t_dec
    slope_col = _per_head_column(ALIBI_SLOPES, hrow)
    sink_col = _per_head_column([sinks_ref[h] for h in range(N_Q_HEADS)], hrow)

    qexp = jnp.where(hrow == _iota((r_exp, ATTN_WIDTH), 1) // HEAD_DIM, _dot(rep, q.astype(BF16)), 0.0)
    kexp = _dot(wk_ref[...].reshape(BB * win, KV_WIDTH).astype(BF16), expand).astype(BF16)
    vexp = _dot(wv_ref[...].reshape(BB * win, KV_WIDTH).astype(BF16), expand).astype(BF16)
    s = jnp.einsum("bqc,bkc->bqk", qexp.astype(BF16).reshape(BB, qrows, ATTN_WIDTH),
                   kexp.reshape(BB, win, ATTN_WIDTH), preferred_element_type=F32).reshape(r_exp, win)
    scol = _iota((r_exp, win), 1)
    s = s - slope_col * (win + trow - scol).astype(F32)
    s = jnp.where(scol > trow, s, -jnp.inf)
    knew_exp = _dot(k_new.astype(BF16), expand).astype(BF16)
    vnew_exp = _dot(v_new.astype(BF16), expand).astype(BF16)
    s_new, v_rep = [], []
    for jn in range(t_dec):
        rep_j = (rc == (rr // qrows) * t_dec + jn).astype(BF16)
        k_rep = _dot(rep_j, knew_exp)
        v_rep.append(_dot(rep_j, vnew_exp))
        sj = jnp.sum(qexp * k_rep, axis=-1, keepdims=True) - slope_col * (trow - jn).astype(F32)
        s_new.append(jnp.where(trow >= jn, sj, -jnp.inf))
    m = jnp.maximum(jnp.max(s, axis=-1, keepdims=True), sink_col)
    for sj in s_new:
        m = jnp.maximum(m, sj)
    pe = jnp.exp(s - m)
    denom = jnp.sum(pe, axis=-1, keepdims=True) + jnp.exp(sink_col - m)
    o = jnp.einsum("bqk,bkc->bqc", pe.astype(BF16).reshape(BB, qrows, win),
                   vexp.reshape(BB, win, ATTN_WIDTH), preferred_element_type=F32).reshape(r_exp, ATTN_WIDTH)
    for jn in range(t_dec):
        pj = jnp.exp(s_new[jn] - m)
        denom = denom + pj
        o = o + pj * v_rep[jn]
    o = jnp.where(hrow == _iota((r_exp, ATTN_WIDTH), 1) // HEAD_DIM, o / denom, 0.0)
    attn = jnp.sum(o.reshape(r_tok, REP, ATTN_WIDTH), axis=1)

    m_tok = mk_ref.shape[1]
    mhead = _iota((r_exp, MEM_WIDTH), 1) // HEAD_DIM
    mqexp = jnp.where(hrow == mhead, _dot(rep, mq.astype(BF16)), 0.0).astype(BF16)
    sm = jnp.einsum("bqc,bmc->bqm", mqexp.reshape(BB, qrows, MEM_WIDTH), mk_ref[...].astype(BF16),
                    preferred_element_type=F32).reshape(r_exp, m_tok)
    mm = jnp.max(sm, axis=-1, keepdims=True)
    pm = jnp.exp(sm - mm)
    dm = jnp.sum(pm, axis=-1, keepdims=True)
    om = jnp.einsum("bqm,bmc->bqc", pm.astype(BF16).reshape(BB, qrows, m_tok), mv_ref[...].astype(BF16),
                    preferred_element_type=F32).reshape(r_exp, MEM_WIDTH)
    om = jnp.where(hrow == mhead, om / dm, 0.0)
    cross = jnp.sum(om.reshape(r_tok, REP, MEM_WIDTH), axis=1)

    u = cc * cvv
    tt = _iota(u.shape, 0) % t_dec
    u1 = jnp.where(tt >= 1, pltpu.roll(u, 1, axis=0), pm1_ref[...])
    u2 = jnp.where(tt >= 2, pltpu.roll(u, 2, axis=0), pm2_ref[...])
    conv_out = cb * (conv_w_ref[0:1, :] * u2 + conv_w_ref[1:2, :] * u1 + conv_w_ref[2:3, :] * u)
    u_ref[...] = u

    nwk_ref[:, 0:win - t_dec, :] = wk_ref[:, t_dec:win, :]
    nwv_ref[:, 0:win - t_dec, :] = wv_ref[:, t_dec:win, :]
    for b in range(BB):
        nwk_ref[b, win - t_dec:win, :] = k_new[b * t_dec:(b + 1) * t_dec, :]
        nwv_ref[b, win - t_dec:win, :] = v_new[b * t_dec:(b + 1) * t_dec, :]

    _mix_out(x, attn, conv_out, cross, g_mix_ref, w_out_ref, g_ffn_ref, w_r_ref, b_r_ref,
             h_ref, hn_ref, tope_ref, gates_ref)


def _mixer_s(x2, t_dec, pm1, pm2, wk, wv, mk, mv, sinks, g_attn, w_in, conv_w, g_mix, w_out, g_ffn, w_r, b_r):
    n, d = x2.shape
    nb = wk.shape[0]
    win = wk.shape[1]
    m_tok = mk.shape[1]
    cw = conv_w.shape[1]
    r_tok = BB * t_dec
    full = lambda a: pl.BlockSpec(a.shape, lambda i, *_: (0,) * a.ndim)
    tok = lambda w: pl.BlockSpec((r_tok, w), lambda i, *_: (i, 0))
    per_b = lambda r, c: pl.BlockSpec((BB, r, c), lambda i, *_: (i, 0, 0))
    grid_spec = pltpu.PrefetchScalarGridSpec(
        num_scalar_prefetch=1,
        grid=(nb // BB,),
        in_specs=[tok(d), tok(cw), tok(cw), per_b(win, KV_WIDTH), per_b(win, KV_WIDTH),
                  per_b(m_tok, MEM_WIDTH), per_b(m_tok, MEM_WIDTH),
                  full(g_attn), full(w_in), full(conv_w), full(g_mix), full(w_out), full(g_ffn),
                  full(w_r), full(b_r)],
        out_specs=[tok(d), pl.BlockSpec((r_tok * SUBLANES, LANES), lambda i, *_: (i, 0)),
                   tok(TOP_K), tok(TOP_K), per_b(win, KV_WIDTH), per_b(win, KV_WIDTH), tok(cw)],
    )
    return pl.pallas_call(
        functools.partial(_mixer_s_kernel, t_dec=t_dec),
        grid_spec=grid_spec,
        out_shape=[jax.ShapeDtypeStruct((n, d), F32), jax.ShapeDtypeStruct((n * SUBLANES, LANES), F32),
                   jax.ShapeDtypeStruct((n, TOP_K), jnp.int32), jax.ShapeDtypeStruct((n, TOP_K), F32),
                   jax.ShapeDtypeStruct((nb, win, KV_WIDTH), F32), jax.ShapeDtypeStruct((nb, win, KV_WIDTH), F32),
                   jax.ShapeDtypeStruct((n, cw), F32)],
        compiler_params=pltpu.CompilerParams(dimension_semantics=("arbitrary",), vmem_limit_bytes=VMEM_LIMIT),
        name="mixer_s",
    )(sinks, x2, pm1, pm2, wk, wv, mk, mv, g_attn, w_in, conv_w, g_mix, w_out, g_ffn, w_r, b_r)


def _lpos_kernel(tope_ref, lpos_ref, counts_ref, tri_ref, *, n_e):
    i = pl.program_id(0)
    tm = tope_ref.shape[0]

    @pl.when(i == 0)
    def _():
        tri_ref[...] = (_iota((tm, tm), 0) > _iota((tm, tm), 1)).astype(BF16)

    te = tope_ref[...]
    col = _iota((tm, n_e), 1)
    hits = [te[:, k:k + 1] == col for k in range(TOP_K)]
    onehot = jnp.zeros((tm, n_e), F32)
    lower = jnp.zeros((tm, n_e), F32)
    for k in range(TOP_K):
        onehot = onehot + hits[k].astype(F32)
        lower = lower + (te[:, k:k + 1] < col).astype(F32)
    ahead = _dot(tri_ref[...], onehot.astype(BF16)) + jnp.sum(lower, axis=0, keepdims=True)
    col4 = _iota((tm, TOP_K), 1)
    pos = jnp.zeros((tm, TOP_K), F32)
    for k in range(TOP_K):
        pos = jnp.where(col4 == k, jnp.sum(jnp.where(hits[k], ahead, 0.0), axis=-1, keepdims=True), pos)
    half = (i % 2) * (tm * TOP_K)
    lpos_ref[...] = (pos.astype(jnp.int32) + half) * SUBLANES
    counts_ref[0] = jnp.sum(onehot, axis=0, keepdims=True)


def _lpos(tope, tm, n_e):
    n = tope.shape[0]
    return pl.pallas_call(
        functools.partial(_lpos_kernel, n_e=n_e),
        grid=(n // tm,),
        in_specs=[pl.BlockSpec((tm, TOP_K), lambda i: (i, 0))],
        out_specs=[pl.BlockSpec((tm, TOP_K), lambda i: (i, 0)), pl.BlockSpec((1, 1, n_e), lambda i: (i, 0, 0))],
        out_shape=[jax.ShapeDtypeStruct((n, TOP_K), jnp.int32), jax.ShapeDtypeStruct((n // tm, 1, n_e), F32)],
        scratch_shapes=[pltpu.VMEM((tm, tm), BF16)],
        compiler_params=pltpu.CompilerParams(dimension_semantics=("arbitrary",)),
        name="lpos",
    )(tope)


def _rows(ref, first, n_rows):
    return ref.at[pl.ds(pl.multiple_of(first, SUBLANES), n_rows * SUBLANES)]


def _group_chunks(meta, g, n_e, max_rows, make_copy, wait):
    off_ref, cnt_ref, lst_ref = meta
    if wait:
        make_copy(0, 0, max_rows * TOP_K).wait()
        return

    def per_expert(e, c):
        off = off_ref[g * n_e + e]
        lst = lst_ref[g * n_e + e]
        _run_copies(cnt_ref[g * n_e + e], max_rows, lambda done, size: make_copy(lst + done, off + done, size),
                    wait=False)
        return c

    lax.fori_loop(0, n_e, per_expert, 0)


def _run_copies(n_rows, max_rows, make_copy, wait):
    done = 0
    for size in [1 << b for b in range(max_rows.bit_length() - 1, -1, -1)]:
        bit = n_rows & (size * SUBLANES)

        @pl.when(bit != 0)
        def _():
            cp = make_copy(done, size)
            cp.wait() if wait else cp.start()
        done = done + bit


def _dispatch_kernel(off_ref, cnt_ref, lst_ref, zrow_ref, zcnt_ref, nused_ref, lpos_ref, hn_a_ref, hn_b_ref, xs_ref,
                     stage_ref, zbuf_ref, sem, zsem, *, n_e, n_blocks, groups_a):
    j = pl.program_id(0)
    nj = pl.num_programs(0)
    tm = hn_a_ref.shape[0] // SUBLANES
    slot = j % 2
    meta = (off_ref, cnt_ref, lst_ref)

    def chunks(g, s, wait):
        _group_chunks(meta, g, n_e, tm,
                      lambda lrow, grow, size: pltpu.make_async_copy(
                          _rows(stage_ref, s * (tm * TOP_K * SUBLANES) + lrow, size), _rows(xs_ref, grow, size),
                          sem.at[s]), wait)

    def zero_fill(wait):
        def zero_run(first_row, n_rows):
            _run_copies(n_rows, BM, lambda done, size: pltpu.make_async_copy(
                _rows(zbuf_ref, 0, size), _rows(xs_ref, first_row + done, size), zsem), wait)

        def expert_pad(e, c):
            zero_run(zrow_ref[e], zcnt_ref[e])
            return c

        def tail_block(b, c):
            zero_run(b * (BM * SUBLANES), jnp.int32(BM * SUBLANES))
            return c

        lax.fori_loop(0, n_e, expert_pad, 0)
        lax.fori_loop(nused_ref[0], n_blocks, tail_block, 0)

    @pl.when(j == 0)
    def _():
        zbuf_ref[...] = jnp.zeros_like(zbuf_ref)
        zero_fill(False)

    @pl.when(j >= 2)
    def _():
        chunks(j - 2, slot, True)

    def place_from(hn_ref):
        def place(t, c):
            tile = hn_ref[pl.ds(pl.multiple_of(t * SUBLANES, SUBLANES), SUBLANES), :]
            for k in range(TOP_K):
                pos = pl.multiple_of(lpos_ref[0, 0, t * TOP_K + k], SUBLANES)
                stage_ref[pl.ds(pos, SUBLANES), :] = tile
            return c
        lax.fori_loop(0, tm, place, 0, unroll=TOKEN_UNROLL)

    @pl.when(j < groups_a)
    def _():
        place_from(hn_a_ref)

    @pl.when(j >= groups_a)
    def _():
        place_from(hn_b_ref)

    chunks(j, slot, False)

    @pl.when(j == nj - 1)
    def _():
        @pl.when(j >= 1)
        def _():
            chunks(j - 1, 1 - slot, True)
        chunks(j, slot, True)
        zero_fill(True)


def _dispatch(hn_a, hn_b, lpos, meta, zrow, zcnt, nused, tm, n_e, n_blocks):
    nt = lpos.shape[0] // tm
    groups_a = hn_a.shape[0] // (tm * SUBLANES)
    assert groups_a >= 1 and groups_a + hn_b.shape[0] // (tm * SUBLANES) == nt
    grid_spec = pltpu.PrefetchScalarGridSpec(
        num_scalar_prefetch=6,
        grid=(nt,),
        in_specs=[pl.BlockSpec((1, 1, tm * TOP_K), lambda j, *_: (j, 0, 0), memory_space=pltpu.SMEM),
                  pl.BlockSpec((tm * SUBLANES, LANES), lambda j, *_: (jnp.minimum(j, groups_a - 1), 0)),
                  pl.BlockSpec((tm * SUBLANES, LANES), lambda j, *_: (jnp.maximum(j - groups_a, 0), 0))],
        out_specs=pl.BlockSpec(memory_space=pl.ANY),
        scratch_shapes=[pltpu.VMEM((2 * tm * TOP_K * SUBLANES, LANES), F32), pltpu.VMEM((BM * SUBLANES, LANES), F32),
                        pltpu.SemaphoreType.DMA((2,)), pltpu.SemaphoreType.DMA(())],
    )
    return pl.pallas_call(
        functools.partial(_dispatch_kernel, n_e=n_e, n_blocks=n_blocks, groups_a=groups_a),
        grid_spec=grid_spec,
        out_shape=jax.ShapeDtypeStruct((n_blocks * BM * SUBLANES, LANES), F32),
        compiler_params=pltpu.CompilerParams(dimension_semantics=("arbitrary",), vmem_limit_bytes=VMEM_LIMIT),
        name="dispatch",
    )(*meta, zrow, zcnt, nused, lpos.reshape(nt, 1, tm * TOP_K), hn_a, hn_b)


def _experts_kernel(blk_e_ref, nused_ref, next_e_ref, blk_rows_ref, xs_ref, bgu_ref, bd_ref, wgu_hbm_ref, wd_hbm_ref, ys_ref,
                    wgu_ref, wd_ref, wgu_bf_ref, wd_bf_ref, wsem):
    i = pl.program_id(0)
    nused = nused_ref[0]
    d, d_ff2 = wgu_ref.shape
    grp = 2 * LANES

    def weight_copies(e):
        gu_cols = d_ff2 // W_PARTS
        dn_rows = (d_ff2 // 2) // W_PARTS
        cps = []
        for p in range(W_PARTS):
            cps.append(pltpu.make_async_copy(wgu_hbm_ref.at[e, :, pl.ds(p * gu_cols, gu_cols)],
                                             wgu_ref.at[:, pl.ds(p * gu_cols, gu_cols)], wsem.at[p]))
            cps.append(pltpu.make_async_copy(wd_hbm_ref.at[e, pl.ds(p * dn_rows, dn_rows), :],
                                             wd_ref.at[pl.ds(p * dn_rows, dn_rows), :], wsem.at[W_PARTS + p]))
        return cps

    @pl.when(i < nused)
    def _():
        e = blk_e_ref[i]
        e_prev = blk_e_ref[jnp.maximum(i - 1, 0)]

        @pl.when(i == 0)
        def _():
            for cp in weight_copies(e):
                cp.start()

        @pl.when((i == 0) | (e != e_prev))
        def _():
            for cp in weight_copies(e):
                cp.wait()
            pr = _iota((grp, grp), 0)
            pc = _iota((grp, grp), 1)
            perm = (pr == jnp.where(pc < LANES, 2 * pc, 2 * (pc - LANES) + 1)).astype(BF16)
            for g in range(d_ff2 // grp):
                w = wgu_ref[:, g * grp:(g + 1) * grp].astype(BF16)
                wgu_bf_ref[:, g * grp:(g + 1) * grp] = _dot(w, perm).astype(BF16)
            wd_bf_ref[...] = wd_ref[...].astype(BF16)

            @pl.when(next_e_ref[e] >= 0)
            def _():
                for cp in weight_copies(next_e_ref[e]):
                    cp.start()

        def ffn(r0, n_rows):
            x = jnp.concatenate([_load_token_tiles(xs_ref, n_rows, s, base=r0 * SUBLANES).astype(BF16)
                                 for s in range(SUBLANES)], axis=1)
            hgu = _dot(x, wgu_bf_ref[...]) + bgu_ref[0]
            acts = []
            for g in range(d_ff2 // grp):
                gate = jnp.minimum(hgu[:, g * grp:g * grp + LANES], SWIGLU_LIMIT)
                up = jnp.clip(hgu[:, g * grp + LANES:(g + 1) * grp], -SWIGLU_LIMIT, SWIGLU_LIMIT)
                glu = gate * (1.0 / (1.0 + jnp.exp(-SWIGLU_ALPHA * gate)))
                acts.append(((up + 1.0) * glu).astype(BF16))
            act = jnp.concatenate(acts, axis=1)
            _store_token_tiles(ys_ref, _dot(act, wd_bf_ref[...]) + bd_ref[0], base=r0 * SUBLANES)

        rows = blk_rows_ref[i]

        @pl.when(rows == BM)
        def _():
            ffn(0, BM)

        @pl.when(rows < BM)
        def _():
            for r0 in range(0, BM, TAIL_CHUNK):
                @pl.when(r0 < rows)
                def _():
                    ffn(r0, TAIL_CHUNK)

                @pl.when(r0 >= rows)
                def _():
                    ys_ref[r0 * SUBLANES:(r0 + TAIL_CHUNK) * SUBLANES, :] = jnp.zeros(
                        (TAIL_CHUNK * SUBLANES, LANES), F32)

    @pl.when(i >= nused)
    def _():
        ys_ref[...] = jnp.zeros_like(ys_ref)


def _experts(xs, blk_e, nused, next_e, blk_rows, w_gate_up, b_gu_perm, w_down, b_down):
    n_e, d, d_ff2 = w_gate_up.shape
    n_blocks = xs.shape[0] // (BM * SUBLANES)
    expert = lambda i, be, nu, *_: be[jnp.minimum(i, jnp.maximum(nu[0] - 1, 0))]
    rows_spec = pl.BlockSpec((BM * SUBLANES, LANES), lambda i, *_: (i, 0))
    grid_spec = pltpu.PrefetchScalarGridSpec(
        num_scalar_prefetch=4,
        grid=(n_blocks,),
        in_specs=[pl.BlockSpec((BM * SUBLANES, LANES), lambda i, be, nu, *_: (jnp.minimum(i, jnp.maximum(nu[0] - 1, 0)), 0)),
                  pl.BlockSpec((1, 1, d_ff2), lambda i, *s: (expert(i, *s), 0, 0)),
                  pl.BlockSpec((1, 1, d), lambda i, *s: (expert(i, *s), 0, 0)),
                  pl.BlockSpec(memory_space=pl.ANY), pl.BlockSpec(memory_space=pl.ANY)],
        out_specs=rows_spec,
        scratch_shapes=[pltpu.VMEM((d, d_ff2), F32), pltpu.VMEM((d_ff2 // 2, d), F32),
                        pltpu.VMEM((d, d_ff2), BF16), pltpu.VMEM((d_ff2 // 2, d), BF16),
                        pltpu.SemaphoreType.DMA((2 * W_PARTS,))],
    )
    return pl.pallas_call(
        _experts_kernel,
        grid_spec=grid_spec,
        out_shape=jax.ShapeDtypeStruct(xs.shape, F32),
        compiler_params=pltpu.CompilerParams(dimension_semantics=("arbitrary",), vmem_limit_bytes=VMEM_LIMIT),
        name="experts",
    )(blk_e, nused, next_e, blk_rows, xs, b_gu_perm, b_down, w_gate_up, w_down)


def _combine_kernel(off_ref, cnt_ref, lst_ref, lpos_ref, gates_ref, h_ref, g_ref, ys_ref, out_ref,
                    stage_ref, acc_ref, sem, *, n_e, g0):
    i = pl.program_id(0)
    ng = pl.num_programs(0)
    tm, d = h_ref.shape
    slot = (g0 + i) % 2
    meta = (off_ref, cnt_ref, lst_ref)

    def chunks(g, s, wait):
        _group_chunks(meta, g, n_e, tm,
                      lambda lrow, grow, size: pltpu.make_async_copy(
                          _rows(ys_ref, grow, size), _rows(stage_ref, s * (tm * TOP_K * SUBLANES) + lrow, size),
                          sem.at[s]), wait)

    @pl.when(i == 0)
    def _():
        chunks(g0, slot, False)

    @pl.when(i + 1 < ng)
    def _():
        chunks(g0 + i + 1, 1 - slot, False)

    chunks(g0 + i, slot, True)

    def mix(t, c):
        acc = None
        for k in range(TOP_K):
            pos = pl.multiple_of(lpos_ref[0, 0, t * TOP_K + k], SUBLANES)
            term = stage_ref[pl.ds(pos, SUBLANES), :] * gates_ref[0, 0, t * TOP_K + k]
            acc = term if acc is None else acc + term
        acc_ref[pl.ds(pl.multiple_of(t * SUBLANES, SUBLANES), SUBLANES), :] = acc
        return c

    lax.fori_loop(0, tm, mix, 0, unroll=TOKEN_UNROLL)

    h = h_ref[...]
    parts = []
    sq = jnp.zeros((tm, LANES), F32)
    for s in range(d // LANES):
        y = h[:, s * LANES:(s + 1) * LANES] + _load_token_tiles(acc_ref, tm, s)
        sq = sq + y * y
        parts.append(y)
    rinv = lax.rsqrt(jnp.sum(sq, axis=-1, keepdims=True) / d + EPS)
    out_ref[...] = jnp.concatenate(parts, axis=1) * rinv * g_ref[...]


def _combine(ys, h, lpos, gates, meta, g_final, tm, n_e, g0):
    n, d = h.shape
    grid_spec = pltpu.PrefetchScalarGridSpec(
        num_scalar_prefetch=3,
        grid=(n // tm,),
        in_specs=[pl.BlockSpec((1, 1, tm * TOP_K), lambda i, *_: (g0 + i, 0, 0), memory_space=pltpu.SMEM),
                  pl.BlockSpec((1, 1, tm * TOP_K), lambda i, *_: (g0 + i, 0, 0), memory_space=pltpu.SMEM),
                  pl.BlockSpec((tm, d), lambda i, *_: (i, 0)),
                  pl.BlockSpec((1, d), lambda i, *_: (0, 0)),
                  pl.BlockSpec(memory_space=pl.ANY)],
        out_specs=pl.BlockSpec((tm, d), lambda i, *_: (i, 0)),
        scratch_shapes=[pltpu.VMEM((2 * tm * TOP_K * SUBLANES, LANES), F32), pltpu.VMEM((tm * SUBLANES, LANES), F32),
                        pltpu.SemaphoreType.DMA((2,))],
    )
    ngroups = lpos.shape[0] // tm
    return pl.pallas_call(
        functools.partial(_combine_kernel, n_e=n_e, g0=g0),
        grid_spec=grid_spec,
        out_shape=jax.ShapeDtypeStruct((n, d), F32),
        compiler_params=pltpu.CompilerParams(dimension_semantics=("arbitrary",), vmem_limit_bytes=VMEM_LIMIT),
        name="combine",
    )(*meta, lpos.reshape(ngroups, 1, tm * TOP_K), gates.reshape(ngroups, 1, tm * TOP_K), h, g_final, ys)


def _largest_tile(cands, *sizes):
    for c in cands:
        if all(s % c == 0 for s in sizes):
            return c
    raise ValueError(f"no tile in {cands} divides {sizes}")


def kernel(x_prompt, x_sample, mem_prompt, cache_win_k, cache_win_v, state_conv, cache_mem_k, cache_mem_v, g_attn_norm, w_in, conv_w, attn_sinks, g_mem_norm, w_mem_kv, g_mix_out, w_out, g_ffn_norm, w_router, b_router, w_gate_up, b_gate_up, w_down, b_down, g_final):
    depth = w_in.shape[0]
    assert depth == 1, "single-layer step"
    b, s, d = x_prompt.shape
    nb, t_dec, _ = x_sample.shape
    n_e = w_router.shape[2]
    d_ff2 = w_gate_up.shape[3]
    cw = conv_w.shape[2]
    win = cache_win_k.shape[2]
    m_tok = cache_mem_k.shape[2]
    assert s % TQ == 0 and nb % BB == 0 and win == WINDOW and t_dec <= SUBLANES and d_ff2 % (2 * LANES) == 0
    assert d == SUBLANES * LANES, "token-tile layout: one token is one (8, 128) f32 tile"

    row = lambda a: a.reshape(1, -1)
    w_in_bf = w_in[0].astype(BF16)
    w_out_bf = w_out[0].astype(BF16)
    w_r_bf = w_router[0].astype(BF16)
    sinks = attn_sinks[0].astype(F32)
    shared = (row(g_attn_norm[0]), w_in_bf, conv_w[0], row(g_mix_out[0]), w_out_bf, row(g_ffn_norm[0]),
              w_r_bf, row(b_router[0]))

    mk_p, mv_p, mkt, mvb = _memkv(mem_prompt, row(g_mem_norm[0]), w_mem_kv[0].astype(BF16))
    h_p, hn_p, tope_p, gates_p, lastk, lastv, convst = _mixer_p(x_prompt, sinks, *shared, mkt, mvb)

    zeros = lambda r: jnp.zeros((nb, r, cw), F32)
    st = state_conv[0]
    pm1 = jnp.concatenate([st[:, 1:2], zeros(t_dec - 1)], axis=1).reshape(nb * t_dec, cw)
    pm2 = jnp.concatenate([st, zeros(t_dec - 2)], axis=1).reshape(nb * t_dec, cw)
    h_s, hn_s, tope_s, gates_s, nwk, nwv, u_s = _mixer_s(
        x_sample.reshape(nb * t_dec, d), t_dec, pm1, pm2,
        cache_win_k[0].reshape(nb, win, KV_WIDTH), cache_win_v[0].reshape(nb, win, KV_WIDTH),
        cache_mem_k[0].reshape(nb, m_tok, MEM_WIDTH), cache_mem_v[0].reshape(nb, m_tok, MEM_WIDTH),
        sinks, *shared)

    n_p, n_s = b * s, nb * t_dec
    n = n_p + n_s
    tm = _largest_tile((512, 256, 128, 64, 32, 16, 8), n_p, n_s)
    tope = jnp.concatenate([tope_p, tope_s], axis=0)
    gates = jnp.concatenate([gates_p, gates_s], axis=0)

    lpos, cnt_f = _lpos(tope, tm, n_e)
    cnt = cnt_f[:, 0, :].astype(jnp.int32)
    counts = jnp.sum(cnt, axis=0)
    padded = (counts + BM - 1) // BM * BM
    pad_ends = jnp.cumsum(padded)
    pad_starts = pad_ends - padded
    nk = n * TOP_K
    n_blocks = -(-nk // BM) + n_e
    nused = (pad_ends[-1:] // BM).astype(jnp.int32)
    blk_start = jnp.arange(n_blocks, dtype=jnp.int32) * BM
    blk_e = jnp.minimum(jnp.sum((pad_ends[None, :] <= blk_start[:, None]).astype(jnp.int32), axis=1), n_e - 1)
    zrow = ((pad_starts + counts) * SUBLANES).astype(jnp.int32)
    zcnt = ((padded - counts) * SUBLANES).astype(jnp.int32)
    off = pad_starts[None, :] + jnp.cumsum(cnt, axis=0) - cnt
    lstart = jnp.cumsum(cnt, axis=1) - cnt
    meta = tuple((a.reshape(-1) * SUBLANES).astype(jnp.int32) for a in (off, cnt, lstart))

    xs = _dispatch(hn_p, hn_s, lpos, meta, zrow, zcnt, nused, tm, n_e, n_blocks)

    grp = 2 * LANES
    b_gu = b_gate_up[0].reshape(n_e, d_ff2 // grp, LANES, 2).transpose(0, 1, 3, 2).reshape(n_e, 1, d_ff2)
    owner = jnp.where(padded > 0, jnp.arange(n_e, dtype=jnp.int32), n_e)
    following = jnp.concatenate([lax.cummin(owner, reverse=True)[1:], jnp.full((1,), n_e, jnp.int32)])
    next_e = jnp.where(following < n_e, following, -1).astype(jnp.int32)
    own = blk_e[:, None] == jnp.arange(n_e, dtype=jnp.int32)[None, :]
    blk_rows = jnp.clip(jnp.sum(jnp.where(own, (pad_starts + counts)[None, :], 0), axis=1) - blk_start, 0, BM)
    blk_rows = blk_rows.astype(jnp.int32)
    ys = _experts(xs, blk_e, nused, next_e, blk_rows, w_gate_up[0], b_gu, w_down[0], b_down[0].reshape(n_e, 1, d))

    g_fin = row(g_final)
    y_p = _combine(ys, h_p, lpos, gates, meta, g_fin, tm, n_e, 0)
    y_s = _combine(ys, h_s, lpos, gates, meta, g_fin, tm, n_e, n_p // tm)

    kv5 = lambda a, bsz, r, hds: a.reshape(1, bsz, r, hds, HEAD_DIM)
    return (y_p.reshape(b, s, d), y_s.reshape(nb, t_dec, d),
            kv5(lastk, b, WINDOW, N_KV_HEADS), kv5(lastv, b, WINDOW, N_KV_HEADS),
            convst.reshape(1, b, 2, cw),
            kv5(mk_p, b, m_tok, N_MEM_HEADS), kv5(mv_p, b, m_tok, N_MEM_HEADS),
            kv5(nwk, nb, win, N_KV_HEADS), kv5(nwv, nb, win, N_KV_HEADS),
            u_s.reshape(nb, t_dec, cw)[:, t_dec - 2:].reshape(1, nb, 2, cw))
```

```python
import functools

import jax
import jax.numpy as jnp
from jax import lax
from jax.experimental import pallas as pl
from jax.experimental.pallas import tpu as pltpu

F32 = jnp.float32
BF16 = jnp.bfloat16

HEAD_DIM = 64
N_Q_HEADS = 8
N_KV_HEADS = 2
WINDOW = 128
ATTN_WIDTH = N_Q_HEADS * HEAD_DIM
KV_WIDTH = N_KV_HEADS * HEAD_DIM
N_MEM_HEADS = 4
MEM_WIDTH = N_MEM_HEADS * HEAD_DIM
TOP_K = 4
SWIGLU_LIMIT = 7.0
SWIGLU_ALPHA = 1.702
EPS = 1e-5
ATTN_SCALE = HEAD_DIM ** -0.5
ALIBI_SLOPES = tuple(2.0 ** (-8.0 * (h + 1) / N_Q_HEADS) for h in range(N_Q_HEADS))

LANES = 128
SUBLANES = 8
VMEM_LIMIT = 56 * 1024 * 1024

TQ = 1024
BB = 16
BM = 512
REP = 8
W_PARTS = 4
MIX_CHUNK = 512
IN_CHUNK = 256
TAIL_CHUNK = 128
TOKEN_UNROLL = 16


def _rms(x):
    return x * lax.rsqrt(jnp.mean(x * x, axis=-1, keepdims=True) + EPS)


def _dot(a, b):
    return jnp.dot(a, b, preferred_element_type=F32)


def _dot_nt(a, b):
    return lax.dot_general(a, b, (((1,), (1,)), ((), ())), preferred_element_type=F32)


def _iota(shape, axis):
    return lax.broadcasted_iota(jnp.int32, shape, axis)


def _store_token_tiles(ref, x, base=0):
    t = x.shape[0]
    for s in range(x.shape[1] // LANES):
        ref[pl.ds(base + s, t, stride=SUBLANES), :] = x[:, s * LANES:(s + 1) * LANES]


def _load_token_tiles(ref, t, s, base=0):
    return ref[pl.ds(base + s, t, stride=SUBLANES), :]


def _memkv_kernel(mem_ref, g_ref, w_ref, mk_ref, mv_ref, mkt_ref, mvb_ref):
    xn = (_rms(mem_ref[0]) * g_ref[...]).astype(BF16)
    kv = _dot(xn, w_ref[...])
    mk = kv[:, :MEM_WIDTH]
    mv = kv[:, MEM_WIDTH:]
    mk_ref[0] = mk
    mv_ref[0] = mv
    mkt_ref[0] = mk.T.astype(BF16)
    mvb_ref[0] = mv.astype(BF16)


def _memkv(mem, g, w_bf):
    b, m, d = mem.shape
    out_f = jax.ShapeDtypeStruct((b, m, MEM_WIDTH), F32)
    out_b = jax.ShapeDtypeStruct((b, m, MEM_WIDTH), BF16)
    out_t = jax.ShapeDtypeStruct((b, MEM_WIDTH, m), BF16)
    blk = lambda r, c: pl.BlockSpec((1, r, c), lambda i: (i, 0, 0))
    return pl.pallas_call(
        _memkv_kernel,
        grid=(b,),
        in_specs=[blk(m, d), pl.BlockSpec((1, d), lambda i: (0, 0)),
                  pl.BlockSpec((d, 2 * MEM_WIDTH), lambda i: (0, 0))],
        out_specs=[blk(m, MEM_WIDTH), blk(m, MEM_WIDTH), blk(MEM_WIDTH, m), blk(m, MEM_WIDTH)],
        out_shape=[out_f, out_f, out_t, out_b],
        name="memkv",
    )(mem, g, w_bf)


def _router_topk(hn, w_r_ref, b_r_ref, tope_ref, gates_ref, rows_at):
    n_e = w_r_ref.shape[1]
    logits = _dot(hn.astype(BF16), w_r_ref[...]) + b_r_ref[...]
    rows = logits.shape[0]
    col = _iota((rows, n_e), 1).astype(F32)
    vals, idxs = [], []
    cur = logits
    for _ in range(TOP_K):
        m = jnp.max(cur, axis=-1, keepdims=True)
        idx = jnp.min(jnp.where(cur == m, col, float(n_e)), axis=-1, keepdims=True)
        vals.append(m)
        idxs.append(idx)
        cur = jnp.where(col == idx, -jnp.inf, cur)
    exps = [jnp.exp(v - vals[0]) for v in vals]
    tot = exps[0] + exps[1] + exps[2] + exps[3]
    col4 = _iota((rows, TOP_K), 1)
    te = jnp.zeros((rows, TOP_K), F32)
    ga = jnp.zeros((rows, TOP_K), F32)
    for k in range(TOP_K):
        te = jnp.where(col4 == k, idxs[k], te)
        ga = jnp.where(col4 == k, exps[k] / tot, ga)
    tope_ref[rows_at, :] = te.astype(jnp.int32)
    gates_ref[rows_at, :] = ga


def _mix_out(x, attn, conv_out, cross, g_mix_ref, w_out_ref, g_ffn_ref, w_r_ref, b_r_ref,
             h_ref, hn_ref, tope_ref, gates_ref):
    rows = x.shape[0]
    chunk = MIX_CHUNK if rows % MIX_CHUNK == 0 else rows
    for r0 in range(0, rows, chunk):
        at = slice(r0, r0 + chunk)
        mix = jnp.concatenate([_rms(attn[at]), _rms(conv_out[at]), _rms(cross[at])], axis=-1) * g_mix_ref[...]
        h = x[at] + _dot(mix.astype(BF16), w_out_ref[...])
        hn = _rms(h) * g_ffn_ref[...]
        h_ref[at, :] = h
        _store_token_tiles(hn_ref, hn, base=r0 * SUBLANES)
        _router_topk(hn, w_r_ref, b_r_ref, tope_ref, gates_ref, at)


def _swa_bias(prev_lim):
    blk = WINDOW
    qi = _iota((blk, 2 * blk), 0)
    kj = _iota((blk, 2 * blk), 1)
    dist = blk + qi - kj
    mask = (dist >= 0) & (dist < WINDOW) & (kj >= prev_lim)
    distf = dist.astype(F32)
    return [jnp.where(mask, -ALIBI_SLOPES[h] * distf, -jnp.inf) for h in range(N_Q_HEADS)]


def _swa_block(q_blk, kk, vv, bias, sinks_ref):
    blk = WINDOW
    lane = _iota((2 * blk, KV_WIDTH), 1)
    lo = lane < HEAD_DIM
    kk_r = pltpu.roll(kk, HEAD_DIM, axis=1)
    vv_r = pltpu.roll(vv, HEAD_DIM, axis=1)
    kdup = [jnp.where(lo, kk, kk_r).astype(BF16), jnp.where(lo, kk_r, kk).astype(BF16)]
    vlo = [jnp.where(lo, vv, 0.0).astype(BF16), jnp.where(lo, vv_r, 0.0).astype(BF16)]
    vhi = [jnp.where(lo, 0.0, vv_r).astype(BF16), jnp.where(lo, 0.0, vv).astype(BF16)]
    qlo = _iota((blk, 2 * HEAD_DIM), 1) < HEAD_DIM
    outs = []
    for p in range(N_Q_HEADS // 2):
        kh = (2 * p) // (N_Q_HEADS // N_KV_HEADS)
        qp = q_blk[:, p * 2 * HEAD_DIM:(p + 1) * 2 * HEAD_DIM]
        acc = None
        for e in range(2):
            h = 2 * p + e
            qm = jnp.where(qlo if e == 0 else jnp.logical_not(qlo), qp, 0.0).astype(BF16)
            s = _dot_nt(qm, kdup[kh]) + bias[h]
            sink = sinks_ref[h]
            m = jnp.maximum(jnp.max(s, axis=-1, keepdims=True), sink)
            pe = jnp.exp(s - m)
            denom = jnp.sum(pe, axis=-1, keepdims=True) + jnp.exp(sink - m)
            o = _dot(pe.astype(BF16), (vlo if e == 0 else vhi)[kh]) / denom
            acc = o if acc is None else acc + o
        outs.append(acc)
    return jnp.concatenate(outs, axis=1)


def _mem_attend_shared(mq, mkt, mvb):
    t = mq.shape[0]
    m_tok = mvb.shape[0]
    qhead = _iota((t, MEM_WIDTH), 1) // HEAD_DIM
    vhead = _iota((m_tok, MEM_WIDTH), 1) // HEAD_DIM
    cross = None
    for h in range(N_MEM_HEADS):
        qm = jnp.where(qhead == h, mq, 0.0).astype(BF16)
        s = _dot(qm, mkt)
        m = jnp.max(s, axis=-1, keepdims=True)
        pe = jnp.exp(s - m)
        denom = jnp.sum(pe, axis=-1, keepdims=True)
        vm = jnp.where(vhead == h, mvb, jnp.zeros_like(mvb))
        o = _dot(pe.astype(BF16), vm) / denom
        cross = o if cross is None else cross + o
    return cross


def _mixer_p_kernel(sinks_ref, x_ref, g_attn_ref, w_in_ref, conv_w_ref, g_mix_ref, w_out_ref, g_ffn_ref,
                    w_r_ref, b_r_ref, mkt_ref, mvb_ref,
                    h_ref, hn_ref, tope_ref, gates_ref, lastk_ref, lastv_ref, convst_ref,
                    ck_ref, cv_ref, cu_ref):
    j = pl.program_id(1)
    nj = pl.num_programs(1)

    @pl.when(j == 0)
    def _():
        ck_ref[...] = jnp.zeros_like(ck_ref)
        cv_ref[...] = jnp.zeros_like(cv_ref)
        cu_ref[...] = jnp.zeros_like(cu_ref)

    x = x_ref[0]
    z = jnp.concatenate([_dot((_rms(x[r0:r0 + IN_CHUNK]) * g_attn_ref[...]).astype(BF16), w_in_ref[...])
                         for r0 in range(0, TQ, IN_CHUNK)], axis=0)
    c0 = ATTN_WIDTH
    c1 = c0 + KV_WIDTH
    c2 = c1 + KV_WIDTH
    cw = conv_w_ref.shape[1]
    c3, c4, c5 = c2 + cw, c2 + 2 * cw, c2 + 3 * cw
    q = z[:, :c0] * ATTN_SCALE
    k = z[:, c0:c1]
    v = z[:, c1:c2]
    cb = z[:, c2:c3]
    cc = z[:, c3:c4]
    cvv = z[:, c4:c5]
    mq = z[:, c5:] * ATTN_SCALE

    blk = WINDOW
    attn_blocks = []
    bias_inner = _swa_bias(0)
    for i in range(TQ // blk):
        if i == 0:
            pk, pv = ck_ref[...], cv_ref[...]
            bias = _swa_bias(jnp.where(j > 0, 0, blk))
        else:
            pk, pv = k[(i - 1) * blk:i * blk], v[(i - 1) * blk:i * blk]
            bias = bias_inner
        kk = jnp.concatenate([pk, k[i * blk:(i + 1) * blk]], axis=0)
        vv = jnp.concatenate([pv, v[i * blk:(i + 1) * blk]], axis=0)
        attn_blocks.append(_swa_block(q[i * blk:(i + 1) * blk], kk, vv, bias, sinks_ref))
    attn = jnp.concatenate(attn_blocks, axis=0)
    ck_ref[...] = k[TQ - blk:]
    cv_ref[...] = v[TQ - blk:]

    u = cc * cvv
    row = _iota(u.shape, 0)
    u1 = jnp.where(row == 0, cu_ref[SUBLANES - 1:SUBLANES, :], pltpu.roll(u, 1, axis=0))
    u2 = jnp.where(row == 0, cu_ref[SUBLANES - 2:SUBLANES - 1, :],
                   jnp.where(row == 1, cu_ref[SUBLANES - 1:SUBLANES, :], pltpu.roll(u, 2, axis=0)))
    conv_out = cb * (conv_w_ref[0:1, :] * u2 + conv_w_ref[1:2, :] * u1 + conv_w_ref[2:3, :] * u)
    cu_ref[...] = u[TQ - SUBLANES:]

    cross = _mem_attend_shared(mq, mkt_ref[0], mvb_ref[0])

    @pl.when(j == nj - 1)
    def _():
        lastk_ref[0] = k[TQ - blk:]
        lastv_ref[0] = v[TQ - blk:]
        convst_ref[0] = u[TQ - 2:]

    _mix_out(x, attn, conv_out, cross, g_mix_ref, w_out_ref, g_ffn_ref, w_r_ref, b_r_ref,
             h_ref, hn_ref, tope_ref, gates_ref)


def _mixer_p(x, sinks, g_attn, w_in, conv_w, g_mix, w_out, g_ffn, w_r, b_r, mkt, mvb):
    b, s, d = x.shape
    nj = s // TQ
    n = b * s
    cw = conv_w.shape[1]
    full = lambda a: pl.BlockSpec(a.shape, lambda bi, ji, *_: (0,) * a.ndim)
    tok = lambda w: pl.BlockSpec((TQ, w), lambda bi, ji, *_: (bi * nj + ji, 0))
    per_b = lambda r, c: pl.BlockSpec((1, r, c), lambda bi, ji, *_: (bi, 0, 0))
    grid_spec = pltpu.PrefetchScalarGridSpec(
        num_scalar_prefetch=1,
        grid=(b, nj),
        in_specs=[pl.BlockSpec((1, TQ, d), lambda bi, ji, *_: (bi, ji, 0)),
                  full(g_attn), full(w_in), full(conv_w), full(g_mix), full(w_out), full(g_ffn),
                  full(w_r), full(b_r), per_b(MEM_WIDTH, mkt.shape[2]), per_b(mvb.shape[1], MEM_WIDTH)],
        out_specs=[tok(d), pl.BlockSpec((TQ * SUBLANES, LANES), lambda bi, ji, *_: (bi * nj + ji, 0)),
                   tok(TOP_K), tok(TOP_K),
                   per_b(WINDOW, KV_WIDTH), per_b(WINDOW, KV_WIDTH), per_b(2, cw)],
        scratch_shapes=[pltpu.VMEM((WINDOW, KV_WIDTH), F32), pltpu.VMEM((WINDOW, KV_WIDTH), F32),
                        pltpu.VMEM((SUBLANES, cw), F32)],
    )
    return pl.pallas_call(
        _mixer_p_kernel,
        grid_spec=grid_spec,
        out_shape=[jax.ShapeDtypeStruct((n, d), F32), jax.ShapeDtypeStruct((n * SUBLANES, LANES), F32),
                   jax.ShapeDtypeStruct((n, TOP_K), jnp.int32), jax.ShapeDtypeStruct((n, TOP_K), F32),
                   jax.ShapeDtypeStruct((b, WINDOW, KV_WIDTH), F32),
                   jax.ShapeDtypeStruct((b, WINDOW, KV_WIDTH), F32),
                   jax.ShapeDtypeStruct((b, 2, cw), F32)],
        compiler_params=pltpu.CompilerParams(dimension_semantics=("arbitrary", "arbitrary"),
                                             vmem_limit_bytes=VMEM_LIMIT),
        name="mixer_p",
    )(sinks, x, g_attn, w_in, conv_w, g_mix, w_out, g_ffn, w_r, b_r, mkt, mvb)


def _per_head_column(values, hrow):
    col = jnp.zeros(hrow.shape, F32)
    for h in range(N_Q_HEADS):
        col = jnp.where(hrow == h, values[h], col)
    return col


def _mixer_s_kernel(sinks_ref, x_ref, pm1_ref, pm2_ref, wk_ref, wv_ref, mk_ref, mv_ref,
                    g_attn_ref, w_in_ref, conv_w_ref, g_mix_ref, w_out_ref, g_ffn_ref, w_r_ref, b_r_ref,
                    h_ref, hn_ref, tope_ref, gates_ref, nwk_ref, nwv_ref, u_ref, *, t_dec):
    r_tok = BB * t_dec
    r_exp = r_tok * REP
    qrows = t_dec * REP
    x = x_ref[...]
    xn = (_rms(x) * g_attn_ref[...]).astype(BF16)
    z = _dot(xn, w_in_ref[...])
    c0 = ATTN_WIDTH
    c1 = c0 + KV_WIDTH
    c2 = c1 + KV_WIDTH
    cw = conv_w_ref.shape[1]
    c3, c4, c5 = c2 + cw, c2 + 2 * cw, c2 + 3 * cw
    q = z[:, :c0] * ATTN_SCALE
    k_new = z[:, c0:c1]
    v_new = z[:, c1:c2]
    cb = z[:, c2:c3]
    cc = z[:, c3:c4]
    cvv = z[:, c4:c5]
    mq = z[:, c5:] * ATTN_SCALE
    win = wk_ref.shape[1]

    xi = _iota((KV_WIDTH, ATTN_WIDTH), 0)
    xl = _iota((KV_WIDTH, ATTN_WIDTH), 1)
    q_per_kv = N_Q_HEADS // N_KV_HEADS
    expand = (xi == (xl // (q_per_kv * HEAD_DIM)) * HEAD_DIM + xl % HEAD_DIM).astype(BF16)
    rr = _iota((r_exp, r_tok), 0)
    rc = _iota((r_exp, r_tok), 1)
    rep = (rr // REP == rc).astype(BF16)

    hrow = _iota((r_exp, 1), 0) % REP
    trow = (_iota((r_exp, 1), 0) // REP) % t_dec
    slope_col = _per_head_column(ALIBI_SLOPES, hrow)
    sink_col = _per_head_column([sinks_ref[h] for h in range(N_Q_HEADS)], hrow)

    qexp = jnp.where(hrow == _iota((r_exp, ATTN_WIDTH), 1) // HEAD_DIM, _dot(rep, q.astype(BF16)), 0.0)
    kexp = _dot(wk_ref[...].reshape(BB * win, KV_WIDTH).astype(BF16), expand).astype(BF16)
    vexp = _dot(wv_ref[...].reshape(BB * win, KV_WIDTH).astype(BF16), expand).astype(BF16)
    s = jnp.einsum("bqc,bkc->bqk", qexp.astype(BF16).reshape(BB, qrows, ATTN_WIDTH),
                   kexp.reshape(BB, win, ATTN_WIDTH), preferred_element_type=F32).reshape(r_exp, win)
    scol = _iota((r_exp, win), 1)
    s = s - slope_col * (win + trow - scol).astype(F32)
    s = jnp.where(scol > trow, s, -jnp.inf)
    knew_exp = _dot(k_new.astype(BF16), expand).astype(BF16)
    vnew_exp = _dot(v_new.astype(BF16), expand).astype(BF16)
    s_new, v_rep = [], []
    for jn in range(t_dec):
        rep_j = (rc == (rr // qrows) * t_dec + jn).astype(BF16)
        k_rep = _dot(rep_j, knew_exp)
        v_rep.append(_dot(rep_j, vnew_exp))
        sj = jnp.sum(qexp * k_rep, axis=-1, keepdims=True) - slope_col * (trow - jn).astype(F32)
        s_new.append(jnp.where(trow >= jn, sj, -jnp.inf))
    m = jnp.maximum(jnp.max(s, axis=-1, keepdims=True), sink_col)
    for sj in s_new:
        m = jnp.maximum(m, sj)
    pe = jnp.exp(s - m)
    denom = jnp.sum(pe, axis=-1, keepdims=True) + jnp.exp(sink_col - m)
    o = jnp.einsum("bqk,bkc->bqc", pe.astype(BF16).reshape(BB, qrows, win),
                   vexp.reshape(BB, win, ATTN_WIDTH), preferred_element_type=F32).reshape(r_exp, ATTN_WIDTH)
    for jn in range(t_dec):
        pj = jnp.exp(s_new[jn] - m)
        denom = denom + pj
        o = o + pj * v_rep[jn]
    o = jnp.where(hrow == _iota((r_exp, ATTN_WIDTH), 1) // HEAD_DIM, o / denom, 0.0)
    attn = jnp.sum(o.reshape(r_tok, REP, ATTN_WIDTH), axis=1)

    m_tok = mk_ref.shape[1]
    mhead = _iota((r_exp, MEM_WIDTH), 1) // HEAD_DIM
    mqexp = jnp.where(hrow == mhead, _dot(rep, mq.astype(BF16)), 0.0).astype(BF16)
    sm = jnp.einsum("bqc,bmc->bqm", mqexp.reshape(BB, qrows, MEM_WIDTH), mk_ref[...].astype(BF16),
                    preferred_element_type=F32).reshape(r_exp, m_tok)
    mm = jnp.max(sm, axis=-1, keepdims=True)
    pm = jnp.exp(sm - mm)
    dm = jnp.sum(pm, axis=-1, keepdims=True)
    om = jnp.einsum("bqm,bmc->bqc", pm.astype(BF16).reshape(BB, qrows, m_tok), mv_ref[...].astype(BF16),
                    preferred_element_type=F32).reshape(r_exp, MEM_WIDTH)
    om = jnp.where(hrow == mhead, om / dm, 0.0)
    cross = jnp.sum(om.reshape(r_tok, REP, MEM_WIDTH), axis=1)

    u = cc * cvv
    tt = _iota(u.shape, 0) % t_dec
    u1 = jnp.where(tt >= 1, pltpu.roll(u, 1, axis=0), pm1_ref[...])
    u2 = jnp.where(tt >= 2, pltpu.roll(u, 2, axis=0), pm2_ref[...])
    conv_out = cb * (conv_w_ref[0:1, :] * u2 + conv_w_ref[1:2, :] * u1 + conv_w_ref[2:3, :] * u)
    u_ref[...] = u

    nwk_ref[:, 0:win - t_dec, :] = wk_ref[:, t_dec:win, :]
    nwv_ref[:, 0:win - t_dec, :] = wv_ref[:, t_dec:win, :]
    for b in range(BB):
        nwk_ref[b, win - t_dec:win, :] = k_new[b * t_dec:(b + 1) * t_dec, :]
        nwv_ref[b, win - t_dec:win, :] = v_new[b * t_dec:(b + 1) * t_dec, :]

    _mix_out(x, attn, conv_out, cross, g_mix_ref, w_out_ref, g_ffn_ref, w_r_ref, b_r_ref,
             h_ref, hn_ref, tope_ref, gates_ref)


def _mixer_s(x2, t_dec, pm1, pm2, wk, wv, mk, mv, sinks, g_attn, w_in, conv_w, g_mix, w_out, g_ffn, w_r, b_r):
    n, d = x2.shape
    nb = wk.shape[0]
    win = wk.shape[1]
    m_tok = mk.shape[1]
    cw = conv_w.shape[1]
    r_tok = BB * t_dec
    full = lambda a: pl.BlockSpec(a.shape, lambda i, *_: (0,) * a.ndim)
    tok = lambda w: pl.BlockSpec((r_tok, w), lambda i, *_: (i, 0))
    per_b = lambda r, c: pl.BlockSpec((BB, r, c), lambda i, *_: (i, 0, 0))
    grid_spec = pltpu.PrefetchScalarGridSpec(
        num_scalar_prefetch=1,
        grid=(nb // BB,),
        in_specs=[tok(d), tok(cw), tok(cw), per_b(win, KV_WIDTH), per_b(win, KV_WIDTH),
                  per_b(m_tok, MEM_WIDTH), per_b(m_tok, MEM_WIDTH),
                  full(g_attn), full(w_in), full(conv_w), full(g_mix), full(w_out), full(g_ffn),
                  full(w_r), full(b_r)],
        out_specs=[tok(d), pl.BlockSpec((r_tok * SUBLANES, LANES), lambda i, *_: (i, 0)),
                   tok(TOP_K), tok(TOP_K), per_b(win, KV_WIDTH), per_b(win, KV_WIDTH), tok(cw)],
    )
    return pl.pallas_call(
        functools.partial(_mixer_s_kernel, t_dec=t_dec),
        grid_spec=grid_spec,
        out_shape=[jax.ShapeDtypeStruct((n, d), F32), jax.ShapeDtypeStruct((n * SUBLANES, LANES), F32),
                   jax.ShapeDtypeStruct((n, TOP_K), jnp.int32), jax.ShapeDtypeStruct((n, TOP_K), F32),
                   jax.ShapeDtypeStruct((nb, win, KV_WIDTH), F32), jax.ShapeDtypeStruct((nb, win, KV_WIDTH), F32),
                   jax.ShapeDtypeStruct((n, cw), F32)],
        compiler_params=pltpu.CompilerParams(dimension_semantics=("arbitrary",), vmem_limit_bytes=VMEM_LIMIT),
        name="mixer_s",
    )(sinks, x2, pm1, pm2, wk, wv, mk, mv, g_attn, w_in, conv_w, g_mix, w_out, g_ffn, w_r, b_r)


def _lpos_kernel(tope_ref, lpos_ref, counts_ref, tri_ref, *, n_e):
    i = pl.program_id(0)
    tm = tope_ref.shape[0]

    @pl.when(i == 0)
    def _():
        tri_ref[...] = (_iota((tm, tm), 0) > _iota((tm, tm), 1)).astype(BF16)

    te = tope_ref[...]
    col = _iota((tm, n_e), 1)
    hits = [te[:, k:k + 1] == col for k in range(TOP_K)]
    onehot = jnp.zeros((tm, n_e), F32)
    lower = jnp.zeros((tm, n_e), F32)
    for k in range(TOP_K):
        onehot = onehot + hits[k].astype(F32)
        lower = lower + (te[:, k:k + 1] < col).astype(F32)
    ahead = _dot(tri_ref[...], onehot.astype(BF16)) + jnp.sum(lower, axis=0, keepdims=True)
    col4 = _iota((tm, TOP_K), 1)
    pos = jnp.zeros((tm, TOP_K), F32)
    for k in range(TOP_K):
        pos = jnp.where(col4 == k, jnp.sum(jnp.where(hits[k], ahead, 0.0), axis=-1, keepdims=True), pos)
    half = (i % 2) * (tm * TOP_K)
    lpos_ref[...] = (pos.astype(jnp.int32) + half) * SUBLANES
    counts_ref[0] = jnp.sum(onehot, axis=0, keepdims=True)


def _lpos(tope, tm, n_e):
    n = tope.shape[0]
    return pl.pallas_call(
        functools.partial(_lpos_kernel, n_e=n_e),
        grid=(n // tm,),
        in_specs=[pl.BlockSpec((tm, TOP_K), lambda i: (i, 0))],
        out_specs=[pl.BlockSpec((tm, TOP_K), lambda i: (i, 0)), pl.BlockSpec((1, 1, n_e), lambda i: (i, 0, 0))],
        out_shape=[jax.ShapeDtypeStruct((n, TOP_K), jnp.int32), jax.ShapeDtypeStruct((n // tm, 1, n_e), F32)],
        scratch_shapes=[pltpu.VMEM((tm, tm), BF16)],
        compiler_params=pltpu.CompilerParams(dimension_semantics=("arbitrary",)),
        name="lpos",
    )(tope)


def _rows(ref, first, n_rows):
    return ref.at[pl.ds(pl.multiple_of(first, SUBLANES), n_rows * SUBLANES)]


def _group_chunks(meta, g, n_e, max_rows, make_copy, wait):
    off_ref, cnt_ref, lst_ref = meta
    if wait:
        make_copy(0, 0, max_rows * TOP_K).wait()
        return

    def per_expert(e, c):
        off = off_ref[g * n_e + e]
        lst = lst_ref[g * n_e + e]
        _run_copies(cnt_ref[g * n_e + e], max_rows, lambda done, size: make_copy(lst + done, off + done, size),
                    wait=False)
        return c

    lax.fori_loop(0, n_e, per_expert, 0)


def _run_copies(n_rows, max_rows, make_copy, wait):
    done = 0
    for size in [1 << b for b in range(max_rows.bit_length() - 1, -1, -1)]:
        bit = n_rows & (size * SUBLANES)

        @pl.when(bit != 0)
        def _():
            cp = make_copy(done, size)
            cp.wait() if wait else cp.start()
        done = done + bit


def _dispatch_kernel(off_ref, cnt_ref, lst_ref, zrow_ref, zcnt_ref, nused_ref, lpos_ref, hn_a_ref, hn_b_ref, xs_ref,
                     stage_ref, zbuf_ref, sem, zsem, *, n_e, n_blocks, groups_a):
    j = pl.program_id(0)
    nj = pl.num_programs(0)
    tm = hn_a_ref.shape[0] // SUBLANES
    slot = j % 2
    meta = (off_ref, cnt_ref, lst_ref)

    def chunks(g, s, wait):
        _group_chunks(meta, g, n_e, tm,
                      lambda lrow, grow, size: pltpu.make_async_copy(
                          _rows(stage_ref, s * (tm * TOP_K * SUBLANES) + lrow, size), _rows(xs_ref, grow, size),
                          sem.at[s]), wait)

    def zero_fill(wait):
        def zero_run(first_row, n_rows):
            _run_copies(n_rows, BM, lambda done, size: pltpu.make_async_copy(
                _rows(zbuf_ref, 0, size), _rows(xs_ref, first_row + done, size), zsem), wait)

        def expert_pad(e, c):
            zero_run(zrow_ref[e], zcnt_ref[e])
            return c

        def tail_block(b, c):
            zero_run(b * (BM * SUBLANES), jnp.int32(BM * SUBLANES))
            return c

        lax.fori_loop(0, n_e, expert_pad, 0)
        lax.fori_loop(nused_ref[0], n_blocks, tail_block, 0)

    @pl.when(j == 0)
    def _():
        zbuf_ref[...] = jnp.zeros_like(zbuf_ref)
        zero_fill(False)

    @pl.when(j >= 2)
    def _():
        chunks(j - 2, slot, True)

    def place_from(hn_ref):
        def place(t, c):
            tile = hn_ref[pl.ds(pl.multiple_of(t * SUBLANES, SUBLANES), SUBLANES), :]
            for k in range(TOP_K):
                pos = pl.multiple_of(lpos_ref[0, 0, t * TOP_K + k], SUBLANES)
                stage_ref[pl.ds(pos, SUBLANES), :] = tile
            return c
        lax.fori_loop(0, tm, place, 0, unroll=TOKEN_UNROLL)

    @pl.when(j < groups_a)
    def _():
        place_from(hn_a_ref)

    @pl.when(j >= groups_a)
    def _():
        place_from(hn_b_ref)

    chunks(j, slot, False)

    @pl.when(j == nj - 1)
    def _():
        @pl.when(j >= 1)
        def _():
            chunks(j - 1, 1 - slot, True)
        chunks(j, slot, True)
        zero_fill(True)


def _dispatch(hn_a, hn_b, lpos, meta, zrow, zcnt, nused, tm, n_e, n_blocks):
    nt = lpos.shape[0] // tm
    groups_a = hn_a.shape[0] // (tm * SUBLANES)
    assert groups_a >= 1 and groups_a + hn_b.shape[0] // (tm * SUBLANES) == nt
    grid_spec = pltpu.PrefetchScalarGridSpec(
        num_scalar_prefetch=6,
        grid=(nt,),
        in_specs=[pl.BlockSpec((1, 1, tm * TOP_K), lambda j, *_: (j, 0, 0), memory_space=pltpu.SMEM),
                  pl.BlockSpec((tm * SUBLANES, LANES), lambda j, *_: (jnp.minimum(j, groups_a - 1), 0)),
                  pl.BlockSpec((tm * SUBLANES, LANES), lambda j, *_: (jnp.maximum(j - groups_a, 0), 0))],
        out_specs=pl.BlockSpec(memory_space=pl.ANY),
        scratch_shapes=[pltpu.VMEM((2 * tm * TOP_K * SUBLANES, LANES), F32), pltpu.VMEM((BM * SUBLANES, LANES), F32),
                        pltpu.SemaphoreType.DMA((2,)), pltpu.SemaphoreType.DMA(())],
    )
    return pl.pallas_call(
        functools.partial(_dispatch_kernel, n_e=n_e, n_blocks=n_blocks, groups_a=groups_a),
        grid_spec=grid_spec,
        out_shape=jax.ShapeDtypeStruct((n_blocks * BM * SUBLANES, LANES), F32),
        compiler_params=pltpu.CompilerParams(dimension_semantics=("arbitrary",), vmem_limit_bytes=VMEM_LIMIT),
        name="dispatch",
    )(*meta, zrow, zcnt, nused, lpos.reshape(nt, 1, tm * TOP_K), hn_a, hn_b)


def _experts_kernel(blk_e_ref, nused_ref, next_e_ref, blk_rows_ref, xs_ref, bgu_ref, bd_ref, wgu_hbm_ref, wd_hbm_ref, ys_ref,
                    wgu_ref, wd_ref, wgu_bf_ref, wd_bf_ref, wsem):
    i = pl.program_id(0)
    nused = nused_ref[0]
    d, d_ff2 = wgu_ref.shape
    grp = 2 * LANES

    def weight_copies(e):
        gu_cols = d_ff2 // W_PARTS
        dn_rows = (d_ff2 // 2) // W_PARTS
        cps = []
        for p in range(W_PARTS):
            cps.append(pltpu.make_async_copy(wgu_hbm_ref.at[e, :, pl.ds(p * gu_cols, gu_cols)],
                                             wgu_ref.at[:, pl.ds(p * gu_cols, gu_cols)], wsem.at[p]))
            cps.append(pltpu.make_async_copy(wd_hbm_ref.at[e, pl.ds(p * dn_rows, dn_rows), :],
                                             wd_ref.at[pl.ds(p * dn_rows, dn_rows), :], wsem.at[W_PARTS + p]))
        return cps

    @pl.when(i < nused)
    def _():
        e = blk_e_ref[i]
        e_prev = blk_e_ref[jnp.maximum(i - 1, 0)]

        @pl.when(i == 0)
        def _():
            for cp in weight_copies(e):
                cp.start()

        @pl.when((i == 0) | (e != e_prev))
        def _():
            for cp in weight_copies(e):
                cp.wait()
            pr = _iota((grp, grp), 0)
            pc = _iota((grp, grp), 1)
            perm = (pr == jnp.where(pc < LANES, 2 * pc, 2 * (pc - LANES) + 1)).astype(BF16)
            for g in range(d_ff2 // grp):
                w = wgu_ref[:, g * grp:(g + 1) * grp].astype(BF16)
                wgu_bf_ref[:, g * grp:(g + 1) * grp] = _dot(w, perm).astype(BF16)
            wd_bf_ref[...] = wd_ref[...].astype(BF16)

            @pl.when(next_e_ref[e] >= 0)
            def _():
                for cp in weight_copies(next_e_ref[e]):
                    cp.start()

        def ffn(r0, n_rows):
            x = jnp.concatenate([_load_token_tiles(xs_ref, n_rows, s, base=r0 * SUBLANES).astype(BF16)
                                 for s in range(SUBLANES)], axis=1)
            hgu = _dot(x, wgu_bf_ref[...]) + bgu_ref[0]
            acts = []
            for g in range(d_ff2 // grp):
                gate = jnp.minimum(hgu[:, g * grp:g * grp + LANES], SWIGLU_LIMIT)
                up = jnp.clip(hgu[:, g * grp + LANES:(g + 1) * grp], -SWIGLU_LIMIT, SWIGLU_LIMIT)
                glu = gate * (1.0 / (1.0 + jnp.exp(-SWIGLU_ALPHA * gate)))
                acts.append(((up + 1.0) * glu).astype(BF16))
            act = jnp.concatenate(acts, axis=1)
            _store_token_tiles(ys_ref, _dot(act, wd_bf_ref[...]) + bd_ref[0], base=r0 * SUBLANES)

        rows = blk_rows_ref[i]

        @pl.when(rows == BM)
        def _():
            ffn(0, BM)

        @pl.when(rows < BM)
        def _():
            for r0 in range(0, BM, TAIL_CHUNK):
                @pl.when(r0 < rows)
                def _():
                    ffn(r0, TAIL_CHUNK)

                @pl.when(r0 >= rows)
                def _():
                    ys_ref[r0 * SUBLANES:(r0 + TAIL_CHUNK) * SUBLANES, :] = jnp.zeros(
                        (TAIL_CHUNK * SUBLANES, LANES), F32)

    @pl.when(i >= nused)
    def _():
        ys_ref[...] = jnp.zeros_like(ys_ref)


def _experts(xs, blk_e, nused, next_e, blk_rows, w_gate_up, b_gu_perm, w_down, b_down):
    n_e, d, d_ff2 = w_gate_up.shape
    n_blocks = xs.shape[0] // (BM * SUBLANES)
    expert = lambda i, be, nu, *_: be[jnp.minimum(i, jnp.maximum(nu[0] - 1, 0))]
    rows_spec = pl.BlockSpec((BM * SUBLANES, LANES), lambda i, *_: (i, 0))
    grid_spec = pltpu.PrefetchScalarGridSpec(
        num_scalar_prefetch=4,
        grid=(n_blocks,),
        in_specs=[pl.BlockSpec((BM * SUBLANES, LANES), lambda i, be, nu, *_: (jnp.minimum(i, jnp.maximum(nu[0] - 1, 0)), 0)),
                  pl.BlockSpec((1, 1, d_ff2), lambda i, *s: (expert(i, *s), 0, 0)),
                  pl.BlockSpec((1, 1, d), lambda i, *s: (expert(i, *s), 0, 0)),
                  pl.BlockSpec(memory_space=pl.ANY), pl.BlockSpec(memory_space=pl.ANY)],
        out_specs=rows_spec,
        scratch_shapes=[pltpu.VMEM((d, d_ff2), F32), pltpu.VMEM((d_ff2 // 2, d), F32),
                        pltpu.VMEM((d, d_ff2), BF16), pltpu.VMEM((d_ff2 // 2, d), BF16),
                        pltpu.SemaphoreType.DMA((2 * W_PARTS,))],
    )
    return pl.pallas_call(
        _experts_kernel,
        grid_spec=grid_spec,
        out_shape=jax.ShapeDtypeStruct(xs.shape, F32),
        compiler_params=pltpu.CompilerParams(dimension_semantics=("arbitrary",), vmem_limit_bytes=VMEM_LIMIT),
        name="experts",
    )(blk_e, nused, next_e, blk_rows, xs, b_gu_perm, b_down, w_gate_up, w_down)


def _combine_kernel(off_ref, cnt_ref, lst_ref, lpos_ref, gates_ref, h_ref, g_ref, ys_ref, out_ref,
                    stage_ref, acc_ref, sem, *, n_e, g0):
    i = pl.program_id(0)
    ng = pl.num_programs(0)
    tm, d = h_ref.shape
    slot = (g0 + i) % 2
    meta = (off_ref, cnt_ref, lst_ref)

    def chunks(g, s, wait):
        _group_chunks(meta, g, n_e, tm,
                      lambda lrow, grow, size: pltpu.make_async_copy(
                          _rows(ys_ref, grow, size), _rows(stage_ref, s * (tm * TOP_K * SUBLANES) + lrow, size),
                          sem.at[s]), wait)

    @pl.when(i == 0)
    def _():
        chunks(g0, slot, False)

    @pl.when(i + 1 < ng)
    def _():
        chunks(g0 + i + 1, 1 - slot, False)

    chunks(g0 + i, slot, True)

    def mix(t, c):
        acc = None
        for k in range(TOP_K):
            pos = pl.multiple_of(lpos_ref[0, 0, t * TOP_K + k], SUBLANES)
            term = stage_ref[pl.ds(pos, SUBLANES), :] * gates_ref[0, 0, t * TOP_K + k]
            acc = term if acc is None else acc + term
        acc_ref[pl.ds(pl.multiple_of(t * SUBLANES, SUBLANES), SUBLANES), :] = acc
        return c

    lax.fori_loop(0, tm, mix, 0, unroll=TOKEN_UNROLL)

    h = h_ref[...]
    parts = []
    sq = jnp.zeros((tm, LANES), F32)
    for s in range(d // LANES):
        y = h[:, s * LANES:(s + 1) * LANES] + _load_token_tiles(acc_ref, tm, s)
        sq = sq + y * y
        parts.append(y)
    rinv = lax.rsqrt(jnp.sum(sq, axis=-1, keepdims=True) / d + EPS)
    out_ref[...] = jnp.concatenate(parts, axis=1) * rinv * g_ref[...]


def _combine(ys, h, lpos, gates, meta, g_final, tm, n_e, g0):
    n, d = h.shape
    grid_spec = pltpu.PrefetchScalarGridSpec(
        num_scalar_prefetch=3,
        grid=(n // tm,),
        in_specs=[pl.BlockSpec((1, 1, tm * TOP_K), lambda i, *_: (g0 + i, 0, 0), memory_space=pltpu.SMEM),
                  pl.BlockSpec((1, 1, tm * TOP_K), lambda i, *_: (g0 + i, 0, 0), memory_space=pltpu.SMEM),
                  pl.BlockSpec((tm, d), lambda i, *_: (i, 0)),
                  pl.BlockSpec((1, d), lambda i, *_: (0, 0)),
                  pl.BlockSpec(memory_space=pl.ANY)],
        out_specs=pl.BlockSpec((tm, d), lambda i, *_: (i, 0)),
        scratch_shapes=[pltpu.VMEM((2 * tm * TOP_K * SUBLANES, LANES), F32), pltpu.VMEM((tm * SUBLANES, LANES), F32),
                        pltpu.SemaphoreType.DMA((2,))],
    )
    ngroups = lpos.shape[0] // tm
    return pl.pallas_call(
        functools.partial(_combine_kernel, n_e=n_e, g0=g0),
        grid_spec=grid_spec,
        out_shape=jax.ShapeDtypeStruct((n, d), F32),
        compiler_params=pltpu.CompilerParams(dimension_semantics=("arbitrary",), vmem_limit_bytes=VMEM_LIMIT),
        name="combine",
    )(*meta, lpos.reshape(ngroups, 1, tm * TOP_K), gates.reshape(ngroups, 1, tm * TOP_K), h, g_final, ys)


def _largest_tile(cands, *sizes):
    for c in cands:
        if all(s % c == 0 for s in sizes):
            return c
    raise ValueError(f"no tile in {cands} divides {sizes}")


def kernel(x_prompt, x_sample, mem_prompt, cache_win_k, cache_win_v, state_conv, cache_mem_k, cache_mem_v, g_attn_norm, w_in, conv_w, attn_sinks, g_mem_norm, w_mem_kv, g_mix_out, w_out, g_ffn_norm, w_router, b_router, w_gate_up, b_gate_up, w_down, b_down, g_final):
    depth = w_in.shape[0]
    assert depth == 1, "single-layer step"
    b, s, d = x_prompt.shape
    nb, t_dec, _ = x_sample.shape
    n_e = w_router.shape[2]
    d_ff2 = w_gate_up.shape[3]
    cw = conv_w.shape[2]
    win = cache_win_k.shape[2]
    m_tok = cache_mem_k.shape[2]
    assert s % TQ == 0 and nb % BB == 0 and win == WINDOW and t_dec <= SUBLANES and d_ff2 % (2 * LANES) == 0
    assert d == SUBLANES * LANES, "token-tile layout: one token is one (8, 128) f32 tile"

    row = lambda a: a.reshape(1, -1)
    w_in_bf = w_in[0].astype(BF16)
    w_out_bf = w_out[0].astype(BF16)
    w_r_bf = w_router[0].astype(BF16)
    sinks = attn_sinks[0].astype(F32)
    shared = (row(g_attn_norm[0]), w_in_bf, conv_w[0], row(g_mix_out[0]), w_out_bf, row(g_ffn_norm[0]),
              w_r_bf, row(b_router[0]))

    mk_p, mv_p, mkt, mvb = _memkv(mem_prompt, row(g_mem_norm[0]), w_mem_kv[0].astype(BF16))
    h_p, hn_p, tope_p, gates_p, lastk, lastv, convst = _mixer_p(x_prompt, sinks, *shared, mkt, mvb)

    zeros = lambda r: jnp.zeros((nb, r, cw), F32)
    st = state_conv[0]
    pm1 = jnp.concatenate([st[:, 1:2], zeros(t_dec - 1)], axis=1).reshape(nb * t_dec, cw)
    pm2 = jnp.concatenate([st, zeros(t_dec - 2)], axis=1).reshape(nb * t_dec, cw)
    h_s, hn_s, tope_s, gates_s, nwk, nwv, u_s = _mixer_s(
        x_sample.reshape(nb * t_dec, d), t_dec, pm1, pm2,
        cache_win_k[0].reshape(nb, win, KV_WIDTH), cache_win_v[0].reshape(nb, win, KV_WIDTH),
        cache_mem_k[0].reshape(nb, m_tok, MEM_WIDTH), cache_mem_v[0].reshape(nb, m_tok, MEM_WIDTH),
        sinks, *shared)

    n_p, n_s = b * s, nb * t_dec
    n = n_p + n_s
    tm = _largest_tile((512, 256, 128, 64, 32, 16, 8), n_p, n_s)
    tope = jnp.concatenate([tope_p, tope_s], axis=0)
    gates = jnp.concatenate([gates_p, gates_s], axis=0)

    lpos, cnt_f = _lpos(tope, tm, n_e)
    cnt = cnt_f[:, 0, :].astype(jnp.int32)
    counts = jnp.sum(cnt, axis=0)
    padded = (counts + BM - 1) // BM * BM
    pad_ends = jnp.cumsum(padded)
    pad_starts = pad_ends - padded
    nk = n * TOP_K
    n_blocks = -(-nk // BM) + n_e
    nused = (pad_ends[-1:] // BM).astype(jnp.int32)
    blk_start = jnp.arange(n_blocks, dtype=jnp.int32) * BM
    blk_e = jnp.minimum(jnp.sum((pad_ends[None, :] <= blk_start[:, None]).astype(jnp.int32), axis=1), n_e - 1)
    zrow = ((pad_starts + counts) * SUBLANES).astype(jnp.int32)
    zcnt = ((padded - counts) * SUBLANES).astype(jnp.int32)
    off = pad_starts[None, :] + jnp.cumsum(cnt, axis=0) - cnt
    lstart = jnp.cumsum(cnt, axis=1) - cnt
    meta = tuple((a.reshape(-1) * SUBLANES).astype(jnp.int32) for a in (off, cnt, lstart))

    xs = _dispatch(hn_p, hn_s, lpos, meta, zrow, zcnt, nused, tm, n_e, n_blocks)

    grp = 2 * LANES
    b_gu = b_gate_up[0].reshape(n_e, d_ff2 // grp, LANES, 2).transpose(0, 1, 3, 2).reshape(n_e, 1, d_ff2)
    owner = jnp.where(padded > 0, jnp.arange(n_e, dtype=jnp.int32), n_e)
    following = jnp.concatenate([lax.cummin(owner, reverse=True)[1:], jnp.full((1,), n_e, jnp.int32)])
    next_e = jnp.where(following < n_e, following, -1).astype(jnp.int32)
    own = blk_e[:, None] == jnp.arange(n_e, dtype=jnp.int32)[None, :]
    blk_rows = jnp.clip(jnp.sum(jnp.where(own, (pad_starts + counts)[None, :], 0), axis=1) - blk_start, 0, BM)
    blk_rows = blk_rows.astype(jnp.int32)
    ys = _experts(xs, blk_e, nused, next_e, blk_rows, w_gate_up[0], b_gu, w_down[0], b_down[0].reshape(n_e, 1, d))

    g_fin = row(g_final)
    y_p = _combine(ys, h_p, lpos, gates, meta, g_fin, tm, n_e, 0)
    y_s = _combine(ys, h_s, lpos, gates, meta, g_fin, tm, n_e, n_p // tm)

    kv5 = lambda a, bsz, r, hds: a.reshape(1, bsz, r, hds, HEAD_DIM)
    return (y_p.reshape(b, s, d), y_s.reshape(nb, t_dec, d),
            kv5(lastk, b, WINDOW, N_KV_HEADS), kv5(lastv, b, WINDOW, N_KV_HEADS),
            convst.reshape(1, b, 2, cw),
            kv5(mk_p, b, m_tok, N_MEM_HEADS), kv5(mv_p, b, m_tok, N_MEM_HEADS),
            kv5(nwk, nb, win, N_KV_HEADS), kv5(nwv, nb, win, N_KV_HEADS),
            u_s.reshape(nb, t_dec, cw)[:, t_dec - 2:].reshape(1, nb, 2, cw))
```

```python
import functools

import jax
import jax.numpy as jnp
from jax import lax
from jax.experimental import pallas as pl
from jax.experimental.pallas import tpu as pltpu

F32 = jnp.float32
BF16 = jnp.bfloat16

HEAD_DIM = 64
N_Q_HEADS = 8
N_KV_HEADS = 2
WINDOW = 128
ATTN_WIDTH = N_Q_HEADS * HEAD_DIM
KV_WIDTH = N_KV_HEADS * HEAD_DIM
N_MEM_HEADS = 4
MEM_WIDTH = N_MEM_HEADS * HEAD_DIM
TOP_K = 4
SWIGLU_LIMIT = 7.0
SWIGLU_ALPHA = 1.702
EPS = 1e-5
ATTN_SCALE = HEAD_DIM ** -0.5
ALIBI_SLOPES = tuple(2.0 ** (-8.0 * (h + 1) / N_Q_HEADS) for h in range(N_Q_HEADS))

LANES = 128
SUBLANES = 8
VMEM_LIMIT = 56 * 1024 * 1024

TQ = 1024
BB = 16
BM = 512
REP = 8
W_PARTS = 4
MIX_CHUNK = 512
IN_CHUNK = 256
TAIL_CHUNK = 128
TOKEN_UNROLL = 16


def _rms(x):
    return x * lax.rsqrt(jnp.mean(x * x, axis=-1, keepdims=True) + EPS)


def _dot(a, b):
    return jnp.dot(a, b, preferred_element_type=F32)


def _dot_nt(a, b):
    return lax.dot_general(a, b, (((1,), (1,)), ((), ())), preferred_element_type=F32)


def _iota(shape, axis):
    return lax.broadcasted_iota(jnp.int32, shape, axis)


def _store_token_tiles(ref, x, base=0):
    t = x.shape[0]
    for s in range(x.shape[1] // LANES):
        ref[pl.ds(base + s, t, stride=SUBLANES), :] = x[:, s * LANES:(s + 1) * LANES]


def _load_token_tiles(ref, t, s, base=0):
    return ref[pl.ds(base + s, t, stride=SUBLANES), :]


def _memkv_kernel(mem_ref, g_ref, w_ref, mk_ref, mv_ref, mkt_ref, mvb_ref):
    xn = (_rms(mem_ref[0]) * g_ref[...]).astype(BF16)
    kv = _dot(xn, w_ref[...])
    mk = kv[:, :MEM_WIDTH]
    mv = kv[:, MEM_WIDTH:]
    mk_ref[0] = mk
    mv_ref[0] = mv
    mkt_ref[0] = mk.T.astype(BF16)
    mvb_ref[0] = mv.astype(BF16)


def _memkv(mem, g, w_bf):
    b, m, d = mem.shape
    out_f = jax.ShapeDtypeStruct((b, m, MEM_WIDTH), F32)
    out_b = jax.ShapeDtypeStruct((b, m, MEM_WIDTH), BF16)
    out_t = jax.ShapeDtypeStruct((b, MEM_WIDTH, m), BF16)
    blk = lambda r, c: pl.BlockSpec((1, r, c), lambda i: (i, 0, 0))
    return pl.pallas_call(
        _memkv_kernel,
        grid=(b,),
        in_specs=[blk(m, d), pl.BlockSpec((1, d), lambda i: (0, 0)),
                  pl.BlockSpec((d, 2 * MEM_WIDTH), lambda i: (0, 0))],
        out_specs=[blk(m, MEM_WIDTH), blk(m, MEM_WIDTH), blk(MEM_WIDTH, m), blk(m, MEM_WIDTH)],
        out_shape=[out_f, out_f, out_t, out_b],
        name="memkv",
    )(mem, g, w_bf)


def _router_topk(hn, w_r_ref, b_r_ref, tope_ref, gates_ref, rows_at):
    n_e = w_r_ref.shape[1]
    logits = _dot(hn.astype(BF16), w_r_ref[...]) + b_r_ref[...]
    rows = logits.shape[0]
    col = _iota((rows, n_e), 1).astype(F32)
    vals, idxs = [], []
    cur = logits
    for _ in range(TOP_K):
        m = jnp.max(cur, axis=-1, keepdims=True)
        idx = jnp.min(jnp.where(cur == m, col, float(n_e)), axis=-1, keepdims=True)
        vals.append(m)
        idxs.append(idx)
        cur = jnp.where(col == idx, -jnp.inf, cur)
    exps = [jnp.exp(v - vals[0]) for v in vals]
    tot = exps[0] + exps[1] + exps[2] + exps[3]
    col4 = _iota((rows, TOP_K), 1)
    te = jnp.zeros((rows, TOP_K), F32)
    ga = jnp.zeros((rows, TOP_K), F32)
    for k in range(TOP_K):
        te = jnp.where(col4 == k, idxs[k], te)
        ga = jnp.where(col4 == k, exps[k] / tot, ga)
    tope_ref[rows_at, :] = te.astype(jnp.int32)
    gates_ref[rows_at, :] = ga


def _mix_out(x, attn, conv_out, cross, g_mix_ref, w_out_ref, g_ffn_ref, w_r_ref, b_r_ref,
             h_ref, hn_ref, tope_ref, gates_ref):
    rows = x.shape[0]
    chunk = MIX_CHUNK if rows % MIX_CHUNK == 0 else rows
    for r0 in range(0, rows, chunk):
        at = slice(r0, r0 + chunk)
        mix = jnp.concatenate([_rms(attn[at]), _rms(conv_out[at]), _rms(cross[at])], axis=-1) * g_mix_ref[...]
        h = x[at] + _dot(mix.astype(BF16), w_out_ref[...])
        hn = _rms(h) * g_ffn_ref[...]
        h_ref[at, :] = h
        _store_token_tiles(hn_ref, hn, base=r0 * SUBLANES)
        _router_topk(hn, w_r_ref, b_r_ref, tope_ref, gates_ref, at)


def _swa_bias(prev_lim):
    blk = WINDOW
    qi = _iota((blk, 2 * blk), 0)
    kj = _iota((blk, 2 * blk), 1)
    dist = blk + qi - kj
    mask = (dist >= 0) & (dist < WINDOW) & (kj >= prev_lim)
    distf = dist.astype(F32)
    return [jnp.where(mask, -ALIBI_SLOPES[h] * distf, -jnp.inf) for h in range(N_Q_HEADS)]


def _swa_block(q_blk, kk, vv, bias, sinks_ref):
    blk = WINDOW
    lane = _iota((2 * blk, KV_WIDTH), 1)
    lo = lane < HEAD_DIM
    kk_r = pltpu.roll(kk, HEAD_DIM, axis=1)
    vv_r = pltpu.roll(vv, HEAD_DIM, axis=1)
    kdup = [jnp.where(lo, kk, kk_r).astype(BF16), jnp.where(lo, kk_r, kk).astype(BF16)]
    vlo = [jnp.where(lo, vv, 0.0).astype(BF16), jnp.where(lo, vv_r, 0.0).astype(BF16)]
    vhi = [jnp.where(lo, 0.0, vv_r).astype(BF16), jnp.where(lo, 0.0, vv).astype(BF16)]
    qlo = _iota((blk, 2 * HEAD_DIM), 1) < HEAD_DIM
    outs = []
    for p in range(N_Q_HEADS // 2):
        kh = (2 * p) // (N_Q_HEADS // N_KV_HEADS)
        qp = q_blk[:, p * 2 * HEAD_DIM:(p + 1) * 2 * HEAD_DIM]
        acc = None
        for e in range(2):
            h = 2 * p + e
            qm = jnp.where(qlo if e == 0 else jnp.logical_not(qlo), qp, 0.0).astype(BF16)
            s = _dot_nt(qm, kdup[kh]) + bias[h]
            sink = sinks_ref[h]
            m = jnp.maximum(jnp.max(s, axis=-1, keepdims=True), sink)
            pe = jnp.exp(s - m)
            denom = jnp.sum(pe, axis=-1, keepdims=True) + jnp.exp(sink - m)
            o = _dot(pe.astype(BF16), (vlo if e == 0 else vhi)[kh]) / denom
            acc = o if acc is None else acc + o
        outs.append(acc)
    return jnp.concatenate(outs, axis=1)


def _mem_attend_shared(mq, mkt, mvb):
    t = mq.shape[0]
    m_tok = mvb.shape[0]
    qhead = _iota((t, MEM_WIDTH), 1) // HEAD_DIM
    vhead = _iota((m_tok, MEM_WIDTH), 1) // HEAD_DIM
    cross = None
    for h in range(N_MEM_HEADS):
        qm = jnp.where(qhead == h, mq, 0.0).astype(BF16)
        s = _dot(qm, mkt)
        m = jnp.max(s, axis=-1, keepdims=True)
        pe = jnp.exp(s - m)
        denom = jnp.sum(pe, axis=-1, keepdims=True)
        vm = jnp.where(vhead == h, mvb, jnp.zeros_like(mvb))
        o = _dot(pe.astype(BF16), vm) / denom
        cross = o if cross is None else cross + o
    return cross


def _mixer_p_kernel(sinks_ref, x_ref, g_attn_ref, w_in_ref, conv_w_ref, g_mix_ref, w_out_ref, g_ffn_ref,
                    w_r_ref, b_r_ref, mkt_ref, mvb_ref,
                    h_ref, hn_ref, tope_ref, gates_ref, lastk_ref, lastv_ref, convst_ref,
                    ck_ref, cv_ref, cu_ref):
    j = pl.program_id(1)
    nj = pl.num_programs(1)

    @pl.when(j == 0)
    def _():
        ck_ref[...] = jnp.zeros_like(ck_ref)
        cv_ref[...] = jnp.zeros_like(cv_ref)
        cu_ref[...] = jnp.zeros_like(cu_ref)

    x = x_ref[0]
    z = jnp.concatenate([_dot((_rms(x[r0:r0 + IN_CHUNK]) * g_attn_ref[...]).astype(BF16), w_in_ref[...])
                         for r0 in range(0, TQ, IN_CHUNK)], axis=0)
    c0 = ATTN_WIDTH
    c1 = c0 + KV_WIDTH
    c2 = c1 + KV_WIDTH
    cw = conv_w_ref.shape[1]
    c3, c4, c5 = c2 + cw, c2 + 2 * cw, c2 + 3 * cw
    q = z[:, :c0] * ATTN_SCALE
    k = z[:, c0:c1]
    v = z[:, c1:c2]
    cb = z[:, c2:c3]
    cc = z[:, c3:c4]
    cvv = z[:, c4:c5]
    mq = z[:, c5:] * ATTN_SCALE

    blk = WINDOW
    attn_blocks = []
    bias_inner = _swa_bias(0)
    for i in range(TQ // blk):
        if i == 0:
            pk, pv = ck_ref[...], cv_ref[...]
            bias = _swa_bias(jnp.where(j > 0, 0, blk))
        else:
            pk, pv = k[(i - 1) * blk:i * blk], v[(i - 1) * blk:i * blk]
            bias = bias_inner
        kk = jnp.concatenate([pk, k[i * blk:(i + 1) * blk]], axis=0)
        vv = jnp.concatenate([pv, v[i * blk:(i + 1) * blk]], axis=0)
        attn_blocks.append(_swa_block(q[i * blk:(i + 1) * blk], kk, vv, bias, sinks_ref))
    attn = jnp.concatenate(attn_blocks, axis=0)
    ck_ref[...] = k[TQ - blk:]
    cv_ref[...] = v[TQ - blk:]

    u = cc * cvv
    row = _iota(u.shape, 0)
    u1 = jnp.where(row == 0, cu_ref[SUBLANES - 1:SUBLANES, :], pltpu.roll(u, 1, axis=0))
    u2 = jnp.where(row == 0, cu_ref[SUBLANES - 2:SUBLANES - 1, :],
                   jnp.where(row == 1, cu_ref[SUBLANES - 1:SUBLANES, :], pltpu.roll(u, 2, axis=0)))
    conv_out = cb * (conv_w_ref[0:1, :] * u2 + conv_w_ref[1:2, :] * u1 + conv_w_ref[2:3, :] * u)
    cu_ref[...] = u[TQ - SUBLANES:]

    cross = _mem_attend_shared(mq, mkt_ref[0], mvb_ref[0])

    @pl.when(j == nj - 1)
    def _():
        lastk_ref[0] = k[TQ - blk:]
        lastv_ref[0] = v[TQ - blk:]
        convst_ref[0] = u[TQ - 2:]

    _mix_out(x, attn, conv_out, cross, g_mix_ref, w_out_ref, g_ffn_ref, w_r_ref, b_r_ref,
             h_ref, hn_ref, tope_ref, gates_ref)


def _mixer_p(x, sinks, g_attn, w_in, conv_w, g_mix, w_out, g_ffn, w_r, b_r, mkt, mvb):
    b, s, d = x.shape
    nj = s // TQ
    n = b * s
    cw = conv_w.shape[1]
    full = lambda a: pl.BlockSpec(a.shape, lambda bi, ji, *_: (0,) * a.ndim)
    tok = lambda w: pl.BlockSpec((TQ, w), lambda bi, ji, *_: (bi * nj + ji, 0))
    per_b = lambda r, c: pl.BlockSpec((1, r, c), lambda bi, ji, *_: (bi, 0, 0))
    grid_spec = pltpu.PrefetchScalarGridSpec(
        num_scalar_prefetch=1,
        grid=(b, nj),
        in_specs=[pl.BlockSpec((1, TQ, d), lambda bi, ji, *_: (bi, ji, 0)),
                  full(g_attn), full(w_in), full(conv_w), full(g_mix), full(w_out), full(g_ffn),
                  full(w_r), full(b_r), per_b(MEM_WIDTH, mkt.shape[2]), per_b(mvb.shape[1], MEM_WIDTH)],
        out_specs=[tok(d), pl.BlockSpec((TQ * SUBLANES, LANES), lambda bi, ji, *_: (bi * nj + ji, 0)),
                   tok(TOP_K), tok(TOP_K),
                   per_b(WINDOW, KV_WIDTH), per_b(WINDOW, KV_WIDTH), per_b(2, cw)],
        scratch_shapes=[pltpu.VMEM((WINDOW, KV_WIDTH), F32), pltpu.VMEM((WINDOW, KV_WIDTH), F32),
                        pltpu.VMEM((SUBLANES, cw), F32)],
    )
    return pl.pallas_call(
        _mixer_p_kernel,
        grid_spec=grid_spec,
        out_shape=[jax.ShapeDtypeStruct((n, d), F32), jax.ShapeDtypeStruct((n * SUBLANES, LANES), F32),
                   jax.ShapeDtypeStruct((n, TOP_K), jnp.int32), jax.ShapeDtypeStruct((n, TOP_K), F32),
                   jax.ShapeDtypeStruct((b, WINDOW, KV_WIDTH), F32),
                   jax.ShapeDtypeStruct((b, WINDOW, KV_WIDTH), F32),
                   jax.ShapeDtypeStruct((b, 2, cw), F32)],
        compiler_params=pltpu.CompilerParams(dimension_semantics=("arbitrary", "arbitrary"),
                                             vmem_limit_bytes=VMEM_LIMIT),
        name="mixer_p",
    )(sinks, x, g_attn, w_in, conv_w, g_mix, w_out, g_ffn, w_r, b_r, mkt, mvb)


def _per_head_column(values, hrow):
    col = jnp.zeros(hrow.shape, F32)
    for h in range(N_Q_HEADS):
        col = jnp.where(hrow == h, values[h], col)
    return col


def _mixer_s_kernel(sinks_ref, x_ref, pm1_ref, pm2_ref, wk_ref, wv_ref, mk_ref, mv_ref,
                    g_attn_ref, w_in_ref, conv_w_ref, g_mix_ref, w_out_ref, g_ffn_ref, w_r_ref, b_r_ref,
                    h_ref, hn_ref, tope_ref, gates_ref, nwk_ref, nwv_ref, u_ref, *, t_dec):
    r_tok = BB * t_dec
    r_exp = r_tok * REP
    qrows = t_dec * REP
    x = x_ref[...]
    xn = (_rms(x) * g_attn_ref[...]).astype(BF16)
    z = _dot(xn, w_in_ref[...])
    c0 = ATTN_WIDTH
    c1 = c0 + KV_WIDTH
    c2 = c1 + KV_WIDTH
    cw = conv_w_ref.shape[1]
    c3, c4, c5 = c2 + cw, c2 + 2 * cw, c2 + 3 * cw
    q = z[:, :c0] * ATTN_SCALE
    k_new = z[:, c0:c1]
    v_new = z[:, c1:c2]
    cb = z[:, c2:c3]
    cc = z[:, c3:c4]
    cvv = z[:, c4:c5]
    mq = z[:, c5:] * ATTN_SCALE
    win = wk_ref.shape[1]

    xi = _iota((KV_WIDTH, ATTN_WIDTH), 0)
    xl = _iota((KV_WIDTH, ATTN_WIDTH), 1)
    q_per_kv = N_Q_HEADS // N_KV_HEADS
    expand = (xi == (xl // (q_per_kv * HEAD_DIM)) * HEAD_DIM + xl % HEAD_DIM).astype(BF16)
    rr = _iota((r_exp, r_tok), 0)
    rc = _iota((r_exp, r_tok), 1)
    rep = (rr // REP == rc).astype(BF16)

    hrow = _iota((r_exp, 1), 0) % REP
    trow = (_iota((r_exp, 1), 0) // REP) % t_dec
    slope_col = _per_head_column(ALIBI_SLOPES, hrow)
    sink_col = _per_head_column([sinks_ref[h] for h in range(N_Q_HEADS)], hrow)

    qexp = jnp.where(hrow == _iota((r_exp, ATTN_WIDTH), 1) // HEAD_DIM, _dot(rep, q.astype(BF16)), 0.0)
    kexp = _dot(wk_ref[...].reshape(BB * win, KV_WIDTH).astype(BF16), expand).astype(BF16)
    vexp = _dot(wv_ref[...].reshape(BB * win, KV_WIDTH).astype(BF16), expand).astype(BF16)
    s = jnp.einsum("bqc,bkc->bqk", qexp.astype(BF16).reshape(BB, qrows, ATTN_WIDTH),
                   kexp.reshape(BB, win, ATTN_WIDTH), preferred_element_type=F32).reshape(r_exp, win)
    scol = _iota((r_exp, win), 1)
    s = s - slope_col * (win + trow - scol).astype(F32)
    s = jnp.where(scol > trow, s, -jnp.inf)
    knew_exp = _dot(k_new.astype(BF16), expand).astype(BF16)
    vnew_exp = _dot(v_new.astype(BF16), expand).astype(BF16)
    s_new, v_rep = [], []
    for jn in range(t_dec):
        rep_j = (rc == (rr // qrows) * t_dec + jn).astype(BF16)
        k_rep = _dot(rep_j, knew_exp)
        v_rep.append(_dot(rep_j, vnew_exp))
        sj = jnp.sum(qexp * k_rep, axis=-1, keepdims=True) - slope_col * (trow - jn).astype(F32)
        s_new.append(jnp.where(trow >= jn, sj, -jnp.inf))
    m = jnp.maximum(jnp.max(s, axis=-1, keepdims=True), sink_col)
    for sj in s_new:
        m = jnp.maximum(m, sj)
    pe = jnp.exp(s - m)
    denom = jnp.sum(pe, axis=-1, keepdims=True) + jnp.exp(sink_col - m)
    o = jnp.einsum("bqk,bkc->bqc", pe.astype(BF16).reshape(BB, qrows, win),
                   vexp.reshape(BB, win, ATTN_WIDTH), preferred_element_type=F32).reshape(r_exp, ATTN_WIDTH)
    for jn in range(t_dec):
        pj = jnp.exp(s_new[jn] - m)
        denom = denom + pj
        o = o + pj * v_rep[jn]
    o = jnp.where(hrow == _iota((r_exp, ATTN_WIDTH), 1) // HEAD_DIM, o / denom, 0.0)
    attn = jnp.sum(o.reshape(r_tok, REP, ATTN_WIDTH), axis=1)

    m_tok = mk_ref.shape[1]
    mhead = _iota((r_exp, MEM_WIDTH), 1) // HEAD_DIM
    mqexp = jnp.where(hrow == mhead, _dot(rep, mq.astype(BF16)), 0.0).astype(BF16)
    sm = jnp.einsum("bqc,bmc->bqm", mqexp.reshape(BB, qrows, MEM_WIDTH), mk_ref[...].astype(BF16),
                    preferred_element_type=F32).reshape(r_exp, m_tok)
    mm = jnp.max(sm, axis=-1, keepdims=True)
    pm = jnp.exp(sm - mm)
    dm = jnp.sum(pm, axis=-1, keepdims=True)
    om = jnp.einsum("bqm,bmc->bqc", pm.astype(BF16).reshape(BB, qrows, m_tok), mv_ref[...].astype(BF16),
                    preferred_element_type=F32).reshape(r_exp, MEM_WIDTH)
    om = jnp.where(hrow == mhead, om / dm, 0.0)
    cross = jnp.sum(om.reshape(r_tok, REP, MEM_WIDTH), axis=1)

    u = cc * cvv
    tt = _iota(u.shape, 0) % t_dec
    u1 = jnp.where(tt >= 1, pltpu.roll(u, 1, axis=0), pm1_ref[...])
    u2 = jnp.where(tt >= 2, pltpu.roll(u, 2, axis=0), pm2_ref[...])
    conv_out = cb * (conv_w_ref[0:1, :] * u2 + conv_w_ref[1:2, :] * u1 + conv_w_ref[2:3, :] * u)
    u_ref[...] = u

    nwk_ref[:, 0:win - t_dec, :] = wk_ref[:, t_dec:win, :]
    nwv_ref[:, 0:win - t_dec, :] = wv_ref[:, t_dec:win, :]
    for b in range(BB):
        nwk_ref[b, win - t_dec:win, :] = k_new[b * t_dec:(b + 1) * t_dec, :]
        nwv_ref[b, win - t_dec:win, :] = v_new[b * t_dec:(b + 1) * t_dec, :]

    _mix_out(x, attn, conv_out, cross, g_mix_ref, w_out_ref, g_ffn_ref, w_r_ref, b_r_ref,
             h_ref, hn_ref, tope_ref, gates_ref)


def _mixer_s(x2, t_dec, pm1, pm2, wk, wv, mk, mv, sinks, g_attn, w_in, conv_w, g_mix, w_out, g_ffn, w_r, b_r):
    n, d = x2.shape
    nb = wk.shape[0]
    win = wk.shape[1]
    m_tok = mk.shape[1]
    cw = conv_w.shape[1]
    r_tok = BB * t_dec
    full = lambda a: pl.BlockSpec(a.shape, lambda i, *_: (0,) * a.ndim)
    tok = lambda w: pl.BlockSpec((r_tok, w), lambda i, *_: (i, 0))
    per_b = lambda r, c: pl.BlockSpec((BB, r, c), lambda i, *_: (i, 0, 0))
    grid_spec = pltpu.PrefetchScalarGridSpec(
        num_scalar_prefetch=1,
        grid=(nb // BB,),
        in_specs=[tok(d), tok(cw), tok(cw), per_b(win, KV_WIDTH), per_b(win, KV_WIDTH),
                  per_b(m_tok, MEM_WIDTH), per_b(m_tok, MEM_WIDTH),
                  full(g_attn), full(w_in), full(conv_w), full(g_mix), full(w_out), full(g_ffn),
                  full(w_r), full(b_r)],
        out_specs=[tok(d), pl.BlockSpec((r_tok * SUBLANES, LANES), lambda i, *_: (i, 0)),
                   tok(TOP_K), tok(TOP_K), per_b(win, KV_WIDTH), per_b(win, KV_WIDTH), tok(cw)],
    )
    return pl.pallas_call(
        functools.partial(_mixer_s_kernel, t_dec=t_dec),
        grid_spec=grid_spec,
        out_shape=[jax.ShapeDtypeStruct((n, d), F32), jax.ShapeDtypeStruct((n * SUBLANES, LANES), F32),
                   jax.ShapeDtypeStruct((n, TOP_K), jnp.int32), jax.ShapeDtypeStruct((n, TOP_K), F32),
                   jax.ShapeDtypeStruct((nb, win, KV_WIDTH), F32), jax.ShapeDtypeStruct((nb, win, KV_WIDTH), F32),
                   jax.ShapeDtypeStruct((n, cw), F32)],
        compiler_params=pltpu.CompilerParams(dimension_semantics=("arbitrary",), vmem_limit_bytes=VMEM_LIMIT),
        name="mixer_s",
    )(sinks, x2, pm1, pm2, wk, wv, mk, mv, g_attn, w_in, conv_w, g_mix, w_out, g_ffn, w_r, b_r)


def _lpos_kernel(tope_ref, lpos_ref, counts_ref, tri_ref, *, n_e):
    i = pl.program_id(0)
    tm = tope_ref.shape[0]

    @pl.when(i == 0)
    def _():
        tri_ref[...] = (_iota((tm, tm), 0) > _iota((tm, tm), 1)).astype(BF16)

    te = tope_ref[...]
    col = _iota((tm, n_e), 1)
    hits = [te[:, k:k + 1] == col for k in range(TOP_K)]
    onehot = jnp.zeros((tm, n_e), F32)
    lower = jnp.zeros((tm, n_e), F32)
    for k in range(TOP_K):
        onehot = onehot + hits[k].astype(F32)
        lower = lower + (te[:, k:k + 1] < col).astype(F32)
    ahead = _dot(tri_ref[...], onehot.astype(BF16)) + jnp.sum(lower, axis=0, keepdims=True)
    col4 = _iota((tm, TOP_K), 1)
    pos = jnp.zeros((tm, TOP_K), F32)
    for k in range(TOP_K):
        pos = jnp.where(col4 == k, jnp.sum(jnp.where(hits[k], ahead, 0.0), axis=-1, keepdims=True), pos)
    half = (i % 2) * (tm * TOP_K)
    lpos_ref[...] = (pos.astype(jnp.int32) + half) * SUBLANES
    counts_ref[0] = jnp.sum(onehot, axis=0, keepdims=True)


def _lpos(tope, tm, n_e):
    n = tope.shape[0]
    return pl.pallas_call(
        functools.partial(_lpos_kernel, n_e=n_e),
        grid=(n // tm,),
        in_specs=[pl.BlockSpec((tm, TOP_K), lambda i: (i, 0))],
        out_specs=[pl.BlockSpec((tm, TOP_K), lambda i: (i, 0)), pl.BlockSpec((1, 1, n_e), lambda i: (i, 0, 0))],
        out_shape=[jax.ShapeDtypeStruct((n, TOP_K), jnp.int32), jax.ShapeDtypeStruct((n // tm, 1, n_e), F32)],
        scratch_shapes=[pltpu.VMEM((tm, tm), BF16)],
        compiler_params=pltpu.CompilerParams(dimension_semantics=("arbitrary",)),
        name="lpos",
    )(tope)


def _rows(ref, first, n_rows):
    return ref.at[pl.ds(pl.multiple_of(first, SUBLANES), n_rows * SUBLANES)]


def _group_chunks(meta, g, n_e, max_rows, make_copy, wait):
    off_ref, cnt_ref, lst_ref = meta
    if wait:
        make_copy(0, 0, max_rows * TOP_K).wait()
        return

    def per_expert(e, c):
        off = off_ref[g * n_e + e]
        lst = lst_ref[g * n_e + e]
        _run_copies(cnt_ref[g * n_e + e], max_rows, lambda done, size: make_copy(lst + done, off + done, size),
                    wait=False)
        return c

    lax.fori_loop(0, n_e, per_expert, 0)


def _run_copies(n_rows, max_rows, make_copy, wait):
    done = 0
    for n, size in enumerate([1 << b for b in range(max_rows.bit_length() - 1, -1, -1)]):
        bit = n_rows & (size * SUBLANES)

        @pl.when(bit != 0)
        def _():
            cp = make_copy(done, size)
            cp.wait() if wait else cp.start(priority=n % 2)
        done = done + bit


def _dispatch_kernel(off_ref, cnt_ref, lst_ref, zrow_ref, zcnt_ref, nused_ref, lpos_ref, hn_a_ref, hn_b_ref, xs_ref,
                     stage_ref, zbuf_ref, sem, zsem, *, n_e, n_blocks, groups_a):
    j = pl.program_id(0)
    nj = pl.num_programs(0)
    tm = hn_a_ref.shape[0] // SUBLANES
    slot = j % 2
    meta = (off_ref, cnt_ref, lst_ref)

    def chunks(g, s, wait):
        _group_chunks(meta, g, n_e, tm,
                      lambda lrow, grow, size: pltpu.make_async_copy(
                          _rows(stage_ref, s * (tm * TOP_K * SUBLANES) + lrow, size), _rows(xs_ref, grow, size),
                          sem.at[s]), wait)

    def zero_fill(wait):
        def zero_run(first_row, n_rows):
            _run_copies(n_rows, BM, lambda done, size: pltpu.make_async_copy(
                _rows(zbuf_ref, 0, size), _rows(xs_ref, first_row + done, size), zsem), wait)

        def expert_pad(e, c):
            zero_run(zrow_ref[e], zcnt_ref[e])
            return c

        def tail_block(b, c):
            zero_run(b * (BM * SUBLANES), jnp.int32(BM * SUBLANES))
            return c

        lax.fori_loop(0, n_e, expert_pad, 0)
        lax.fori_loop(nused_ref[0], n_blocks, tail_block, 0)

    @pl.when(j == 0)
    def _():
        zbuf_ref[...] = jnp.zeros_like(zbuf_ref)
        zero_fill(False)

    @pl.when(j >= 2)
    def _():
        chunks(j - 2, slot, True)

    def place_from(hn_ref):
        def place(t, c):
            tile = hn_ref[pl.ds(pl.multiple_of(t * SUBLANES, SUBLANES), SUBLANES), :]
            for k in range(TOP_K):
                pos = pl.multiple_of(lpos_ref[0, 0, t * TOP_K + k], SUBLANES)
                stage_ref[pl.ds(pos, SUBLANES), :] = tile
            return c
        lax.fori_loop(0, tm, place, 0, unroll=TOKEN_UNROLL)

    @pl.when(j < groups_a)
    def _():
        place_from(hn_a_ref)

    @pl.when(j >= groups_a)
    def _():
        place_from(hn_b_ref)

    chunks(j, slot, False)

    @pl.when(j == nj - 1)
    def _():
        @pl.when(j >= 1)
        def _():
            chunks(j - 1, 1 - slot, True)
        chunks(j, slot, True)
        zero_fill(True)


def _dispatch(hn_a, hn_b, lpos, meta, zrow, zcnt, nused, tm, n_e, n_blocks):
    nt = lpos.shape[0] // tm
    groups_a = hn_a.shape[0] // (tm * SUBLANES)
    assert groups_a >= 1 and groups_a + hn_b.shape[0] // (tm * SUBLANES) == nt
    grid_spec = pltpu.PrefetchScalarGridSpec(
        num_scalar_prefetch=6,
        grid=(nt,),
        in_specs=[pl.BlockSpec((1, 1, tm * TOP_K), lambda j, *_: (j, 0, 0), memory_space=pltpu.SMEM),
                  pl.BlockSpec((tm * SUBLANES, LANES), lambda j, *_: (jnp.minimum(j, groups_a - 1), 0)),
                  pl.BlockSpec((tm * SUBLANES, LANES), lambda j, *_: (jnp.maximum(j - groups_a, 0), 0))],
        out_specs=pl.BlockSpec(memory_space=pl.ANY),
        scratch_shapes=[pltpu.VMEM((2 * tm * TOP_K * SUBLANES, LANES), F32), pltpu.VMEM((BM * SUBLANES, LANES), F32),
                        pltpu.SemaphoreType.DMA((2,)), pltpu.SemaphoreType.DMA(())],
    )
    return pl.pallas_call(
        functools.partial(_dispatch_kernel, n_e=n_e, n_blocks=n_blocks, groups_a=groups_a),
        grid_spec=grid_spec,
        out_shape=jax.ShapeDtypeStruct((n_blocks * BM * SUBLANES, LANES), F32),
        compiler_params=pltpu.CompilerParams(dimension_semantics=("arbitrary",), vmem_limit_bytes=VMEM_LIMIT),
        name="dispatch",
    )(*meta, zrow, zcnt, nused, lpos.reshape(nt, 1, tm * TOP_K), hn_a, hn_b)


def _experts_kernel(blk_e_ref, nused_ref, next_e_ref, blk_rows_ref, xs_ref, bgu_ref, bd_ref, wgu_hbm_ref, wd_hbm_ref, ys_ref,
                    wgu_ref, wd_ref, wgu_bf_ref, wd_bf_ref, wsem):
    i = pl.program_id(0)
    nused = nused_ref[0]
    d, d_ff2 = wgu_ref.shape
    grp = 2 * LANES

    def weight_copies(e):
        gu_cols = d_ff2 // W_PARTS
        dn_rows = (d_ff2 // 2) // W_PARTS
        cps = []
        for p in range(W_PARTS):
            cps.append(pltpu.make_async_copy(wgu_hbm_ref.at[e, :, pl.ds(p * gu_cols, gu_cols)],
                                             wgu_ref.at[:, pl.ds(p * gu_cols, gu_cols)], wsem.at[p]))
            cps.append(pltpu.make_async_copy(wd_hbm_ref.at[e, pl.ds(p * dn_rows, dn_rows), :],
                                             wd_ref.at[pl.ds(p * dn_rows, dn_rows), :], wsem.at[W_PARTS + p]))
        return cps

    @pl.when(i < nused)
    def _():
        e = blk_e_ref[i]
        e_prev = blk_e_ref[jnp.maximum(i - 1, 0)]

        @pl.when(i == 0)
        def _():
            for cp in weight_copies(e):
                cp.start()

        @pl.when((i == 0) | (e != e_prev))
        def _():
            for cp in weight_copies(e):
                cp.wait()
            pr = _iota((grp, grp), 0)
            pc = _iota((grp, grp), 1)
            perm = (pr == jnp.where(pc < LANES, 2 * pc, 2 * (pc - LANES) + 1)).astype(BF16)
            for g in range(d_ff2 // grp):
                w = wgu_ref[:, g * grp:(g + 1) * grp].astype(BF16)
                wgu_bf_ref[:, g * grp:(g + 1) * grp] = _dot(w, perm).astype(BF16)
            wd_bf_ref[...] = wd_ref[...].astype(BF16)

            @pl.when(next_e_ref[e] >= 0)
            def _():
                for cp in weight_copies(next_e_ref[e]):
                    cp.start()

        def ffn(r0, n_rows):
            x = jnp.concatenate([_load_token_tiles(xs_ref, n_rows, s, base=r0 * SUBLANES).astype(BF16)
                                 for s in range(SUBLANES)], axis=1)
            hgu = _dot(x, wgu_bf_ref[...]) + bgu_ref[0]
            acts = []
            for g in range(d_ff2 // grp):
                gate = jnp.minimum(hgu[:, g * grp:g * grp + LANES], SWIGLU_LIMIT)
                up = jnp.clip(hgu[:, g * grp + LANES:(g + 1) * grp], -SWIGLU_LIMIT, SWIGLU_LIMIT)
                glu = gate * (1.0 / (1.0 + jnp.exp(-SWIGLU_ALPHA * gate)))
                acts.append(((up + 1.0) * glu).astype(BF16))
            act = jnp.concatenate(acts, axis=1)
            _store_token_tiles(ys_ref, _dot(act, wd_bf_ref[...]) + bd_ref[0], base=r0 * SUBLANES)

        rows = blk_rows_ref[i]

        @pl.when(rows == BM)
        def _():
            ffn(0, BM)

        @pl.when(rows < BM)
        def _():
            for r0 in range(0, BM, TAIL_CHUNK):
                @pl.when(r0 < rows)
                def _():
                    ffn(r0, TAIL_CHUNK)

                @pl.when(r0 >= rows)
                def _():
                    ys_ref[r0 * SUBLANES:(r0 + TAIL_CHUNK) * SUBLANES, :] = jnp.zeros(
                        (TAIL_CHUNK * SUBLANES, LANES), F32)

    @pl.when(i >= nused)
    def _():
        ys_ref[...] = jnp.zeros_like(ys_ref)


def _experts(xs, blk_e, nused, next_e, blk_rows, w_gate_up, b_gu_perm, w_down, b_down):
    n_e, d, d_ff2 = w_gate_up.shape
    n_blocks = xs.shape[0] // (BM * SUBLANES)
    expert = lambda i, be, nu, *_: be[jnp.minimum(i, jnp.maximum(nu[0] - 1, 0))]
    rows_spec = pl.BlockSpec((BM * SUBLANES, LANES), lambda i, *_: (i, 0))
    grid_spec = pltpu.PrefetchScalarGridSpec(
        num_scalar_prefetch=4,
        grid=(n_blocks,),
        in_specs=[pl.BlockSpec((BM * SUBLANES, LANES), lambda i, be, nu, *_: (jnp.minimum(i, jnp.maximum(nu[0] - 1, 0)), 0)),
                  pl.BlockSpec((1, 1, d_ff2), lambda i, *s: (expert(i, *s), 0, 0)),
                  pl.BlockSpec((1, 1, d), lambda i, *s: (expert(i, *s), 0, 0)),
                  pl.BlockSpec(memory_space=pl.ANY), pl.BlockSpec(memory_space=pl.ANY)],
        out_specs=rows_spec,
        scratch_shapes=[pltpu.VMEM((d, d_ff2), F32), pltpu.VMEM((d_ff2 // 2, d), F32),
                        pltpu.VMEM((d, d_ff2), BF16), pltpu.VMEM((d_ff2 // 2, d), BF16),
                        pltpu.SemaphoreType.DMA((2 * W_PARTS,))],
    )
    return pl.pallas_call(
        _experts_kernel,
        grid_spec=grid_spec,
        out_shape=jax.ShapeDtypeStruct(xs.shape, F32),
        compiler_params=pltpu.CompilerParams(dimension_semantics=("arbitrary",), vmem_limit_bytes=VMEM_LIMIT),
        name="experts",
    )(blk_e, nused, next_e, blk_rows, xs, b_gu_perm, b_down, w_gate_up, w_down)


def _combine_kernel(off_ref, cnt_ref, lst_ref, lpos_ref, gates_ref, h_ref, g_ref, ys_ref, out_ref,
                    stage_ref, acc_ref, sem, *, n_e, g0):
    i = pl.program_id(0)
    ng = pl.num_programs(0)
    tm, d = h_ref.shape
    slot = (g0 + i) % 2
    meta = (off_ref, cnt_ref, lst_ref)

    def chunks(g, s, wait):
        _group_chunks(meta, g, n_e, tm,
                      lambda lrow, grow, size: pltpu.make_async_copy(
                          _rows(ys_ref, grow, size), _rows(stage_ref, s * (tm * TOP_K * SUBLANES) + lrow, size),
                          sem.at[s]), wait)

    @pl.when(i == 0)
    def _():
        chunks(g0, slot, False)

    @pl.when(i + 1 < ng)
    def _():
        chunks(g0 + i + 1, 1 - slot, False)

    chunks(g0 + i, slot, True)

    def mix(t, c):
        acc = None
        for k in range(TOP_K):
            pos = pl.multiple_of(lpos_ref[0, 0, t * TOP_K + k], SUBLANES)
            term = stage_ref[pl.ds(pos, SUBLANES), :] * gates_ref[0, 0, t * TOP_K + k]
            acc = term if acc is None else acc + term
        acc_ref[pl.ds(pl.multiple_of(t * SUBLANES, SUBLANES), SUBLANES), :] = acc
        return c

    lax.fori_loop(0, tm, mix, 0, unroll=TOKEN_UNROLL)

    h = h_ref[...]
    parts = []
    sq = jnp.zeros((tm, LANES), F32)
    for s in range(d // LANES):
        y = h[:, s * LANES:(s + 1) * LANES] + _load_token_tiles(acc_ref, tm, s)
        sq = sq + y * y
        parts.append(y)
    rinv = lax.rsqrt(jnp.sum(sq, axis=-1, keepdims=True) / d + EPS)
    out_ref[...] = jnp.concatenate(parts, axis=1) * rinv * g_ref[...]


def _combine(ys, h, lpos, gates, meta, g_final, tm, n_e, g0):
    n, d = h.shape
    grid_spec = pltpu.PrefetchScalarGridSpec(
        num_scalar_prefetch=3,
        grid=(n // tm,),
        in_specs=[pl.BlockSpec((1, 1, tm * TOP_K), lambda i, *_: (g0 + i, 0, 0), memory_space=pltpu.SMEM),
                  pl.BlockSpec((1, 1, tm * TOP_K), lambda i, *_: (g0 + i, 0, 0), memory_space=pltpu.SMEM),
                  pl.BlockSpec((tm, d), lambda i, *_: (i, 0)),
                  pl.BlockSpec((1, d), lambda i, *_: (0, 0)),
                  pl.BlockSpec(memory_space=pl.ANY)],
        out_specs=pl.BlockSpec((tm, d), lambda i, *_: (i, 0)),
        scratch_shapes=[pltpu.VMEM((2 * tm * TOP_K * SUBLANES, LANES), F32), pltpu.VMEM((tm * SUBLANES, LANES), F32),
                        pltpu.SemaphoreType.DMA((2,))],
    )
    ngroups = lpos.shape[0] // tm
    return pl.pallas_call(
        functools.partial(_combine_kernel, n_e=n_e, g0=g0),
        grid_spec=grid_spec,
        out_shape=jax.ShapeDtypeStruct((n, d), F32),
        compiler_params=pltpu.CompilerParams(dimension_semantics=("arbitrary",), vmem_limit_bytes=VMEM_LIMIT),
        name="combine",
    )(*meta, lpos.reshape(ngroups, 1, tm * TOP_K), gates.reshape(ngroups, 1, tm * TOP_K), h, g_final, ys)


def _largest_tile(cands, *sizes):
    for c in cands:
        if all(s % c == 0 for s in sizes):
            return c
    raise ValueError(f"no tile in {cands} divides {sizes}")


def kernel(x_prompt, x_sample, mem_prompt, cache_win_k, cache_win_v, state_conv, cache_mem_k, cache_mem_v, g_attn_norm, w_in, conv_w, attn_sinks, g_mem_norm, w_mem_kv, g_mix_out, w_out, g_ffn_norm, w_router, b_router, w_gate_up, b_gate_up, w_down, b_down, g_final):
    depth = w_in.shape[0]
    assert depth == 1, "single-layer step"
    b, s, d = x_prompt.shape
    nb, t_dec, _ = x_sample.shape
    n_e = w_router.shape[2]
    d_ff2 = w_gate_up.shape[3]
    cw = conv_w.shape[2]
    win = cache_win_k.shape[2]
    m_tok = cache_mem_k.shape[2]
    assert s % TQ == 0 and nb % BB == 0 and win == WINDOW and t_dec <= SUBLANES and d_ff2 % (2 * LANES) == 0
    assert d == SUBLANES * LANES, "token-tile layout: one token is one (8, 128) f32 tile"

    row = lambda a: a.reshape(1, -1)
    w_in_bf = w_in[0].astype(BF16)
    w_out_bf = w_out[0].astype(BF16)
    w_r_bf = w_router[0].astype(BF16)
    sinks = attn_sinks[0].astype(F32)
    shared = (row(g_attn_norm[0]), w_in_bf, conv_w[0], row(g_mix_out[0]), w_out_bf, row(g_ffn_norm[0]),
              w_r_bf, row(b_router[0]))

    mk_p, mv_p, mkt, mvb = _memkv(mem_prompt, row(g_mem_norm[0]), w_mem_kv[0].astype(BF16))
    h_p, hn_p, tope_p, gates_p, lastk, lastv, convst = _mixer_p(x_prompt, sinks, *shared, mkt, mvb)

    zeros = lambda r: jnp.zeros((nb, r, cw), F32)
    st = state_conv[0]
    pm1 = jnp.concatenate([st[:, 1:2], zeros(t_dec - 1)], axis=1).reshape(nb * t_dec, cw)
    pm2 = jnp.concatenate([st, zeros(t_dec - 2)], axis=1).reshape(nb * t_dec, cw)
    h_s, hn_s, tope_s, gates_s, nwk, nwv, u_s = _mixer_s(
        x_sample.reshape(nb * t_dec, d), t_dec, pm1, pm2,
        cache_win_k[0].reshape(nb, win, KV_WIDTH), cache_win_v[0].reshape(nb, win, KV_WIDTH),
        cache_mem_k[0].reshape(nb, m_tok, MEM_WIDTH), cache_mem_v[0].reshape(nb, m_tok, MEM_WIDTH),
        sinks, *shared)

    n_p, n_s = b * s, nb * t_dec
    n = n_p + n_s
    tm = _largest_tile((512, 256, 128, 64, 32, 16, 8), n_p, n_s)
    tope = jnp.concatenate([tope_p, tope_s], axis=0)
    gates = jnp.concatenate([gates_p, gates_s], axis=0)

    lpos, cnt_f = _lpos(tope, tm, n_e)
    cnt = cnt_f[:, 0, :].astype(jnp.int32)
    counts = jnp.sum(cnt, axis=0)
    padded = (counts + BM - 1) // BM * BM
    pad_ends = jnp.cumsum(padded)
    pad_starts = pad_ends - padded
    nk = n * TOP_K
    n_blocks = -(-nk // BM) + n_e
    nused = (pad_ends[-1:] // BM).astype(jnp.int32)
    blk_start = jnp.arange(n_blocks, dtype=jnp.int32) * BM
    blk_e = jnp.minimum(jnp.sum((pad_ends[None, :] <= blk_start[:, None]).astype(jnp.int32), axis=1), n_e - 1)
    zrow = ((pad_starts + counts) * SUBLANES).astype(jnp.int32)
    zcnt = ((padded - counts) * SUBLANES).astype(jnp.int32)
    off = pad_starts[None, :] + jnp.cumsum(cnt, axis=0) - cnt
    lstart = jnp.cumsum(cnt, axis=1) - cnt
    meta = tuple((a.reshape(-1) * SUBLANES).astype(jnp.int32) for a in (off, cnt, lstart))

    xs = _dispatch(hn_p, hn_s, lpos, meta, zrow, zcnt, nused, tm, n_e, n_blocks)

    grp = 2 * LANES
    b_gu = b_gate_up[0].reshape(n_e, d_ff2 // grp, LANES, 2).transpose(0, 1, 3, 2).reshape(n_e, 1, d_ff2)
    owner = jnp.where(padded > 0, jnp.arange(n_e, dtype=jnp.int32), n_e)
    following = jnp.concatenate([lax.cummin(owner, reverse=True)[1:], jnp.full((1,), n_e, jnp.int32)])
    next_e = jnp.where(following < n_e, following, -1).astype(jnp.int32)
    own = blk_e[:, None] == jnp.arange(n_e, dtype=jnp.int32)[None, :]
    blk_rows = jnp.clip(jnp.sum(jnp.where(own, (pad_starts + counts)[None, :], 0), axis=1) - blk_start, 0, BM)
    blk_rows = blk_rows.astype(jnp.int32)
    ys = _experts(xs, blk_e, nused, next_e, blk_rows, w_gate_up[0], b_gu, w_down[0], b_down[0].reshape(n_e, 1, d))

    g_fin = row(g_final)
    y_p = _combine(ys, h_p, lpos, gates, meta, g_fin, tm, n_e, 0)
    y_s = _combine(ys, h_s, lpos, gates, meta, g_fin, tm, n_e, n_p // tm)

    kv5 = lambda a, bsz, r, hds: a.reshape(1, bsz, r, hds, HEAD_DIM)
    return (y_p.reshape(b, s, d), y_s.reshape(nb, t_dec, d),
            kv5(lastk, b, WINDOW, N_KV_HEADS), kv5(lastv, b, WINDOW, N_KV_HEADS),
            convst.reshape(1, b, 2, cw),
            kv5(mk_p, b, m_tok, N_MEM_HEADS), kv5(mv_p, b, m_tok, N_MEM_HEADS),
            kv5(nwk, nb, win, N_KV_HEADS), kv5(nwv, nb, win, N_KV_HEADS),
            u_s.reshape(nb, t_dec, cw)[:, t_dec - 2:].reshape(1, nb, 2, cw))
```
